```python
import math
import jax
import jax.numpy as jnp
from jax import lax
import numpy as np

D_MODEL = 2048
BATCH = 2
SEQ = 4096
DEPTH = 1
DEC_BATCH = 128
DEC_SEQ = 4
PAST_LEN = 2048
PAGE_SIZE = 128

N_HEADS = 16
N_KV_HEADS = 4
HEAD_DIM = 128
ROPE_THETA = 500000.0
ROT_FRACTION_DIV = 4
IDX_HEADS = 16
IDX_DIM = 64
TOPK_MAX = 256
Q_BLOCK = 128
GDN_QK_HEADS = 8
GDN_V_HEADS = 16
GDN_DK = 128
GDN_DV = 128
CONV_W = 4
GDN_CHUNK = 64
CONV_DIM = 2 * GDN_QK_HEADS * GDN_DK + GDN_V_HEADS * GDN_DV
N_EXPERTS = 32
TOP_K = 4
D_FF = D_MODEL
SWIGLU_LIMIT = 7.0
SWIGLU_ALPHA = 1.702
MOE_BLOCK = 128
NORM_EPS = 1e-6
ATTN_WIDTH = N_HEADS * HEAD_DIM
GDN_WIDTH = GDN_V_HEADS * GDN_DV
IN_SPLITS = (ATTN_WIDTH, N_KV_HEADS * HEAD_DIM, N_KV_HEADS * HEAD_DIM, IDX_HEADS * IDX_DIM, IDX_DIM, IDX_HEADS,
             GDN_QK_HEADS * GDN_DK, GDN_QK_HEADS * GDN_DK, GDN_WIDTH, GDN_WIDTH, GDN_V_HEADS, GDN_V_HEADS, 2 * D_MODEL)
IN_TOTAL = sum(IN_SPLITS)

kernel_name = 'hybrid_dsa_gdn_moe_adaln_step'


def rms_norm(x, g):
    xf = x.astype(jnp.float32)
    y = xf * lax.rsqrt(jnp.mean(xf * xf, axis=-1, keepdims=True) + NORM_EPS)
    return (y * g.astype(jnp.float32)).astype(x.dtype)


def l2_normalize(x):
    xf = x.astype(jnp.float32)
    return xf * lax.rsqrt(jnp.sum(xf * xf, axis=-1, keepdims=True) + NORM_EPS)


def partial_rotary(x, pos):
    d = x.shape[-1]
    rot = d // ROT_FRACTION_DIV
    half = rot // 2
    inv = ROPE_THETA ** (-jnp.arange(half, dtype=jnp.float32) * (2.0 / rot))
    ang = pos.astype(jnp.float32)[:, None] * inv[None, :]
    cos = jnp.cos(ang)[None, :, None, :]
    sin = jnp.sin(ang)[None, :, None, :]
    xf = x.astype(jnp.float32)
    x1 = xf[..., :half]
    x2 = xf[..., half:rot]
    out = jnp.concatenate([x1 * cos - x2 * sin, x2 * cos + x1 * sin, xf[..., rot:]], axis=-1)
    return out.astype(x.dtype)


def indexer_select(qi, wi, kidx, q_pos, k_pos, n_sel):
    s = jax.nn.relu(jnp.einsum('bthd,bsd->bths', qi, kidx).astype(jnp.float32))
    score = jnp.einsum('bths,bth->bts', s, wi.astype(jnp.float32))
    score = jnp.where(k_pos[None, None, :] <= q_pos[None, :, None], score, -jnp.inf)
    _, idx = lax.top_k(score, n_sel)
    return idx.astype(jnp.int32)


def sparse_attention(q, k_sel, v_sel, sel_pos, q_pos):
    b, t, h, d = q.shape
    qg = q.reshape(b, t, N_KV_HEADS, h // N_KV_HEADS, d)
    s = jnp.einsum('btngd,btknd->btngk', qg, k_sel).astype(jnp.float32) * (d ** -0.5)
    valid = (sel_pos <= q_pos[None, :, None])[:, :, None, None, :]
    p = jax.nn.softmax(jnp.where(valid, s, -jnp.inf), axis=-1).astype(v_sel.dtype)
    o = jnp.einsum('btngk,btknd->btngd', p, v_sel)
    return o.reshape(b, t, h, d)


def attend_prompt(q, k, v, qi, ki, wi):
    b, t = q.shape[:2]
    n_sel = min(TOPK_MAX, t // 4)
    nb = t // Q_BLOCK
    k_pos = jnp.arange(t, dtype=jnp.int32)
    bidx = jnp.arange(b)[:, None, None]

    def blocks(a):
        return jnp.swapaxes(a.reshape((b, nb, Q_BLOCK) + a.shape[2:]), 0, 1)

    def one_block(args):
        qb, qib, wib, qpos = args
        idx = indexer_select(qib, wib, ki, qpos, k_pos, n_sel)
        return sparse_attention(qb, k[bidx, idx], v[bidx, idx], idx, qpos)

    o = lax.map(one_block, (blocks(q), blocks(qi), blocks(wi), k_pos.reshape(nb, Q_BLOCK)))
    return jnp.swapaxes(o, 0, 1).reshape(q.shape)


def make_attend_sample(ck, cv, cki, page_table):
    def attend(q, k, v, qi, ki, wi):
        db, t = q.shape[:2]
        past = page_table.shape[1] * PAGE_SIZE
        n_sel = min(TOPK_MAX, (past + t) // 4)
        past_ki = cki[page_table].reshape(db, past, IDX_DIM)
        all_ki = jnp.concatenate([past_ki.astype(ki.dtype), ki], axis=1)
        q_pos = past + jnp.arange(t, dtype=jnp.int32)
        k_pos = jnp.arange(past + t, dtype=jnp.int32)
        idx = indexer_select(qi, wi, all_ki, q_pos, k_pos, n_sel)
        bidx = jnp.arange(db)[:, None, None]
        past_idx = jnp.minimum(idx, past - 1)
        phys = page_table[bidx, past_idx // PAGE_SIZE]
        off = past_idx % PAGE_SIZE
        new_idx = jnp.clip(idx - past, 0, t - 1)
        is_new = (idx >= past)[..., None, None]
        k_sel = jnp.where(is_new, k[bidx, new_idx], ck[phys, off].astype(k.dtype))
        v_sel = jnp.where(is_new, v[bidx, new_idx], cv[phys, off].astype(v.dtype))
        return sparse_attention(q, k_sel, v_sel, idx, q_pos)
    return attend


def gated_delta_chunked(q, k, v, g, beta, s0):
    f32 = jnp.float32
    bsz, t, h, dk = q.shape
    dv = v.shape[-1]
    c = GDN_CHUNK
    n = -(-t // c)
    pad = n * c - t

    def chunk4(a):
        a = jnp.pad(a.astype(f32), ((0, 0), (0, pad), (0, 0), (0, 0)))
        return a.reshape(bsz, n, c, a.shape[2], a.shape[3]).transpose(1, 0, 3, 2, 4)

    def chunk3(a):
        a = jnp.pad(a.astype(f32), ((0, 0), (0, pad), (0, 0)))
        return a.reshape(bsz, n, c, a.shape[2]).transpose(1, 0, 3, 2)

    qc = chunk4(q) * (dk ** -0.5)
    kc = chunk4(k)
    vc = chunk4(v)
    gc = jnp.cumsum(chunk3(g), axis=-1)
    bc = chunk3(beta)
    kb = kc * bc[..., None]
    vb = vc * bc[..., None]
    pos = jnp.arange(c)
    causal = pos[:, None] >= pos[None, :]
    strict = pos[:, None] > pos[None, :]
    decay = jnp.exp(jnp.where(causal, gc[..., :, None] - gc[..., None, :], -jnp.inf))
    a_mat = jnp.where(strict, jnp.einsum('nbhid,nbhjd->nbhij', kb, kc) * decay, 0.0)
    unit_lower = a_mat + jnp.eye(c, dtype=f32)
    u = lax.linalg.triangular_solve(unit_lower, vb, left_side=True, lower=True, unit_diagonal=True)
    w = lax.linalg.triangular_solve(unit_lower, kb * jnp.exp(gc)[..., None], left_side=True, lower=True, unit_diagonal=True)
    qk = jnp.einsum('nbhid,nbhjd->nbhij', qc, kc) * decay

    def step(s, xs):
        q_i, k_i, u_i, w_i, g_i, qk_i = xs
        v_new = u_i - jnp.einsum('bhcd,bhde->bhce', w_i, s)
        o = jnp.einsum('bhcd,bhde->bhce', q_i * jnp.exp(g_i)[..., None], s) + jnp.einsum('bhij,bhje->bhie', qk_i, v_new)
        g_last = g_i[..., -1]
        s = s * jnp.exp(g_last)[..., None, None] + jnp.einsum('bhcd,bhce->bhde', k_i * jnp.exp(g_last[..., None] - g_i)[..., None], v_new)
        return s, o

    s_fin, o = lax.scan(step, s0.astype(f32), (qc, kc, u, w, gc, qk))
    o = o.transpose(1, 0, 3, 2, 4).reshape(bsz, n * c, h, dv)[:, :t]
    return o, s_fin


def gdn_branch(qkv, z, b_logit, a_logit, conv_buf, s0, conv_w, a_log, dt_bias, norm_w):
    bsz, t, _ = qkv.shape
    xc = jnp.concatenate([conv_buf.astype(qkv.dtype), qkv], axis=1)
    conv = xc[:, 0:t] * conv_w[0]
    for j in range(1, CONV_W):
        conv = conv + xc[:, j:j + t] * conv_w[j]
    act = jax.nn.silu(conv)
    qd = GDN_QK_HEADS * GDN_DK
    rep = GDN_V_HEADS // GDN_QK_HEADS
    q = jnp.repeat(l2_normalize(act[..., :qd].reshape(bsz, t, GDN_QK_HEADS, GDN_DK)), rep, axis=2)
    k = jnp.repeat(l2_normalize(act[..., qd:2 * qd].reshape(bsz, t, GDN_QK_HEADS, GDN_DK)), rep, axis=2)
    v = act[..., 2 * qd:].reshape(bsz, t, GDN_V_HEADS, GDN_DV)
    beta = jax.nn.sigmoid(b_logit.astype(jnp.float32))
    g = -jnp.exp(a_log.astype(jnp.float32)) * jax.nn.softplus(a_logit.astype(jnp.float32) + dt_bias.astype(jnp.float32))
    o, s_new = gated_delta_chunked(q, k, v, g, beta, s0)
    o = rms_norm(o, norm_w) * jax.nn.silu(z.reshape(bsz, t, GDN_V_HEADS, GDN_DV).astype(jnp.float32))
    return o.reshape(bsz, t, GDN_WIDTH).astype(qkv.dtype), xc[:, t:], s_new.astype(qkv.dtype)


def moe_ffn(x, w_router, b_router, w_gu, b_gu, w_dn, b_dn):
    t, d = x.shape
    logits = x.astype(jnp.float32) @ w_router.astype(jnp.float32) + b_router.astype(jnp.float32)
    top_val, top_idx = lax.top_k(logits, TOP_K)
    gates = jax.nn.softmax(top_val, axis=-1)
    tk = t * TOP_K
    flat_e = top_idx.reshape(-1).astype(jnp.int32)
    flat_tok = jnp.repeat(jnp.arange(t, dtype=jnp.int32), TOP_K)
    flat_g = gates.reshape(-1)
    order = jnp.argsort(flat_e)
    e_sorted = flat_e[order]
    counts = jnp.bincount(flat_e, length=N_EXPERTS).astype(jnp.int32)
    padded = (counts + MOE_BLOCK - 1) // MOE_BLOCK * MOE_BLOCK
    pad_end = jnp.cumsum(padded)
    pad_start = pad_end - padded
    start = jnp.cumsum(counts) - counts
    dest = pad_start[e_sorted] + jnp.arange(tk, dtype=jnp.int32) - start[e_sorted]
    n_blocks = -(-(tk + N_EXPERTS * (MOE_BLOCK - 1)) // MOE_BLOCK)
    n_rows = n_blocks * MOE_BLOCK
    row_tok = jnp.zeros((n_rows,), jnp.int32).at[dest].set(flat_tok[order])
    row_gate = jnp.zeros((n_rows,), jnp.float32).at[dest].set(flat_g[order])
    blk_e = jnp.minimum(jnp.searchsorted(pad_end, jnp.arange(n_blocks, dtype=jnp.int32) * MOE_BLOCK, side='right'), N_EXPERTS - 1)
    xb = x[row_tok].reshape(n_blocks, MOE_BLOCK, d)

    def expert_block(args):
        xs, e = args
        gu = xs @ w_gu[e] + b_gu[e]
        gate = jnp.minimum(gu[:, :D_FF], SWIGLU_LIMIT)
        up = jnp.clip(gu[:, D_FF:], -SWIGLU_LIMIT, SWIGLU_LIMIT)
        hdn = (up + 1.0) * gate * jax.nn.sigmoid(SWIGLU_ALPHA * gate)
        return hdn @ w_dn[e] + b_dn[e]

    yb = lax.map(expert_block, (xb, blk_e))
    y = jnp.zeros((t, d), jnp.float32).at[row_tok].add(yb.reshape(n_rows, d).astype(jnp.float32) * row_gate[:, None])
    return y.astype(x.dtype)


def decoder_layer(x, c, pos, attend, conv_buf, s0, p):
    bsz, t, d = x.shape
    mod = jax.nn.silu(c) @ p['w_ada'] + p['b_ada']
    shift1, scale1, gate1, shift2, scale2, gate2 = jnp.split(mod, 6, axis=-1)
    h = rms_norm(x, p['g_norm1']) * (1.0 + scale1[:, None]) + shift1[:, None]
    proj = h @ p['w_in']
    aq, ak, av, iq, ik, iw, gq, gk, gv, gz, gb, ga, gl = jnp.split(proj, np.cumsum(IN_SPLITS)[:-1].tolist(), axis=-1)
    aq = partial_rotary(aq.reshape(bsz, t, N_HEADS, HEAD_DIM), pos)
    ak = partial_rotary(ak.reshape(bsz, t, N_KV_HEADS, HEAD_DIM), pos)
    av = av.reshape(bsz, t, N_KV_HEADS, HEAD_DIM)
    iq = partial_rotary(iq.reshape(bsz, t, IDX_HEADS, IDX_DIM), pos) * (IDX_DIM ** -0.5)
    ik = partial_rotary(ik[:, :, None, :], pos)[:, :, 0]
    iw = iw * (IDX_HEADS ** -0.5)
    attn_o = attend(aq, ak, av, iq, ik, iw).reshape(bsz, t, ATTN_WIDTH)
    gdn_o, conv_new, s_new = gdn_branch(jnp.concatenate([gq, gk, gv], axis=-1), gz, gb, ga, conv_buf, s0,
                                        p['gdn_conv_w'], p['gdn_a_log'], p['gdn_dt_bias'], p['gdn_norm_w'])
    g_attn, g_gdn = jnp.split(jax.nn.sigmoid(gl), 2, axis=-1)
    merged = g_attn * (attn_o @ p['w_o_attn']) + g_gdn * (gdn_o @ p['w_o_gdn'])
    x = x + gate1[:, None] * (merged @ p['w_out'])
    h2 = rms_norm(x, p['g_norm2']) * (1.0 + scale2[:, None]) + shift2[:, None]
    ffn = moe_ffn(h2.reshape(bsz * t, d), p['w_router'], p['b_router'], p['w_gu'], p['b_gu'], p['w_dn'], p['b_dn'])
    x = x + gate2[:, None] * ffn.reshape(bsz, t, d)
    return x, ak, av, ik, conv_new, s_new


def setup_inputs(seed: int = 0) -> dict:
    key = jax.random.key(seed)
    ks = jax.random.split(key, 32)
    f32 = jnp.float32
    n_pages = PAST_LEN // PAGE_SIZE
    used = DEC_BATCH * n_pages
    n_phys = max(used + 1, (used * 5) // 4)

    def nrm(k, shape, s):
        return jax.random.normal(k, shape, f32) * s

    page_table = jax.random.permutation(ks[10], n_phys)[:used].reshape(DEC_BATCH, n_pages).astype(jnp.int32)
    dt = jnp.exp(jax.random.uniform(ks[17], (DEPTH, GDN_V_HEADS), f32, math.log(1e-3), math.log(1e-1)))
    return {
        'x_prompt': nrm(ks[0], (BATCH, SEQ, D_MODEL), 1.0),
        'x_sample': nrm(ks[1], (DEC_BATCH, DEC_SEQ, D_MODEL), 1.0),
        'c_prompt': nrm(ks[2], (BATCH, D_MODEL), 1.0),
        'c_sample': nrm(ks[3], (DEC_BATCH, D_MODEL), 1.0),
        'cache_k': nrm(ks[4], (DEPTH, n_phys, PAGE_SIZE, N_KV_HEADS, HEAD_DIM), 1.0),
        'cache_v': nrm(ks[5], (DEPTH, n_phys, PAGE_SIZE, N_KV_HEADS, HEAD_DIM), 1.0),
        'cache_kidx': nrm(ks[6], (DEPTH, n_phys, PAGE_SIZE, IDX_DIM), 1.0),
        'state_gdn': nrm(ks[7], (DEPTH, DEC_BATCH, GDN_V_HEADS, GDN_DK, GDN_DV), 0.1),
        'state_conv': nrm(ks[8], (DEPTH, DEC_BATCH, CONV_W - 1, CONV_DIM), 1.0),
        'page_table': page_table,
        'w_ada': nrm(ks[9], (DEPTH, D_MODEL, 6 * D_MODEL), 0.5 * D_MODEL ** -0.5),
        'b_ada': nrm(ks[11], (DEPTH, 6 * D_MODEL), 0.02),
        'g_norm1': 1.0 + nrm(ks[12], (DEPTH, D_MODEL), 0.05),
        'g_norm2': 1.0 + nrm(ks[13], (DEPTH, D_MODEL), 0.05),
        'g_final': 1.0 + nrm(ks[14], (D_MODEL,), 0.05),
        'w_in': nrm(ks[15], (DEPTH, D_MODEL, IN_TOTAL), D_MODEL ** -0.5),
        'gdn_conv_w': nrm(ks[16], (DEPTH, CONV_W, CONV_DIM), 0.5),
        'gdn_a_log': jnp.log(jax.random.uniform(ks[18], (DEPTH, GDN_V_HEADS), f32, 1.0, 16.0)),
        'gdn_dt_bias': dt + jnp.log(-jnp.expm1(-dt)),
        'gdn_norm_w': 1.0 + nrm(ks[19], (DEPTH, GDN_DV), 0.05),
        'w_o_attn': nrm(ks[20], (DEPTH, ATTN_WIDTH, D_MODEL), ATTN_WIDTH ** -0.5),
        'w_o_gdn': nrm(ks[21], (DEPTH, GDN_WIDTH, D_MODEL), GDN_WIDTH ** -0.5),
        'w_out': nrm(ks[22], (DEPTH, D_MODEL, D_MODEL), D_MODEL ** -0.5),
        'w_router': nrm(ks[23], (DEPTH, D_MODEL, N_EXPERTS), D_MODEL ** -0.5),
        'b_router': nrm(ks[24], (DEPTH, N_EXPERTS), 0.01),
        'w_gu': nrm(ks[25], (DEPTH, N_EXPERTS, D_MODEL, 2 * D_FF), D_MODEL ** -0.5),
        'b_gu': nrm(ks[26], (DEPTH, N_EXPERTS, 2 * D_FF), 0.01),
        'w_dn': nrm(ks[27], (DEPTH, N_EXPERTS, D_FF, D_MODEL), D_FF ** -0.5),
        'b_dn': nrm(ks[28], (DEPTH, N_EXPERTS, D_MODEL), 0.01),
    }


def reference(x_prompt, x_sample, c_prompt, c_sample, cache_k, cache_v, cache_kidx, state_gdn, state_conv, page_table,
              w_ada, b_ada, g_norm1, g_norm2, g_final, w_in, gdn_conv_w, gdn_a_log, gdn_dt_bias, gdn_norm_w,
              w_o_attn, w_o_gdn, w_out, w_router, b_router, w_gu, b_gu, w_dn, b_dn):
    bsz, t = x_prompt.shape[:2]
    ts = x_sample.shape[1]
    pos_p = jnp.arange(t, dtype=jnp.int32)
    pos_s = PAST_LEN + jnp.arange(ts, dtype=jnp.int32)
    xp = x_prompt
    xs = x_sample
    kp_l, vp_l, kip_l, cp_l, sp_l = [], [], [], [], []
    ks_l, vs_l, kis_l, cs_l, ss_l = [], [], [], [], []
    for l in range(DEPTH):
        p = {'w_ada': w_ada[l], 'b_ada': b_ada[l], 'g_norm1': g_norm1[l], 'g_norm2': g_norm2[l], 'w_in': w_in[l],
             'gdn_conv_w': gdn_conv_w[l], 'gdn_a_log': gdn_a_log[l], 'gdn_dt_bias': gdn_dt_bias[l],
             'gdn_norm_w': gdn_norm_w[l], 'w_o_attn': w_o_attn[l], 'w_o_gdn': w_o_gdn[l], 'w_out': w_out[l],
             'w_router': w_router[l], 'b_router': b_router[l], 'w_gu': w_gu[l], 'b_gu': b_gu[l],
             'w_dn': w_dn[l], 'b_dn': b_dn[l]}
        conv0 = jnp.zeros((bsz, CONV_W - 1, CONV_DIM), x_prompt.dtype)
        s0 = jnp.zeros((bsz, GDN_V_HEADS, GDN_DK, GDN_DV), jnp.float32)
        xp, k_n, v_n, ki_n, c_n, s_n = decoder_layer(xp, c_prompt, pos_p, attend_prompt, conv0, s0, p)
        kp_l.append(k_n)
        vp_l.append(v_n)
        kip_l.append(ki_n)
        cp_l.append(c_n)
        sp_l.append(s_n)
        attend_s = make_attend_sample(cache_k[l], cache_v[l], cache_kidx[l], page_table)
        xs, k_n, v_n, ki_n, c_n, s_n = decoder_layer(xs, c_sample, pos_s, attend_s, state_conv[l], state_gdn[l], p)
        ks_l.append(k_n)
        vs_l.append(v_n)
        kis_l.append(ki_n)
        cs_l.append(c_n)
        ss_l.append(s_n)
    y_prompt = rms_norm(xp, g_final)
    y_sample = rms_norm(xs, g_final)
    return (y_prompt, y_sample, jnp.stack(kp_l), jnp.stack(vp_l), jnp.stack(kip_l), jnp.stack(cp_l), jnp.stack(sp_l),
            jnp.stack(ks_l), jnp.stack(vs_l), jnp.stack(kis_l), jnp.stack(cs_l), jnp.stack(ss_l))
```

```python
import functools

import jax
import jax.numpy as jnp
from jax import lax
from jax.experimental import pallas as pl
from jax.experimental.pallas import tpu as pltpu

F32 = jnp.float32
BF16 = jnp.bfloat16
I32 = jnp.int32

D_MODEL = 2048
N_HEADS = 16
N_KV_HEADS = 4
HEAD_DIM = 128
ROPE_THETA = 500000.0
ROT_FRACTION_DIV = 4
IDX_HEADS = 16
IDX_DIM = 64
TOPK_MAX = 256
GDN_QK_HEADS = 8
GDN_V_HEADS = 16
GDN_DK = 128
GDN_DV = 128
CONV_W = 4
N_EXPERTS = 32
TOP_K = 4
D_FF = D_MODEL
SWIGLU_LIMIT = 7.0
SWIGLU_ALPHA = 1.702
NORM_EPS = 1e-6
PAGE_SIZE = 128
LANE = 128

C_AQ, C_AK, C_AV, C_IQ = 0, 2048, 2560, 3072
C_GQ, C_GK, C_GV, C_GZ, C_GL = 4096, 5120, 6144, 8192, 10240
C_MISC = 14336
L_IW, L_GB, L_GA = 64, 80, 96
PROJ_TN = 512
PROJ_W = 14848
MISC_BLK = C_MISC // LANE

VMEM_LIMIT = 48 * 1024 * 1024
NEG_BIG = -1e30
INT_MIN = -2147483648

NT_DIMS = (((1,), (1,)), ((), ()))


def _cparams(sem):
    return pltpu.CompilerParams(dimension_semantics=sem, vmem_limit_bytes=VMEM_LIMIT)


def _split_bf16(a):
    hi = a.astype(BF16)
    lo = (a - hi.astype(F32)).astype(BF16)
    return hi, lo


def _dot3(a, b, dims=None):
    ah, al = _split_bf16(a)
    bh, bl = _split_bf16(b)
    if dims is None:
        d = lambda x, y: jnp.dot(x, y, preferred_element_type=F32)
    else:
        d = lambda x, y: lax.dot_general(x, y, dims, preferred_element_type=F32)
    return d(ah, bh) + d(al, bh) + d(ah, bl)


def _sigmoid(x):
    return 1.0 / (1.0 + jnp.exp(-x))


def _ada_kernel(c_ref, w_ref, b_ref, o_ref):
    c = c_ref[...]
    a = (c * _sigmoid(c)).astype(BF16)
    o_ref[...] = jnp.dot(a, w_ref[...].astype(BF16), preferred_element_type=F32) + b_ref[...]


def _ada(c, w, b):
    m, k = c.shape
    n = w.shape[1]
    tn = 1024
    return pl.pallas_call(
        _ada_kernel,
        grid=(n // tn,),
        in_specs=[pl.BlockSpec((m, k), lambda j: (0, 0)),
                  pl.BlockSpec((k, tn), lambda j: (0, j)),
                  pl.BlockSpec((1, tn), lambda j: (0, j))],
        out_specs=pl.BlockSpec((m, tn), lambda j: (0, j)),
        out_shape=jax.ShapeDtypeStruct((m, n), F32),
        compiler_params=_cparams(("arbitrary",)),
        name="ada",
    )(c, w, b.reshape(1, n))


def _rot_slab(x, tabs, half):
    a, b, c = tabs
    return x * a + pltpu.roll(x, LANE - half, 1) * b + pltpu.roll(x, half, 1) * c


def _inproj_kernel(x_ref, g_ref, sc_ref, sh_ref, w_ref, tab_ref, o32_ref, o16_ref, h_scr):
    j = pl.program_id(1)

    @pl.when(j == 0)
    def _():
        x = x_ref[...]
        ms = jnp.mean(x * x, axis=-1, keepdims=True)
        y = x * lax.rsqrt(ms + NORM_EPS) * g_ref[...]
        h_scr[...] = (y * (1.0 + sc_ref[...]) + sh_ref[...]).astype(BF16)

    acc = jnp.dot(h_scr[...], w_ref[...], preferred_element_type=F32)
    n_slab = PROJ_TN // LANE

    def store(fn):
        for s in range(n_slab):
            v = fn(acc[:, s * LANE:(s + 1) * LANE])
            o32_ref[:, s * LANE:(s + 1) * LANE] = v
            o16_ref[:, s * LANE:(s + 1) * LANE] = v.astype(BF16)

    rot_head = j < 5
    rot_idx = ((j >= 6) & (j < 8)) | (j == C_MISC // PROJ_TN)

    @pl.when(rot_head)
    def _():
        tabs = (tab_ref[0], tab_ref[1], tab_ref[2])
        store(lambda v: _rot_slab(v, tabs, HEAD_DIM // ROT_FRACTION_DIV // 2))

    @pl.when(rot_idx)
    def _():
        tabs = (tab_ref[0], tab_ref[1], tab_ref[2])
        store(lambda v: _rot_slab(v, tabs, IDX_DIM // ROT_FRACTION_DIV // 2))

    @pl.when(jnp.logical_not(rot_head | rot_idx))
    def _():
        store(lambda v: v)


def _tab_type(j):
    return jnp.where(j < 6, 0, jnp.where(j < 28, 1, 2))


def _inproj(x, g_norm, scale, shift, w_r, tabs, rows_per_mod, tm):
    m = x.shape[0]
    tt = tabs.shape[2]
    n_t = tt // tm
    mod_rows = scale.shape[1]
    tiles_per_mod = rows_per_mod // tm
    if mod_rows == 1:
        mod_spec = pl.BlockSpec((None, 1, D_MODEL), lambda i, j: (i // tiles_per_mod, 0, 0))
    else:
        mod_spec = pl.BlockSpec((None, tm, D_MODEL), lambda i, j: (i // tiles_per_mod, i % tiles_per_mod, 0))
    return pl.pallas_call(
        _inproj_kernel,
        grid=(m // tm, PROJ_W // PROJ_TN),
        in_specs=[pl.BlockSpec((tm, D_MODEL), lambda i, j: (i, 0)),
                  pl.BlockSpec((1, D_MODEL), lambda i, j: (0, 0)),
                  mod_spec, mod_spec,
                  pl.BlockSpec((D_MODEL, PROJ_TN), lambda i, j: (0, j)),
                  pl.BlockSpec((None, 3, tm, LANE), lambda i, j: (_tab_type(j), 0, i % n_t, 0))],
        out_specs=[pl.BlockSpec((tm, PROJ_TN), lambda i, j: (i, j)),
                   pl.BlockSpec((tm, PROJ_TN), lambda i, j: (i, j))],
        out_shape=[jax.ShapeDtypeStruct((m, PROJ_W), F32), jax.ShapeDtypeStruct((m, PROJ_W), BF16)],
        scratch_shapes=[pltpu.VMEM((tm, D_MODEL), BF16)],
        compiler_params=_cparams(("parallel", "arbitrary")),
        name="inproj",
    )(x, g_norm.reshape(1, D_MODEL), scale, shift, w_r, tabs)


def _rot_tables(pos):
    tt = pos.shape[0]
    posf = pos.astype(F32)

    def tab(rot, period, scale):
        half = rot // 2
        inv = ROPE_THETA ** (-jnp.arange(half, dtype=F32) * (2.0 / rot))
        ang = posf[:, None] * inv[None, :]
        cos, sin = jnp.cos(ang), jnp.sin(ang)
        zh = jnp.zeros((tt, half), F32)
        rest = period - rot
        a = jnp.concatenate([cos, cos, jnp.ones((tt, rest), F32)], axis=1) * scale
        b = jnp.concatenate([-sin, zh, jnp.zeros((tt, rest), F32)], axis=1) * scale
        c = jnp.concatenate([zh, sin, jnp.zeros((tt, rest), F32)], axis=1) * scale
        return a, b, c

    head = jnp.stack([jnp.tile(t, (1, LANE // HEAD_DIM)) for t in tab(HEAD_DIM // ROT_FRACTION_DIV, HEAD_DIM, 1.0)])
    idxq = jnp.stack([jnp.tile(t, (1, LANE // IDX_DIM))
                      for t in tab(IDX_DIM // ROT_FRACTION_DIV, IDX_DIM, IDX_DIM ** -0.5)])
    ka, kb, kc = tab(IDX_DIM // ROT_FRACTION_DIV, IDX_DIM, 1.0)
    rest = LANE - IDX_DIM
    lane = jnp.arange(rest)
    tail_a = jnp.where(lane < (L_GB - L_IW), IDX_HEADS ** -0.5, 1.0).astype(F32)
    ma = jnp.concatenate([ka, jnp.broadcast_to(tail_a, (tt, rest))], axis=1)
    mb = jnp.concatenate([kb, jnp.zeros((tt, rest), F32)], axis=1)
    mc = jnp.concatenate([kc, jnp.zeros((tt, rest), F32)], axis=1)
    misc = jnp.stack([ma, mb, mc])
    return jnp.stack([head, idxq, misc])


def _reorder_w_in(w_in):
    offs = {}
    o = 0
    for name, wdt in (("aq", 2048), ("ak", 512), ("av", 512), ("iq", 1024), ("ik", 64), ("iw", 16), ("gq", 1024),
                      ("gk", 1024), ("gv", 2048), ("gz", 2048), ("gb", 16), ("ga", 16), ("gl", 4096)):
        offs[name] = (o, wdt)
        o += wdt

    def seg(name):
        s, wdt = offs[name]
        return w_in[:, s:s + wdt]

    k = w_in.shape[0]
    misc = jnp.concatenate([seg("ik"), seg("iw"), seg("gb"), seg("ga"), jnp.zeros((k, LANE - 112), w_in.dtype)], axis=1)
    parts = [seg("aq"), seg("ak"), seg("av"), seg("iq"), seg("gq"), seg("gk"), seg("gv"), seg("gz"), seg("gl"), misc,
             jnp.zeros((k, PROJ_W - C_MISC - LANE), w_in.dtype)]
    return jnp.concatenate(parts, axis=1).astype(BF16)


def _order_key(score):
    bits = pltpu.bitcast(score, I32)
    return bits ^ ((bits >> 31) & 0x7FFFFFFF)


def _kth_largest_key(count_ge, n_rows, n_sel):
    def body(i, t):
        bit = lax.shift_left(jnp.int32(1), 31 - i)
        cand = t ^ bit
        return jnp.where(count_ge(cand) >= n_sel, cand, t)

    return lax.fori_loop(0, 32, body, jnp.full((n_rows, 1), INT_MIN, I32))


def _attn_prompt_kernel(q_ref, iq_ref, mq_ref, k_ref, v_ref, mk_ref, o_ref, key_scr, *, tq, kc, n_sel):
    qb = pl.program_id(1)
    q0 = qb * tq
    nkc = (q0 + tq + kc - 1) // kc
    qpos = q0 + lax.broadcasted_iota(I32, (tq, 1), 0)
    kiota = lax.broadcasted_iota(I32, (1, kc), 1)
    wq = mq_ref[...]

    def score_body(c, carry):
        ks = pl.multiple_of(c * kc, kc)
        kt = mk_ref[pl.ds(ks, kc), :][:, :IDX_DIM]
        acc = jnp.zeros((tq, kc), F32)
        for h in range(IDX_HEADS):
            s = lax.dot_general(iq_ref[:, h * IDX_DIM:(h + 1) * IDX_DIM], kt, NT_DIMS, preferred_element_type=F32)
            acc = acc + wq[:, L_IW + h:L_IW + h + 1] * jnp.maximum(s, 0.0)
        acc = jnp.where(ks + kiota <= qpos, acc, -jnp.inf)
        key_scr[c] = _order_key(acc)
        return carry

    lax.fori_loop(0, nkc, score_body, 0)

    def count_ge(cand):
        def body(c, acc):
            m = jnp.where(key_scr[c] >= cand, 1.0, 0.0)
            for s in range(kc // LANE):
                acc = acc + m[:, s * LANE:(s + 1) * LANE]
            return acc

        acc = lax.fori_loop(0, nkc, body, jnp.zeros((tq, LANE), F32))
        return jnp.sum(acc, axis=1, keepdims=True)

    thr = _kth_largest_key(count_ge, tq, float(n_sel))

    group = N_HEADS // N_KV_HEADS
    scale = HEAD_DIM ** -0.5
    for n in range(N_KV_HEADS):
        qs = jnp.concatenate(
            [q_ref[:, (n * group + g) * HEAD_DIM:(n * group + g + 1) * HEAD_DIM] for g in range(group)], axis=0)
        qs = (qs.astype(F32) * scale).astype(BF16)

        def attn_body(c, carry, n=n, qs=qs):
            m_i, l_i, acc = carry
            ks = pl.multiple_of(c * kc, kc)
            kn = k_ref[pl.ds(ks, kc), n * HEAD_DIM:(n + 1) * HEAD_DIM]
            vn = v_ref[pl.ds(ks, kc), n * HEAD_DIM:(n + 1) * HEAD_DIM]
            s = lax.dot_general(qs, kn, NT_DIMS, preferred_element_type=F32)
            sel = (key_scr[c] >= thr) & (ks + kiota <= qpos)
            s = jnp.where(sel[None], s.reshape(group, tq, kc), NEG_BIG).reshape(group * tq, kc)
            m_new = jnp.maximum(m_i, jnp.max(s, axis=1, keepdims=True))
            alpha = jnp.exp(m_i - m_new)
            p = jnp.exp(s - m_new)
            l_new = alpha * l_i + jnp.sum(p, axis=1, keepdims=True)
            acc = alpha * acc + jnp.dot(p.astype(BF16), vn, preferred_element_type=F32)
            return m_new, l_new, acc

        init = (jnp.full((group * tq, 1), NEG_BIG, F32), jnp.zeros((group * tq, 1), F32),
                jnp.zeros((group * tq, HEAD_DIM), F32))
        _, l_f, acc_f = lax.fori_loop(0, nkc, attn_body, init)
        out = acc_f / l_f
        for g in range(group):
            o_ref[:, (n * group + g) * HEAD_DIM:(n * group + g + 1) * HEAD_DIM] = out[g * tq:(g + 1) * tq].astype(BF16)


def _attn_prompt(p32, p16, bsz, t, tq=128, kc=512):
    nq = t // tq
    kc = min(kc, t)
    n_sel = min(TOPK_MAX, t // 4)
    kern = functools.partial(_attn_prompt_kernel, tq=tq, kc=kc, n_sel=n_sel)
    return pl.pallas_call(
        kern,
        grid=(bsz, nq),
        in_specs=[pl.BlockSpec((tq, N_HEADS * HEAD_DIM), lambda b, i: (b * nq + i, 0)),
                  pl.BlockSpec((tq, IDX_HEADS * IDX_DIM), lambda b, i: (b * nq + i, C_IQ // (IDX_HEADS * IDX_DIM))),
                  pl.BlockSpec((tq, LANE), lambda b, i: (b * nq + i, MISC_BLK)),
                  pl.BlockSpec((t, N_KV_HEADS * HEAD_DIM), lambda b, i: (b, C_AK // (N_KV_HEADS * HEAD_DIM))),
                  pl.BlockSpec((t, N_KV_HEADS * HEAD_DIM), lambda b, i: (b, C_AV // (N_KV_HEADS * HEAD_DIM))),
                  pl.BlockSpec((t, LANE), lambda b, i: (b, MISC_BLK))],
        out_specs=pl.BlockSpec((tq, N_HEADS * HEAD_DIM), lambda b, i: (b * nq + i, 0)),
        out_shape=jax.ShapeDtypeStruct((bsz * t, N_HEADS * HEAD_DIM), BF16),
        scratch_shapes=[pltpu.VMEM((t // kc, tq, kc), I32)],
        compiler_params=_cparams(("parallel", "arbitrary")),
        name="attn_prompt",
    )(p16, p16, p32, p16, p16, p16)


def _attn_sample_kernel(pt_ref, iq_ref, wc_ref, q_ref, kn_ref, vn_ref, ikn_ref, *rest, n_pages, n_sel, past):
    kidx_refs = rest[:n_pages]
    k_refs = rest[n_pages:2 * n_pages]
    v_refs = rest[2 * n_pages:3 * n_pages]
    o_ref, sc_scr, k_scr, v_scr = rest[3 * n_pages:]
    del pt_ref
    rows = 8
    n_keys = (n_pages + 1) * PAGE_SIZE
    iq = iq_ref[...].astype(BF16)
    wc = wc_ref[...]

    def idx_score(kt):
        s = lax.dot_general(iq, kt.astype(BF16), NT_DIMS, preferred_element_type=F32)
        x = jnp.maximum(s, 0.0) * wc
        return jnp.sum(x.reshape(IDX_HEADS, rows, PAGE_SIZE), axis=0)

    for p in range(n_pages + 1):
        sl = slice(p * PAGE_SIZE, (p + 1) * PAGE_SIZE)
        if p < n_pages:
            sc_scr[:, sl] = idx_score(kidx_refs[p][...])
            k_scr[sl, :] = k_refs[p][...].astype(BF16)
            v_scr[sl, :] = v_refs[p][...].astype(BF16)
        else:
            sc_scr[:, sl] = idx_score(ikn_ref[...])
            k_scr[sl, :] = kn_ref[...].astype(BF16)
            v_scr[sl, :] = vn_ref[...].astype(BF16)

    qpos = past + lax.broadcasted_iota(I32, (rows, 1), 0)
    kpos = lax.broadcasted_iota(I32, (1, n_keys), 1)
    causal = kpos <= qpos
    keys = _order_key(jnp.where(causal, sc_scr[...], -jnp.inf))

    def count_ge(cand):
        return jnp.sum(jnp.where(keys >= cand, 1.0, 0.0), axis=1, keepdims=True)

    thr = _kth_largest_key(count_ge, rows, float(n_sel))
    sel = ((keys >= thr) & causal)[None]
    group = N_HEADS // N_KV_HEADS
    scale = HEAD_DIM ** -0.5
    gr = group * rows
    for n in range(N_KV_HEADS):
        qn = (q_ref[n * gr:(n + 1) * gr, :] * scale).astype(BF16)
        s = lax.dot_general(qn, k_scr[:, n * HEAD_DIM:(n + 1) * HEAD_DIM], NT_DIMS, preferred_element_type=F32)
        s = jnp.where(sel, s.reshape(group, rows, n_keys), NEG_BIG).reshape(gr, n_keys)
        m = jnp.max(s, axis=1, keepdims=True)
        p = jnp.exp(s - m)
        l = jnp.sum(p, axis=1, keepdims=True)
        o = jnp.dot(p.astype(BF16), v_scr[:, n * HEAD_DIM:(n + 1) * HEAD_DIM], preferred_element_type=F32)
        o_ref[n * gr:(n + 1) * gr, :] = o / l


def _attn_sample(p32, cache_k, cache_v, cache_kidx, page_table, ts):
    db, n_pages = page_table.shape
    past = n_pages * PAGE_SIZE
    rows = 8
    group = N_HEADS // N_KV_HEADS
    n_phys = cache_k.shape[0]
    ck = cache_k.reshape(n_phys, PAGE_SIZE, N_KV_HEADS * HEAD_DIM)
    cv = cache_v.reshape(n_phys, PAGE_SIZE, N_KV_HEADS * HEAD_DIM)
    ps = p32.reshape(db, ts, PROJ_W)
    pad_t = ((0, 0), (0, 0), (0, rows - ts), (0, 0))
    iq = ps[:, :, C_IQ:C_IQ + IDX_HEADS * IDX_DIM].reshape(db, ts, IDX_HEADS, IDX_DIM).transpose(0, 2, 1, 3)
    iq = jnp.pad(iq, pad_t).reshape(db, IDX_HEADS * rows, IDX_DIM)
    wc = ps[:, :, C_MISC + L_IW:C_MISC + L_IW + IDX_HEADS].transpose(0, 2, 1)
    wc = jnp.pad(wc, ((0, 0), (0, 0), (0, rows - ts))).reshape(db, IDX_HEADS * rows, 1)
    q = ps[:, :, :N_HEADS * HEAD_DIM].reshape(db, ts, N_HEADS, HEAD_DIM).transpose(0, 2, 1, 3)
    q = jnp.pad(q, pad_t).reshape(db, N_HEADS * rows, HEAD_DIM)
    pad_k = ((0, 0), (0, PAGE_SIZE - ts), (0, 0))
    kn = jnp.pad(ps[:, :, C_AK:C_AK + N_KV_HEADS * HEAD_DIM], pad_k)
    vn = jnp.pad(ps[:, :, C_AV:C_AV + N_KV_HEADS * HEAD_DIM], pad_k)
    ikn = jnp.pad(ps[:, :, C_MISC:C_MISC + IDX_DIM], pad_k)
    n_sel = min(TOPK_MAX, (past + ts) // 4)
    kw = N_KV_HEADS * HEAD_DIM

    def page_spec(width, p):
        return pl.BlockSpec((None, PAGE_SIZE, width), lambda d, pt, p=p: (pt[d * n_pages + p], 0, 0))

    def per_d(r, w):
        return pl.BlockSpec((None, r, w), lambda d, pt: (d, 0, 0))

    in_specs = ([per_d(IDX_HEADS * rows, IDX_DIM), per_d(IDX_HEADS * rows, 1), per_d(N_HEADS * rows, HEAD_DIM),
                 per_d(PAGE_SIZE, kw), per_d(PAGE_SIZE, kw), per_d(PAGE_SIZE, IDX_DIM)]
                + [page_spec(IDX_DIM, p) for p in range(n_pages)]
                + [page_spec(kw, p) for p in range(n_pages)]
                + [page_spec(kw, p) for p in range(n_pages)])
    n_keys = (n_pages + 1) * PAGE_SIZE
    grid_spec = pltpu.PrefetchScalarGridSpec(
        num_scalar_prefetch=1, grid=(db,), in_specs=in_specs,
        out_specs=per_d(N_HEADS * rows, HEAD_DIM),
        scratch_shapes=[pltpu.VMEM((rows, n_keys), F32), pltpu.VMEM((n_keys, kw), BF16), pltpu.VMEM((n_keys, kw), BF16)])
    kern = functools.partial(_attn_sample_kernel, n_pages=n_pages, n_sel=n_sel, past=past)
    o = pl.pallas_call(
        kern, grid_spec=grid_spec,
        out_shape=jax.ShapeDtypeStruct((db, N_HEADS * rows, HEAD_DIM), F32),
        compiler_params=_cparams(("arbitrary",)),
        name="attn_sample",
    )(page_table.reshape(-1), iq, wc, q, kn, vn, ikn,
      *([cache_kidx] * n_pages), *([ck] * n_pages), *([cv] * n_pages))
    o = o.reshape(db, N_HEADS, rows, HEAD_DIM)[:, :, :ts].transpose(0, 2, 1, 3)
    return o.reshape(db * ts, N_HEADS * HEAD_DIM).astype(BF16)


def _gdn_kernel(q_ref, k_ref, v_ref, z_ref, m_ref, wq_ref, wk_ref, wv_ref, cq_ref, ck_ref, cv_ref, hp_ref, nw_ref,
                s0_ref, o_ref, sf_ref, s_scr, hist_scr, xc_scr, *, c, t_valid):
    g_idx = pl.program_id(1)
    tb = pl.program_id(2)
    n_tb = pl.num_programs(2)
    dk, dv = GDN_DK, GDN_DV

    @pl.when(tb == 0)
    def _():
        s_scr[...] = s0_ref[...]
        hist_scr[:, 0:dk] = cq_ref[...]
        hist_scr[:, dk:2 * dk] = ck_ref[...]
        hist_scr[:, 2 * dk:] = cv_ref[...]

    def conv_silu(x, w_ref, lo, width):
        xc_scr[0:8, 0:width] = hist_scr[:, lo:lo + width]
        xc_scr[8:8 + c, 0:width] = x
        acc = jnp.zeros((c, width), F32)
        for j in range(CONV_W):
            acc = acc + xc_scr[8 - (CONV_W - 1) + j:8 - (CONV_W - 1) + j + c, 0:width] * w_ref[j:j + 1, :]
        hist_scr[:, lo:lo + width] = x[c - 8:c, :]
        return acc * _sigmoid(acc)

    def l2n(x):
        return x * lax.rsqrt(jnp.sum(x * x, axis=-1, keepdims=True) + NORM_EPS)

    q = l2n(conv_silu(q_ref[...], wq_ref, 0, dk))
    k = l2n(conv_silu(k_ref[...], wk_ref, dk, dk))
    v2 = conv_silu(v_ref[...], wv_ref, 2 * dk, 2 * dv)
    qs = q * (dk ** -0.5)

    misc = m_ref[...]
    hp = hp_ref[...]
    x = misc + hp[1:2, :]
    softplus = jnp.maximum(x, 0.0) + jnp.log(1.0 + jnp.exp(-jnp.abs(x)))
    live = tb * c + lax.broadcasted_iota(I32, (c, 1), 0) < t_valid
    g_all = jnp.where(live, -jnp.exp(hp[0:1, :]) * softplus, 0.0)
    beta_all = jnp.where(live, _sigmoid(misc), 0.0)
    lane = lax.broadcasted_iota(I32, (1, LANE), 1)
    ri = lax.broadcasted_iota(I32, (c, c), 0)
    ci = lax.broadcasted_iota(I32, (c, c), 1)
    causal = ri >= ci
    strict = ri > ci
    tri = jnp.where(causal, 1.0, 0.0).astype(BF16)
    eye = jnp.where(ri == ci, 1.0, 0.0)

    def same_blk(log_b):
        return (ri >> log_b) == (ci >> log_b)

    for j in range(2):
        h = 2 * g_idx + j
        g_col = jnp.sum(jnp.where(lane == L_GA + h, g_all, 0.0), axis=1, keepdims=True)
        beta = jnp.sum(jnp.where(lane == L_GB + h, beta_all, 0.0), axis=1, keepdims=True)
        gb = jnp.broadcast_to(g_col, (c, LANE))
        g_hi, g_lo = _split_bf16(gb)
        g_lo2 = (gb - g_hi.astype(F32) - g_lo.astype(F32)).astype(BF16)
        gc = (jnp.dot(tri, g_hi, preferred_element_type=F32) + jnp.dot(tri, g_lo, preferred_element_type=F32)
              + jnp.dot(tri, g_lo2, preferred_element_type=F32))
        gct = gc.T
        decay = jnp.exp(jnp.where(causal, gc[:, :c] - gct[:c, :], -jnp.inf))
        kb = k * beta
        a_mat = jnp.where(strict, _dot3(kb, k, NT_DIMS) * decay, 0.0)
        nm = jnp.where(same_blk(3), -a_mat, 0.0)
        pm = eye + nm
        for _ in range(2):
            nm = _dot3(nm, nm)
            pm = pm + _dot3(pm, nm)
        for lb in range(3, (c - 1).bit_length()):
            off = jnp.where(same_blk(lb + 1) & jnp.logical_not(same_blk(lb)), a_mat, 0.0)
            pm = pm - _dot3(pm, _dot3(off, pm))
        v = v2[:, j * dv:(j + 1) * dv]
        u = _dot3(pm, v * beta)
        w = _dot3(pm, kb * jnp.exp(gc))
        qk = jnp.where(causal, _dot3(qs, k, NT_DIMS) * decay, 0.0)
        s_h = s_scr[j]
        v_new = u - _dot3(w, s_h)
        o = _dot3(qs * jnp.exp(gc), s_h) + _dot3(qk, v_new)
        g_last = gc[c - 1:c, :]
        kdec = k * jnp.exp(g_last - gc)
        s_new = s_h * jnp.exp(g_last) + _dot3(kdec.T, v_new)
        s_scr[j] = s_new
        z = z_ref[:, j * dv:(j + 1) * dv]
        on = o * lax.rsqrt(jnp.mean(o * o, axis=-1, keepdims=True) + NORM_EPS) * nw_ref[...]
        o_ref[:, j * dv:(j + 1) * dv] = (on * (z * _sigmoid(z))).astype(BF16)

    @pl.when(tb == n_tb - 1)
    def _():
        sf_ref[...] = s_scr[...]


def _gdn(p32, cols, conv0, s0, conv_w, a_log, dt_bias, norm_w, bsz, t, t_valid, c=128):
    c_gq, c_gk, c_gv, c_gz, c_misc = cols
    n_tb = t // c
    qkh = GDN_QK_HEADS
    hp = jnp.zeros((8, LANE), F32)
    hp = hp.at[0, L_GA:L_GA + GDN_V_HEADS].set(a_log.astype(F32)).at[1, L_GA:L_GA + GDN_V_HEADS].set(dt_bias.astype(F32))
    row = lambda b, g, i: b * n_tb + i
    qd = GDN_QK_HEADS * GDN_DK
    in_specs = [
        pl.BlockSpec((c, GDN_DK), lambda b, g, i: (row(b, g, i), c_gq // GDN_DK + g)),
        pl.BlockSpec((c, GDN_DK), lambda b, g, i: (row(b, g, i), c_gk // GDN_DK + g)),
        pl.BlockSpec((c, 2 * GDN_DV), lambda b, g, i: (row(b, g, i), c_gv // (2 * GDN_DV) + g)),
        pl.BlockSpec((c, 2 * GDN_DV), lambda b, g, i: (row(b, g, i), c_gz // (2 * GDN_DV) + g)),
        pl.BlockSpec((c, LANE), lambda b, g, i: (row(b, g, i), c_misc // LANE)),
        pl.BlockSpec((CONV_W, GDN_DK), lambda b, g, i: (0, g)),
        pl.BlockSpec((CONV_W, GDN_DK), lambda b, g, i: (0, qkh + g)),
        pl.BlockSpec((CONV_W, 2 * GDN_DV), lambda b, g, i: (0, qd // GDN_DV + g)),
        pl.BlockSpec((None, 8, GDN_DK), lambda b, g, i: (b, 0, g)),
        pl.BlockSpec((None, 8, GDN_DK), lambda b, g, i: (b, 0, qkh + g)),
        pl.BlockSpec((None, 8, 2 * GDN_DV), lambda b, g, i: (b, 0, qd // GDN_DV + g)),
        pl.BlockSpec((8, LANE), lambda b, g, i: (0, 0)),
        pl.BlockSpec((1, GDN_DV), lambda b, g, i: (0, 0)),
        pl.BlockSpec((None, 2, GDN_DK, GDN_DV), lambda b, g, i: (b, g, 0, 0)),
    ]
    out_specs = [pl.BlockSpec((c, 2 * GDN_DV), lambda b, g, i: (row(b, g, i), g)),
                 pl.BlockSpec((None, 2, GDN_DK, GDN_DV), lambda b, g, i: (b, g, 0, 0))]
    return pl.pallas_call(
        functools.partial(_gdn_kernel, c=c, t_valid=t_valid),
        grid=(bsz, qkh, n_tb),
        in_specs=in_specs, out_specs=out_specs,
        out_shape=[jax.ShapeDtypeStruct((bsz * t, GDN_V_HEADS * GDN_DV), BF16),
                   jax.ShapeDtypeStruct((bsz, GDN_V_HEADS, GDN_DK, GDN_DV), F32)],
        scratch_shapes=[pltpu.VMEM((2, GDN_DK, GDN_DV), F32), pltpu.VMEM((8, 2 * GDN_DK + 2 * GDN_DV), F32),
                        pltpu.VMEM((c + 8, 2 * GDN_DV), F32)],
        compiler_params=_cparams(("parallel", "parallel", "arbitrary")),
        name="gdn",
    )(p32, p32, p32, p32, p32, conv_w, conv_w, conv_w, conv0, conv0, conv0, hp, norm_w.reshape(1, GDN_DV), s0)


def _merge_kernel(a_ref, g_ref, wa_ref, wg_ref, ga_ref, gg_ref, o_ref):
    pa = jnp.dot(a_ref[...], wa_ref[...], preferred_element_type=F32)
    pg = jnp.dot(g_ref[...], wg_ref[...], preferred_element_type=F32)
    o_ref[...] = (_sigmoid(ga_ref[...]) * pa + _sigmoid(gg_ref[...]) * pg).astype(BF16)


def _merge(attn_o, gdn_o, w_oa, w_og, p32, tm):
    m = attn_o.shape[0]
    tn = 512
    return pl.pallas_call(
        _merge_kernel,
        grid=(m // tm, D_MODEL // tn),
        in_specs=[pl.BlockSpec((tm, D_MODEL), lambda i, j: (i, 0)),
                  pl.BlockSpec((tm, D_MODEL), lambda i, j: (i, 0)),
                  pl.BlockSpec((D_MODEL, tn), lambda i, j: (0, j)),
                  pl.BlockSpec((D_MODEL, tn), lambda i, j: (0, j)),
                  pl.BlockSpec((tm, tn), lambda i, j: (i, C_GL // tn + j)),
                  pl.BlockSpec((tm, tn), lambda i, j: (i, (C_GL + D_MODEL) // tn + j))],
        out_specs=pl.BlockSpec((tm, tn), lambda i, j: (i, j)),
        out_shape=jax.ShapeDtypeStruct((m, D_MODEL), BF16),
        compiler_params=_cparams(("parallel", "arbitrary")),
        name="merge",
    )(attn_o, gdn_o, w_oa, w_og, p32, p32)


def _outproj_kernel(mg_ref, w_ref, x_ref, g1_ref, sc_ref, sh_ref, gn_ref, wr_ref, br_ref,
                    x1_ref, h2_ref, ti_ref, tg_ref):
    x1 = x_ref[...] + g1_ref[...] * jnp.dot(mg_ref[...], w_ref[...], preferred_element_type=F32)
    x1_ref[...] = x1
    ms = jnp.mean(x1 * x1, axis=-1, keepdims=True)
    h2 = x1 * lax.rsqrt(ms + NORM_EPS) * gn_ref[...] * (1.0 + sc_ref[...]) + sh_ref[...]
    h2_ref[...] = h2.astype(BF16)
    logits = _dot3(h2, wr_ref[...]) + br_ref[...]
    tm = logits.shape[0]
    lane = lax.broadcasted_iota(I32, (tm, LANE), 1)
    lanef = lane.astype(F32)
    l = jnp.where(lane < N_EXPERTS, logits, -jnp.inf)
    vals, idxs = [], []
    for _ in range(TOP_K):
        mx = jnp.max(l, axis=1, keepdims=True)
        ix = jnp.min(jnp.where(l == mx, lanef, float(LANE)), axis=1, keepdims=True)
        vals.append(mx)
        idxs.append(ix)
        l = jnp.where(lanef == ix, -jnp.inf, l)
    es = [jnp.exp(v - vals[0]) for v in vals]
    den = es[0]
    for e in es[1:]:
        den = den + e
    ti = jnp.zeros((tm, LANE), F32)
    tg = jnp.zeros((tm, LANE), F32)
    for kk in range(TOP_K):
        ti = jnp.where(lane == kk, idxs[kk], ti)
        tg = jnp.where(lane == kk, es[kk] / den, tg)
    ti_ref[...] = ti.astype(I32)
    tg_ref[...] = tg


def _outproj(merged, w_out, x, gate1, scale2, shift2, g_norm2, w_router, b_router, rows_per_mod, tm):
    m = x.shape[0]
    mod_rows = gate1.shape[1]
    tiles_per_mod = rows_per_mod // tm
    if mod_rows == 1:
        mod_spec = pl.BlockSpec((None, 1, D_MODEL), lambda i: (i // tiles_per_mod, 0, 0))
    else:
        mod_spec = pl.BlockSpec((None, tm, D_MODEL), lambda i: (i // tiles_per_mod, i % tiles_per_mod, 0))
    wr = jnp.pad(w_router.astype(F32), ((0, 0), (0, LANE - N_EXPERTS)))
    br = jnp.pad(b_router.astype(F32), (0, LANE - N_EXPERTS)).reshape(1, LANE)
    row = pl.BlockSpec((tm, D_MODEL), lambda i: (i, 0))
    small = pl.BlockSpec((tm, LANE), lambda i: (i, 0))
    return pl.pallas_call(
        _outproj_kernel,
        grid=(m // tm,),
        in_specs=[row, pl.BlockSpec((D_MODEL, D_MODEL), lambda i: (0, 0)), row, mod_spec, mod_spec, mod_spec,
                  pl.BlockSpec((1, D_MODEL), lambda i: (0, 0)),
                  pl.BlockSpec((D_MODEL, LANE), lambda i: (0, 0)),
                  pl.BlockSpec((1, LANE), lambda i: (0, 0))],
        out_specs=[row, row, small, small],
        out_shape=[jax.ShapeDtypeStruct((m, D_MODEL), F32), jax.ShapeDtypeStruct((m, D_MODEL), BF16),
                   jax.ShapeDtypeStruct((m, LANE), I32), jax.ShapeDtypeStruct((m, LANE), F32)],
        compiler_params=_cparams(("parallel",)),
        name="outproj",
    )(merged, w_out, x, gate1, scale2, shift2, g_norm2.reshape(1, D_MODEL), wr, br)


def _moe_kernel(be_ref, nu_ref, x_ref, wg_ref, wu_ref, wd_ref, bg_ref, bu_ref, bd_ref, o_ref):
    i = pl.program_id(0)
    j = pl.program_id(1)
    used = i < nu_ref[0]
    del be_ref

    @pl.when(used)
    def _():
        x = x_ref[...]
        g = jnp.dot(x, wg_ref[...].astype(BF16), preferred_element_type=F32) + bg_ref[...]
        u = jnp.dot(x, wu_ref[...].astype(BF16), preferred_element_type=F32) + bu_ref[...]
        gate = jnp.minimum(g, SWIGLU_LIMIT)
        up = jnp.clip(u, -SWIGLU_LIMIT, SWIGLU_LIMIT)
        hdn = (up + 1.0) * gate * _sigmoid(SWIGLU_ALPHA * gate)
        y = jnp.dot(hdn.astype(BF16), wd_ref[...].astype(BF16), preferred_element_type=F32)

        @pl.when(j == 0)
        def _():
            o_ref[...] = y + bd_ref[...]

        @pl.when(j != 0)
        def _():
            o_ref[...] += y

    @pl.when(jnp.logical_not(used) & (j == 0))
    def _():
        o_ref[...] = jnp.zeros_like(o_ref)


def _moe_ffn(xs, blk_e, n_used, w_gu, b_gu, w_dn, b_dn, rb, tf):
    n_rows = xs.shape[0]
    n_blocks = n_rows // rb
    n_f = D_FF // tf
    bgu = b_gu.reshape(N_EXPERTS, 1, 2 * D_FF)
    bdn = b_dn.reshape(N_EXPERTS, 1, D_MODEL)

    def jj(i, j, nu):
        return jnp.where(i < nu[0], j, 0)

    grid_spec = pltpu.PrefetchScalarGridSpec(
        num_scalar_prefetch=2, grid=(n_blocks, n_f),
        in_specs=[pl.BlockSpec((rb, D_MODEL), lambda i, j, be, nu: (i, 0)),
                  pl.BlockSpec((None, D_MODEL, tf), lambda i, j, be, nu: (be[i], 0, jj(i, j, nu))),
                  pl.BlockSpec((None, D_MODEL, tf), lambda i, j, be, nu: (be[i], 0, n_f + jj(i, j, nu))),
                  pl.BlockSpec((None, tf, D_MODEL), lambda i, j, be, nu: (be[i], jj(i, j, nu), 0)),
                  pl.BlockSpec((None, 1, tf), lambda i, j, be, nu: (be[i], 0, jj(i, j, nu))),
                  pl.BlockSpec((None, 1, tf), lambda i, j, be, nu: (be[i], 0, n_f + jj(i, j, nu))),
                  pl.BlockSpec((None, 1, D_MODEL), lambda i, j, be, nu: (be[i], 0, 0))],
        out_specs=pl.BlockSpec((rb, D_MODEL), lambda i, j, be, nu: (i, 0)))
    return pl.pallas_call(
        _moe_kernel, grid_spec=grid_spec,
        out_shape=jax.ShapeDtypeStruct((n_rows, D_MODEL), F32),
        compiler_params=_cparams(("parallel", "arbitrary")),
        name="moe_ffn",
    )(blk_e, n_used, xs, w_gu, w_gu, w_dn, bgu, bgu, bdn)


def _final_kernel(x1_ref, yg_ref, tg_ref, g2_ref, gf_ref, o_ref):
    tg = tg_ref[...]
    ffn = jnp.zeros_like(x1_ref)
    for kk in range(TOP_K):
        ffn = ffn + tg[:, kk:kk + 1] * yg_ref[:, kk * D_MODEL:(kk + 1) * D_MODEL]
    x2 = x1_ref[...] + g2_ref[...] * ffn
    ms = jnp.mean(x2 * x2, axis=-1, keepdims=True)
    o_ref[...] = x2 * lax.rsqrt(ms + NORM_EPS) * gf_ref[...]


def _final(x1, yg, tg, gate2, g_final, rows_per_mod, tm):
    m = x1.shape[0]
    mod_rows = gate2.shape[1]
    tiles_per_mod = rows_per_mod // tm
    if mod_rows == 1:
        mod_spec = pl.BlockSpec((None, 1, D_MODEL), lambda i: (i // tiles_per_mod, 0, 0))
    else:
        mod_spec = pl.BlockSpec((None, tm, D_MODEL), lambda i: (i // tiles_per_mod, i % tiles_per_mod, 0))
    row = pl.BlockSpec((tm, D_MODEL), lambda i: (i, 0))
    return pl.pallas_call(
        _final_kernel,
        grid=(m // tm,),
        in_specs=[row, pl.BlockSpec((tm, TOP_K * D_MODEL), lambda i: (i, 0)),
                  pl.BlockSpec((tm, LANE), lambda i: (i, 0)), mod_spec,
                  pl.BlockSpec((1, D_MODEL), lambda i: (0, 0))],
        out_specs=row,
        out_shape=jax.ShapeDtypeStruct((m, D_MODEL), F32),
        compiler_params=_cparams(("parallel",)),
        name="final",
    )(x1, yg, tg, gate2, g_final.reshape(1, D_MODEL))


def _route(topi, rb):
    tt = topi.shape[0]
    sel = jnp.sum((topi[:, :, None] == jnp.arange(N_EXPERTS, dtype=I32)[None, None, :]).astype(I32), axis=1)
    counts = jnp.sum(sel, axis=0)
    padded = (counts + rb - 1) // rb * rb
    pad_end = jnp.cumsum(padded)
    pad_start = pad_end - padded
    rank = jnp.cumsum(sel, axis=0) - sel
    dest = jnp.take_along_axis(pad_start[None, :] + rank, topi, axis=1)
    n_blocks = -(-(tt * TOP_K + N_EXPERTS * (rb - 1)) // rb)
    n_rows = n_blocks * rb
    tok = jnp.repeat(jnp.arange(tt, dtype=I32), TOP_K)
    row_tok = jnp.zeros((n_rows,), I32).at[dest.reshape(-1)].set(tok)
    blk_e = jnp.minimum(jnp.searchsorted(pad_end, jnp.arange(n_blocks, dtype=I32) * rb, side="right"),
                        N_EXPERTS - 1).astype(I32)
    n_used = (pad_end[-1] // rb).astype(I32).reshape(1)
    return dest, row_tok, blk_e, n_used


def kernel(x_prompt, x_sample, c_prompt, c_sample, cache_k, cache_v, cache_kidx, state_gdn, state_conv, page_table,
           w_ada, b_ada, g_norm1, g_norm2, g_final, w_in, gdn_conv_w, gdn_a_log, gdn_dt_bias, gdn_norm_w,
           w_o_attn, w_o_gdn, w_out, w_router, b_router, w_gu, b_gu, w_dn, b_dn):
    bsz, t, d = x_prompt.shape
    db, ts, _ = x_sample.shape
    depth = w_ada.shape[0]
    assert depth == 1 and d == D_MODEL
    past = page_table.shape[1] * PAGE_SIZE
    conv_dim = gdn_conv_w.shape[-1]

    n_c = bsz + db
    n_cp = -(-n_c // 8) * 8
    c_all = jnp.pad(jnp.concatenate([c_prompt, c_sample], axis=0), ((0, n_cp - n_c), (0, 0)))
    mod = _ada(c_all, w_ada[0], b_ada[0])
    mod_p = mod[:bsz].reshape(bsz, 1, 6, d)
    mod_s = jnp.repeat(mod[bsz:n_c], ts, axis=0).reshape(1, db * ts, 6, d)
    mods = {"p": [mod_p[:, :, i] for i in range(6)], "s": [mod_s[:, :, i] for i in range(6)]}

    w_r = _reorder_w_in(w_in[0])
    w_oa = w_o_attn[0].astype(BF16)
    w_og = w_o_gdn[0].astype(BF16)
    w_o = w_out[0].astype(BF16)

    xp = x_prompt.reshape(bsz * t, d)
    xs = x_sample.reshape(db * ts, d)
    tm_p = min(512, t)
    tm_s = db * ts

    tabs_p = _rot_tables(jnp.arange(t, dtype=I32))
    tabs_s = _rot_tables(jnp.tile(past + jnp.arange(ts, dtype=I32), db))
    p32_p, p16_p = _inproj(xp, g_norm1[0], mods["p"][1], mods["p"][0], w_r, tabs_p, t, tm_p)
    p32_s, _ = _inproj(xs, g_norm1[0], mods["s"][1], mods["s"][0], w_r, tabs_s, db * ts, tm_s)

    attn_p = _attn_prompt(p32_p, p16_p, bsz, t)
    attn_s = _attn_sample(p32_s, cache_k[0], cache_v[0], cache_kidx[0], page_table, ts)

    conv0_p = jnp.zeros((bsz, 8, conv_dim), F32)
    s0_p = jnp.zeros((bsz, GDN_V_HEADS, GDN_DK, GDN_DV), F32)
    gdn_p, sfin_p = _gdn(p32_p, (C_GQ, C_GK, C_GV, C_GZ, C_MISC), conv0_p, s0_p, gdn_conv_w[0], gdn_a_log[0],
                         gdn_dt_bias[0], gdn_norm_w[0], bsz, t, t)
    c_s = 128
    gw = C_GL - C_GQ
    g_s = jnp.concatenate([p32_s[:, C_GQ:C_GL], p32_s[:, C_MISC:C_MISC + LANE]], axis=1)
    g_s = jnp.pad(g_s.reshape(db, ts, gw + LANE), ((0, 0), (0, c_s - ts), (0, 0))).reshape(db * c_s, gw + LANE)
    conv0_s = jnp.pad(state_conv[0], ((0, 0), (8 - (CONV_W - 1), 0), (0, 0)))
    gdn_s, sfin_s = _gdn(g_s, (0, C_GK - C_GQ, C_GV - C_GQ, C_GZ - C_GQ, gw), conv0_s, state_gdn[0], gdn_conv_w[0],
                         gdn_a_log[0], gdn_dt_bias[0], gdn_norm_w[0], db, c_s, ts)
    gdn_s = gdn_s.reshape(db, c_s, -1)[:, :ts].reshape(db * ts, -1)

    outs = {}
    for name, attn_o, gdn_o, p32, x, rows_per_mod, tm in (("p", attn_p, gdn_p, p32_p, xp, t, min(256, t)),
                                                          ("s", attn_s, gdn_s, p32_s, xs, db * ts, min(256, db * ts))):
        merged = _merge(attn_o, gdn_o, w_oa, w_og, p32, tm)
        m = mods[name]
        outs[name] = _outproj(merged, w_o, x, m[2], m[4], m[3], g_norm2[0], w_router[0], b_router[0], rows_per_mod, tm)

    rb, tf = 512, 256
    h2 = jnp.concatenate([outs["p"][1], outs["s"][1]], axis=0)
    topi = jnp.concatenate([outs["p"][2], outs["s"][2]], axis=0)[:, :TOP_K]
    dest, row_tok, blk_e, n_used = _route(topi, rb)
    yb = _moe_ffn(h2[row_tok], blk_e, n_used, w_gu[0], b_gu[0], w_dn[0], b_dn[0], rb, tf)
    yg = yb[dest.reshape(-1)].reshape(-1, TOP_K * d)

    n_p = bsz * t
    y_p = _final(outs["p"][0], yg[:n_p], outs["p"][3], mods["p"][5], g_final, t, min(256, t))
    y_s = _final(outs["s"][0], yg[n_p:], outs["s"][3], mods["s"][5], g_final, db * ts, min(256, db * ts))

    kv = N_KV_HEADS * HEAD_DIM
    new_k_p = p32_p[:, C_AK:C_AK + kv].reshape(1, bsz, t, N_KV_HEADS, HEAD_DIM)
    new_v_p = p32_p[:, C_AV:C_AV + kv].reshape(1, bsz, t, N_KV_HEADS, HEAD_DIM)
    new_ki_p = p32_p[:, C_MISC:C_MISC + IDX_DIM].reshape(1, bsz, t, IDX_DIM)
    new_k_s = p32_s[:, C_AK:C_AK + kv].reshape(1, db, ts, N_KV_HEADS, HEAD_DIM)
    new_v_s = p32_s[:, C_AV:C_AV + kv].reshape(1, db, ts, N_KV_HEADS, HEAD_DIM)
    new_ki_s = p32_s[:, C_MISC:C_MISC + IDX_DIM].reshape(1, db, ts, IDX_DIM)
    pre_p = p32_p[:, C_GQ:C_GQ + conv_dim].reshape(bsz, t, conv_dim)
    conv_p = pre_p[:, t - (CONV_W - 1):][None]
    pre_s = p32_s[:, C_GQ:C_GQ + conv_dim].reshape(db, ts, conv_dim)
    conv_s = jnp.concatenate([state_conv[0], pre_s], axis=1)[:, ts:][None]
    return (y_p.reshape(bsz, t, d), y_s.reshape(db, ts, d), new_k_p, new_v_p, new_ki_p, conv_p, sfin_p[None],
            new_k_s, new_v_s, new_ki_s, conv_s, sfin_s[None])
```

```python
import functools

import jax
import jax.numpy as jnp
from jax import lax
from jax.experimental import pallas as pl
from jax.experimental.pallas import tpu as pltpu

F32 = jnp.float32
BF16 = jnp.bfloat16
I32 = jnp.int32

D_MODEL = 2048
N_HEADS = 16
N_KV_HEADS = 4
HEAD_DIM = 128
ROPE_THETA = 500000.0
ROT_FRACTION_DIV = 4
IDX_HEADS = 16
IDX_DIM = 64
TOPK_MAX = 256
GDN_QK_HEADS = 8
GDN_V_HEADS = 16
GDN_DK = 128
GDN_DV = 128
CONV_W = 4
N_EXPERTS = 32
TOP_K = 4
D_FF = D_MODEL
SWIGLU_LIMIT = 7.0
SWIGLU_ALPHA = 1.702
NORM_EPS = 1e-6
PAGE_SIZE = 128
LANE = 128

C_AQ, C_AK, C_AV, C_IQ = 0, 2048, 2560, 3072
C_GQ, C_GK, C_GV, C_GZ, C_GL = 4096, 5120, 6144, 8192, 10240
C_MISC = 14336
L_IW, L_GB, L_GA = 64, 80, 96
PROJ_TN = 512
PROJ_W = 14848
MISC_BLK = C_MISC // LANE

VMEM_LIMIT = 48 * 1024 * 1024
NEG_BIG = -1e30
INT_MIN = -2147483648

NT_DIMS = (((1,), (1,)), ((), ()))


def _cparams(sem):
    return pltpu.CompilerParams(dimension_semantics=sem, vmem_limit_bytes=VMEM_LIMIT)


def _split_bf16(a):
    hi = a.astype(BF16)
    lo = (a - hi.astype(F32)).astype(BF16)
    return hi, lo


def _dot3(a, b, dims=None):
    ah, al = _split_bf16(a)
    bh, bl = _split_bf16(b)
    if dims is None:
        d = lambda x, y: jnp.dot(x, y, preferred_element_type=F32)
    else:
        d = lambda x, y: lax.dot_general(x, y, dims, preferred_element_type=F32)
    return d(ah, bh) + d(al, bh) + d(ah, bl)


def _sigmoid(x):
    return 1.0 / (1.0 + jnp.exp(-x))


def _ada_kernel(c_ref, w_ref, b_ref, o_ref):
    c = c_ref[...]
    a = (c * _sigmoid(c)).astype(BF16)
    o_ref[...] = jnp.dot(a, w_ref[...].astype(BF16), preferred_element_type=F32) + b_ref[...]


def _ada(c, w, b):
    m, k = c.shape
    n = w.shape[2]
    tn = 1024
    return pl.pallas_call(
        _ada_kernel,
        grid=(n // tn,),
        in_specs=[pl.BlockSpec((m, k), lambda j: (0, 0)),
                  pl.BlockSpec((None, k, tn), lambda j: (0, 0, j)),
                  pl.BlockSpec((1, tn), lambda j: (0, j))],
        out_specs=pl.BlockSpec((m, tn), lambda j: (0, j)),
        out_shape=jax.ShapeDtypeStruct((m, n), F32),
        compiler_params=_cparams(("arbitrary",)),
        name="ada",
    )(c, w, b.reshape(1, n))


def _rot_slab(x, tabs, half):
    a, b, c = tabs
    return x * a + pltpu.roll(x, LANE - half, 1) * b + pltpu.roll(x, half, 1) * c


def _inproj_kernel(x_ref, g_ref, sc_ref, sh_ref, w_ref, tab_ref, o32_ref, o16_ref, h_scr):
    j = pl.program_id(1)

    @pl.when(j == 0)
    def _():
        x = x_ref[...]
        ms = jnp.mean(x * x, axis=-1, keepdims=True)
        y = x * lax.rsqrt(ms + NORM_EPS) * g_ref[...]
        h_scr[...] = (y * (1.0 + sc_ref[...]) + sh_ref[...]).astype(BF16)

    acc = jnp.dot(h_scr[...], w_ref[...], preferred_element_type=F32)
    n_slab = PROJ_TN // LANE

    def store(fn):
        for s in range(n_slab):
            v = fn(acc[:, s * LANE:(s + 1) * LANE])
            o32_ref[:, s * LANE:(s + 1) * LANE] = v
            o16_ref[:, s * LANE:(s + 1) * LANE] = v.astype(BF16)

    rot_head = j < 5
    rot_idx = ((j >= 6) & (j < 8)) | (j == C_MISC // PROJ_TN)

    @pl.when(rot_head)
    def _():
        tabs = (tab_ref[0], tab_ref[1], tab_ref[2])
        store(lambda v: _rot_slab(v, tabs, HEAD_DIM // ROT_FRACTION_DIV // 2))

    @pl.when(rot_idx)
    def _():
        tabs = (tab_ref[0], tab_ref[1], tab_ref[2])
        store(lambda v: _rot_slab(v, tabs, IDX_DIM // ROT_FRACTION_DIV // 2))

    @pl.when(jnp.logical_not(rot_head | rot_idx))
    def _():
        store(lambda v: v)


def _tab_type(j):
    return jnp.where(j < 6, 0, jnp.where(j < 28, 1, 2))


def _inproj(x, g_norm, scale, shift, w_r, tabs, rows_per_mod, tm):
    m = x.shape[0]
    tt = tabs.shape[2]
    n_t = tt // tm
    mod_rows = scale.shape[1]
    tiles_per_mod = rows_per_mod // tm
    if mod_rows == 1:
        mod_spec = pl.BlockSpec((None, 1, D_MODEL), lambda i, j: (i // tiles_per_mod, 0, 0))
    else:
        mod_spec = pl.BlockSpec((None, tm, D_MODEL), lambda i, j: (i // tiles_per_mod, i % tiles_per_mod, 0))
    return pl.pallas_call(
        _inproj_kernel,
        grid=(m // tm, PROJ_W // PROJ_TN),
        in_specs=[pl.BlockSpec((tm, D_MODEL), lambda i, j: (i, 0)),
                  pl.BlockSpec((1, D_MODEL), lambda i, j: (0, 0)),
                  mod_spec, mod_spec,
                  pl.BlockSpec((D_MODEL, PROJ_TN), lambda i, j: (0, j)),
                  pl.BlockSpec((None, 3, tm, LANE), lambda i, j: (_tab_type(j), 0, i % n_t, 0))],
        out_specs=[pl.BlockSpec((tm, PROJ_TN), lambda i, j: (i, j)),
                   pl.BlockSpec((tm, PROJ_TN), lambda i, j: (i, j))],
        out_shape=[jax.ShapeDtypeStruct((m, PROJ_W), F32), jax.ShapeDtypeStruct((m, PROJ_W), BF16)],
        scratch_shapes=[pltpu.VMEM((tm, D_MODEL), BF16)],
        compiler_params=_cparams(("parallel", "arbitrary")),
        name="inproj",
    )(x, g_norm.reshape(1, D_MODEL), scale, shift, w_r, tabs)


def _rot_tables(pos):
    tt = pos.shape[0]
    posf = pos.astype(F32)

    def tab(rot, period, scale):
        half = rot // 2
        inv = ROPE_THETA ** (-jnp.arange(half, dtype=F32) * (2.0 / rot))
        ang = posf[:, None] * inv[None, :]
        cos, sin = jnp.cos(ang), jnp.sin(ang)
        zh = jnp.zeros((tt, half), F32)
        rest = period - rot
        a = jnp.concatenate([cos, cos, jnp.ones((tt, rest), F32)], axis=1) * scale
        b = jnp.concatenate([-sin, zh, jnp.zeros((tt, rest), F32)], axis=1) * scale
        c = jnp.concatenate([zh, sin, jnp.zeros((tt, rest), F32)], axis=1) * scale
        return a, b, c

    head = jnp.stack([jnp.tile(t, (1, LANE // HEAD_DIM)) for t in tab(HEAD_DIM // ROT_FRACTION_DIV, HEAD_DIM, 1.0)])
    idxq = jnp.stack([jnp.tile(t, (1, LANE // IDX_DIM))
                      for t in tab(IDX_DIM // ROT_FRACTION_DIV, IDX_DIM, IDX_DIM ** -0.5)])
    ka, kb, kc = tab(IDX_DIM // ROT_FRACTION_DIV, IDX_DIM, 1.0)
    rest = LANE - IDX_DIM
    lane = jnp.arange(rest)
    tail_a = jnp.where(lane < (L_GB - L_IW), IDX_HEADS ** -0.5, 1.0).astype(F32)
    ma = jnp.concatenate([ka, jnp.broadcast_to(tail_a, (tt, rest))], axis=1)
    mb = jnp.concatenate([kb, jnp.zeros((tt, rest), F32)], axis=1)
    mc = jnp.concatenate([kc, jnp.zeros((tt, rest), F32)], axis=1)
    misc = jnp.stack([ma, mb, mc])
    return jnp.stack([head, idxq, misc])


def _reorder_w_in(w_in):
    offs = {}
    o = 0
    for name, wdt in (("aq", 2048), ("ak", 512), ("av", 512), ("iq", 1024), ("ik", 64), ("iw", 16), ("gq", 1024),
                      ("gk", 1024), ("gv", 2048), ("gz", 2048), ("gb", 16), ("ga", 16), ("gl", 4096)):
        offs[name] = (o, wdt)
        o += wdt

    def seg(name):
        s, wdt = offs[name]
        return w_in[:, s:s + wdt]

    k = w_in.shape[0]
    misc = jnp.concatenate([seg("ik"), seg("iw"), seg("gb"), seg("ga"), jnp.zeros((k, LANE - 112), w_in.dtype)], axis=1)
    parts = [seg("aq"), seg("ak"), seg("av"), seg("iq"), seg("gq"), seg("gk"), seg("gv"), seg("gz"), seg("gl"), misc,
             jnp.zeros((k, PROJ_W - C_MISC - LANE), w_in.dtype)]
    return jnp.concatenate(parts, axis=1).astype(BF16)


def _order_key(score):
    bits = pltpu.bitcast(score, I32)
    return bits ^ ((bits >> 31) & 0x7FFFFFFF)


def _kth_largest_key(count_ge, n_rows, n_sel):
    def body(i, t):
        bit = lax.shift_left(jnp.int32(1), 31 - i)
        cand = t ^ bit
        return jnp.where(count_ge(cand) >= n_sel, cand, t)

    return lax.fori_loop(0, 32, body, jnp.full((n_rows, 1), INT_MIN, I32))


def _attn_prompt_kernel(q_ref, iq_ref, mq_ref, k_ref, v_ref, mk_ref, o_ref, key_scr, *, tq, kc, n_sel):
    qb = pl.program_id(1)
    q0 = qb * tq
    nkc = (q0 + tq + kc - 1) // kc
    qpos = q0 + lax.broadcasted_iota(I32, (tq, 1), 0)
    kiota = lax.broadcasted_iota(I32, (1, kc), 1)
    wq = mq_ref[...]

    def score_body(c, carry):
        ks = pl.multiple_of(c * kc, kc)
        kt = mk_ref[pl.ds(ks, kc), :][:, :IDX_DIM]
        acc = jnp.zeros((tq, kc), F32)
        for h in range(IDX_HEADS):
            s = lax.dot_general(iq_ref[:, h * IDX_DIM:(h + 1) * IDX_DIM], kt, NT_DIMS, preferred_element_type=F32)
            acc = acc + wq[:, L_IW + h:L_IW + h + 1] * jnp.maximum(s, 0.0)
        acc = jnp.where(ks + kiota <= qpos, acc, -jnp.inf)
        key_scr[c] = _order_key(acc)
        return carry

    lax.fori_loop(0, nkc, score_body, 0)

    def count_ge(cand):
        def body(c, acc):
            m = jnp.where(key_scr[c] >= cand, 1.0, 0.0)
            for s in range(kc // LANE):
                acc = acc + m[:, s * LANE:(s + 1) * LANE]
            return acc

        acc = lax.fori_loop(0, nkc, body, jnp.zeros((tq, LANE), F32))
        return jnp.sum(acc, axis=1, keepdims=True)

    thr = _kth_largest_key(count_ge, tq, float(n_sel))

    group = N_HEADS // N_KV_HEADS
    scale = HEAD_DIM ** -0.5
    for n in range(N_KV_HEADS):
        qs = jnp.concatenate(
            [q_ref[:, (n * group + g) * HEAD_DIM:(n * group + g + 1) * HEAD_DIM] for g in range(group)], axis=0)
        qs = (qs.astype(F32) * scale).astype(BF16)

        def attn_body(c, carry, n=n, qs=qs):
            m_i, l_i, acc = carry
            ks = pl.multiple_of(c * kc, kc)
            kn = k_ref[pl.ds(ks, kc), n * HEAD_DIM:(n + 1) * HEAD_DIM]
            vn = v_ref[pl.ds(ks, kc), n * HEAD_DIM:(n + 1) * HEAD_DIM]
            s = lax.dot_general(qs, kn, NT_DIMS, preferred_element_type=F32)
            sel = (key_scr[c] >= thr) & (ks + kiota <= qpos)
            s = jnp.where(sel[None], s.reshape(group, tq, kc), NEG_BIG).reshape(group * tq, kc)
            m_new = jnp.maximum(m_i, jnp.max(s, axis=1, keepdims=True))
            alpha = jnp.exp(m_i - m_new)
            p = jnp.exp(s - m_new)
            l_new = alpha * l_i + jnp.sum(p, axis=1, keepdims=True)
            acc = alpha * acc + jnp.dot(p.astype(BF16), vn, preferred_element_type=F32)
            return m_new, l_new, acc

        init = (jnp.full((group * tq, 1), NEG_BIG, F32), jnp.zeros((group * tq, 1), F32),
                jnp.zeros((group * tq, HEAD_DIM), F32))
        _, l_f, acc_f = lax.fori_loop(0, nkc, attn_body, init)
        out = acc_f / l_f
        for g in range(group):
            o_ref[:, (n * group + g) * HEAD_DIM:(n * group + g + 1) * HEAD_DIM] = out[g * tq:(g + 1) * tq].astype(BF16)


def _attn_prompt(p32, p16, bsz, t, tq=128, kc=512):
    nq = t // tq
    kc = min(kc, t)
    n_sel = min(TOPK_MAX, t // 4)
    kern = functools.partial(_attn_prompt_kernel, tq=tq, kc=kc, n_sel=n_sel)
    return pl.pallas_call(
        kern,
        grid=(bsz, nq),
        in_specs=[pl.BlockSpec((tq, N_HEADS * HEAD_DIM), lambda b, i: (b * nq + i, 0)),
                  pl.BlockSpec((tq, IDX_HEADS * IDX_DIM), lambda b, i: (b * nq + i, C_IQ // (IDX_HEADS * IDX_DIM))),
                  pl.BlockSpec((tq, LANE), lambda b, i: (b * nq + i, MISC_BLK)),
                  pl.BlockSpec((t, N_KV_HEADS * HEAD_DIM), lambda b, i: (b, C_AK // (N_KV_HEADS * HEAD_DIM))),
                  pl.BlockSpec((t, N_KV_HEADS * HEAD_DIM), lambda b, i: (b, C_AV // (N_KV_HEADS * HEAD_DIM))),
                  pl.BlockSpec((t, LANE), lambda b, i: (b, MISC_BLK))],
        out_specs=pl.BlockSpec((tq, N_HEADS * HEAD_DIM), lambda b, i: (b * nq + i, 0)),
        out_shape=jax.ShapeDtypeStruct((bsz * t, N_HEADS * HEAD_DIM), BF16),
        scratch_shapes=[pltpu.VMEM((t // kc, tq, kc), I32)],
        compiler_params=_cparams(("parallel", "arbitrary")),
        name="attn_prompt",
    )(p16, p16, p32, p16, p16, p16)


def _attn_sample_kernel(pt_ref, iq_ref, wc_ref, q_ref, kn_ref, vn_ref, ikn_ref, *rest, n_pages, n_sel, past):
    kidx_refs = rest[:n_pages]
    k_refs = rest[n_pages:2 * n_pages]
    v_refs = rest[2 * n_pages:3 * n_pages]
    o_ref, sc_scr, k_scr, v_scr = rest[3 * n_pages:]
    del pt_ref
    rows = 8
    n_keys = (n_pages + 1) * PAGE_SIZE
    iq = iq_ref[...].astype(BF16)
    wc = wc_ref[...]

    def idx_score(kt):
        s = lax.dot_general(iq, kt.astype(BF16), NT_DIMS, preferred_element_type=F32)
        x = jnp.maximum(s, 0.0) * wc
        return jnp.sum(x.reshape(IDX_HEADS, rows, PAGE_SIZE), axis=0)

    for p in range(n_pages + 1):
        sl = slice(p * PAGE_SIZE, (p + 1) * PAGE_SIZE)
        if p < n_pages:
            sc_scr[:, sl] = idx_score(kidx_refs[p][...])
            k_scr[sl, :] = k_refs[p][...].astype(BF16)
            v_scr[sl, :] = v_refs[p][...].astype(BF16)
        else:
            sc_scr[:, sl] = idx_score(ikn_ref[...])
            k_scr[sl, :] = kn_ref[...].astype(BF16)
            v_scr[sl, :] = vn_ref[...].astype(BF16)

    qpos = past + lax.broadcasted_iota(I32, (rows, 1), 0)
    kpos = lax.broadcasted_iota(I32, (1, n_keys), 1)
    causal = kpos <= qpos
    keys = _order_key(jnp.where(causal, sc_scr[...], -jnp.inf))

    def count_ge(cand):
        return jnp.sum(jnp.where(keys >= cand, 1.0, 0.0), axis=1, keepdims=True)

    thr = _kth_largest_key(count_ge, rows, float(n_sel))
    sel = ((keys >= thr) & causal)[None]
    group = N_HEADS // N_KV_HEADS
    scale = HEAD_DIM ** -0.5
    gr = group * rows
    for n in range(N_KV_HEADS):
        qn = (q_ref[n * gr:(n + 1) * gr, :] * scale).astype(BF16)
        s = lax.dot_general(qn, k_scr[:, n * HEAD_DIM:(n + 1) * HEAD_DIM], NT_DIMS, preferred_element_type=F32)
        s = jnp.where(sel, s.reshape(group, rows, n_keys), NEG_BIG).reshape(gr, n_keys)
        m = jnp.max(s, axis=1, keepdims=True)
        p = jnp.exp(s - m)
        l = jnp.sum(p, axis=1, keepdims=True)
        o = jnp.dot(p.astype(BF16), v_scr[:, n * HEAD_DIM:(n + 1) * HEAD_DIM], preferred_element_type=F32)
        o_ref[n * gr:(n + 1) * gr, :] = o / l


def _attn_sample(p32, cache_k, cache_v, cache_kidx, page_table, ts):
    db, n_pages = page_table.shape
    past = n_pages * PAGE_SIZE
    rows = 8
    group = N_HEADS // N_KV_HEADS
    n_phys = cache_k.shape[0]
    ck = cache_k.reshape(n_phys, PAGE_SIZE, N_KV_HEADS * HEAD_DIM)
    cv = cache_v.reshape(n_phys, PAGE_SIZE, N_KV_HEADS * HEAD_DIM)
    ps = p32.reshape(db, ts, PROJ_W)
    pad_t = ((0, 0), (0, 0), (0, rows - ts), (0, 0))
    iq = ps[:, :, C_IQ:C_IQ + IDX_HEADS * IDX_DIM].reshape(db, ts, IDX_HEADS, IDX_DIM).transpose(0, 2, 1, 3)
    iq = jnp.pad(iq, pad_t).reshape(db, IDX_HEADS * rows, IDX_DIM)
    wc = ps[:, :, C_MISC + L_IW:C_MISC + L_IW + IDX_HEADS].transpose(0, 2, 1)
    wc = jnp.pad(wc, ((0, 0), (0, 0), (0, rows - ts))).reshape(db, IDX_HEADS * rows, 1)
    q = ps[:, :, :N_HEADS * HEAD_DIM].reshape(db, ts, N_HEADS, HEAD_DIM).transpose(0, 2, 1, 3)
    q = jnp.pad(q, pad_t).reshape(db, N_HEADS * rows, HEAD_DIM)
    pad_k = ((0, 0), (0, PAGE_SIZE - ts), (0, 0))
    kn = jnp.pad(ps[:, :, C_AK:C_AK + N_KV_HEADS * HEAD_DIM], pad_k)
    vn = jnp.pad(ps[:, :, C_AV:C_AV + N_KV_HEADS * HEAD_DIM], pad_k)
    ikn = jnp.pad(ps[:, :, C_MISC:C_MISC + IDX_DIM], pad_k)
    n_sel = min(TOPK_MAX, (past + ts) // 4)
    kw = N_KV_HEADS * HEAD_DIM

    def page_spec(width, p):
        return pl.BlockSpec((None, PAGE_SIZE, width), lambda d, pt, p=p: (pt[d * n_pages + p], 0, 0))

    def per_d(r, w):
        return pl.BlockSpec((None, r, w), lambda d, pt: (d, 0, 0))

    in_specs = ([per_d(IDX_HEADS * rows, IDX_DIM), per_d(IDX_HEADS * rows, 1), per_d(N_HEADS * rows, HEAD_DIM),
                 per_d(PAGE_SIZE, kw), per_d(PAGE_SIZE, kw), per_d(PAGE_SIZE, IDX_DIM)]
                + [page_spec(IDX_DIM, p) for p in range(n_pages)]
                + [page_spec(kw, p) for p in range(n_pages)]
                + [page_spec(kw, p) for p in range(n_pages)])
    n_keys = (n_pages + 1) * PAGE_SIZE
    grid_spec = pltpu.PrefetchScalarGridSpec(
        num_scalar_prefetch=1, grid=(db,), in_specs=in_specs,
        out_specs=per_d(N_HEADS * rows, HEAD_DIM),
        scratch_shapes=[pltpu.VMEM((rows, n_keys), F32), pltpu.VMEM((n_keys, kw), BF16), pltpu.VMEM((n_keys, kw), BF16)])
    kern = functools.partial(_attn_sample_kernel, n_pages=n_pages, n_sel=n_sel, past=past)
    o = pl.pallas_call(
        kern, grid_spec=grid_spec,
        out_shape=jax.ShapeDtypeStruct((db, N_HEADS * rows, HEAD_DIM), F32),
        compiler_params=_cparams(("arbitrary",)),
        name="attn_sample",
    )(page_table.reshape(-1), iq, wc, q, kn, vn, ikn,
      *([cache_kidx] * n_pages), *([ck] * n_pages), *([cv] * n_pages))
    o = o.reshape(db, N_HEADS, rows, HEAD_DIM)[:, :, :ts].transpose(0, 2, 1, 3)
    return o.reshape(db * ts, N_HEADS * HEAD_DIM).astype(BF16)


def _dot3s(ah, al, bh, bl, dims=None):
    if dims is None:
        d = lambda x, y: jnp.dot(x, y, preferred_element_type=F32)
    else:
        d = lambda x, y: lax.dot_general(x, y, dims, preferred_element_type=F32)
    return d(ah, bh) + d(al, bh) + d(ah, bl)


def _cumsum_rows(sel_bf16, g_rows):
    gb = jnp.broadcast_to(g_rows, (g_rows.shape[0], LANE))
    g_hi, g_lo = _split_bf16(gb)
    g_lo2 = (gb - g_hi.astype(F32) - g_lo.astype(F32)).astype(BF16)
    d = lambda y: jnp.dot(sel_bf16, y, preferred_element_type=F32)
    return d(g_hi) + d(g_lo) + d(g_lo2)


def _l2n(x):
    return x * lax.rsqrt(jnp.sum(x * x, axis=-1, keepdims=True) + NORM_EPS)


def _gate_params(misc, hp, live):
    x = misc + hp[1:2, :]
    softplus = jnp.maximum(x, 0.0) + jnp.log(1.0 + jnp.exp(-jnp.abs(x)))
    g_all = -jnp.exp(hp[0:1, :]) * softplus
    beta_all = _sigmoid(misc)
    if live is not None:
        g_all = jnp.where(live, g_all, 0.0)
        beta_all = jnp.where(live, beta_all, 0.0)
    return g_all, beta_all


def _gdn_kernel(q_ref, k_ref, v_ref, z_ref, m_ref, wq_ref, wk_ref, wv_ref, cq_ref, ck_ref, cv_ref, hp_ref, nw_ref,
                s0_ref, o_ref, sf_ref, s_scr, hq_scr, hk_scr, hv_scr, xc_scr, *, c, nh):
    g_idx = pl.program_id(1)
    tb = pl.program_id(2)
    n_tb = pl.num_programs(2)
    dk, dv = GDN_DK, GDN_DV

    @pl.when(tb == 0)
    def _():
        s_scr[...] = s0_ref[...]
        hq_scr[...] = cq_ref[...]
        hk_scr[...] = ck_ref[...]
        hv_scr[...] = cv_ref[...]

    def conv_silu(x_ref, w_ref, h_scr):
        width = x_ref.shape[1]
        x = x_ref[...]
        xc_scr[0:8, 0:width] = h_scr[...]
        xc_scr[8:8 + c, 0:width] = x
        acc = jnp.zeros((c, width), F32)
        for j in range(CONV_W):
            acc = acc + xc_scr[8 - (CONV_W - 1) + j:8 - (CONV_W - 1) + j + c, 0:width] * w_ref[j:j + 1, :]
        h_scr[...] = x[c - 8:c, :]
        return acc * _sigmoid(acc)

    qa = conv_silu(q_ref, wq_ref, hq_scr)
    ka = conv_silu(k_ref, wk_ref, hk_scr)
    va = conv_silu(v_ref, wv_ref, hv_scr)
    g_all, beta_all = _gate_params(m_ref[...], hp_ref[...], None)
    lane = lax.broadcasted_iota(I32, (1, LANE), 1)
    ri = lax.broadcasted_iota(I32, (c, c), 0)
    ci = lax.broadcasted_iota(I32, (c, c), 1)
    causal = ri >= ci
    strict = ri > ci
    tri = jnp.where(causal, 1.0, 0.0).astype(BF16)
    eye = jnp.where(ri == ci, 1.0, 0.0)

    def same_blk(log_b):
        return (ri >> log_b) == (ci >> log_b)

    heads = range(2 * nh)
    ks = [_l2n(ka[:, hh * dk:(hh + 1) * dk]) for hh in range(nh)]
    qss = [_l2n(qa[:, hh * dk:(hh + 1) * dk]) * (dk ** -0.5) for hh in range(nh)]
    kk_qk = [lax.dot_general(jnp.concatenate([ks[hh], qss[hh]], axis=0).astype(BF16), ks[hh].astype(BF16), NT_DIMS,
                             preferred_element_type=F32) for hh in range(nh)]
    g_cols = [jnp.sum(jnp.where(lane == L_GA + 2 * nh * g_idx + hl, g_all, 0.0), axis=1, keepdims=True)
              for hl in heads]
    betas = [jnp.sum(jnp.where(lane == L_GB + 2 * nh * g_idx + hl, beta_all, 0.0), axis=1, keepdims=True)
             for hl in heads]
    gcs = [_cumsum_rows(tri, g) for g in g_cols]
    decays = [jnp.exp(jnp.where(causal, gc - gc.T, -jnp.inf)) for gc in gcs]
    a_mats = [jnp.where(strict, kk_qk[hl // 2][0:c] * betas[hl] * decays[hl], 0.0) for hl in heads]
    nms = [jnp.where(same_blk(3), -a, 0.0) for a in a_mats]
    pms = [eye + n for n in nms]
    for _ in range(2):
        nms = [_dot3(n, n) for n in nms]
        pms = [p + _dot3(p, n) for p, n in zip(pms, nms)]
    for lb in range(3, (c - 1).bit_length()):
        join = same_blk(lb + 1) & jnp.logical_not(same_blk(lb))
        t1s = [_dot3(jnp.where(join, a, 0.0), p) for a, p in zip(a_mats, pms)]
        pms = [p - _dot3(p, t1) for p, t1 in zip(pms, t1s)]
    us = [_dot3(pms[hl], va[:, hl * dv:(hl + 1) * dv] * betas[hl]) for hl in heads]
    ws = [_dot3(pms[hl], ks[hl // 2] * betas[hl] * jnp.exp(gcs[hl])) for hl in heads]
    qks = [jnp.where(causal, kk_qk[hl // 2][c:2 * c] * decays[hl], 0.0) for hl in heads]
    s_in = [s_scr[hl] for hl in heads]
    v_news = [us[hl] - _dot3(ws[hl], s_in[hl]) for hl in heads]
    os_ = [jnp.dot((qss[hl // 2] * jnp.exp(gcs[hl])).astype(BF16), s_in[hl].astype(BF16), preferred_element_type=F32)
           + jnp.dot(qks[hl].astype(BF16), v_news[hl].astype(BF16), preferred_element_type=F32) for hl in heads]
    for hl in heads:
        g_last = gcs[hl][c - 1:c, :]
        kdec = ks[hl // 2] * jnp.exp(g_last - gcs[hl])
        s_scr[hl] = s_in[hl] * jnp.exp(g_last) + _dot3(kdec.T, v_news[hl])
    for hl in heads:
        o = os_[hl]
        z = z_ref[:, hl * dv:(hl + 1) * dv]
        on = o * lax.rsqrt(jnp.mean(o * o, axis=-1, keepdims=True) + NORM_EPS) * nw_ref[...]
        o_ref[:, hl * dv:(hl + 1) * dv] = (on * (z * _sigmoid(z))).astype(BF16)

    @pl.when(tb == n_tb - 1)
    def _():
        sf_ref[...] = s_scr[...]


def _gdn_hp(a_log, dt_bias):
    hp = jnp.zeros((8, LANE), F32)
    return hp.at[0, L_GA:L_GA + GDN_V_HEADS].set(a_log.astype(F32)).at[1, L_GA:L_GA + GDN_V_HEADS].set(
        dt_bias.astype(F32))


def _gdn(p32, conv0, s0, conv_w, a_log, dt_bias, norm_w, bsz, t, c=128, nh=4):
    n_tb = t // c
    n_g = GDN_QK_HEADS // nh
    qw, vw = nh * GDN_DK, 2 * nh * GDN_DV
    row = lambda b, g, i: b * n_tb + i
    in_specs = [
        pl.BlockSpec((c, qw), lambda b, g, i: (row(b, g, i), C_GQ // qw + g)),
        pl.BlockSpec((c, qw), lambda b, g, i: (row(b, g, i), C_GK // qw + g)),
        pl.BlockSpec((c, vw), lambda b, g, i: (row(b, g, i), C_GV // vw + g)),
        pl.BlockSpec((c, vw), lambda b, g, i: (row(b, g, i), C_GZ // vw + g)),
        pl.BlockSpec((c, LANE), lambda b, g, i: (row(b, g, i), MISC_BLK)),
        pl.BlockSpec((CONV_W, qw), lambda b, g, i: (0, g)),
        pl.BlockSpec((CONV_W, qw), lambda b, g, i: (0, n_g + g)),
        pl.BlockSpec((CONV_W, vw), lambda b, g, i: (0, n_g + g)),
        pl.BlockSpec((None, 8, qw), lambda b, g, i: (b, 0, g)),
        pl.BlockSpec((None, 8, qw), lambda b, g, i: (b, 0, n_g + g)),
        pl.BlockSpec((None, 8, vw), lambda b, g, i: (b, 0, n_g + g)),
        pl.BlockSpec((8, LANE), lambda b, g, i: (0, 0)),
        pl.BlockSpec((1, GDN_DV), lambda b, g, i: (0, 0)),
        pl.BlockSpec((None, 2 * nh, GDN_DK, GDN_DV), lambda b, g, i: (b, g, 0, 0)),
    ]
    out_specs = [pl.BlockSpec((c, vw), lambda b, g, i: (row(b, g, i), g)),
                 pl.BlockSpec((None, 2 * nh, GDN_DK, GDN_DV), lambda b, g, i: (b, g, 0, 0))]
    return pl.pallas_call(
        functools.partial(_gdn_kernel, c=c, nh=nh),
        grid=(bsz, n_g, n_tb),
        in_specs=in_specs, out_specs=out_specs,
        out_shape=[jax.ShapeDtypeStruct((bsz * t, GDN_V_HEADS * GDN_DV), BF16),
                   jax.ShapeDtypeStruct((bsz, GDN_V_HEADS, GDN_DK, GDN_DV), F32)],
        scratch_shapes=[pltpu.VMEM((2 * nh, GDN_DK, GDN_DV), F32), pltpu.VMEM((8, qw), F32), pltpu.VMEM((8, qw), F32),
                        pltpu.VMEM((8, vw), F32), pltpu.VMEM((c + 8, vw), F32)],
        compiler_params=_cparams(("parallel", "parallel", "arbitrary")),
        name="gdn",
    )(p32, p32, p32, p32, p32, conv_w, conv_w, conv_w, conv0, conv0, conv0, _gdn_hp(a_log, dt_bias),
      norm_w.reshape(1, GDN_DV), s0)


def _gdn_sample_kernel(x_ref, z_ref, m_ref, w_ref, hp_ref, nw_ref, s0_ref, o_ref, sf_ref, *, ts):
    rows, nvh, dk, dv = 8, GDN_V_HEADS, GDN_DK, GDN_DV
    qd = GDN_QK_HEADS * dk
    x8 = x_ref[...]
    w = w_ref[...]
    acc = x8 * w[0:1, :]
    for j in range(1, CONV_W):
        acc = acc + pltpu.roll(x8, rows - j, 0) * w[j:j + 1, :]
    act = acc * _sigmoid(acc)
    live = lax.broadcasted_iota(I32, (rows, 1), 0) < ts
    g_all, beta_all = _gate_params(m_ref[...], hp_ref[...], live)
    qn = [_l2n(act[:, g * dk:(g + 1) * dk]) * (dk ** -0.5) for g in range(GDN_QK_HEADS)]
    kn = [_l2n(act[:, qd + g * dk:qd + (g + 1) * dk]) for g in range(GDN_QK_HEADS)]
    rep = nvh // GDN_QK_HEADS
    cat = lambda parts: jnp.concatenate(parts, axis=0)
    k = cat([kn[h // rep] for h in range(nvh)])
    qs = cat([qn[h // rep] for h in range(nvh)])
    v = cat([act[:, 2 * qd + h * dv:2 * qd + (h + 1) * dv] for h in range(nvh)])
    z = cat([z_ref[:, h * dv:(h + 1) * dv] for h in range(nvh)])
    beta = cat([beta_all[:, L_GB + h:L_GB + h + 1] for h in range(nvh)])
    g_col = cat([g_all[:, L_GA + h:L_GA + h + 1] for h in range(nvh)])
    n = nvh * rows
    ri = lax.broadcasted_iota(I32, (n, n), 0)
    ci = lax.broadcasted_iota(I32, (n, n), 1)
    same = (ri >> 3) == (ci >> 3)
    causal = same & (ri >= ci)
    strict = same & (ri > ci)
    eye = jnp.where(ri == ci, 1.0, 0.0)
    gc = _cumsum_rows(jnp.where(causal, 1.0, 0.0).astype(BF16), g_col)
    gl = _cumsum_rows(jnp.where(same, 1.0, 0.0).astype(BF16), g_col)
    decay = jnp.exp(jnp.where(causal, gc - gc.T, -jnp.inf))
    kb = k * beta
    a_mat = jnp.where(strict, _dot3(kb, k, NT_DIMS) * decay, 0.0)
    nm = -a_mat
    pm = eye + nm
    for _ in range(2):
        nm = _dot3(nm, nm)
        pm = pm + _dot3(pm, nm)
    u = _dot3(pm, v * beta)
    w_rows = _dot3(pm, kb * jnp.exp(gc))
    qk = jnp.where(causal, _dot3(qs, k, NT_DIMS) * decay, 0.0)
    qg = qs * jnp.exp(gc)
    kdt = (k * jnp.exp(gl - gc)).T
    s_in = [s0_ref[h] for h in range(nvh)]
    v_new, o1 = [], []
    for h in range(nvh):
        sl = slice(h * rows, (h + 1) * rows)
        r = _dot3(cat([w_rows[sl], qg[sl]]), s_in[h])
        v_new.append(u[sl] - r[0:rows])
        o1.append(r[rows:2 * rows])
    v_new = cat(v_new)
    o = cat(o1) + _dot3(qk, v_new)
    vh, vl = _split_bf16(v_new)
    lane = lax.broadcasted_iota(I32, (1, n), 1)
    for h in range(nvh):
        kh, kl = _split_bf16(jnp.where((lane >> 3) == h, kdt, 0.0))
        sf_ref[h] = s_in[h] * jnp.exp(gl[h * rows:h * rows + 1, :]) + _dot3s(kh, kl, vh, vl)
    on = o * lax.rsqrt(jnp.mean(o * o, axis=-1, keepdims=True) + NORM_EPS) * nw_ref[...]
    o_ref[...] = on * (z * _sigmoid(z))


def _gdn_sample(p32, state_conv, s0, conv_w, a_log, dt_bias, norm_w, db, ts):
    rows = 8
    conv_dim = conv_w.shape[-1]
    ps = p32.reshape(db, ts, PROJ_W)
    x8 = jnp.concatenate([state_conv, ps[:, :, C_GQ:C_GQ + conv_dim],
                          jnp.zeros((db, rows - ts - (CONV_W - 1), conv_dim), F32)], axis=1)
    pad_t = ((0, 0), (0, rows - ts), (0, 0))
    z8 = jnp.pad(ps[:, :, C_GZ:C_GZ + GDN_V_HEADS * GDN_DV], pad_t)
    m8 = jnp.pad(ps[:, :, C_MISC:C_MISC + LANE], pad_t)
    n = GDN_V_HEADS * rows
    per_d = lambda w: pl.BlockSpec((None, rows, w), lambda d: (d, 0, 0))
    state = pl.BlockSpec((None, GDN_V_HEADS, GDN_DK, GDN_DV), lambda d: (d, 0, 0, 0))
    o, sf = pl.pallas_call(
        functools.partial(_gdn_sample_kernel, ts=ts),
        grid=(db,),
        in_specs=[per_d(conv_dim), per_d(GDN_V_HEADS * GDN_DV), per_d(LANE),
                  pl.BlockSpec((CONV_W, conv_dim), lambda d: (0, 0)),
                  pl.BlockSpec((8, LANE), lambda d: (0, 0)),
                  pl.BlockSpec((1, GDN_DV), lambda d: (0, 0)), state],
        out_specs=[pl.BlockSpec((None, n, GDN_DV), lambda d: (d, 0, 0)), state],
        out_shape=[jax.ShapeDtypeStruct((db, n, GDN_DV), F32),
                   jax.ShapeDtypeStruct((db, GDN_V_HEADS, GDN_DK, GDN_DV), F32)],
        compiler_params=_cparams(("parallel",)),
        name="gdn_sample",
    )(x8, z8, m8, conv_w, _gdn_hp(a_log, dt_bias), norm_w.reshape(1, GDN_DV), s0)
    o = o.reshape(db, GDN_V_HEADS, rows, GDN_DV)[:, :, :ts].transpose(0, 2, 1, 3)
    return o.reshape(db * ts, GDN_V_HEADS * GDN_DV).astype(BF16), sf


def _merge_kernel(a_ref, g_ref, wa_ref, wg_ref, ga_ref, gg_ref, o_ref):
    pa = jnp.dot(a_ref[...], wa_ref[...], preferred_element_type=F32)
    pg = jnp.dot(g_ref[...], wg_ref[...], preferred_element_type=F32)
    o_ref[...] = (_sigmoid(ga_ref[...]) * pa + _sigmoid(gg_ref[...]) * pg).astype(BF16)


def _merge(attn_o, gdn_o, w_oa, w_og, p32, tm):
    m = attn_o.shape[0]
    tn = 512
    return pl.pallas_call(
        _merge_kernel,
        grid=(m // tm, D_MODEL // tn),
        in_specs=[pl.BlockSpec((tm, D_MODEL), lambda i, j: (i, 0)),
                  pl.BlockSpec((tm, D_MODEL), lambda i, j: (i, 0)),
                  pl.BlockSpec((D_MODEL, tn), lambda i, j: (0, j)),
                  pl.BlockSpec((D_MODEL, tn), lambda i, j: (0, j)),
                  pl.BlockSpec((tm, tn), lambda i, j: (i, C_GL // tn + j)),
                  pl.BlockSpec((tm, tn), lambda i, j: (i, (C_GL + D_MODEL) // tn + j))],
        out_specs=pl.BlockSpec((tm, tn), lambda i, j: (i, j)),
        out_shape=jax.ShapeDtypeStruct((m, D_MODEL), BF16),
        compiler_params=_cparams(("parallel", "arbitrary")),
        name="merge",
    )(attn_o, gdn_o, w_oa, w_og, p32, p32)


def _outproj_kernel(mg_ref, w_ref, x_ref, g1_ref, sc_ref, sh_ref, gn_ref, wr_ref, br_ref,
                    x1_ref, h2_ref, ti_ref, tg_ref):
    x1 = x_ref[...] + g1_ref[...] * jnp.dot(mg_ref[...], w_ref[...], preferred_element_type=F32)
    x1_ref[...] = x1
    ms = jnp.mean(x1 * x1, axis=-1, keepdims=True)
    h2 = x1 * lax.rsqrt(ms + NORM_EPS) * gn_ref[...] * (1.0 + sc_ref[...]) + sh_ref[...]
    h2_ref[...] = h2
    logits = _dot3(h2, wr_ref[...]) + br_ref[...]
    tm = logits.shape[0]
    lane = lax.broadcasted_iota(I32, (tm, LANE), 1)
    lanef = lane.astype(F32)
    l = jnp.where(lane < N_EXPERTS, logits, -jnp.inf)
    vals, idxs = [], []
    for _ in range(TOP_K):
        mx = jnp.max(l, axis=1, keepdims=True)
        ix = jnp.min(jnp.where(l == mx, lanef, float(LANE)), axis=1, keepdims=True)
        vals.append(mx)
        idxs.append(ix)
        l = jnp.where(lanef == ix, -jnp.inf, l)
    es = [jnp.exp(v - vals[0]) for v in vals]
    den = es[0]
    for e in es[1:]:
        den = den + e
    ti = jnp.zeros((tm, LANE), F32)
    tg = jnp.zeros((tm, LANE), F32)
    for kk in range(TOP_K):
        ti = jnp.where(lane == kk, idxs[kk], ti)
        tg = jnp.where(lane == kk, es[kk] / den, tg)
    ti_ref[...] = ti.astype(I32)
    tg_ref[...] = tg


def _outproj(merged, w_out, x, gate1, scale2, shift2, g_norm2, w_router, b_router, rows_per_mod, tm):
    m = x.shape[0]
    mod_rows = gate1.shape[1]
    tiles_per_mod = rows_per_mod // tm
    if mod_rows == 1:
        mod_spec = pl.BlockSpec((None, 1, D_MODEL), lambda i: (i // tiles_per_mod, 0, 0))
    else:
        mod_spec = pl.BlockSpec((None, tm, D_MODEL), lambda i: (i // tiles_per_mod, i % tiles_per_mod, 0))
    wr = jnp.pad(w_router.astype(F32), ((0, 0), (0, LANE - N_EXPERTS)))
    br = jnp.pad(b_router.astype(F32), (0, LANE - N_EXPERTS)).reshape(1, LANE)
    row = pl.BlockSpec((tm, D_MODEL), lambda i: (i, 0))
    small = pl.BlockSpec((tm, LANE), lambda i: (i, 0))
    return pl.pallas_call(
        _outproj_kernel,
        grid=(m // tm,),
        in_specs=[row, pl.BlockSpec((D_MODEL, D_MODEL), lambda i: (0, 0)), row, mod_spec, mod_spec, mod_spec,
                  pl.BlockSpec((1, D_MODEL), lambda i: (0, 0)),
                  pl.BlockSpec((D_MODEL, LANE), lambda i: (0, 0)),
                  pl.BlockSpec((1, LANE), lambda i: (0, 0))],
        out_specs=[row, row, small, small],
        out_shape=[jax.ShapeDtypeStruct((m, D_MODEL), F32), jax.ShapeDtypeStruct((m, D_MODEL), F32),
                   jax.ShapeDtypeStruct((m, LANE), I32), jax.ShapeDtypeStruct((m, LANE), F32)],
        compiler_params=_cparams(("parallel",)),
        name="outproj",
    )(merged, w_out, x, gate1, scale2, shift2, g_norm2.reshape(1, D_MODEL), wr, br)


GATHER_ROWS = 256


def _row_copy(src_ref, o_ref, sem, src_row, dst_row):
    return pltpu.make_async_copy(src_ref.at[pl.ds(src_row, 1)], o_ref.at[pl.ds(dst_row, 1)], sem)


def _gather_kernel(idx_ref, src_ref, o_ref, sem):
    n = o_ref.shape[0]

    def issue(r, carry):
        _row_copy(src_ref, o_ref, sem, idx_ref[0, r], r).start()
        return carry

    def wait(r, carry):
        _row_copy(src_ref, o_ref, sem, 0, r).wait()
        return carry

    lax.fori_loop(0, n, issue, 0, unroll=8)
    lax.fori_loop(0, n, wait, 0, unroll=8)


def _gather_rows(src, idx):
    n_valid, w = idx.shape[0], src.shape[1]
    n_steps = -(-n_valid // GATHER_ROWS)
    n = n_steps * GATHER_ROWS
    idx = jnp.pad(idx, (0, n - n_valid))
    out = pl.pallas_call(
        _gather_kernel,
        grid=(n_steps,),
        in_specs=[pl.BlockSpec((None, 1, GATHER_ROWS), lambda i: (i, 0, 0), memory_space=pltpu.SMEM),
                  pl.BlockSpec(memory_space=pl.ANY)],
        out_specs=pl.BlockSpec((GATHER_ROWS, w), lambda i: (i, 0)),
        out_shape=jax.ShapeDtypeStruct((n, w), src.dtype),
        scratch_shapes=[pltpu.SemaphoreType.DMA(())],
        compiler_params=pltpu.CompilerParams(dimension_semantics=("arbitrary",), vmem_limit_bytes=VMEM_LIMIT,
                                             disable_bounds_checks=True),
        name="gather_rows",
    )(idx.reshape(n_steps, 1, GATHER_ROWS), src)
    return out if n == n_valid else out[:n_valid]


def _moe_kernel(be_ref, nu_ref, x_ref, wg_ref, wu_ref, wd_ref, bg_ref, bu_ref, bd_ref, o_ref):
    i = pl.program_id(0)
    j = pl.program_id(1)
    used = i < nu_ref[0]
    del be_ref

    @pl.when(used)
    def _():
        x = x_ref[...].astype(BF16)
        g = jnp.dot(x, wg_ref[...].astype(BF16), preferred_element_type=F32) + bg_ref[...]
        u = jnp.dot(x, wu_ref[...].astype(BF16), preferred_element_type=F32) + bu_ref[...]
        gate = jnp.minimum(g, SWIGLU_LIMIT)
        up = jnp.clip(u, -SWIGLU_LIMIT, SWIGLU_LIMIT)
        hdn = (up + 1.0) * gate * _sigmoid(SWIGLU_ALPHA * gate)
        y = jnp.dot(hdn.astype(BF16), wd_ref[...].astype(BF16), preferred_element_type=F32)

        @pl.when(j == 0)
        def _():
            o_ref[...] = y + bd_ref[...]

        @pl.when(j != 0)
        def _():
            o_ref[...] += y

    @pl.when(jnp.logical_not(used) & (j == 0))
    def _():
        o_ref[...] = jnp.zeros_like(o_ref)


def _moe_ffn(xs, blk_e, n_used, w_gu, b_gu, w_dn, b_dn, rb, tf):
    n_rows = xs.shape[0]
    n_blocks = n_rows // rb
    n_f = D_FF // tf
    bgu = b_gu.reshape(N_EXPERTS, 1, 2 * D_FF)
    bdn = b_dn.reshape(N_EXPERTS, 1, D_MODEL)

    def jj(i, j, nu):
        return jnp.where(i < nu[0], j, 0)

    grid_spec = pltpu.PrefetchScalarGridSpec(
        num_scalar_prefetch=2, grid=(n_blocks, n_f),
        in_specs=[pl.BlockSpec((rb, D_MODEL), lambda i, j, be, nu: (i, 0)),
                  pl.BlockSpec((None, None, D_MODEL, tf), lambda i, j, be, nu: (0, be[i], 0, jj(i, j, nu))),
                  pl.BlockSpec((None, None, D_MODEL, tf), lambda i, j, be, nu: (0, be[i], 0, n_f + jj(i, j, nu))),
                  pl.BlockSpec((None, None, tf, D_MODEL), lambda i, j, be, nu: (0, be[i], jj(i, j, nu), 0)),
                  pl.BlockSpec((None, 1, tf), lambda i, j, be, nu: (be[i], 0, jj(i, j, nu))),
                  pl.BlockSpec((None, 1, tf), lambda i, j, be, nu: (be[i], 0, n_f + jj(i, j, nu))),
                  pl.BlockSpec((None, 1, D_MODEL), lambda i, j, be, nu: (be[i], 0, 0))],
        out_specs=pl.BlockSpec((rb, D_MODEL), lambda i, j, be, nu: (i, 0)))
    return pl.pallas_call(
        _moe_kernel, grid_spec=grid_spec,
        out_shape=jax.ShapeDtypeStruct((n_rows, D_MODEL), F32),
        compiler_params=_cparams(("parallel", "arbitrary")),
        name="moe_ffn",
    )(blk_e, n_used, xs, w_gu, w_gu, w_dn, bgu, bgu, bdn)


def _final_kernel(x1_ref, yg_ref, tg_ref, g2_ref, gf_ref, o_ref):
    tg = tg_ref[...]
    ffn = jnp.zeros_like(x1_ref)
    for kk in range(TOP_K):
        ffn = ffn + tg[:, kk:kk + 1] * yg_ref[:, kk * D_MODEL:(kk + 1) * D_MODEL]
    x2 = x1_ref[...] + g2_ref[...] * ffn
    ms = jnp.mean(x2 * x2, axis=-1, keepdims=True)
    o_ref[...] = x2 * lax.rsqrt(ms + NORM_EPS) * gf_ref[...]


def _final(x1, yg, tg, gate2, g_final, rows_per_mod, tm):
    m = x1.shape[0]
    mod_rows = gate2.shape[1]
    tiles_per_mod = rows_per_mod // tm
    if mod_rows == 1:
        mod_spec = pl.BlockSpec((None, 1, D_MODEL), lambda i: (i // tiles_per_mod, 0, 0))
    else:
        mod_spec = pl.BlockSpec((None, tm, D_MODEL), lambda i: (i // tiles_per_mod, i % tiles_per_mod, 0))
    row = pl.BlockSpec((tm, D_MODEL), lambda i: (i, 0))
    return pl.pallas_call(
        _final_kernel,
        grid=(m // tm,),
        in_specs=[row, pl.BlockSpec((tm, TOP_K * D_MODEL), lambda i: (i, 0)),
                  pl.BlockSpec((tm, LANE), lambda i: (i, 0)), mod_spec,
                  pl.BlockSpec((1, D_MODEL), lambda i: (0, 0))],
        out_specs=row,
        out_shape=jax.ShapeDtypeStruct((m, D_MODEL), F32),
        compiler_params=_cparams(("parallel",)),
        name="final",
    )(x1, yg, tg, gate2, g_final.reshape(1, D_MODEL))


def _route(topi, rb):
    tt = topi.shape[0]
    sel = jnp.sum((topi[:, :, None] == jnp.arange(N_EXPERTS, dtype=I32)[None, None, :]).astype(I32), axis=1)
    counts = jnp.sum(sel, axis=0)
    padded = (counts + rb - 1) // rb * rb
    pad_end = jnp.cumsum(padded)
    pad_start = pad_end - padded
    rank = jnp.cumsum(sel, axis=0) - sel
    dest = jnp.take_along_axis(pad_start[None, :] + rank, topi, axis=1)
    n_blocks = -(-(tt * TOP_K + N_EXPERTS * (rb - 1)) // rb)
    n_rows = n_blocks * rb
    tok = jnp.repeat(jnp.arange(tt, dtype=I32), TOP_K)
    row_tok = jnp.zeros((n_rows,), I32).at[dest.reshape(-1)].set(tok)
    blk_e = jnp.minimum(jnp.searchsorted(pad_end, jnp.arange(n_blocks, dtype=I32) * rb, side="right"),
                        N_EXPERTS - 1).astype(I32)
    n_used = (pad_end[-1] // rb).astype(I32).reshape(1)
    return dest, row_tok, blk_e, n_used


def kernel(x_prompt, x_sample, c_prompt, c_sample, cache_k, cache_v, cache_kidx, state_gdn, state_conv, page_table,
           w_ada, b_ada, g_norm1, g_norm2, g_final, w_in, gdn_conv_w, gdn_a_log, gdn_dt_bias, gdn_norm_w,
           w_o_attn, w_o_gdn, w_out, w_router, b_router, w_gu, b_gu, w_dn, b_dn):
    bsz, t, d = x_prompt.shape
    db, ts, _ = x_sample.shape
    depth = w_ada.shape[0]
    assert depth == 1 and d == D_MODEL
    past = page_table.shape[1] * PAGE_SIZE
    conv_dim = gdn_conv_w.shape[-1]

    n_c = bsz + db
    n_cp = -(-n_c // 8) * 8
    c_all = jnp.pad(jnp.concatenate([c_prompt, c_sample], axis=0), ((0, n_cp - n_c), (0, 0)))
    mod = _ada(c_all, w_ada, b_ada[0])
    mod_p = mod[:bsz].reshape(bsz, 1, 6, d)
    mod_s = jnp.repeat(mod[bsz:n_c], ts, axis=0).reshape(1, db * ts, 6, d)
    mods = {"p": [mod_p[:, :, i] for i in range(6)], "s": [mod_s[:, :, i] for i in range(6)]}

    w_r = _reorder_w_in(w_in[0])
    w_oa = w_o_attn[0].astype(BF16)
    w_og = w_o_gdn[0].astype(BF16)
    w_o = w_out[0].astype(BF16)

    xp = x_prompt.reshape(bsz * t, d)
    xs = x_sample.reshape(db * ts, d)
    tm_p = min(512, t)
    tm_s = db * ts

    tabs_p = _rot_tables(jnp.arange(t, dtype=I32))
    tabs_s = _rot_tables(jnp.tile(past + jnp.arange(ts, dtype=I32), db))
    p32_p, p16_p = _inproj(xp, g_norm1[0], mods["p"][1], mods["p"][0], w_r, tabs_p, t, tm_p)
    p32_s, _ = _inproj(xs, g_norm1[0], mods["s"][1], mods["s"][0], w_r, tabs_s, db * ts, tm_s)

    attn_p = _attn_prompt(p32_p, p16_p, bsz, t)
    attn_s = _attn_sample(p32_s, cache_k[0], cache_v[0], cache_kidx[0], page_table, ts)

    conv0_p = jnp.zeros((bsz, 8, conv_dim), F32)
    s0_p = jnp.zeros((bsz, GDN_V_HEADS, GDN_DK, GDN_DV), F32)
    gdn_p, sfin_p = _gdn(p32_p, conv0_p, s0_p, gdn_conv_w[0], gdn_a_log[0], gdn_dt_bias[0], gdn_norm_w[0], bsz, t)
    gdn_s, sfin_s = _gdn_sample(p32_s, state_conv[0], state_gdn[0], gdn_conv_w[0], gdn_a_log[0], gdn_dt_bias[0],
                                gdn_norm_w[0], db, ts)

    outs = {}
    for name, attn_o, gdn_o, p32, x, rows_per_mod, tm in (("p", attn_p, gdn_p, p32_p, xp, t, min(256, t)),
                                                          ("s", attn_s, gdn_s, p32_s, xs, db * ts, min(256, db * ts))):
        merged = _merge(attn_o, gdn_o, w_oa, w_og, p32, tm)
        m = mods[name]
        outs[name] = _outproj(merged, w_o, x, m[2], m[4], m[3], g_norm2[0], w_router[0], b_router[0], rows_per_mod, tm)

    rb, tf = 512, 256
    h2 = jnp.concatenate([outs["p"][1], outs["s"][1]], axis=0)
    topi = jnp.concatenate([outs["p"][2], outs["s"][2]], axis=0)[:, :TOP_K]
    dest, row_tok, blk_e, n_used = _route(topi, rb)
    yb = _moe_ffn(_gather_rows(h2, row_tok), blk_e, n_used, w_gu, b_gu[0], w_dn, b_dn[0], rb, tf)
    yg = _gather_rows(yb, dest.reshape(-1)).reshape(-1, TOP_K * d)

    n_p = bsz * t
    y_p = _final(outs["p"][0], yg[:n_p], outs["p"][3], mods["p"][5], g_final, t, min(256, t))
    y_s = _final(outs["s"][0], yg[n_p:], outs["s"][3], mods["s"][5], g_final, db * ts, min(256, db * ts))

    kv = N_KV_HEADS * HEAD_DIM
    new_k_p = p32_p[:, C_AK:C_AK + kv].reshape(1, bsz, t, N_KV_HEADS, HEAD_DIM)
    new_v_p = p32_p[:, C_AV:C_AV + kv].reshape(1, bsz, t, N_KV_HEADS, HEAD_DIM)
    new_ki_p = p32_p[:, C_MISC:C_MISC + IDX_DIM].reshape(1, bsz, t, IDX_DIM)
    new_k_s = p32_s[:, C_AK:C_AK + kv].reshape(1, db, ts, N_KV_HEADS, HEAD_DIM)
    new_v_s = p32_s[:, C_AV:C_AV + kv].reshape(1, db, ts, N_KV_HEADS, HEAD_DIM)
    new_ki_s = p32_s[:, C_MISC:C_MISC + IDX_DIM].reshape(1, db, ts, IDX_DIM)
    pre_p = p32_p[:, C_GQ:C_GQ + conv_dim].reshape(bsz, t, conv_dim)
    conv_p = pre_p[:, t - (CONV_W - 1):][None]
    pre_s = p32_s[:, C_GQ:C_GQ + conv_dim].reshape(db, ts, conv_dim)
    conv_s = jnp.concatenate([state_conv[0], pre_s], axis=1)[:, ts:][None]
    return (y_p.reshape(bsz, t, d), y_s.reshape(db, ts, d), new_k_p, new_v_p, new_ki_p, conv_p, sfin_p[None],
            new_k_s, new_v_s, new_ki_s, conv_s, sfin_s[None])
```

```python
import functools

import jax
import jax.numpy as jnp
from jax import lax
from jax.experimental import pallas as pl
from jax.experimental.pallas import tpu as pltpu

F32 = jnp.float32
BF16 = jnp.bfloat16
I32 = jnp.int32

D_MODEL = 2048
N_HEADS = 16
N_KV_HEADS = 4
HEAD_DIM = 128
ROPE_THETA = 500000.0
ROT_FRACTION_DIV = 4
IDX_HEADS = 16
IDX_DIM = 64
TOPK_MAX = 256
GDN_QK_HEADS = 8
GDN_V_HEADS = 16
GDN_DK = 128
GDN_DV = 128
CONV_W = 4
N_EXPERTS = 32
TOP_K = 4
D_FF = D_MODEL
SWIGLU_LIMIT = 7.0
SWIGLU_ALPHA = 1.702
NORM_EPS = 1e-6
PAGE_SIZE = 128
LANE = 128

C_AQ, C_AK, C_AV, C_IQ = 0, 2048, 2560, 3072
C_GQ, C_GK, C_GV, C_GZ, C_GL = 4096, 5120, 6144, 8192, 10240
C_MISC = 14336
L_IW, L_GB, L_GA = 64, 80, 96
PROJ_TN = 512
PROJ_W = 14848
MISC_BLK = C_MISC // LANE

VMEM_LIMIT = 48 * 1024 * 1024
MOE_VMEM_LIMIT = 56 * 1024 * 1024
NEG_BIG = -1e30
INT_MIN = -2147483648

NT_DIMS = (((1,), (1,)), ((), ()))


def _cparams(sem):
    return pltpu.CompilerParams(dimension_semantics=sem, vmem_limit_bytes=VMEM_LIMIT)


def _split_bf16(a):
    hi = a.astype(BF16)
    lo = (a - hi.astype(F32)).astype(BF16)
    return hi, lo


def _dot3(a, b, dims=None):
    ah, al = _split_bf16(a)
    bh, bl = _split_bf16(b)
    if dims is None:
        d = lambda x, y: jnp.dot(x, y, preferred_element_type=F32)
    else:
        d = lambda x, y: lax.dot_general(x, y, dims, preferred_element_type=F32)
    return d(ah, bh) + d(al, bh) + d(ah, bl)


def _sigmoid(x):
    return 1.0 / (1.0 + jnp.exp(-x))


def _ada_kernel(c_ref, w_ref, b_ref, o_ref):
    c = c_ref[...]
    a = (c * _sigmoid(c)).astype(BF16)
    o_ref[...] = jnp.dot(a, w_ref[...].astype(BF16), preferred_element_type=F32) + b_ref[...]


def _ada(c, w, b):
    m, k = c.shape
    n = w.shape[2]
    tn = 1024
    return pl.pallas_call(
        _ada_kernel,
        grid=(n // tn,),
        in_specs=[pl.BlockSpec((m, k), lambda j: (0, 0)),
                  pl.BlockSpec((None, k, tn), lambda j: (0, 0, j)),
                  pl.BlockSpec((1, tn), lambda j: (0, j))],
        out_specs=pl.BlockSpec((m, tn), lambda j: (0, j)),
        out_shape=jax.ShapeDtypeStruct((m, n), F32),
        compiler_params=_cparams(("arbitrary",)),
        name="ada",
    )(c, w, b.reshape(1, n))


def _rot_slab(x, tabs, half):
    a, b, c = tabs
    return x * a + pltpu.roll(x, LANE - half, 1) * b + pltpu.roll(x, half, 1) * c


def _inproj_kernel(x_ref, g_ref, sc_ref, sh_ref, w_ref, tab_ref, o32_ref, o16_ref, h_scr):
    j = pl.program_id(1)

    @pl.when(j == 0)
    def _():
        x = x_ref[...]
        ms = jnp.mean(x * x, axis=-1, keepdims=True)
        y = x * lax.rsqrt(ms + NORM_EPS) * g_ref[...]
        h_scr[...] = (y * (1.0 + sc_ref[...]) + sh_ref[...]).astype(BF16)

    acc = jnp.dot(h_scr[...], w_ref[...], preferred_element_type=F32)
    n_slab = PROJ_TN // LANE

    def store(fn):
        for s in range(n_slab):
            v = fn(acc[:, s * LANE:(s + 1) * LANE])
            o32_ref[:, s * LANE:(s + 1) * LANE] = v
            o16_ref[:, s * LANE:(s + 1) * LANE] = v.astype(BF16)

    rot_head = j < 5
    rot_idx = ((j >= 6) & (j < 8)) | (j == C_MISC // PROJ_TN)

    @pl.when(rot_head)
    def _():
        tabs = (tab_ref[0], tab_ref[1], tab_ref[2])
        store(lambda v: _rot_slab(v, tabs, HEAD_DIM // ROT_FRACTION_DIV // 2))

    @pl.when(rot_idx)
    def _():
        tabs = (tab_ref[0], tab_ref[1], tab_ref[2])
        store(lambda v: _rot_slab(v, tabs, IDX_DIM // ROT_FRACTION_DIV // 2))

    @pl.when(jnp.logical_not(rot_head | rot_idx))
    def _():
        store(lambda v: v)


def _tab_type(j):
    return jnp.where(j < 6, 0, jnp.where(j < 28, 1, 2))


def _inproj(x, g_norm, scale, shift, w_r, tabs, rows_per_mod, tm):
    m = x.shape[0]
    tt = tabs.shape[2]
    n_t = tt // tm
    mod_rows = scale.shape[1]
    tiles_per_mod = rows_per_mod // tm
    if mod_rows == 1:
        mod_spec = pl.BlockSpec((None, 1, D_MODEL), lambda i, j: (i // tiles_per_mod, 0, 0))
    else:
        mod_spec = pl.BlockSpec((None, tm, D_MODEL), lambda i, j: (i // tiles_per_mod, i % tiles_per_mod, 0))
    return pl.pallas_call(
        _inproj_kernel,
        grid=(m // tm, PROJ_W // PROJ_TN),
        in_specs=[pl.BlockSpec((tm, D_MODEL), lambda i, j: (i, 0)),
                  pl.BlockSpec((1, D_MODEL), lambda i, j: (0, 0)),
                  mod_spec, mod_spec,
                  pl.BlockSpec((D_MODEL, PROJ_TN), lambda i, j: (0, j)),
                  pl.BlockSpec((None, 3, tm, LANE), lambda i, j: (_tab_type(j), 0, i % n_t, 0))],
        out_specs=[pl.BlockSpec((tm, PROJ_TN), lambda i, j: (i, j)),
                   pl.BlockSpec((tm, PROJ_TN), lambda i, j: (i, j))],
        out_shape=[jax.ShapeDtypeStruct((m, PROJ_W), F32), jax.ShapeDtypeStruct((m, PROJ_W), BF16)],
        scratch_shapes=[pltpu.VMEM((tm, D_MODEL), BF16)],
        compiler_params=_cparams(("parallel", "arbitrary")),
        name="inproj",
    )(x, g_norm.reshape(1, D_MODEL), scale, shift, w_r, tabs)


def _rot_tables(pos):
    tt = pos.shape[0]
    posf = pos.astype(F32)

    def tab(rot, period, scale):
        half = rot // 2
        inv = ROPE_THETA ** (-jnp.arange(half, dtype=F32) * (2.0 / rot))
        ang = posf[:, None] * inv[None, :]
        cos, sin = jnp.cos(ang), jnp.sin(ang)
        zh = jnp.zeros((tt, half), F32)
        rest = period - rot
        a = jnp.concatenate([cos, cos, jnp.ones((tt, rest), F32)], axis=1) * scale
        b = jnp.concatenate([-sin, zh, jnp.zeros((tt, rest), F32)], axis=1) * scale
        c = jnp.concatenate([zh, sin, jnp.zeros((tt, rest), F32)], axis=1) * scale
        return a, b, c

    head = jnp.stack([jnp.tile(t, (1, LANE // HEAD_DIM)) for t in tab(HEAD_DIM // ROT_FRACTION_DIV, HEAD_DIM, 1.0)])
    idxq = jnp.stack([jnp.tile(t, (1, LANE // IDX_DIM))
                      for t in tab(IDX_DIM // ROT_FRACTION_DIV, IDX_DIM, IDX_DIM ** -0.5)])
    ka, kb, kc = tab(IDX_DIM // ROT_FRACTION_DIV, IDX_DIM, 1.0)
    rest = LANE - IDX_DIM
    lane = jnp.arange(rest)
    tail_a = jnp.where(lane < (L_GB - L_IW), IDX_HEADS ** -0.5, 1.0).astype(F32)
    ma = jnp.concatenate([ka, jnp.broadcast_to(tail_a, (tt, rest))], axis=1)
    mb = jnp.concatenate([kb, jnp.zeros((tt, rest), F32)], axis=1)
    mc = jnp.concatenate([kc, jnp.zeros((tt, rest), F32)], axis=1)
    misc = jnp.stack([ma, mb, mc])
    return jnp.stack([head, idxq, misc])


def _reorder_w_in(w_in):
    offs = {}
    o = 0
    for name, wdt in (("aq", 2048), ("ak", 512), ("av", 512), ("iq", 1024), ("ik", 64), ("iw", 16), ("gq", 1024),
                      ("gk", 1024), ("gv", 2048), ("gz", 2048), ("gb", 16), ("ga", 16), ("gl", 4096)):
        offs[name] = (o, wdt)
        o += wdt

    def seg(name):
        s, wdt = offs[name]
        return w_in[:, s:s + wdt]

    k = w_in.shape[0]
    misc = jnp.concatenate([seg("ik"), seg("iw"), seg("gb"), seg("ga"), jnp.zeros((k, LANE - 112), w_in.dtype)], axis=1)
    parts = [seg("aq"), seg("ak"), seg("av"), seg("iq"), seg("gq"), seg("gk"), seg("gv"), seg("gz"), seg("gl"), misc,
             jnp.zeros((k, PROJ_W - C_MISC - LANE), w_in.dtype)]
    return jnp.concatenate(parts, axis=1).astype(BF16)


def _order_key(score):
    bits = pltpu.bitcast(score, I32)
    return bits ^ ((bits >> 31) & 0x7FFFFFFF)


def _kth_largest_key(count_ge, n_rows, n_sel):
    def body(i, t):
        bit = lax.shift_left(jnp.int32(1), 31 - i)
        cand = t ^ bit
        return jnp.where(count_ge(cand) >= n_sel, cand, t)

    return lax.fori_loop(0, 32, body, jnp.full((n_rows, 1), INT_MIN, I32))


def _attn_prompt_kernel(q_ref, iq_ref, mq_ref, k_ref, v_ref, mk_ref, o_ref, key_scr, *, tq, kc, n_sel):
    qb = pl.program_id(1)
    q0 = qb * tq
    nkc = (q0 + tq + kc - 1) // kc
    qpos = q0 + lax.broadcasted_iota(I32, (tq, 1), 0)
    kiota = lax.broadcasted_iota(I32, (1, kc), 1)
    wq = mq_ref[...]

    def score_body(c, carry):
        ks = pl.multiple_of(c * kc, kc)
        kt = mk_ref[pl.ds(ks, kc), :][:, :IDX_DIM]
        acc = jnp.zeros((tq, kc), F32)
        for h in range(IDX_HEADS):
            s = lax.dot_general(iq_ref[:, h * IDX_DIM:(h + 1) * IDX_DIM], kt, NT_DIMS, preferred_element_type=F32)
            acc = acc + wq[:, L_IW + h:L_IW + h + 1] * jnp.maximum(s, 0.0)
        acc = jnp.where(ks + kiota <= qpos, acc, -jnp.inf)
        key_scr[c] = _order_key(acc)
        return carry

    lax.fori_loop(0, nkc, score_body, 0)

    def count_ge(cand):
        def body(c, acc):
            m = jnp.where(key_scr[c] >= cand, 1.0, 0.0)
            for s in range(kc // LANE):
                acc = acc + m[:, s * LANE:(s + 1) * LANE]
            return acc

        acc = lax.fori_loop(0, nkc, body, jnp.zeros((tq, LANE), F32))
        return jnp.sum(acc, axis=1, keepdims=True)

    thr = _kth_largest_key(count_ge, tq, float(n_sel))

    group = N_HEADS // N_KV_HEADS
    scale = HEAD_DIM ** -0.5
    for n in range(N_KV_HEADS):
        qs = jnp.concatenate(
            [q_ref[:, (n * group + g) * HEAD_DIM:(n * group + g + 1) * HEAD_DIM] for g in range(group)], axis=0)
        qs = (qs.astype(F32) * scale).astype(BF16)

        def attn_body(c, carry, n=n, qs=qs):
            m_i, l_i, acc = carry
            ks = pl.multiple_of(c * kc, kc)
            kn = k_ref[pl.ds(ks, kc), n * HEAD_DIM:(n + 1) * HEAD_DIM]
            vn = v_ref[pl.ds(ks, kc), n * HEAD_DIM:(n + 1) * HEAD_DIM]
            s = lax.dot_general(qs, kn, NT_DIMS, preferred_element_type=F32)
            sel = (key_scr[c] >= thr) & (ks + kiota <= qpos)
            s = jnp.where(sel[None], s.reshape(group, tq, kc), NEG_BIG).reshape(group * tq, kc)
            m_new = jnp.maximum(m_i, jnp.max(s, axis=1, keepdims=True))
            alpha = jnp.exp(m_i - m_new)
            p = jnp.exp(s - m_new)
            l_new = alpha * l_i + jnp.sum(p, axis=1, keepdims=True)
            acc = alpha * acc + jnp.dot(p.astype(BF16), vn, preferred_element_type=F32)
            return m_new, l_new, acc

        init = (jnp.full((group * tq, 1), NEG_BIG, F32), jnp.zeros((group * tq, 1), F32),
                jnp.zeros((group * tq, HEAD_DIM), F32))
        _, l_f, acc_f = lax.fori_loop(0, nkc, attn_body, init)
        out = acc_f / l_f
        for g in range(group):
            o_ref[:, (n * group + g) * HEAD_DIM:(n * group + g + 1) * HEAD_DIM] = out[g * tq:(g + 1) * tq].astype(BF16)


def _attn_prompt(p32, p16, bsz, t, tq=128, kc=512):
    nq = t // tq
    kc = min(kc, t)
    n_sel = min(TOPK_MAX, t // 4)
    kern = functools.partial(_attn_prompt_kernel, tq=tq, kc=kc, n_sel=n_sel)
    return pl.pallas_call(
        kern,
        grid=(bsz, nq),
        in_specs=[pl.BlockSpec((tq, N_HEADS * HEAD_DIM), lambda b, i: (b * nq + i, 0)),
                  pl.BlockSpec((tq, IDX_HEADS * IDX_DIM), lambda b, i: (b * nq + i, C_IQ // (IDX_HEADS * IDX_DIM))),
                  pl.BlockSpec((tq, LANE), lambda b, i: (b * nq + i, MISC_BLK)),
                  pl.BlockSpec((t, N_KV_HEADS * HEAD_DIM), lambda b, i: (b, C_AK // (N_KV_HEADS * HEAD_DIM))),
                  pl.BlockSpec((t, N_KV_HEADS * HEAD_DIM), lambda b, i: (b, C_AV // (N_KV_HEADS * HEAD_DIM))),
                  pl.BlockSpec((t, LANE), lambda b, i: (b, MISC_BLK))],
        out_specs=pl.BlockSpec((tq, N_HEADS * HEAD_DIM), lambda b, i: (b * nq + i, 0)),
        out_shape=jax.ShapeDtypeStruct((bsz * t, N_HEADS * HEAD_DIM), BF16),
        scratch_shapes=[pltpu.VMEM((t // kc, tq, kc), I32)],
        compiler_params=_cparams(("parallel", "arbitrary")),
        name="attn_prompt",
    )(p16, p16, p32, p16, p16, p16)


def _attn_sample_kernel(pt_ref, iq_ref, wc_ref, q_ref, kn_ref, vn_ref, ikn_ref, *rest, n_pages, n_sel, past):
    kidx_refs = rest[:n_pages]
    k_refs = rest[n_pages:2 * n_pages]
    v_refs = rest[2 * n_pages:3 * n_pages]
    o_ref, sc_scr, selx_scr, s_scr = rest[3 * n_pages:]
    del pt_ref
    rows, nkv = 8, N_KV_HEADS
    n_keys = (n_pages + 1) * PAGE_SIZE
    pw = PAGE_SIZE * nkv
    iq = iq_ref[...].astype(BF16)
    wc = wc_ref[...]

    def idx_score(kt):
        s = lax.dot_general(iq, kt.astype(BF16), NT_DIMS, preferred_element_type=F32)
        x = jnp.maximum(s, 0.0) * wc
        return jnp.sum(x.reshape(IDX_HEADS, rows, PAGE_SIZE), axis=0)

    for p in range(n_pages + 1):
        sl = slice(p * PAGE_SIZE, (p + 1) * PAGE_SIZE)
        sc_scr[:, sl] = idx_score(kidx_refs[p][...] if p < n_pages else ikn_ref[...])

    qpos = past + lax.broadcasted_iota(I32, (rows, 1), 0)
    kpos = lax.broadcasted_iota(I32, (1, n_keys), 1)
    causal = kpos <= qpos
    keys = _order_key(jnp.where(causal, sc_scr[...], -jnp.inf))

    def count_ge(cand):
        return jnp.sum(jnp.where(keys >= cand, 1.0, 0.0), axis=1, keepdims=True)

    thr = _kth_largest_key(count_ge, rows, float(n_sel))
    sel = jnp.where((keys >= thr) & causal, 1.0, 0.0).astype(BF16)
    expand = jnp.where((lax.broadcasted_iota(I32, (PAGE_SIZE, pw), 1) >> 2)
                       == lax.broadcasted_iota(I32, (PAGE_SIZE, pw), 0), 1.0, 0.0).astype(BF16)
    for p in range(n_pages + 1):
        selx_scr[:, p * pw:(p + 1) * pw] = jnp.dot(sel[:, p * PAGE_SIZE:(p + 1) * PAGE_SIZE], expand,
                                                   preferred_element_type=F32)
    n_q = N_HEADS * rows
    own_head = ((lax.broadcasted_iota(I32, (n_q, pw), 0) >> 5)
                == (lax.broadcasted_iota(I32, (n_q, pw), 1) & (nkv - 1))).reshape(N_HEADS, rows, pw)
    q = (q_ref[...] * (HEAD_DIM ** -0.5)).astype(BF16)
    m_parts = []
    for p in range(n_pages + 1):
        kb = (k_refs[p] if p < n_pages else kn_ref)[...].astype(BF16)
        s = lax.dot_general(q, kb, NT_DIMS, preferred_element_type=F32)
        ok = own_head & (selx_scr[:, p * pw:(p + 1) * pw] > 0.5)[None]
        s = jnp.where(ok, s.reshape(N_HEADS, rows, pw), NEG_BIG).reshape(n_q, pw)
        s_scr[:, p * pw:(p + 1) * pw] = s
        m_parts.append(jnp.max(s, axis=1, keepdims=True))
    m = m_parts[0]
    for mp in m_parts[1:]:
        m = jnp.maximum(m, mp)
    l_parts, o_parts = [], []
    for p in range(n_pages + 1):
        vb = (v_refs[p] if p < n_pages else vn_ref)[...].astype(BF16)
        pr = jnp.exp(s_scr[:, p * pw:(p + 1) * pw] - m)
        l_parts.append(jnp.sum(pr, axis=1, keepdims=True))
        o_parts.append(jnp.dot(pr.astype(BF16), vb, preferred_element_type=F32))
    l = l_parts[0]
    acc = o_parts[0]
    for lp, op in zip(l_parts[1:], o_parts[1:]):
        l = l + lp
        acc = acc + op
    o_ref[...] = acc / l


def _attn_sample(p32, cache_k, cache_v, cache_kidx, page_table, ts):
    db, n_pages = page_table.shape
    past = n_pages * PAGE_SIZE
    rows = 8
    group = N_HEADS // N_KV_HEADS
    ps = p32.reshape(db, ts, PROJ_W)
    pad_t = ((0, 0), (0, 0), (0, rows - ts), (0, 0))
    iq = ps[:, :, C_IQ:C_IQ + IDX_HEADS * IDX_DIM].reshape(db, ts, IDX_HEADS, IDX_DIM).transpose(0, 2, 1, 3)
    iq = jnp.pad(iq, pad_t).reshape(db, IDX_HEADS * rows, IDX_DIM)
    wc = ps[:, :, C_MISC + L_IW:C_MISC + L_IW + IDX_HEADS].transpose(0, 2, 1)
    wc = jnp.pad(wc, ((0, 0), (0, 0), (0, rows - ts))).reshape(db, IDX_HEADS * rows, 1)
    q = ps[:, :, :N_HEADS * HEAD_DIM].reshape(db, ts, N_HEADS, HEAD_DIM).transpose(0, 2, 1, 3)
    q = jnp.pad(q, pad_t).reshape(db, N_HEADS * rows, HEAD_DIM)
    pad_k = ((0, 0), (0, PAGE_SIZE - ts), (0, 0))
    pw = PAGE_SIZE * N_KV_HEADS
    kn = jnp.pad(ps[:, :, C_AK:C_AK + N_KV_HEADS * HEAD_DIM], pad_k).reshape(db, pw, HEAD_DIM)
    vn = jnp.pad(ps[:, :, C_AV:C_AV + N_KV_HEADS * HEAD_DIM], pad_k).reshape(db, pw, HEAD_DIM)
    ikn = jnp.pad(ps[:, :, C_MISC:C_MISC + IDX_DIM], pad_k)
    n_sel = min(TOPK_MAX, (past + ts) // 4)
    n_phys = cache_k.shape[1]
    ck = cache_k.reshape(1, n_phys, pw, HEAD_DIM)
    cv = cache_v.reshape(1, n_phys, pw, HEAD_DIM)

    def kv_page(p):
        return pl.BlockSpec((None, None, pw, HEAD_DIM), lambda d, pt, p=p: (0, pt[d * n_pages + p], 0, 0))

    def idx_page(p):
        return pl.BlockSpec((None, None, PAGE_SIZE, IDX_DIM), lambda d, pt, p=p: (0, pt[d * n_pages + p], 0, 0))

    def per_d(r, w):
        return pl.BlockSpec((None, r, w), lambda d, pt: (d, 0, 0))

    in_specs = ([per_d(IDX_HEADS * rows, IDX_DIM), per_d(IDX_HEADS * rows, 1), per_d(N_HEADS * rows, HEAD_DIM),
                 per_d(pw, HEAD_DIM), per_d(pw, HEAD_DIM), per_d(PAGE_SIZE, IDX_DIM)]
                + [idx_page(p) for p in range(n_pages)]
                + [kv_page(p) for p in range(n_pages)]
                + [kv_page(p) for p in range(n_pages)])
    n_keys = (n_pages + 1) * PAGE_SIZE
    grid_spec = pltpu.PrefetchScalarGridSpec(
        num_scalar_prefetch=1, grid=(db,), in_specs=in_specs,
        out_specs=per_d(N_HEADS * rows, HEAD_DIM),
        scratch_shapes=[pltpu.VMEM((rows, n_keys), F32), pltpu.VMEM((rows, n_keys * N_KV_HEADS), F32),
                        pltpu.VMEM((N_HEADS * rows, n_keys * N_KV_HEADS), F32)])
    kern = functools.partial(_attn_sample_kernel, n_pages=n_pages, n_sel=n_sel, past=past)
    o = pl.pallas_call(
        kern, grid_spec=grid_spec,
        out_shape=jax.ShapeDtypeStruct((db, N_HEADS * rows, HEAD_DIM), F32),
        compiler_params=_cparams(("arbitrary",)),
        name="attn_sample",
    )(page_table.reshape(-1), iq, wc, q, kn, vn, ikn,
      *([cache_kidx] * n_pages), *([ck] * n_pages), *([cv] * n_pages))
    o = o.reshape(db, N_HEADS, rows, HEAD_DIM)[:, :, :ts].transpose(0, 2, 1, 3)
    return o.reshape(db * ts, N_HEADS * HEAD_DIM).astype(BF16)


def _dot3s(ah, al, bh, bl, dims=None):
    if dims is None:
        d = lambda x, y: jnp.dot(x, y, preferred_element_type=F32)
    else:
        d = lambda x, y: lax.dot_general(x, y, dims, preferred_element_type=F32)
    return d(ah, bh) + d(al, bh) + d(ah, bl)


def _cumsum_rows(sel_bf16, g_rows):
    gb = jnp.broadcast_to(g_rows, (g_rows.shape[0], LANE))
    g_hi, g_lo = _split_bf16(gb)
    g_lo2 = (gb - g_hi.astype(F32) - g_lo.astype(F32)).astype(BF16)
    d = lambda y: jnp.dot(sel_bf16, y, preferred_element_type=F32)
    return d(g_hi) + d(g_lo) + d(g_lo2)


def _l2n(x):
    return x * lax.rsqrt(jnp.sum(x * x, axis=-1, keepdims=True) + NORM_EPS)


def _gate_params(misc, hp, live):
    x = misc + hp[1:2, :]
    softplus = jnp.maximum(x, 0.0) + jnp.log(1.0 + jnp.exp(-jnp.abs(x)))
    g_all = -jnp.exp(hp[0:1, :]) * softplus
    beta_all = _sigmoid(misc)
    if live is not None:
        g_all = jnp.where(live, g_all, 0.0)
        beta_all = jnp.where(live, beta_all, 0.0)
    return g_all, beta_all


def _gdn_kernel(q_ref, k_ref, v_ref, z_ref, m_ref, wq_ref, wk_ref, wv_ref, cq_ref, ck_ref, cv_ref, hp_ref, nw_ref,
                s0_ref, o_ref, sf_ref, s_scr, hq_scr, hk_scr, hv_scr, xc_scr, *, c, nh):
    g_idx = pl.program_id(1)
    tb = pl.program_id(2)
    n_tb = pl.num_programs(2)
    dk, dv = GDN_DK, GDN_DV

    @pl.when(tb == 0)
    def _():
        s_scr[...] = s0_ref[...]
        hq_scr[...] = cq_ref[...]
        hk_scr[...] = ck_ref[...]
        hv_scr[...] = cv_ref[...]

    def conv_silu(x_ref, w_ref, h_scr):
        width = x_ref.shape[1]
        x = x_ref[...]
        xc_scr[0:8, 0:width] = h_scr[...]
        xc_scr[8:8 + c, 0:width] = x
        acc = jnp.zeros((c, width), F32)
        for j in range(CONV_W):
            acc = acc + xc_scr[8 - (CONV_W - 1) + j:8 - (CONV_W - 1) + j + c, 0:width] * w_ref[j:j + 1, :]
        h_scr[...] = x[c - 8:c, :]
        return acc * _sigmoid(acc)

    qa = conv_silu(q_ref, wq_ref, hq_scr)
    ka = conv_silu(k_ref, wk_ref, hk_scr)
    va = conv_silu(v_ref, wv_ref, hv_scr)
    g_all, beta_all = _gate_params(m_ref[...], hp_ref[...], None)
    lane = lax.broadcasted_iota(I32, (1, LANE), 1)
    ri = lax.broadcasted_iota(I32, (c, c), 0)
    ci = lax.broadcasted_iota(I32, (c, c), 1)
    causal = ri >= ci
    strict = ri > ci
    tri = jnp.where(causal, 1.0, 0.0).astype(BF16)
    eye = jnp.where(ri == ci, 1.0, 0.0)

    def same_blk(log_b):
        return (ri >> log_b) == (ci >> log_b)

    heads = range(2 * nh)
    ks = [_l2n(ka[:, hh * dk:(hh + 1) * dk]) for hh in range(nh)]
    qss = [_l2n(qa[:, hh * dk:(hh + 1) * dk]) * (dk ** -0.5) for hh in range(nh)]
    kk_qk = [lax.dot_general(jnp.concatenate([ks[hh], qss[hh]], axis=0).astype(BF16), ks[hh].astype(BF16), NT_DIMS,
                             preferred_element_type=F32) for hh in range(nh)]
    g_cols = [jnp.sum(jnp.where(lane == L_GA + 2 * nh * g_idx + hl, g_all, 0.0), axis=1, keepdims=True)
              for hl in heads]
    betas = [jnp.sum(jnp.where(lane == L_GB + 2 * nh * g_idx + hl, beta_all, 0.0), axis=1, keepdims=True)
             for hl in heads]
    gcs = [_cumsum_rows(tri, g) for g in g_cols]
    decays = [jnp.exp(jnp.where(causal, gc - gc.T, -jnp.inf)) for gc in gcs]
    a_mats = [jnp.where(strict, kk_qk[hl // 2][0:c] * betas[hl] * decays[hl], 0.0) for hl in heads]
    nms = [jnp.where(same_blk(3), -a, 0.0) for a in a_mats]
    pms = [eye + n for n in nms]
    for _ in range(2):
        nms = [_dot3(n, n) for n in nms]
        pms = [p + _dot3(p, n) for p, n in zip(pms, nms)]
    for lb in range(3, (c - 1).bit_length()):
        join = same_blk(lb + 1) & jnp.logical_not(same_blk(lb))
        t1s = [_dot3(jnp.where(join, a, 0.0), p) for a, p in zip(a_mats, pms)]
        pms = [p - _dot3(p, t1) for p, t1 in zip(pms, t1s)]
    us = [_dot3(pms[hl], va[:, hl * dv:(hl + 1) * dv] * betas[hl]) for hl in heads]
    ws = [_dot3(pms[hl], ks[hl // 2] * betas[hl] * jnp.exp(gcs[hl])) for hl in heads]
    qks = [jnp.where(causal, kk_qk[hl // 2][c:2 * c] * decays[hl], 0.0) for hl in heads]
    s_in = [s_scr[hl] for hl in heads]
    v_news = [us[hl] - _dot3(ws[hl], s_in[hl]) for hl in heads]
    os_ = [jnp.dot((qss[hl // 2] * jnp.exp(gcs[hl])).astype(BF16), s_in[hl].astype(BF16), preferred_element_type=F32)
           + jnp.dot(qks[hl].astype(BF16), v_news[hl].astype(BF16), preferred_element_type=F32) for hl in heads]
    for hl in heads:
        g_last = gcs[hl][c - 1:c, :]
        kdec = ks[hl // 2] * jnp.exp(g_last - gcs[hl])
        s_scr[hl] = s_in[hl] * jnp.exp(g_last) + _dot3(kdec.T, v_news[hl])
    for hl in heads:
        o = os_[hl]
        z = z_ref[:, hl * dv:(hl + 1) * dv]
        on = o * lax.rsqrt(jnp.mean(o * o, axis=-1, keepdims=True) + NORM_EPS) * nw_ref[...]
        o_ref[:, hl * dv:(hl + 1) * dv] = (on * (z * _sigmoid(z))).astype(BF16)

    @pl.when(tb == n_tb - 1)
    def _():
        sf_ref[...] = s_scr[...]


def _gdn_hp(a_log, dt_bias):
    hp = jnp.zeros((8, LANE), F32)
    return hp.at[0, L_GA:L_GA + GDN_V_HEADS].set(a_log.astype(F32)).at[1, L_GA:L_GA + GDN_V_HEADS].set(
        dt_bias.astype(F32))


def _gdn(p32, conv0, s0, conv_w, a_log, dt_bias, norm_w, bsz, t, c=128, nh=4):
    n_tb = t // c
    n_g = GDN_QK_HEADS // nh
    qw, vw = nh * GDN_DK, 2 * nh * GDN_DV
    row = lambda b, g, i: b * n_tb + i
    in_specs = [
        pl.BlockSpec((c, qw), lambda b, g, i: (row(b, g, i), C_GQ // qw + g)),
        pl.BlockSpec((c, qw), lambda b, g, i: (row(b, g, i), C_GK // qw + g)),
        pl.BlockSpec((c, vw), lambda b, g, i: (row(b, g, i), C_GV // vw + g)),
        pl.BlockSpec((c, vw), lambda b, g, i: (row(b, g, i), C_GZ // vw + g)),
        pl.BlockSpec((c, LANE), lambda b, g, i: (row(b, g, i), MISC_BLK)),
        pl.BlockSpec((CONV_W, qw), lambda b, g, i: (0, g)),
        pl.BlockSpec((CONV_W, qw), lambda b, g, i: (0, n_g + g)),
        pl.BlockSpec((CONV_W, vw), lambda b, g, i: (0, n_g + g)),
        pl.BlockSpec((None, 8, qw), lambda b, g, i: (b, 0, g)),
        pl.BlockSpec((None, 8, qw), lambda b, g, i: (b, 0, n_g + g)),
        pl.BlockSpec((None, 8, vw), lambda b, g, i: (b, 0, n_g + g)),
        pl.BlockSpec((8, LANE), lambda b, g, i: (0, 0)),
        pl.BlockSpec((1, GDN_DV), lambda b, g, i: (0, 0)),
        pl.BlockSpec((None, 2 * nh, GDN_DK, GDN_DV), lambda b, g, i: (b, g, 0, 0)),
    ]
    out_specs = [pl.BlockSpec((c, vw), lambda b, g, i: (row(b, g, i), g)),
                 pl.BlockSpec((None, 2 * nh, GDN_DK, GDN_DV), lambda b, g, i: (b, g, 0, 0))]
    return pl.pallas_call(
        functools.partial(_gdn_kernel, c=c, nh=nh),
        grid=(bsz, n_g, n_tb),
        in_specs=in_specs, out_specs=out_specs,
        out_shape=[jax.ShapeDtypeStruct((bsz * t, GDN_V_HEADS * GDN_DV), BF16),
                   jax.ShapeDtypeStruct((bsz, GDN_V_HEADS, GDN_DK, GDN_DV), F32)],
        scratch_shapes=[pltpu.VMEM((2 * nh, GDN_DK, GDN_DV), F32), pltpu.VMEM((8, qw), F32), pltpu.VMEM((8, qw), F32),
                        pltpu.VMEM((8, vw), F32), pltpu.VMEM((c + 8, vw), F32)],
        compiler_params=_cparams(("parallel", "parallel", "arbitrary")),
        name="gdn",
    )(p32, p32, p32, p32, p32, conv_w, conv_w, conv_w, conv0, conv0, conv0, _gdn_hp(a_log, dt_bias),
      norm_w.reshape(1, GDN_DV), s0)


def _gdn_sample_kernel(x_ref, z_ref, m_ref, w_ref, hp_ref, nw_ref, s0_ref, o_ref, sf_ref, *, ts):
    rows, nvh, dk, dv = 8, GDN_V_HEADS, GDN_DK, GDN_DV
    qd = GDN_QK_HEADS * dk
    x8 = x_ref[...]
    w = w_ref[...]
    acc = x8 * w[0:1, :]
    for j in range(1, CONV_W):
        acc = acc + pltpu.roll(x8, rows - j, 0) * w[j:j + 1, :]
    act = acc * _sigmoid(acc)
    live = lax.broadcasted_iota(I32, (rows, 1), 0) < ts
    g_all, beta_all = _gate_params(m_ref[...], hp_ref[...], live)
    qn = [_l2n(act[:, g * dk:(g + 1) * dk]) * (dk ** -0.5) for g in range(GDN_QK_HEADS)]
    kn = [_l2n(act[:, qd + g * dk:qd + (g + 1) * dk]) for g in range(GDN_QK_HEADS)]
    rep = nvh // GDN_QK_HEADS
    cat = lambda parts: jnp.concatenate(parts, axis=0)
    k = cat([kn[h // rep] for h in range(nvh)])
    qs = cat([qn[h // rep] for h in range(nvh)])
    v = cat([act[:, 2 * qd + h * dv:2 * qd + (h + 1) * dv] for h in range(nvh)])
    z = cat([z_ref[:, h * dv:(h + 1) * dv] for h in range(nvh)])
    beta = cat([beta_all[:, L_GB + h:L_GB + h + 1] for h in range(nvh)])
    g_col = cat([g_all[:, L_GA + h:L_GA + h + 1] for h in range(nvh)])
    n = nvh * rows
    ri = lax.broadcasted_iota(I32, (n, n), 0)
    ci = lax.broadcasted_iota(I32, (n, n), 1)
    same = (ri >> 3) == (ci >> 3)
    causal = same & (ri >= ci)
    strict = same & (ri > ci)
    eye = jnp.where(ri == ci, 1.0, 0.0)
    gc = _cumsum_rows(jnp.where(causal, 1.0, 0.0).astype(BF16), g_col)
    gl = _cumsum_rows(jnp.where(same, 1.0, 0.0).astype(BF16), g_col)
    decay = jnp.exp(jnp.where(causal, gc - gc.T, -jnp.inf))
    kb = k * beta
    a_mat = jnp.where(strict, _dot3(kb, k, NT_DIMS) * decay, 0.0)
    nm = -a_mat
    pm = eye + nm
    for _ in range(2):
        nm = _dot3(nm, nm)
        pm = pm + _dot3(pm, nm)
    u = _dot3(pm, v * beta)
    w_rows = _dot3(pm, kb * jnp.exp(gc))
    qk = jnp.where(causal, _dot3(qs, k, NT_DIMS) * decay, 0.0)
    qg = qs * jnp.exp(gc)
    kdt = (k * jnp.exp(gl - gc)).T
    s_in = [s0_ref[h] for h in range(nvh)]
    v_new, o1 = [], []
    for h in range(nvh):
        sl = slice(h * rows, (h + 1) * rows)
        r = _dot3(cat([w_rows[sl], qg[sl]]), s_in[h])
        v_new.append(u[sl] - r[0:rows])
        o1.append(r[rows:2 * rows])
    v_new = cat(v_new)
    o = cat(o1) + _dot3(qk, v_new)
    vh, vl = _split_bf16(v_new)
    lane = lax.broadcasted_iota(I32, (1, n), 1)
    for h in range(nvh):
        kh, kl = _split_bf16(jnp.where((lane >> 3) == h, kdt, 0.0))
        sf_ref[h] = s_in[h] * jnp.exp(gl[h * rows:h * rows + 1, :]) + _dot3s(kh, kl, vh, vl)
    on = o * lax.rsqrt(jnp.mean(o * o, axis=-1, keepdims=True) + NORM_EPS) * nw_ref[...]
    o_ref[...] = on * (z * _sigmoid(z))


def _gdn_sample(p32, state_conv, s0, conv_w, a_log, dt_bias, norm_w, db, ts):
    rows = 8
    conv_dim = conv_w.shape[-1]
    ps = p32.reshape(db, ts, PROJ_W)
    x8 = jnp.concatenate([state_conv, ps[:, :, C_GQ:C_GQ + conv_dim],
                          jnp.zeros((db, rows - ts - (CONV_W - 1), conv_dim), F32)], axis=1)
    pad_t = ((0, 0), (0, rows - ts), (0, 0))
    z8 = jnp.pad(ps[:, :, C_GZ:C_GZ + GDN_V_HEADS * GDN_DV], pad_t)
    m8 = jnp.pad(ps[:, :, C_MISC:C_MISC + LANE], pad_t)
    n = GDN_V_HEADS * rows
    per_d = lambda w: pl.BlockSpec((None, rows, w), lambda d: (d, 0, 0))
    state = pl.BlockSpec((None, GDN_V_HEADS, GDN_DK, GDN_DV), lambda d: (d, 0, 0, 0))
    o, sf = pl.pallas_call(
        functools.partial(_gdn_sample_kernel, ts=ts),
        grid=(db,),
        in_specs=[per_d(conv_dim), per_d(GDN_V_HEADS * GDN_DV), per_d(LANE),
                  pl.BlockSpec((CONV_W, conv_dim), lambda d: (0, 0)),
                  pl.BlockSpec((8, LANE), lambda d: (0, 0)),
                  pl.BlockSpec((1, GDN_DV), lambda d: (0, 0)), state],
        out_specs=[pl.BlockSpec((None, n, GDN_DV), lambda d: (d, 0, 0)), state],
        out_shape=[jax.ShapeDtypeStruct((db, n, GDN_DV), F32),
                   jax.ShapeDtypeStruct((db, GDN_V_HEADS, GDN_DK, GDN_DV), F32)],
        compiler_params=_cparams(("parallel",)),
        name="gdn_sample",
    )(x8, z8, m8, conv_w, _gdn_hp(a_log, dt_bias), norm_w.reshape(1, GDN_DV), s0)
    o = o.reshape(db, GDN_V_HEADS, rows, GDN_DV)[:, :, :ts].transpose(0, 2, 1, 3)
    return o.reshape(db * ts, GDN_V_HEADS * GDN_DV).astype(BF16), sf


def _merge_kernel(a_ref, g_ref, wa_ref, wg_ref, ga_ref, gg_ref, o_ref):
    pa = jnp.dot(a_ref[...], wa_ref[...], preferred_element_type=F32)
    pg = jnp.dot(g_ref[...], wg_ref[...], preferred_element_type=F32)
    o_ref[...] = (_sigmoid(ga_ref[...]) * pa + _sigmoid(gg_ref[...]) * pg).astype(BF16)


def _merge(attn_o, gdn_o, w_oa, w_og, p32, tm):
    m = attn_o.shape[0]
    tn = 512
    return pl.pallas_call(
        _merge_kernel,
        grid=(m // tm, D_MODEL // tn),
        in_specs=[pl.BlockSpec((tm, D_MODEL), lambda i, j: (i, 0)),
                  pl.BlockSpec((tm, D_MODEL), lambda i, j: (i, 0)),
                  pl.BlockSpec((D_MODEL, tn), lambda i, j: (0, j)),
                  pl.BlockSpec((D_MODEL, tn), lambda i, j: (0, j)),
                  pl.BlockSpec((tm, tn), lambda i, j: (i, C_GL // tn + j)),
                  pl.BlockSpec((tm, tn), lambda i, j: (i, (C_GL + D_MODEL) // tn + j))],
        out_specs=pl.BlockSpec((tm, tn), lambda i, j: (i, j)),
        out_shape=jax.ShapeDtypeStruct((m, D_MODEL), BF16),
        compiler_params=_cparams(("parallel", "arbitrary")),
        name="merge",
    )(attn_o, gdn_o, w_oa, w_og, p32, p32)


def _outproj_kernel(mg_ref, w_ref, x_ref, g1_ref, sc_ref, sh_ref, gn_ref, wr_ref, br_ref,
                    x1_ref, h2_ref, ti_ref, tg_ref):
    x1 = x_ref[...] + g1_ref[...] * jnp.dot(mg_ref[...], w_ref[...], preferred_element_type=F32)
    x1_ref[...] = x1
    ms = jnp.mean(x1 * x1, axis=-1, keepdims=True)
    h2 = x1 * lax.rsqrt(ms + NORM_EPS) * gn_ref[...] * (1.0 + sc_ref[...]) + sh_ref[...]
    h2_ref[...] = h2
    logits = _dot3(h2, wr_ref[...]) + br_ref[...]
    tm = logits.shape[0]
    lane = lax.broadcasted_iota(I32, (tm, LANE), 1)
    lanef = lane.astype(F32)
    l = jnp.where(lane < N_EXPERTS, logits, -jnp.inf)
    vals, idxs = [], []
    for _ in range(TOP_K):
        mx = jnp.max(l, axis=1, keepdims=True)
        ix = jnp.min(jnp.where(l == mx, lanef, float(LANE)), axis=1, keepdims=True)
        vals.append(mx)
        idxs.append(ix)
        l = jnp.where(lanef == ix, -jnp.inf, l)
    es = [jnp.exp(v - vals[0]) for v in vals]
    den = es[0]
    for e in es[1:]:
        den = den + e
    ti = jnp.zeros((tm, LANE), F32)
    tg = jnp.zeros((tm, LANE), F32)
    for kk in range(TOP_K):
        ti = jnp.where(lane == kk, idxs[kk], ti)
        tg = jnp.where(lane == kk, es[kk] / den, tg)
    ti_ref[...] = ti.astype(I32)
    tg_ref[...] = tg


def _outproj(merged, w_out, x, gate1, scale2, shift2, g_norm2, w_router, b_router, rows_per_mod, tm):
    m = x.shape[0]
    mod_rows = gate1.shape[1]
    tiles_per_mod = rows_per_mod // tm
    if mod_rows == 1:
        mod_spec = pl.BlockSpec((None, 1, D_MODEL), lambda i: (i // tiles_per_mod, 0, 0))
    else:
        mod_spec = pl.BlockSpec((None, tm, D_MODEL), lambda i: (i // tiles_per_mod, i % tiles_per_mod, 0))
    wr = jnp.pad(w_router.astype(F32), ((0, 0), (0, LANE - N_EXPERTS)))
    br = jnp.pad(b_router.astype(F32), (0, LANE - N_EXPERTS)).reshape(1, LANE)
    row = pl.BlockSpec((tm, D_MODEL), lambda i: (i, 0))
    small = pl.BlockSpec((tm, LANE), lambda i: (i, 0))
    return pl.pallas_call(
        _outproj_kernel,
        grid=(m // tm,),
        in_specs=[row, pl.BlockSpec((D_MODEL, D_MODEL), lambda i: (0, 0)), row, mod_spec, mod_spec, mod_spec,
                  pl.BlockSpec((1, D_MODEL), lambda i: (0, 0)),
                  pl.BlockSpec((D_MODEL, LANE), lambda i: (0, 0)),
                  pl.BlockSpec((1, LANE), lambda i: (0, 0))],
        out_specs=[row, row, small, small],
        out_shape=[jax.ShapeDtypeStruct((m, D_MODEL), F32), jax.ShapeDtypeStruct((m, D_MODEL), F32),
                   jax.ShapeDtypeStruct((m, LANE), I32), jax.ShapeDtypeStruct((m, LANE), F32)],
        compiler_params=_cparams(("parallel",)),
        name="outproj",
    )(merged, w_out, x, gate1, scale2, shift2, g_norm2.reshape(1, D_MODEL), wr, br)


GATHER_ROWS = 256


def _row_copy(src_ref, o_ref, sem, src_row, dst_row):
    return pltpu.make_async_copy(src_ref.at[pl.ds(src_row, 1)], o_ref.at[pl.ds(dst_row, 1)], sem)


def _gather_kernel(nr_ref, idx_ref, src_ref, o_ref, sem):
    n = o_ref.shape[0]
    live = pl.program_id(0) * n < nr_ref[0]

    @pl.when(live)
    def _():
        def issue(r, carry):
            _row_copy(src_ref, o_ref, sem, idx_ref[0, r], r).start()
            return carry

        def wait(r, carry):
            _row_copy(src_ref, o_ref, sem, 0, r).wait()
            return carry

        lax.fori_loop(0, n, issue, 0, unroll=8)
        lax.fori_loop(0, n, wait, 0, unroll=8)

    @pl.when(jnp.logical_not(live))
    def _():
        o_ref[...] = jnp.zeros_like(o_ref)


def _gather_rows(src, idx, n_live):
    n_valid, w = idx.shape[0], src.shape[1]
    n_steps = -(-n_valid // GATHER_ROWS)
    n = n_steps * GATHER_ROWS
    idx = jnp.pad(idx, (0, n - n_valid))
    grid_spec = pltpu.PrefetchScalarGridSpec(
        num_scalar_prefetch=1, grid=(n_steps,),
        in_specs=[pl.BlockSpec((None, 1, GATHER_ROWS), lambda i, nr: (i, 0, 0), memory_space=pltpu.SMEM),
                  pl.BlockSpec(memory_space=pl.ANY)],
        out_specs=pl.BlockSpec((GATHER_ROWS, w), lambda i, nr: (i, 0)),
        scratch_shapes=[pltpu.SemaphoreType.DMA(())])
    out = pl.pallas_call(
        _gather_kernel, grid_spec=grid_spec,
        out_shape=jax.ShapeDtypeStruct((n, w), src.dtype),
        compiler_params=pltpu.CompilerParams(dimension_semantics=("arbitrary",), vmem_limit_bytes=VMEM_LIMIT,
                                             disable_bounds_checks=True),
        name="gather_rows",
    )(n_live, idx.reshape(n_steps, 1, GATHER_ROWS), src)
    return out if n == n_valid else out[:n_valid]


def _moe_kernel(be_ref, nu_ref, x_ref, wg_ref, wu_ref, wd_ref, bg_ref, bu_ref, bd_ref, o_ref, x16_scr, h_scr,
                *, n_f, tf):
    i = pl.program_id(0)
    j = pl.program_id(1)
    used = i < nu_ref[0]
    del be_ref

    @pl.when(used & (j == 0))
    def _():
        x16_scr[...] = x_ref[...].astype(BF16)

    @pl.when(used & (j < n_f))
    def _():
        x = x16_scr[...]
        g = jnp.dot(x, wg_ref[...].astype(BF16), preferred_element_type=F32) + bg_ref[...]
        u = jnp.dot(x, wu_ref[...].astype(BF16), preferred_element_type=F32) + bu_ref[...]
        gate = jnp.minimum(g, SWIGLU_LIMIT)
        up = jnp.clip(u, -SWIGLU_LIMIT, SWIGLU_LIMIT)
        h_scr[j] = ((up + 1.0) * gate * _sigmoid(SWIGLU_ALPHA * gate)).astype(BF16)

    @pl.when(used & (j >= n_f))
    def _():
        acc = jnp.dot(h_scr[0], wd_ref[0:tf, :].astype(BF16), preferred_element_type=F32)
        for k in range(1, n_f):
            acc = acc + jnp.dot(h_scr[k], wd_ref[k * tf:(k + 1) * tf, :].astype(BF16), preferred_element_type=F32)
        o_ref[...] = acc + bd_ref[...]

    @pl.when(jnp.logical_not(used) & (j >= n_f))
    def _():
        o_ref[...] = jnp.zeros_like(o_ref)


def _moe_ffn(xs, blk_e, n_used, w_gu, b_gu, w_dn, b_dn, rb, tf):
    n_rows = xs.shape[0]
    n_blocks = n_rows // rb
    n_f = D_FF // tf
    tn = tf
    n_o = D_MODEL // tn
    bgu = b_gu.reshape(N_EXPERTS, 1, 2 * D_FF)
    bdn = b_dn.reshape(N_EXPERTS, 1, D_MODEL)

    def jf(i, j, nu):
        return jnp.where(i < nu[0], jnp.minimum(j, n_f - 1), 0)

    def jo(i, j, nu):
        return jnp.where(i < nu[0], jnp.maximum(j - n_f, 0), 0)

    grid_spec = pltpu.PrefetchScalarGridSpec(
        num_scalar_prefetch=2, grid=(n_blocks, n_f + n_o),
        in_specs=[pl.BlockSpec((rb, D_MODEL), lambda i, j, be, nu: (i, 0)),
                  pl.BlockSpec((None, None, D_MODEL, tf), lambda i, j, be, nu: (0, be[i], 0, jf(i, j, nu))),
                  pl.BlockSpec((None, None, D_MODEL, tf), lambda i, j, be, nu: (0, be[i], 0, n_f + jf(i, j, nu))),
                  pl.BlockSpec((None, None, D_FF, tn), lambda i, j, be, nu: (0, be[i], 0, jo(i, j, nu))),
                  pl.BlockSpec((None, 1, tf), lambda i, j, be, nu: (be[i], 0, jf(i, j, nu))),
                  pl.BlockSpec((None, 1, tf), lambda i, j, be, nu: (be[i], 0, n_f + jf(i, j, nu))),
                  pl.BlockSpec((None, 1, tn), lambda i, j, be, nu: (be[i], 0, jo(i, j, nu)))],
        out_specs=pl.BlockSpec((rb, tn), lambda i, j, be, nu: (i, jnp.maximum(j - n_f, 0))),
        scratch_shapes=[pltpu.VMEM((rb, D_MODEL), BF16), pltpu.VMEM((n_f, rb, tf), BF16)])
    return pl.pallas_call(
        functools.partial(_moe_kernel, n_f=n_f, tf=tf), grid_spec=grid_spec,
        out_shape=jax.ShapeDtypeStruct((n_rows, D_MODEL), F32),
        compiler_params=pltpu.CompilerParams(dimension_semantics=("parallel", "arbitrary"),
                                             vmem_limit_bytes=MOE_VMEM_LIMIT),
        name="moe_ffn",
    )(blk_e, n_used, xs, w_gu, w_gu, w_dn, bgu, bgu, bdn)


def _final_kernel(x1_ref, yg_ref, tg_ref, g2_ref, gf_ref, o_ref):
    tg = tg_ref[...]
    ffn = jnp.zeros_like(x1_ref)
    for kk in range(TOP_K):
        ffn = ffn + tg[:, kk:kk + 1] * yg_ref[:, kk * D_MODEL:(kk + 1) * D_MODEL]
    x2 = x1_ref[...] + g2_ref[...] * ffn
    ms = jnp.mean(x2 * x2, axis=-1, keepdims=True)
    o_ref[...] = x2 * lax.rsqrt(ms + NORM_EPS) * gf_ref[...]


def _final(x1, yg, tg, gate2, g_final, rows_per_mod, tm):
    m = x1.shape[0]
    mod_rows = gate2.shape[1]
    tiles_per_mod = rows_per_mod // tm
    if mod_rows == 1:
        mod_spec = pl.BlockSpec((None, 1, D_MODEL), lambda i: (i // tiles_per_mod, 0, 0))
    else:
        mod_spec = pl.BlockSpec((None, tm, D_MODEL), lambda i: (i // tiles_per_mod, i % tiles_per_mod, 0))
    row = pl.BlockSpec((tm, D_MODEL), lambda i: (i, 0))
    return pl.pallas_call(
        _final_kernel,
        grid=(m // tm,),
        in_specs=[row, pl.BlockSpec((tm, TOP_K * D_MODEL), lambda i: (i, 0)),
                  pl.BlockSpec((tm, LANE), lambda i: (i, 0)), mod_spec,
                  pl.BlockSpec((1, D_MODEL), lambda i: (0, 0))],
        out_specs=row,
        out_shape=jax.ShapeDtypeStruct((m, D_MODEL), F32),
        compiler_params=_cparams(("parallel",)),
        name="final",
    )(x1, yg, tg, gate2, g_final.reshape(1, D_MODEL))


def _route(topi, rb):
    tt = topi.shape[0]
    sel = jnp.sum((topi[:, :, None] == jnp.arange(N_EXPERTS, dtype=I32)[None, None, :]).astype(I32), axis=1)
    counts = jnp.sum(sel, axis=0)
    padded = (counts + rb - 1) // rb * rb
    pad_end = jnp.cumsum(padded)
    pad_start = pad_end - padded
    rank = jnp.cumsum(sel, axis=0) - sel
    dest = jnp.take_along_axis(pad_start[None, :] + rank, topi, axis=1)
    n_blocks = -(-(tt * TOP_K + N_EXPERTS * (rb - 1)) // rb)
    n_rows = n_blocks * rb
    tok = jnp.repeat(jnp.arange(tt, dtype=I32), TOP_K)
    row_tok = (jnp.arange(n_rows, dtype=I32) % tt).at[dest.reshape(-1)].set(tok)
    blk_e = jnp.minimum(jnp.searchsorted(pad_end, jnp.arange(n_blocks, dtype=I32) * rb, side="right"),
                        N_EXPERTS - 1).astype(I32)
    n_used = (pad_end[-1] // rb).astype(I32).reshape(1)
    return dest, row_tok, blk_e, n_used


def kernel(x_prompt, x_sample, c_prompt, c_sample, cache_k, cache_v, cache_kidx, state_gdn, state_conv, page_table,
           w_ada, b_ada, g_norm1, g_norm2, g_final, w_in, gdn_conv_w, gdn_a_log, gdn_dt_bias, gdn_norm_w,
           w_o_attn, w_o_gdn, w_out, w_router, b_router, w_gu, b_gu, w_dn, b_dn):
    bsz, t, d = x_prompt.shape
    db, ts, _ = x_sample.shape
    depth = w_ada.shape[0]
    assert depth == 1 and d == D_MODEL
    past = page_table.shape[1] * PAGE_SIZE
    conv_dim = gdn_conv_w.shape[-1]

    n_c = bsz + db
    n_cp = -(-n_c // 8) * 8
    c_all = jnp.pad(jnp.concatenate([c_prompt, c_sample], axis=0), ((0, n_cp - n_c), (0, 0)))
    mod = _ada(c_all, w_ada, b_ada[0])
    mod_p = mod[:bsz].reshape(bsz, 1, 6, d)
    mod_s = jnp.repeat(mod[bsz:n_c], ts, axis=0).reshape(1, db * ts, 6, d)
    mods = {"p": [mod_p[:, :, i] for i in range(6)], "s": [mod_s[:, :, i] for i in range(6)]}

    w_r = _reorder_w_in(w_in[0])
    w_oa = w_o_attn[0].astype(BF16)
    w_og = w_o_gdn[0].astype(BF16)
    w_o = w_out[0].astype(BF16)

    xp = x_prompt.reshape(bsz * t, d)
    xs = x_sample.reshape(db * ts, d)
    tm_p = min(512, t)
    tm_s = db * ts

    tabs_p = _rot_tables(jnp.arange(t, dtype=I32))
    tabs_s = _rot_tables(jnp.tile(past + jnp.arange(ts, dtype=I32), db))
    p32_p, p16_p = _inproj(xp, g_norm1[0], mods["p"][1], mods["p"][0], w_r, tabs_p, t, tm_p)
    p32_s, _ = _inproj(xs, g_norm1[0], mods["s"][1], mods["s"][0], w_r, tabs_s, db * ts, tm_s)

    attn_p = _attn_prompt(p32_p, p16_p, bsz, t)
    attn_s = _attn_sample(p32_s, cache_k, cache_v, cache_kidx, page_table, ts)

    conv0_p = jnp.zeros((bsz, 8, conv_dim), F32)
    s0_p = jnp.zeros((bsz, GDN_V_HEADS, GDN_DK, GDN_DV), F32)
    gdn_p, sfin_p = _gdn(p32_p, conv0_p, s0_p, gdn_conv_w[0], gdn_a_log[0], gdn_dt_bias[0], gdn_norm_w[0], bsz, t)
    gdn_s, sfin_s = _gdn_sample(p32_s, state_conv[0], state_gdn[0], gdn_conv_w[0], gdn_a_log[0], gdn_dt_bias[0],
                                gdn_norm_w[0], db, ts)

    outs = {}
    for name, attn_o, gdn_o, p32, x, rows_per_mod, tm in (("p", attn_p, gdn_p, p32_p, xp, t, min(256, t)),
                                                          ("s", attn_s, gdn_s, p32_s, xs, db * ts, min(256, db * ts))):
        merged = _merge(attn_o, gdn_o, w_oa, w_og, p32, tm)
        m = mods[name]
        outs[name] = _outproj(merged, w_o, x, m[2], m[4], m[3], g_norm2[0], w_router[0], b_router[0], rows_per_mod, tm)

    rb, tf = 512, 512
    h2 = jnp.concatenate([outs["p"][1], outs["s"][1]], axis=0)
    topi = jnp.concatenate([outs["p"][2], outs["s"][2]], axis=0)[:, :TOP_K]
    dest, row_tok, blk_e, n_used = _route(topi, rb)
    yb = _moe_ffn(_gather_rows(h2, row_tok, n_used * rb), blk_e, n_used, w_gu, b_gu[0], w_dn, b_dn[0], rb, tf)
    n_tok = jnp.full((1,), h2.shape[0] * TOP_K, I32)
    yg = _gather_rows(yb, dest.reshape(-1), n_tok).reshape(-1, TOP_K * d)

    n_p = bsz * t
    y_p = _final(outs["p"][0], yg[:n_p], outs["p"][3], mods["p"][5], g_final, t, min(256, t))
    y_s = _final(outs["s"][0], yg[n_p:], outs["s"][3], mods["s"][5], g_final, db * ts, min(256, db * ts))

    kv = N_KV_HEADS * HEAD_DIM
    new_k_p = p32_p[:, C_AK:C_AK + kv].reshape(1, bsz, t, N_KV_HEADS, HEAD_DIM)
    new_v_p = p32_p[:, C_AV:C_AV + kv].reshape(1, bsz, t, N_KV_HEADS, HEAD_DIM)
    new_ki_p = p32_p[:, C_MISC:C_MISC + IDX_DIM].reshape(1, bsz, t, IDX_DIM)
    new_k_s = p32_s[:, C_AK:C_AK + kv].reshape(1, db, ts, N_KV_HEADS, HEAD_DIM)
    new_v_s = p32_s[:, C_AV:C_AV + kv].reshape(1, db, ts, N_KV_HEADS, HEAD_DIM)
    new_ki_s = p32_s[:, C_MISC:C_MISC + IDX_DIM].reshape(1, db, ts, IDX_DIM)
    pre_p = p32_p[:, C_GQ:C_GQ + conv_dim].reshape(bsz, t, conv_dim)
    conv_p = pre_p[:, t - (CONV_W - 1):][None]
    pre_s = p32_s[:, C_GQ:C_GQ + conv_dim].reshape(db, ts, conv_dim)
    conv_s = jnp.concatenate([state_conv[0], pre_s], axis=1)[:, ts:][None]
    return (y_p.reshape(bsz, t, d), y_s.reshape(db, ts, d), new_k_p, new_v_p, new_ki_p, conv_p, sfin_p[None],
            new_k_s, new_v_s, new_ki_s, conv_s, sfin_s[None])
```

```python
import functools

import jax
import jax.numpy as jnp
from jax import lax
from jax.experimental import pallas as pl
from jax.experimental.pallas import tpu as pltpu

F32 = jnp.float32
BF16 = jnp.bfloat16
I32 = jnp.int32

D_MODEL = 2048
N_HEADS = 16
N_KV_HEADS = 4
HEAD_DIM = 128
ROPE_THETA = 500000.0
ROT_FRACTION_DIV = 4
IDX_HEADS = 16
IDX_DIM = 64
TOPK_MAX = 256
GDN_QK_HEADS = 8
GDN_V_HEADS = 16
GDN_DK = 128
GDN_DV = 128
CONV_W = 4
N_EXPERTS = 32
TOP_K = 4
D_FF = D_MODEL
SWIGLU_LIMIT = 7.0
SWIGLU_ALPHA = 1.702
NORM_EPS = 1e-6
PAGE_SIZE = 128
LANE = 128

C_AQ, C_AK, C_AV, C_IQ = 0, 2048, 2560, 3072
C_GQ, C_GK, C_GV, C_GZ, C_GL = 4096, 5120, 6144, 8192, 10240
C_MISC = 14336
L_IW, L_GB, L_GA = 64, 80, 96
PROJ_TN = 512
PROJ_W = 14848
MISC_BLK = C_MISC // LANE

VMEM_LIMIT = 48 * 1024 * 1024
MOE_VMEM_LIMIT = 56 * 1024 * 1024
NEG_BIG = -1e30
INT_MIN = -2147483648

NT_DIMS = (((1,), (1,)), ((), ()))


def _cparams(sem):
    return pltpu.CompilerParams(dimension_semantics=sem, vmem_limit_bytes=VMEM_LIMIT)


def _split_bf16(a):
    hi = a.astype(BF16)
    lo = (a - hi.astype(F32)).astype(BF16)
    return hi, lo


def _dot3(a, b, dims=None):
    ah, al = _split_bf16(a)
    bh, bl = _split_bf16(b)
    if dims is None:
        d = lambda x, y: jnp.dot(x, y, preferred_element_type=F32)
    else:
        d = lambda x, y: lax.dot_general(x, y, dims, preferred_element_type=F32)
    return d(ah, bh) + d(al, bh) + d(ah, bl)


def _sigmoid(x):
    return 1.0 / (1.0 + jnp.exp(-x))


def _ada_kernel(c_ref, w_ref, b_ref, o_ref):
    c = c_ref[...]
    a = (c * _sigmoid(c)).astype(BF16)
    o_ref[...] = jnp.dot(a, w_ref[...].astype(BF16), preferred_element_type=F32) + b_ref[...]


def _ada(c, w, b):
    m, k = c.shape
    n = w.shape[2]
    tn = 1024
    return pl.pallas_call(
        _ada_kernel,
        grid=(n // tn,),
        in_specs=[pl.BlockSpec((m, k), lambda j: (0, 0)),
                  pl.BlockSpec((None, k, tn), lambda j: (0, 0, j)),
                  pl.BlockSpec((1, tn), lambda j: (0, j))],
        out_specs=pl.BlockSpec((m, tn), lambda j: (0, j)),
        out_shape=jax.ShapeDtypeStruct((m, n), F32),
        compiler_params=_cparams(("arbitrary",)),
        name="ada",
    )(c, w, b.reshape(1, n))


def _rot_slab(x, tabs, half):
    a, b, c = tabs
    return x * a + pltpu.roll(x, LANE - half, 1) * b + pltpu.roll(x, half, 1) * c


def _inproj_kernel(x_ref, g_ref, sc_ref, sh_ref, w_ref, tab_ref, o32_ref, o16_ref, h_scr):
    j = pl.program_id(1)

    @pl.when(j == 0)
    def _():
        x = x_ref[...]
        ms = jnp.mean(x * x, axis=-1, keepdims=True)
        y = x * lax.rsqrt(ms + NORM_EPS) * g_ref[...]
        h_scr[...] = (y * (1.0 + sc_ref[...]) + sh_ref[...]).astype(BF16)

    acc = jnp.dot(h_scr[...], w_ref[...], preferred_element_type=F32)
    n_slab = PROJ_TN // LANE

    def store(fn):
        for s in range(n_slab):
            v = fn(acc[:, s * LANE:(s + 1) * LANE])
            o32_ref[:, s * LANE:(s + 1) * LANE] = v
            o16_ref[:, s * LANE:(s + 1) * LANE] = v.astype(BF16)

    rot_head = j < 5
    rot_idx = ((j >= 6) & (j < 8)) | (j == C_MISC // PROJ_TN)

    @pl.when(rot_head)
    def _():
        tabs = (tab_ref[0], tab_ref[1], tab_ref[2])
        store(lambda v: _rot_slab(v, tabs, HEAD_DIM // ROT_FRACTION_DIV // 2))

    @pl.when(rot_idx)
    def _():
        tabs = (tab_ref[0], tab_ref[1], tab_ref[2])
        store(lambda v: _rot_slab(v, tabs, IDX_DIM // ROT_FRACTION_DIV // 2))

    @pl.when(jnp.logical_not(rot_head | rot_idx))
    def _():
        store(lambda v: v)


def _tab_type(j):
    return jnp.where(j < 6, 0, jnp.where(j < 28, 1, 2))


def _inproj(x, g_norm, scale, shift, w_r, tabs, rows_per_mod, tm):
    m = x.shape[0]
    tt = tabs.shape[2]
    n_t = tt // tm
    mod_rows = scale.shape[1]
    tiles_per_mod = rows_per_mod // tm
    if mod_rows == 1:
        mod_spec = pl.BlockSpec((None, 1, D_MODEL), lambda i, j: (i // tiles_per_mod, 0, 0))
    else:
        mod_spec = pl.BlockSpec((None, tm, D_MODEL), lambda i, j: (i // tiles_per_mod, i % tiles_per_mod, 0))
    return pl.pallas_call(
        _inproj_kernel,
        grid=(m // tm, PROJ_W // PROJ_TN),
        in_specs=[pl.BlockSpec((tm, D_MODEL), lambda i, j: (i, 0)),
                  pl.BlockSpec((1, D_MODEL), lambda i, j: (0, 0)),
                  mod_spec, mod_spec,
                  pl.BlockSpec((D_MODEL, PROJ_TN), lambda i, j: (0, j)),
                  pl.BlockSpec((None, 3, tm, LANE), lambda i, j: (_tab_type(j), 0, i % n_t, 0))],
        out_specs=[pl.BlockSpec((tm, PROJ_TN), lambda i, j: (i, j)),
                   pl.BlockSpec((tm, PROJ_TN), lambda i, j: (i, j))],
        out_shape=[jax.ShapeDtypeStruct((m, PROJ_W), F32), jax.ShapeDtypeStruct((m, PROJ_W), BF16)],
        scratch_shapes=[pltpu.VMEM((tm, D_MODEL), BF16)],
        compiler_params=_cparams(("parallel", "arbitrary")),
        name="inproj",
    )(x, g_norm.reshape(1, D_MODEL), scale, shift, w_r, tabs)


def _rot_tables(pos):
    tt = pos.shape[0]
    posf = pos.astype(F32)

    def tab(rot, period, scale):
        half = rot // 2
        inv = ROPE_THETA ** (-jnp.arange(half, dtype=F32) * (2.0 / rot))
        ang = posf[:, None] * inv[None, :]
        cos, sin = jnp.cos(ang), jnp.sin(ang)
        zh = jnp.zeros((tt, half), F32)
        rest = period - rot
        a = jnp.concatenate([cos, cos, jnp.ones((tt, rest), F32)], axis=1) * scale
        b = jnp.concatenate([-sin, zh, jnp.zeros((tt, rest), F32)], axis=1) * scale
        c = jnp.concatenate([zh, sin, jnp.zeros((tt, rest), F32)], axis=1) * scale
        return a, b, c

    head = jnp.stack([jnp.tile(t, (1, LANE // HEAD_DIM)) for t in tab(HEAD_DIM // ROT_FRACTION_DIV, HEAD_DIM, 1.0)])
    idxq = jnp.stack([jnp.tile(t, (1, LANE // IDX_DIM))
                      for t in tab(IDX_DIM // ROT_FRACTION_DIV, IDX_DIM, IDX_DIM ** -0.5)])
    ka, kb, kc = tab(IDX_DIM // ROT_FRACTION_DIV, IDX_DIM, 1.0)
    rest = LANE - IDX_DIM
    lane = jnp.arange(rest)
    tail_a = jnp.where(lane < (L_GB - L_IW), IDX_HEADS ** -0.5, 1.0).astype(F32)
    ma = jnp.concatenate([ka, jnp.broadcast_to(tail_a, (tt, rest))], axis=1)
    mb = jnp.concatenate([kb, jnp.zeros((tt, rest), F32)], axis=1)
    mc = jnp.concatenate([kc, jnp.zeros((tt, rest), F32)], axis=1)
    misc = jnp.stack([ma, mb, mc])
    return jnp.stack([head, idxq, misc])


def _reorder_w_in(w_in):
    offs = {}
    o = 0
    for name, wdt in (("aq", 2048), ("ak", 512), ("av", 512), ("iq", 1024), ("ik", 64), ("iw", 16), ("gq", 1024),
                      ("gk", 1024), ("gv", 2048), ("gz", 2048), ("gb", 16), ("ga", 16), ("gl", 4096)):
        offs[name] = (o, wdt)
        o += wdt

    def seg(name):
        s, wdt = offs[name]
        return w_in[:, s:s + wdt]

    k = w_in.shape[0]
    misc = jnp.concatenate([seg("ik"), seg("iw"), seg("gb"), seg("ga"), jnp.zeros((k, LANE - 112), w_in.dtype)], axis=1)
    parts = [seg("aq"), seg("ak"), seg("av"), seg("iq"), seg("gq"), seg("gk"), seg("gv"), seg("gz"), seg("gl"), misc,
             jnp.zeros((k, PROJ_W - C_MISC - LANE), w_in.dtype)]
    return jnp.concatenate(parts, axis=1).astype(BF16)


def _order_key(score):
    bits = pltpu.bitcast(score, I32)
    return bits ^ ((bits >> 31) & 0x7FFFFFFF)


def _kth_largest_key(count_ge, n_rows, n_sel):
    def body(i, t):
        bit = lax.shift_left(jnp.int32(1), 31 - i)
        cand = t ^ bit
        return jnp.where(count_ge(cand) >= n_sel, cand, t)

    return lax.fori_loop(0, 32, body, jnp.full((n_rows, 1), INT_MIN, I32))


def _attn_prompt_kernel(q_ref, iq_ref, mq_ref, k_ref, v_ref, mk_ref, o_ref, key_scr, *, tq, kc, n_sel):
    qb = pl.program_id(1)
    q0 = qb * tq
    nkc = (q0 + tq + kc - 1) // kc
    qpos = q0 + lax.broadcasted_iota(I32, (tq, 1), 0)
    kiota = lax.broadcasted_iota(I32, (1, kc), 1)
    wq = mq_ref[...]

    def score_body(c, carry):
        ks = pl.multiple_of(c * kc, kc)
        kt = mk_ref[pl.ds(ks, kc), :][:, :IDX_DIM]
        acc = jnp.zeros((tq, kc), F32)
        for h in range(IDX_HEADS):
            s = lax.dot_general(iq_ref[:, h * IDX_DIM:(h + 1) * IDX_DIM], kt, NT_DIMS, preferred_element_type=F32)
            acc = acc + wq[:, L_IW + h:L_IW + h + 1] * jnp.maximum(s, 0.0)
        acc = jnp.where(ks + kiota <= qpos, acc, -jnp.inf)
        key_scr[c] = _order_key(acc)
        return carry

    lax.fori_loop(0, nkc, score_body, 0)

    def count_ge(cand):
        def body(c, acc):
            m = jnp.where(key_scr[c] >= cand, 1.0, 0.0)
            for s in range(kc // LANE):
                acc = acc + m[:, s * LANE:(s + 1) * LANE]
            return acc

        acc = lax.fori_loop(0, nkc, body, jnp.zeros((tq, LANE), F32))
        return jnp.sum(acc, axis=1, keepdims=True)

    thr = _kth_largest_key(count_ge, tq, float(n_sel))

    group = N_HEADS // N_KV_HEADS
    scale = HEAD_DIM ** -0.5
    qss = []
    for n in range(N_KV_HEADS):
        qs = jnp.concatenate(
            [q_ref[:, (n * group + g) * HEAD_DIM:(n * group + g + 1) * HEAD_DIM] for g in range(group)], axis=0)
        qss.append((qs.astype(F32) * scale).astype(BF16))

    def attn_body(c, carry):
        ks = pl.multiple_of(c * kc, kc)
        sel = (key_scr[c] >= thr) & (ks + kiota <= qpos)
        bias = jnp.where(sel, 0.0, NEG_BIG)[None]
        new = []
        for n in range(N_KV_HEADS):
            m_i, l_i, acc = carry[n]
            kn = k_ref[pl.ds(ks, kc), n * HEAD_DIM:(n + 1) * HEAD_DIM]
            vn = v_ref[pl.ds(ks, kc), n * HEAD_DIM:(n + 1) * HEAD_DIM]
            s = lax.dot_general(qss[n], kn, NT_DIMS, preferred_element_type=F32)
            s = (s.reshape(group, tq, kc) + bias).reshape(group * tq, kc)
            m_new = jnp.maximum(m_i, jnp.max(s, axis=1, keepdims=True))
            alpha = jnp.exp(m_i - m_new)
            p = jnp.exp(s - m_new)
            l_new = alpha * l_i + jnp.sum(p, axis=1, keepdims=True)
            acc = alpha * acc + jnp.dot(p.astype(BF16), vn, preferred_element_type=F32)
            new.append((m_new, l_new, acc))
        return tuple(new)

    init = tuple((jnp.full((group * tq, 1), NEG_BIG, F32), jnp.zeros((group * tq, 1), F32),
                  jnp.zeros((group * tq, HEAD_DIM), F32)) for _ in range(N_KV_HEADS))
    fin = lax.fori_loop(0, nkc, attn_body, init)
    for n in range(N_KV_HEADS):
        _, l_f, acc_f = fin[n]
        out = acc_f / l_f
        for g in range(group):
            o_ref[:, (n * group + g) * HEAD_DIM:(n * group + g + 1) * HEAD_DIM] = out[g * tq:(g + 1) * tq].astype(BF16)


def _attn_prompt(p32, p16, bsz, t, tq=128, kc=512):
    nq = t // tq
    kc = min(kc, t)
    n_sel = min(TOPK_MAX, t // 4)
    kern = functools.partial(_attn_prompt_kernel, tq=tq, kc=kc, n_sel=n_sel)
    return pl.pallas_call(
        kern,
        grid=(bsz, nq),
        in_specs=[pl.BlockSpec((tq, N_HEADS * HEAD_DIM), lambda b, i: (b * nq + i, 0)),
                  pl.BlockSpec((tq, IDX_HEADS * IDX_DIM), lambda b, i: (b * nq + i, C_IQ // (IDX_HEADS * IDX_DIM))),
                  pl.BlockSpec((tq, LANE), lambda b, i: (b * nq + i, MISC_BLK)),
                  pl.BlockSpec((t, N_KV_HEADS * HEAD_DIM), lambda b, i: (b, C_AK // (N_KV_HEADS * HEAD_DIM))),
                  pl.BlockSpec((t, N_KV_HEADS * HEAD_DIM), lambda b, i: (b, C_AV // (N_KV_HEADS * HEAD_DIM))),
                  pl.BlockSpec((t, LANE), lambda b, i: (b, MISC_BLK))],
        out_specs=pl.BlockSpec((tq, N_HEADS * HEAD_DIM), lambda b, i: (b * nq + i, 0)),
        out_shape=jax.ShapeDtypeStruct((bsz * t, N_HEADS * HEAD_DIM), BF16),
        scratch_shapes=[pltpu.VMEM((t // kc, tq, kc), I32)],
        compiler_params=_cparams(("parallel", "arbitrary")),
        name="attn_prompt",
    )(p16, p16, p32, p16, p16, p16)


def _attn_sample_kernel(pt_ref, iq_ref, wc_ref, q_ref, kn_ref, vn_ref, ikn_ref, *rest, n_pages, n_sel, past):
    kidx_refs = rest[:n_pages]
    k_refs = rest[n_pages:2 * n_pages]
    v_refs = rest[2 * n_pages:3 * n_pages]
    o_ref, sc_scr, selx_scr, s_scr = rest[3 * n_pages:]
    del pt_ref
    rows, nkv = 8, N_KV_HEADS
    n_keys = (n_pages + 1) * PAGE_SIZE
    pw = PAGE_SIZE * nkv
    iq = iq_ref[...].astype(BF16)
    wc = wc_ref[...]

    def idx_score(kt, dims):
        s = lax.dot_general(iq, kt.astype(BF16), dims, preferred_element_type=F32)
        x = jnp.maximum(s, 0.0) * wc
        return jnp.sum(x.reshape(IDX_HEADS, rows, PAGE_SIZE), axis=0)

    nn_dims = (((1,), (0,)), ((), ()))
    for p in range(n_pages):
        sc_scr[:, p * PAGE_SIZE:(p + 1) * PAGE_SIZE] = idx_score(kidx_refs[p][...], nn_dims)
    sc_scr[:, n_pages * PAGE_SIZE:] = idx_score(ikn_ref[...], NT_DIMS)

    qpos = past + lax.broadcasted_iota(I32, (rows, 1), 0)
    kpos = lax.broadcasted_iota(I32, (1, n_keys), 1)
    causal = kpos <= qpos
    keys = _order_key(jnp.where(causal, sc_scr[...], -jnp.inf))

    def count_ge(cand):
        return jnp.sum(jnp.where(keys >= cand, 1.0, 0.0), axis=1, keepdims=True)

    thr = _kth_largest_key(count_ge, rows, float(n_sel))
    sel = jnp.where((keys >= thr) & causal, 1.0, 0.0).astype(BF16)
    expand = jnp.where((lax.broadcasted_iota(I32, (PAGE_SIZE, pw), 1) >> 2)
                       == lax.broadcasted_iota(I32, (PAGE_SIZE, pw), 0), 1.0, 0.0).astype(BF16)
    for p in range(n_pages + 1):
        selx_scr[:, p * pw:(p + 1) * pw] = jnp.dot(sel[:, p * PAGE_SIZE:(p + 1) * PAGE_SIZE], expand,
                                                   preferred_element_type=F32)
    n_q = N_HEADS * rows
    own_head = ((lax.broadcasted_iota(I32, (n_q, pw), 0) >> 5)
                == (lax.broadcasted_iota(I32, (n_q, pw), 1) & (nkv - 1))).reshape(N_HEADS, rows, pw)
    q = (q_ref[...] * (HEAD_DIM ** -0.5)).astype(BF16)
    m_parts = []
    for p in range(n_pages + 1):
        kb = (k_refs[p] if p < n_pages else kn_ref)[...].astype(BF16)
        s = lax.dot_general(q, kb, NT_DIMS, preferred_element_type=F32)
        ok = own_head & (selx_scr[:, p * pw:(p + 1) * pw] > 0.5)[None]
        s = jnp.where(ok, s.reshape(N_HEADS, rows, pw), NEG_BIG).reshape(n_q, pw)
        s_scr[:, p * pw:(p + 1) * pw] = s
        m_parts.append(jnp.max(s, axis=1, keepdims=True))
    m = m_parts[0]
    for mp in m_parts[1:]:
        m = jnp.maximum(m, mp)
    l_parts, o_parts = [], []
    for p in range(n_pages + 1):
        vb = (v_refs[p] if p < n_pages else vn_ref)[...].astype(BF16)
        pr = jnp.exp(s_scr[:, p * pw:(p + 1) * pw] - m)
        l_parts.append(jnp.sum(pr, axis=1, keepdims=True))
        o_parts.append(jnp.dot(pr.astype(BF16), vb, preferred_element_type=F32))
    l = l_parts[0]
    acc = o_parts[0]
    for lp, op in zip(l_parts[1:], o_parts[1:]):
        l = l + lp
        acc = acc + op
    o_ref[...] = acc / l


def _attn_sample(p32, cache_k, cache_v, cache_kidx, page_table, ts):
    db, n_pages = page_table.shape
    past = n_pages * PAGE_SIZE
    rows = 8
    group = N_HEADS // N_KV_HEADS
    ps = p32.reshape(db, ts, PROJ_W)
    pad_t = ((0, 0), (0, 0), (0, rows - ts), (0, 0))
    iq = ps[:, :, C_IQ:C_IQ + IDX_HEADS * IDX_DIM].reshape(db, ts, IDX_HEADS, IDX_DIM).transpose(0, 2, 1, 3)
    iq = jnp.pad(iq, pad_t).reshape(db, IDX_HEADS * rows, IDX_DIM)
    wc = ps[:, :, C_MISC + L_IW:C_MISC + L_IW + IDX_HEADS].transpose(0, 2, 1)
    wc = jnp.pad(wc, ((0, 0), (0, 0), (0, rows - ts))).reshape(db, IDX_HEADS * rows, 1)
    q = ps[:, :, :N_HEADS * HEAD_DIM].reshape(db, ts, N_HEADS, HEAD_DIM).transpose(0, 2, 1, 3)
    q = jnp.pad(q, pad_t).reshape(db, N_HEADS * rows, HEAD_DIM)
    pad_k = ((0, 0), (0, PAGE_SIZE - ts), (0, 0))
    pw = PAGE_SIZE * N_KV_HEADS
    kn = jnp.pad(ps[:, :, C_AK:C_AK + N_KV_HEADS * HEAD_DIM], pad_k).reshape(db, pw, HEAD_DIM)
    vn = jnp.pad(ps[:, :, C_AV:C_AV + N_KV_HEADS * HEAD_DIM], pad_k).reshape(db, pw, HEAD_DIM)
    ikn = jnp.pad(ps[:, :, C_MISC:C_MISC + IDX_DIM], pad_k)
    n_sel = min(TOPK_MAX, (past + ts) // 4)
    n_phys = cache_k.shape[1]
    ck = cache_k.reshape(1, n_phys, pw, HEAD_DIM)
    cv = cache_v.reshape(1, n_phys, pw, HEAD_DIM)

    def kv_page(p):
        return pl.BlockSpec((None, None, pw, HEAD_DIM), lambda d, pt, p=p: (0, pt[d * n_pages + p], 0, 0))

    ckt = jnp.swapaxes(cache_kidx, 2, 3)

    def idx_page(p):
        return pl.BlockSpec((None, None, IDX_DIM, PAGE_SIZE), lambda d, pt, p=p: (0, pt[d * n_pages + p], 0, 0))

    def per_d(r, w):
        return pl.BlockSpec((None, r, w), lambda d, pt: (d, 0, 0))

    in_specs = ([per_d(IDX_HEADS * rows, IDX_DIM), per_d(IDX_HEADS * rows, 1), per_d(N_HEADS * rows, HEAD_DIM),
                 per_d(pw, HEAD_DIM), per_d(pw, HEAD_DIM), per_d(PAGE_SIZE, IDX_DIM)]
                + [idx_page(p) for p in range(n_pages)]
                + [kv_page(p) for p in range(n_pages)]
                + [kv_page(p) for p in range(n_pages)])
    n_keys = (n_pages + 1) * PAGE_SIZE
    grid_spec = pltpu.PrefetchScalarGridSpec(
        num_scalar_prefetch=1, grid=(db,), in_specs=in_specs,
        out_specs=per_d(N_HEADS * rows, HEAD_DIM),
        scratch_shapes=[pltpu.VMEM((rows, n_keys), F32), pltpu.VMEM((rows, n_keys * N_KV_HEADS), F32),
                        pltpu.VMEM((N_HEADS * rows, n_keys * N_KV_HEADS), F32)])
    kern = functools.partial(_attn_sample_kernel, n_pages=n_pages, n_sel=n_sel, past=past)
    o = pl.pallas_call(
        kern, grid_spec=grid_spec,
        out_shape=jax.ShapeDtypeStruct((db, N_HEADS * rows, HEAD_DIM), F32),
        compiler_params=_cparams(("arbitrary",)),
        name="attn_sample",
    )(page_table.reshape(-1), iq, wc, q, kn, vn, ikn,
      *([ckt] * n_pages), *([ck] * n_pages), *([cv] * n_pages))
    o = o.reshape(db, N_HEADS, rows, HEAD_DIM)[:, :, :ts].transpose(0, 2, 1, 3)
    return o.reshape(db * ts, N_HEADS * HEAD_DIM).astype(BF16)


def _dot3s(ah, al, bh, bl, dims=None):
    if dims is None:
        d = lambda x, y: jnp.dot(x, y, preferred_element_type=F32)
    else:
        d = lambda x, y: lax.dot_general(x, y, dims, preferred_element_type=F32)
    return d(ah, bh) + d(al, bh) + d(ah, bl)


def _cumsum_rows(sel_bf16, g_rows):
    gb = jnp.broadcast_to(g_rows, (g_rows.shape[0], LANE))
    g_hi, g_lo = _split_bf16(gb)
    g_lo2 = (gb - g_hi.astype(F32) - g_lo.astype(F32)).astype(BF16)
    d = lambda y: jnp.dot(sel_bf16, y, preferred_element_type=F32)
    return d(g_hi) + d(g_lo) + d(g_lo2)


def _l2n(x):
    return x * lax.rsqrt(jnp.sum(x * x, axis=-1, keepdims=True) + NORM_EPS)


def _gate_params(misc, hp, live):
    x = misc + hp[1:2, :]
    softplus = jnp.maximum(x, 0.0) + jnp.log(1.0 + jnp.exp(-jnp.abs(x)))
    g_all = -jnp.exp(hp[0:1, :]) * softplus
    beta_all = _sigmoid(misc)
    if live is not None:
        g_all = jnp.where(live, g_all, 0.0)
        beta_all = jnp.where(live, beta_all, 0.0)
    return g_all, beta_all


def _gdn_kernel(q_ref, k_ref, v_ref, z_ref, m_ref, wq_ref, wk_ref, wv_ref, cq_ref, ck_ref, cv_ref, hp_ref, nw_ref,
                s0_ref, o_ref, sf_ref, s_scr, hq_scr, hk_scr, hv_scr, xc_scr, *, c, nh):
    g_idx = pl.program_id(1)
    tb = pl.program_id(2)
    n_tb = pl.num_programs(2)
    dk, dv = GDN_DK, GDN_DV

    @pl.when(tb == 0)
    def _():
        s_scr[...] = s0_ref[...]
        hq_scr[...] = cq_ref[...]
        hk_scr[...] = ck_ref[...]
        hv_scr[...] = cv_ref[...]

    def conv_silu(x_ref, w_ref, h_scr):
        width = x_ref.shape[1]
        x = x_ref[...]
        xc_scr[0:8, 0:width] = h_scr[...]
        xc_scr[8:8 + c, 0:width] = x
        acc = jnp.zeros((c, width), F32)
        for j in range(CONV_W):
            acc = acc + xc_scr[8 - (CONV_W - 1) + j:8 - (CONV_W - 1) + j + c, 0:width] * w_ref[j:j + 1, :]
        h_scr[...] = x[c - 8:c, :]
        return acc * _sigmoid(acc)

    qa = conv_silu(q_ref, wq_ref, hq_scr)
    ka = conv_silu(k_ref, wk_ref, hk_scr)
    va = conv_silu(v_ref, wv_ref, hv_scr)
    g_all, beta_all = _gate_params(m_ref[...], hp_ref[...], None)
    lane = lax.broadcasted_iota(I32, (1, LANE), 1)
    ri = lax.broadcasted_iota(I32, (c, c), 0)
    ci = lax.broadcasted_iota(I32, (c, c), 1)
    causal = ri >= ci
    strict = ri > ci
    tri = jnp.where(causal, 1.0, 0.0).astype(BF16)
    eye = jnp.where(ri == ci, 1.0, 0.0)

    def same_blk(log_b):
        return (ri >> log_b) == (ci >> log_b)

    heads = range(2 * nh)
    ks = [_l2n(ka[:, hh * dk:(hh + 1) * dk]) for hh in range(nh)]
    qss = [_l2n(qa[:, hh * dk:(hh + 1) * dk]) * (dk ** -0.5) for hh in range(nh)]
    kk_qk = [lax.dot_general(jnp.concatenate([ks[hh], qss[hh]], axis=0).astype(BF16), ks[hh].astype(BF16), NT_DIMS,
                             preferred_element_type=F32) for hh in range(nh)]
    g_cols = [jnp.sum(jnp.where(lane == L_GA + 2 * nh * g_idx + hl, g_all, 0.0), axis=1, keepdims=True)
              for hl in heads]
    betas = [jnp.sum(jnp.where(lane == L_GB + 2 * nh * g_idx + hl, beta_all, 0.0), axis=1, keepdims=True)
             for hl in heads]
    gcs = [_cumsum_rows(tri, g) for g in g_cols]
    decays = [jnp.exp(jnp.where(causal, gc - gc.T, -jnp.inf)) for gc in gcs]
    a_mats = [jnp.where(strict, kk_qk[hl // 2][0:c] * betas[hl] * decays[hl], 0.0) for hl in heads]
    nms = [jnp.where(same_blk(3), -a, 0.0) for a in a_mats]
    pms = [eye + n for n in nms]
    for _ in range(2):
        nms = [_dot3(n, n) for n in nms]
        pms = [p + _dot3(p, n) for p, n in zip(pms, nms)]
    for lb in range(3, (c - 1).bit_length()):
        join = same_blk(lb + 1) & jnp.logical_not(same_blk(lb))
        t1s = [_dot3(jnp.where(join, a, 0.0), p) for a, p in zip(a_mats, pms)]
        pms = [p - _dot3(p, t1) for p, t1 in zip(pms, t1s)]
    us = [_dot3(pms[hl], va[:, hl * dv:(hl + 1) * dv] * betas[hl]) for hl in heads]
    ws = [_dot3(pms[hl], ks[hl // 2] * betas[hl] * jnp.exp(gcs[hl])) for hl in heads]
    qks = [jnp.where(causal, kk_qk[hl // 2][c:2 * c] * decays[hl], 0.0) for hl in heads]
    s_in = [s_scr[hl] for hl in heads]
    v_news = [us[hl] - _dot3(ws[hl], s_in[hl]) for hl in heads]
    os_ = [jnp.dot((qss[hl // 2] * jnp.exp(gcs[hl])).astype(BF16), s_in[hl].astype(BF16), preferred_element_type=F32)
           + jnp.dot(qks[hl].astype(BF16), v_news[hl].astype(BF16), preferred_element_type=F32) for hl in heads]
    for hl in heads:
        g_last = gcs[hl][c - 1:c, :]
        kdec = ks[hl // 2] * jnp.exp(g_last - gcs[hl])
        s_scr[hl] = s_in[hl] * jnp.exp(g_last) + _dot3(kdec.T, v_news[hl])
    for hl in heads:
        o = os_[hl]
        z = z_ref[:, hl * dv:(hl + 1) * dv]
        on = o * lax.rsqrt(jnp.mean(o * o, axis=-1, keepdims=True) + NORM_EPS) * nw_ref[...]
        o_ref[:, hl * dv:(hl + 1) * dv] = (on * (z * _sigmoid(z))).astype(BF16)

    @pl.when(tb == n_tb - 1)
    def _():
        sf_ref[...] = s_scr[...]


def _gdn_hp(a_log, dt_bias):
    hp = jnp.zeros((8, LANE), F32)
    return hp.at[0, L_GA:L_GA + GDN_V_HEADS].set(a_log.astype(F32)).at[1, L_GA:L_GA + GDN_V_HEADS].set(
        dt_bias.astype(F32))


def _gdn(p32, conv0, s0, conv_w, a_log, dt_bias, norm_w, bsz, t, c=128, nh=4):
    n_tb = t // c
    n_g = GDN_QK_HEADS // nh
    qw, vw = nh * GDN_DK, 2 * nh * GDN_DV
    row = lambda b, g, i: b * n_tb + i
    in_specs = [
        pl.BlockSpec((c, qw), lambda b, g, i: (row(b, g, i), C_GQ // qw + g)),
        pl.BlockSpec((c, qw), lambda b, g, i: (row(b, g, i), C_GK // qw + g)),
        pl.BlockSpec((c, vw), lambda b, g, i: (row(b, g, i), C_GV // vw + g)),
        pl.BlockSpec((c, vw), lambda b, g, i: (row(b, g, i), C_GZ // vw + g)),
        pl.BlockSpec((c, LANE), lambda b, g, i: (row(b, g, i), MISC_BLK)),
        pl.BlockSpec((CONV_W, qw), lambda b, g, i: (0, g)),
        pl.BlockSpec((CONV_W, qw), lambda b, g, i: (0, n_g + g)),
        pl.BlockSpec((CONV_W, vw), lambda b, g, i: (0, n_g + g)),
        pl.BlockSpec((None, 8, qw), lambda b, g, i: (b, 0, g)),
        pl.BlockSpec((None, 8, qw), lambda b, g, i: (b, 0, n_g + g)),
        pl.BlockSpec((None, 8, vw), lambda b, g, i: (b, 0, n_g + g)),
        pl.BlockSpec((8, LANE), lambda b, g, i: (0, 0)),
        pl.BlockSpec((1, GDN_DV), lambda b, g, i: (0, 0)),
        pl.BlockSpec((None, 2 * nh, GDN_DK, GDN_DV), lambda b, g, i: (b, g, 0, 0)),
    ]
    out_specs = [pl.BlockSpec((c, vw), lambda b, g, i: (row(b, g, i), g)),
                 pl.BlockSpec((None, 2 * nh, GDN_DK, GDN_DV), lambda b, g, i: (b, g, 0, 0))]
    return pl.pallas_call(
        functools.partial(_gdn_kernel, c=c, nh=nh),
        grid=(bsz, n_g, n_tb),
        in_specs=in_specs, out_specs=out_specs,
        out_shape=[jax.ShapeDtypeStruct((bsz * t, GDN_V_HEADS * GDN_DV), BF16),
                   jax.ShapeDtypeStruct((bsz, GDN_V_HEADS, GDN_DK, GDN_DV), F32)],
        scratch_shapes=[pltpu.VMEM((2 * nh, GDN_DK, GDN_DV), F32), pltpu.VMEM((8, qw), F32), pltpu.VMEM((8, qw), F32),
                        pltpu.VMEM((8, vw), F32), pltpu.VMEM((c + 8, vw), F32)],
        compiler_params=_cparams(("parallel", "parallel", "arbitrary")),
        name="gdn",
    )(p32, p32, p32, p32, p32, conv_w, conv_w, conv_w, conv0, conv0, conv0, _gdn_hp(a_log, dt_bias),
      norm_w.reshape(1, GDN_DV), s0)


def _gdn_sample_kernel(x_ref, z_ref, m_ref, w_ref, hp_ref, nw_ref, s0_ref, o_ref, sf_ref, *, ts):
    rows, nvh, dk, dv = 8, GDN_V_HEADS, GDN_DK, GDN_DV
    qd = GDN_QK_HEADS * dk
    x8 = x_ref[...]
    w = w_ref[...]
    acc = x8 * w[0:1, :]
    for j in range(1, CONV_W):
        acc = acc + pltpu.roll(x8, rows - j, 0) * w[j:j + 1, :]
    act = acc * _sigmoid(acc)
    live = lax.broadcasted_iota(I32, (rows, 1), 0) < ts
    g_all, beta_all = _gate_params(m_ref[...], hp_ref[...], live)
    qn = [_l2n(act[:, g * dk:(g + 1) * dk]) * (dk ** -0.5) for g in range(GDN_QK_HEADS)]
    kn = [_l2n(act[:, qd + g * dk:qd + (g + 1) * dk]) for g in range(GDN_QK_HEADS)]
    rep = nvh // GDN_QK_HEADS
    cat = lambda parts: jnp.concatenate(parts, axis=0)
    k = cat([kn[h // rep] for h in range(nvh)])
    qs = cat([qn[h // rep] for h in range(nvh)])
    v = cat([act[:, 2 * qd + h * dv:2 * qd + (h + 1) * dv] for h in range(nvh)])
    z = cat([z_ref[:, h * dv:(h + 1) * dv] for h in range(nvh)])
    beta = cat([beta_all[:, L_GB + h:L_GB + h + 1] for h in range(nvh)])
    g_col = cat([g_all[:, L_GA + h:L_GA + h + 1] for h in range(nvh)])
    n = nvh * rows
    ri = lax.broadcasted_iota(I32, (n, n), 0)
    ci = lax.broadcasted_iota(I32, (n, n), 1)
    same = (ri >> 3) == (ci >> 3)
    causal = same & (ri >= ci)
    strict = same & (ri > ci)
    eye = jnp.where(ri == ci, 1.0, 0.0)
    gc = _cumsum_rows(jnp.where(causal, 1.0, 0.0).astype(BF16), g_col)
    gl = _cumsum_rows(jnp.where(same, 1.0, 0.0).astype(BF16), g_col)
    decay = jnp.exp(jnp.where(causal, gc - gc.T, -jnp.inf))
    kb = k * beta
    a_mat = jnp.where(strict, _dot3(kb, k, NT_DIMS) * decay, 0.0)
    nm = -a_mat
    pm = eye + nm
    for _ in range(2):
        nm = _dot3(nm, nm)
        pm = pm + _dot3(pm, nm)
    u = _dot3(pm, v * beta)
    w_rows = _dot3(pm, kb * jnp.exp(gc))
    qk = jnp.where(causal, _dot3(qs, k, NT_DIMS) * decay, 0.0)
    qg = qs * jnp.exp(gc)
    kdt = (k * jnp.exp(gl - gc)).T
    s_in = [s0_ref[h] for h in range(nvh)]
    v_new, o1 = [], []
    for h in range(nvh):
        sl = slice(h * rows, (h + 1) * rows)
        r = _dot3(cat([w_rows[sl], qg[sl]]), s_in[h])
        v_new.append(u[sl] - r[0:rows])
        o1.append(r[rows:2 * rows])
    v_new = cat(v_new)
    o = cat(o1) + _dot3(qk, v_new)
    vh, vl = _split_bf16(v_new)
    lane = lax.broadcasted_iota(I32, (1, n), 1)
    for h in range(nvh):
        kh, kl = _split_bf16(jnp.where((lane >> 3) == h, kdt, 0.0))
        sf_ref[h] = s_in[h] * jnp.exp(gl[h * rows:h * rows + 1, :]) + _dot3s(kh, kl, vh, vl)
    on = o * lax.rsqrt(jnp.mean(o * o, axis=-1, keepdims=True) + NORM_EPS) * nw_ref[...]
    o_ref[...] = on * (z * _sigmoid(z))


def _gdn_sample(p32, state_conv, s0, conv_w, a_log, dt_bias, norm_w, db, ts):
    rows = 8
    conv_dim = conv_w.shape[-1]
    ps = p32.reshape(db, ts, PROJ_W)
    x8 = jnp.concatenate([state_conv, ps[:, :, C_GQ:C_GQ + conv_dim],
                          jnp.zeros((db, rows - ts - (CONV_W - 1), conv_dim), F32)], axis=1)
    pad_t = ((0, 0), (0, rows - ts), (0, 0))
    z8 = jnp.pad(ps[:, :, C_GZ:C_GZ + GDN_V_HEADS * GDN_DV], pad_t)
    m8 = jnp.pad(ps[:, :, C_MISC:C_MISC + LANE], pad_t)
    n = GDN_V_HEADS * rows
    per_d = lambda w: pl.BlockSpec((None, rows, w), lambda d: (d, 0, 0))
    state = pl.BlockSpec((None, GDN_V_HEADS, GDN_DK, GDN_DV), lambda d: (d, 0, 0, 0))
    o, sf = pl.pallas_call(
        functools.partial(_gdn_sample_kernel, ts=ts),
        grid=(db,),
        in_specs=[per_d(conv_dim), per_d(GDN_V_HEADS * GDN_DV), per_d(LANE),
                  pl.BlockSpec((CONV_W, conv_dim), lambda d: (0, 0)),
                  pl.BlockSpec((8, LANE), lambda d: (0, 0)),
                  pl.BlockSpec((1, GDN_DV), lambda d: (0, 0)), state],
        out_specs=[pl.BlockSpec((None, n, GDN_DV), lambda d: (d, 0, 0)), state],
        out_shape=[jax.ShapeDtypeStruct((db, n, GDN_DV), F32),
                   jax.ShapeDtypeStruct((db, GDN_V_HEADS, GDN_DK, GDN_DV), F32)],
        compiler_params=_cparams(("parallel",)),
        name="gdn_sample",
    )(x8, z8, m8, conv_w, _gdn_hp(a_log, dt_bias), norm_w.reshape(1, GDN_DV), s0)
    o = o.reshape(db, GDN_V_HEADS, rows, GDN_DV)[:, :, :ts].transpose(0, 2, 1, 3)
    return o.reshape(db * ts, GDN_V_HEADS * GDN_DV).astype(BF16), sf


def _merge_kernel(a_ref, g_ref, wa_ref, wg_ref, ga_ref, gg_ref, o_ref):
    pa = jnp.dot(a_ref[...], wa_ref[...], preferred_element_type=F32)
    pg = jnp.dot(g_ref[...], wg_ref[...], preferred_element_type=F32)
    o_ref[...] = (_sigmoid(ga_ref[...]) * pa + _sigmoid(gg_ref[...]) * pg).astype(BF16)


def _merge(attn_o, gdn_o, w_oa, w_og, p32, tm):
    m = attn_o.shape[0]
    tn = 512
    return pl.pallas_call(
        _merge_kernel,
        grid=(m // tm, D_MODEL // tn),
        in_specs=[pl.BlockSpec((tm, D_MODEL), lambda i, j: (i, 0)),
                  pl.BlockSpec((tm, D_MODEL), lambda i, j: (i, 0)),
                  pl.BlockSpec((D_MODEL, tn), lambda i, j: (0, j)),
                  pl.BlockSpec((D_MODEL, tn), lambda i, j: (0, j)),
                  pl.BlockSpec((tm, tn), lambda i, j: (i, C_GL // tn + j)),
                  pl.BlockSpec((tm, tn), lambda i, j: (i, (C_GL + D_MODEL) // tn + j))],
        out_specs=pl.BlockSpec((tm, tn), lambda i, j: (i, j)),
        out_shape=jax.ShapeDtypeStruct((m, D_MODEL), BF16),
        compiler_params=_cparams(("parallel", "arbitrary")),
        name="merge",
    )(attn_o, gdn_o, w_oa, w_og, p32, p32)


def _outproj_kernel(mg_ref, w_ref, x_ref, g1_ref, sc_ref, sh_ref, gn_ref, wr_ref, br_ref,
                    x1_ref, h2_ref, ti_ref, tg_ref):
    x1 = x_ref[...] + g1_ref[...] * jnp.dot(mg_ref[...], w_ref[...], preferred_element_type=F32)
    x1_ref[...] = x1
    ms = jnp.mean(x1 * x1, axis=-1, keepdims=True)
    h2 = x1 * lax.rsqrt(ms + NORM_EPS) * gn_ref[...] * (1.0 + sc_ref[...]) + sh_ref[...]
    h2_ref[...] = h2
    logits = _dot3(h2, wr_ref[...]) + br_ref[...]
    tm = logits.shape[0]
    lane = lax.broadcasted_iota(I32, (tm, LANE), 1)
    lanef = lane.astype(F32)
    l = jnp.where(lane < N_EXPERTS, logits, -jnp.inf)
    vals, idxs = [], []
    for _ in range(TOP_K):
        mx = jnp.max(l, axis=1, keepdims=True)
        ix = jnp.min(jnp.where(l == mx, lanef, float(LANE)), axis=1, keepdims=True)
        vals.append(mx)
        idxs.append(ix)
        l = jnp.where(lanef == ix, -jnp.inf, l)
    es = [jnp.exp(v - vals[0]) for v in vals]
    den = es[0]
    for e in es[1:]:
        den = den + e
    ti = jnp.zeros((tm, LANE), F32)
    tg = jnp.zeros((tm, LANE), F32)
    for kk in range(TOP_K):
        ti = jnp.where(lane == kk, idxs[kk], ti)
        tg = jnp.where(lane == kk, es[kk] / den, tg)
    ti_ref[...] = ti.astype(I32)
    tg_ref[...] = tg


def _outproj(merged, w_out, x, gate1, scale2, shift2, g_norm2, w_router, b_router, rows_per_mod, tm):
    m = x.shape[0]
    mod_rows = gate1.shape[1]
    tiles_per_mod = rows_per_mod // tm
    if mod_rows == 1:
        mod_spec = pl.BlockSpec((None, 1, D_MODEL), lambda i: (i // tiles_per_mod, 0, 0))
    else:
        mod_spec = pl.BlockSpec((None, tm, D_MODEL), lambda i: (i // tiles_per_mod, i % tiles_per_mod, 0))
    wr = jnp.pad(w_router.astype(F32), ((0, 0), (0, LANE - N_EXPERTS)))
    br = jnp.pad(b_router.astype(F32), (0, LANE - N_EXPERTS)).reshape(1, LANE)
    row = pl.BlockSpec((tm, D_MODEL), lambda i: (i, 0))
    small = pl.BlockSpec((tm, LANE), lambda i: (i, 0))
    return pl.pallas_call(
        _outproj_kernel,
        grid=(m // tm,),
        in_specs=[row, pl.BlockSpec((D_MODEL, D_MODEL), lambda i: (0, 0)), row, mod_spec, mod_spec, mod_spec,
                  pl.BlockSpec((1, D_MODEL), lambda i: (0, 0)),
                  pl.BlockSpec((D_MODEL, LANE), lambda i: (0, 0)),
                  pl.BlockSpec((1, LANE), lambda i: (0, 0))],
        out_specs=[row, row, small, small],
        out_shape=[jax.ShapeDtypeStruct((m, D_MODEL), F32), jax.ShapeDtypeStruct((m, D_MODEL), F32),
                   jax.ShapeDtypeStruct((m, LANE), I32), jax.ShapeDtypeStruct((m, LANE), F32)],
        compiler_params=_cparams(("parallel",)),
        name="outproj",
    )(merged, w_out, x, gate1, scale2, shift2, g_norm2.reshape(1, D_MODEL), wr, br)


GATHER_ROWS = 256


def _row_copy(src_ref, o_ref, sem, src_row, dst_row):
    return pltpu.make_async_copy(src_ref.at[pl.ds(src_row, 1)], o_ref.at[pl.ds(dst_row, 1)], sem)


def _gather_kernel(nr_ref, idx_ref, src_ref, o_ref, sem):
    n = o_ref.shape[0]
    live = pl.program_id(0) * n < nr_ref[0]

    @pl.when(live)
    def _():
        def issue(r, carry):
            _row_copy(src_ref, o_ref, sem, idx_ref[0, r], r).start()
            return carry

        def wait(r, carry):
            _row_copy(src_ref, o_ref, sem, 0, r).wait()
            return carry

        lax.fori_loop(0, n, issue, 0, unroll=8)
        lax.fori_loop(0, n, wait, 0, unroll=8)

    @pl.when(jnp.logical_not(live))
    def _():
        o_ref[...] = jnp.zeros_like(o_ref)


def _gather_rows(src, idx, n_live):
    n_valid, w = idx.shape[0], src.shape[1]
    n_steps = -(-n_valid // GATHER_ROWS)
    n = n_steps * GATHER_ROWS
    idx = jnp.pad(idx, (0, n - n_valid))
    grid_spec = pltpu.PrefetchScalarGridSpec(
        num_scalar_prefetch=1, grid=(n_steps,),
        in_specs=[pl.BlockSpec((None, 1, GATHER_ROWS), lambda i, nr: (i, 0, 0), memory_space=pltpu.SMEM),
                  pl.BlockSpec(memory_space=pl.ANY)],
        out_specs=pl.BlockSpec((GATHER_ROWS, w), lambda i, nr: (i, 0)),
        scratch_shapes=[pltpu.SemaphoreType.DMA(())])
    out = pl.pallas_call(
        _gather_kernel, grid_spec=grid_spec,
        out_shape=jax.ShapeDtypeStruct((n, w), src.dtype),
        compiler_params=pltpu.CompilerParams(dimension_semantics=("arbitrary",), vmem_limit_bytes=VMEM_LIMIT,
                                             disable_bounds_checks=True),
        name="gather_rows",
    )(n_live, idx.reshape(n_steps, 1, GATHER_ROWS), src)
    return out if n == n_valid else out[:n_valid]


def _moe_kernel(be_ref, nu_ref, x_ref, wg_ref, wu_ref, wd_ref, bg_ref, bu_ref, bd_ref, o_ref, x16_scr, h_scr,
                *, n_f, tf):
    i = pl.program_id(0)
    j = pl.program_id(1)
    used = i < nu_ref[0]
    del be_ref

    @pl.when(used & (j == 0))
    def _():
        x16_scr[...] = x_ref[...].astype(BF16)

    @pl.when(used & (j < n_f))
    def _():
        x = x16_scr[...]
        g = jnp.dot(x, wg_ref[...].astype(BF16), preferred_element_type=F32) + bg_ref[...]
        u = jnp.dot(x, wu_ref[...].astype(BF16), preferred_element_type=F32) + bu_ref[...]
        gate = jnp.minimum(g, SWIGLU_LIMIT)
        up = jnp.clip(u, -SWIGLU_LIMIT, SWIGLU_LIMIT)
        h_scr[j] = ((up + 1.0) * gate * _sigmoid(SWIGLU_ALPHA * gate)).astype(BF16)

    @pl.when(used & (j >= n_f))
    def _():
        acc = jnp.dot(h_scr[0], wd_ref[0:tf, :].astype(BF16), preferred_element_type=F32)
        for k in range(1, n_f):
            acc = acc + jnp.dot(h_scr[k], wd_ref[k * tf:(k + 1) * tf, :].astype(BF16), preferred_element_type=F32)
        o_ref[...] = acc + bd_ref[...]

    @pl.when(jnp.logical_not(used) & (j >= n_f))
    def _():
        o_ref[...] = jnp.zeros_like(o_ref)


def _moe_ffn(xs, blk_e, n_used, w_gu, b_gu, w_dn, b_dn, rb, tf):
    n_rows = xs.shape[0]
    n_blocks = n_rows // rb
    n_f = D_FF // tf
    tn = tf
    n_o = D_MODEL // tn
    bgu = b_gu.reshape(N_EXPERTS, 1, 2 * D_FF)
    bdn = b_dn.reshape(N_EXPERTS, 1, D_MODEL)

    def jf(i, j, nu):
        return jnp.where(i < nu[0], jnp.minimum(j, n_f - 1), 0)

    def jo(i, j, nu):
        return jnp.where(i < nu[0], jnp.maximum(j - n_f, 0), 0)

    grid_spec = pltpu.PrefetchScalarGridSpec(
        num_scalar_prefetch=2, grid=(n_blocks, n_f + n_o),
        in_specs=[pl.BlockSpec((rb, D_MODEL), lambda i, j, be, nu: (i, 0)),
                  pl.BlockSpec((None, None, D_MODEL, tf), lambda i, j, be, nu: (0, be[i], 0, jf(i, j, nu))),
                  pl.BlockSpec((None, None, D_MODEL, tf), lambda i, j, be, nu: (0, be[i], 0, n_f + jf(i, j, nu))),
                  pl.BlockSpec((None, None, D_FF, tn), lambda i, j, be, nu: (0, be[i], 0, jo(i, j, nu))),
                  pl.BlockSpec((None, 1, tf), lambda i, j, be, nu: (be[i], 0, jf(i, j, nu))),
                  pl.BlockSpec((None, 1, tf), lambda i, j, be, nu: (be[i], 0, n_f + jf(i, j, nu))),
                  pl.BlockSpec((None, 1, tn), lambda i, j, be, nu: (be[i], 0, jo(i, j, nu)))],
        out_specs=pl.BlockSpec((rb, tn), lambda i, j, be, nu: (i, jnp.maximum(j - n_f, 0))),
        scratch_shapes=[pltpu.VMEM((rb, D_MODEL), BF16), pltpu.VMEM((n_f, rb, tf), BF16)])
    return pl.pallas_call(
        functools.partial(_moe_kernel, n_f=n_f, tf=tf), grid_spec=grid_spec,
        out_shape=jax.ShapeDtypeStruct((n_rows, D_MODEL), F32),
        compiler_params=pltpu.CompilerParams(dimension_semantics=("parallel", "arbitrary"),
                                             vmem_limit_bytes=MOE_VMEM_LIMIT),
        name="moe_ffn",
    )(blk_e, n_used, xs, w_gu, w_gu, w_dn, bgu, bgu, bdn)


def _final_kernel(x1_ref, yg_ref, tg_ref, g2_ref, gf_ref, o_ref):
    tg = tg_ref[...]
    ffn = jnp.zeros_like(x1_ref)
    for kk in range(TOP_K):
        ffn = ffn + tg[:, kk:kk + 1] * yg_ref[kk]
    x2 = x1_ref[...] + g2_ref[...] * ffn
    ms = jnp.mean(x2 * x2, axis=-1, keepdims=True)
    o_ref[...] = x2 * lax.rsqrt(ms + NORM_EPS) * gf_ref[...]


def _final(x1, yg, row0, tg, gate2, g_final, rows_per_mod, tm):
    m = x1.shape[0]
    blk0 = row0 // tm
    mod_rows = gate2.shape[1]
    tiles_per_mod = rows_per_mod // tm
    if mod_rows == 1:
        mod_spec = pl.BlockSpec((None, 1, D_MODEL), lambda i: (i // tiles_per_mod, 0, 0))
    else:
        mod_spec = pl.BlockSpec((None, tm, D_MODEL), lambda i: (i // tiles_per_mod, i % tiles_per_mod, 0))
    row = pl.BlockSpec((tm, D_MODEL), lambda i: (i, 0))
    return pl.pallas_call(
        _final_kernel,
        grid=(m // tm,),
        in_specs=[row, pl.BlockSpec((TOP_K, tm, D_MODEL), lambda i: (0, blk0 + i, 0)),
                  pl.BlockSpec((tm, LANE), lambda i: (i, 0)), mod_spec,
                  pl.BlockSpec((1, D_MODEL), lambda i: (0, 0))],
        out_specs=row,
        out_shape=jax.ShapeDtypeStruct((m, D_MODEL), F32),
        compiler_params=_cparams(("parallel",)),
        name="final",
    )(x1, yg, tg, gate2, g_final.reshape(1, D_MODEL))


def _route(topi, rb):
    tt = topi.shape[0]
    sel = jnp.sum((topi[:, :, None] == jnp.arange(N_EXPERTS, dtype=I32)[None, None, :]).astype(I32), axis=1)
    counts = jnp.sum(sel, axis=0)
    padded = (counts + rb - 1) // rb * rb
    pad_end = jnp.cumsum(padded)
    pad_start = pad_end - padded
    rank = jnp.cumsum(sel, axis=0) - sel
    dest = jnp.take_along_axis(pad_start[None, :] + rank, topi, axis=1)
    n_blocks = -(-(tt * TOP_K + N_EXPERTS * (rb - 1)) // rb)
    n_rows = n_blocks * rb
    tok = jnp.repeat(jnp.arange(tt, dtype=I32), TOP_K)
    row_tok = (jnp.arange(n_rows, dtype=I32) % tt).at[dest.reshape(-1)].set(tok)
    blk_e = jnp.minimum(jnp.searchsorted(pad_end, jnp.arange(n_blocks, dtype=I32) * rb, side="right"),
                        N_EXPERTS - 1).astype(I32)
    n_used = (pad_end[-1] // rb).astype(I32).reshape(1)
    return dest, row_tok, blk_e, n_used


def kernel(x_prompt, x_sample, c_prompt, c_sample, cache_k, cache_v, cache_kidx, state_gdn, state_conv, page_table,
           w_ada, b_ada, g_norm1, g_norm2, g_final, w_in, gdn_conv_w, gdn_a_log, gdn_dt_bias, gdn_norm_w,
           w_o_attn, w_o_gdn, w_out, w_router, b_router, w_gu, b_gu, w_dn, b_dn):
    bsz, t, d = x_prompt.shape
    db, ts, _ = x_sample.shape
    depth = w_ada.shape[0]
    assert depth == 1 and d == D_MODEL
    past = page_table.shape[1] * PAGE_SIZE
    conv_dim = gdn_conv_w.shape[-1]

    n_c = bsz + db
    n_cp = -(-n_c // 8) * 8
    c_all = jnp.pad(jnp.concatenate([c_prompt, c_sample], axis=0), ((0, n_cp - n_c), (0, 0)))
    mod = _ada(c_all, w_ada, b_ada[0])
    mod_p = mod[:bsz].reshape(bsz, 1, 6, d)
    mod_s = jnp.repeat(mod[bsz:n_c], ts, axis=0).reshape(1, db * ts, 6, d)
    mods = {"p": [mod_p[:, :, i] for i in range(6)], "s": [mod_s[:, :, i] for i in range(6)]}

    w_r = _reorder_w_in(w_in[0])
    w_oa = w_o_attn[0].astype(BF16)
    w_og = w_o_gdn[0].astype(BF16)
    w_o = w_out[0].astype(BF16)

    xp = x_prompt.reshape(bsz * t, d)
    xs = x_sample.reshape(db * ts, d)
    tm_p = min(512, t)
    tm_s = db * ts

    tabs_p = _rot_tables(jnp.arange(t, dtype=I32))
    tabs_s = _rot_tables(jnp.tile(past + jnp.arange(ts, dtype=I32), db))
    p32_p, p16_p = _inproj(xp, g_norm1[0], mods["p"][1], mods["p"][0], w_r, tabs_p, t, tm_p)
    p32_s, _ = _inproj(xs, g_norm1[0], mods["s"][1], mods["s"][0], w_r, tabs_s, db * ts, tm_s)

    attn_p = _attn_prompt(p32_p, p16_p, bsz, t)
    attn_s = _attn_sample(p32_s, cache_k, cache_v, cache_kidx, page_table, ts)

    conv0_p = jnp.zeros((bsz, 8, conv_dim), F32)
    s0_p = jnp.zeros((bsz, GDN_V_HEADS, GDN_DK, GDN_DV), F32)
    gdn_p, sfin_p = _gdn(p32_p, conv0_p, s0_p, gdn_conv_w[0], gdn_a_log[0], gdn_dt_bias[0], gdn_norm_w[0], bsz, t)
    gdn_s, sfin_s = _gdn_sample(p32_s, state_conv[0], state_gdn[0], gdn_conv_w[0], gdn_a_log[0], gdn_dt_bias[0],
                                gdn_norm_w[0], db, ts)

    outs = {}
    for name, attn_o, gdn_o, p32, x, rows_per_mod, tm in (("p", attn_p, gdn_p, p32_p, xp, t, min(256, t)),
                                                          ("s", attn_s, gdn_s, p32_s, xs, db * ts, min(256, db * ts))):
        merged = _merge(attn_o, gdn_o, w_oa, w_og, p32, tm)
        m = mods[name]
        outs[name] = _outproj(merged, w_o, x, m[2], m[4], m[3], g_norm2[0], w_router[0], b_router[0], rows_per_mod, tm)

    rb, tf = 512, 512
    h2 = jnp.concatenate([outs["p"][1], outs["s"][1]], axis=0)
    topi = jnp.concatenate([outs["p"][2], outs["s"][2]], axis=0)[:, :TOP_K]
    dest, row_tok, blk_e, n_used = _route(topi, rb)
    yb = _moe_ffn(_gather_rows(h2, row_tok, n_used * rb), blk_e, n_used, w_gu, b_gu[0], w_dn, b_dn[0], rb, tf)
    n_tok = jnp.full((1,), h2.shape[0] * TOP_K, I32)
    yg = _gather_rows(yb, dest.T.reshape(-1), n_tok).reshape(TOP_K, h2.shape[0], d)

    n_p = bsz * t
    y_p = _final(outs["p"][0], yg, 0, outs["p"][3], mods["p"][5], g_final, t, min(256, t))
    y_s = _final(outs["s"][0], yg, n_p, outs["s"][3], mods["s"][5], g_final, db * ts, min(256, db * ts))

    kv = N_KV_HEADS * HEAD_DIM
    new_k_p = p32_p[:, C_AK:C_AK + kv].reshape(1, bsz, t, N_KV_HEADS, HEAD_DIM)
    new_v_p = p32_p[:, C_AV:C_AV + kv].reshape(1, bsz, t, N_KV_HEADS, HEAD_DIM)
    new_ki_p = p32_p[:, C_MISC:C_MISC + IDX_DIM].reshape(1, bsz, t, IDX_DIM)
    new_k_s = p32_s[:, C_AK:C_AK + kv].reshape(1, db, ts, N_KV_HEADS, HEAD_DIM)
    new_v_s = p32_s[:, C_AV:C_AV + kv].reshape(1, db, ts, N_KV_HEADS, HEAD_DIM)
    new_ki_s = p32_s[:, C_MISC:C_MISC + IDX_DIM].reshape(1, db, ts, IDX_DIM)
    conv_p = p32_p.reshape(bsz, t, PROJ_W)[:, t - (CONV_W - 1):, C_GQ:C_GQ + conv_dim][None]
    pre_s = p32_s[:, C_GQ:C_GQ + conv_dim].reshape(db, ts, conv_dim)
    conv_s = jnp.concatenate([state_conv[0], pre_s], axis=1)[:, ts:][None]
    return (y_p.reshape(bsz, t, d), y_s.reshape(db, ts, d), new_k_p, new_v_p, new_ki_p, conv_p, sfin_p[None],
            new_k_s, new_v_s, new_ki_s, conv_s, sfin_s[None])
```

```python
import functools

import jax
import jax.numpy as jnp
from jax import lax
from jax.experimental import pallas as pl
from jax.experimental.pallas import tpu as pltpu

F32 = jnp.float32
BF16 = jnp.bfloat16
I32 = jnp.int32

D_MODEL = 2048
N_HEADS = 16
N_KV_HEADS = 4
HEAD_DIM = 128
ROPE_THETA = 500000.0
ROT_FRACTION_DIV = 4
IDX_HEADS = 16
IDX_DIM = 64
TOPK_MAX = 256
GDN_QK_HEADS = 8
GDN_V_HEADS = 16
GDN_DK = 128
GDN_DV = 128
CONV_W = 4
N_EXPERTS = 32
TOP_K = 4
D_FF = D_MODEL
SWIGLU_LIMIT = 7.0
SWIGLU_ALPHA = 1.702
NORM_EPS = 1e-6
PAGE_SIZE = 128
LANE = 128

C_AQ, C_AK, C_AV, C_IQ = 0, 2048, 2560, 3072
C_GQ, C_GK, C_GV, C_GZ, C_GL = 4096, 5120, 6144, 8192, 10240
C_MISC = 14336
L_IW, L_GB, L_GA = 64, 80, 96
PROJ_TN = 512
PROJ_W = 14848
MISC_BLK = C_MISC // LANE

VMEM_LIMIT = 48 * 1024 * 1024
MOE_VMEM_LIMIT = 56 * 1024 * 1024
NEG_BIG = -1e30
INT_MIN = -2147483648

NT_DIMS = (((1,), (1,)), ((), ()))


def _cparams(sem):
    return pltpu.CompilerParams(dimension_semantics=sem, vmem_limit_bytes=VMEM_LIMIT)


def _split_bf16(a):
    hi = a.astype(BF16)
    lo = (a - hi.astype(F32)).astype(BF16)
    return hi, lo


def _dot3(a, b, dims=None):
    ah, al = _split_bf16(a)
    bh, bl = _split_bf16(b)
    if dims is None:
        d = lambda x, y: jnp.dot(x, y, preferred_element_type=F32)
    else:
        d = lambda x, y: lax.dot_general(x, y, dims, preferred_element_type=F32)
    return d(ah, bh) + d(al, bh) + d(ah, bl)


def _sigmoid(x):
    return 1.0 / (1.0 + jnp.exp(-x))


def _ada_kernel(c_ref, w_ref, b_ref, o_ref):
    c = c_ref[...]
    a = (c * _sigmoid(c)).astype(BF16)
    o_ref[...] = jnp.dot(a, w_ref[...].astype(BF16), preferred_element_type=F32) + b_ref[...]


def _ada(c, w, b):
    m, k = c.shape
    n = w.shape[2]
    tn = 1024
    return pl.pallas_call(
        _ada_kernel,
        grid=(n // tn,),
        in_specs=[pl.BlockSpec((m, k), lambda j: (0, 0)),
                  pl.BlockSpec((None, k, tn), lambda j: (0, 0, j)),
                  pl.BlockSpec((1, tn), lambda j: (0, j))],
        out_specs=pl.BlockSpec((m, tn), lambda j: (0, j)),
        out_shape=jax.ShapeDtypeStruct((m, n), F32),
        compiler_params=_cparams(("arbitrary",)),
        name="ada",
    )(c, w, b.reshape(1, n))


def _rot_slab(x, tabs, half):
    a, b, c = tabs
    return x * a + pltpu.roll(x, LANE - half, 1) * b + pltpu.roll(x, half, 1) * c


def _inproj_kernel(x_ref, g_ref, sc_ref, sh_ref, w_ref, tab_ref, o32_ref, o16_ref, h_scr):
    j = pl.program_id(1)

    @pl.when(j == 0)
    def _():
        x = x_ref[...]
        ms = jnp.mean(x * x, axis=-1, keepdims=True)
        y = x * lax.rsqrt(ms + NORM_EPS) * g_ref[...]
        h_scr[...] = (y * (1.0 + sc_ref[...]) + sh_ref[...]).astype(BF16)

    acc = jnp.dot(h_scr[...], w_ref[...], preferred_element_type=F32)
    n_slab = PROJ_TN // LANE

    def store(fn):
        for s in range(n_slab):
            v = fn(acc[:, s * LANE:(s + 1) * LANE])
            o32_ref[:, s * LANE:(s + 1) * LANE] = v
            o16_ref[:, s * LANE:(s + 1) * LANE] = v.astype(BF16)

    rot_head = j < 5
    rot_idx = ((j >= 6) & (j < 8)) | (j == C_MISC // PROJ_TN)

    @pl.when(rot_head)
    def _():
        tabs = (tab_ref[0], tab_ref[1], tab_ref[2])
        store(lambda v: _rot_slab(v, tabs, HEAD_DIM // ROT_FRACTION_DIV // 2))

    @pl.when(rot_idx)
    def _():
        tabs = (tab_ref[0], tab_ref[1], tab_ref[2])
        store(lambda v: _rot_slab(v, tabs, IDX_DIM // ROT_FRACTION_DIV // 2))

    @pl.when(jnp.logical_not(rot_head | rot_idx))
    def _():
        store(lambda v: v)


def _tab_type(j):
    return jnp.where(j < 6, 0, jnp.where(j < 28, 1, 2))


def _inproj(x, g_norm, scale, shift, w_r, tabs, rows_per_mod, tm):
    m = x.shape[0]
    tt = tabs.shape[2]
    n_t = tt // tm
    mod_rows = scale.shape[1]
    tiles_per_mod = rows_per_mod // tm
    if mod_rows == 1:
        mod_spec = pl.BlockSpec((None, 1, D_MODEL), lambda i, j: (i // tiles_per_mod, 0, 0))
    else:
        mod_spec = pl.BlockSpec((None, tm, D_MODEL), lambda i, j: (i // tiles_per_mod, i % tiles_per_mod, 0))
    return pl.pallas_call(
        _inproj_kernel,
        grid=(m // tm, PROJ_W // PROJ_TN),
        in_specs=[pl.BlockSpec((tm, D_MODEL), lambda i, j: (i, 0)),
                  pl.BlockSpec((1, D_MODEL), lambda i, j: (0, 0)),
                  mod_spec, mod_spec,
                  pl.BlockSpec((D_MODEL, PROJ_TN), lambda i, j: (0, j)),
                  pl.BlockSpec((None, 3, tm, LANE), lambda i, j: (_tab_type(j), 0, i % n_t, 0))],
        out_specs=[pl.BlockSpec((tm, PROJ_TN), lambda i, j: (i, j)),
                   pl.BlockSpec((tm, PROJ_TN), lambda i, j: (i, j))],
        out_shape=[jax.ShapeDtypeStruct((m, PROJ_W), F32), jax.ShapeDtypeStruct((m, PROJ_W), BF16)],
        scratch_shapes=[pltpu.VMEM((tm, D_MODEL), BF16)],
        compiler_params=_cparams(("parallel", "arbitrary")),
        name="inproj",
    )(x, g_norm.reshape(1, D_MODEL), scale, shift, w_r, tabs)


def _rot_tables(pos):
    tt = pos.shape[0]
    posf = pos.astype(F32)

    def tab(rot, period, scale):
        half = rot // 2
        inv = ROPE_THETA ** (-jnp.arange(half, dtype=F32) * (2.0 / rot))
        ang = posf[:, None] * inv[None, :]
        cos, sin = jnp.cos(ang), jnp.sin(ang)
        zh = jnp.zeros((tt, half), F32)
        rest = period - rot
        a = jnp.concatenate([cos, cos, jnp.ones((tt, rest), F32)], axis=1) * scale
        b = jnp.concatenate([-sin, zh, jnp.zeros((tt, rest), F32)], axis=1) * scale
        c = jnp.concatenate([zh, sin, jnp.zeros((tt, rest), F32)], axis=1) * scale
        return a, b, c

    head = jnp.stack([jnp.tile(t, (1, LANE // HEAD_DIM)) for t in tab(HEAD_DIM // ROT_FRACTION_DIV, HEAD_DIM, 1.0)])
    idxq = jnp.stack([jnp.tile(t, (1, LANE // IDX_DIM))
                      for t in tab(IDX_DIM // ROT_FRACTION_DIV, IDX_DIM, IDX_DIM ** -0.5)])
    ka, kb, kc = tab(IDX_DIM // ROT_FRACTION_DIV, IDX_DIM, 1.0)
    rest = LANE - IDX_DIM
    lane = jnp.arange(rest)
    tail_a = jnp.where(lane < (L_GB - L_IW), IDX_HEADS ** -0.5, 1.0).astype(F32)
    ma = jnp.concatenate([ka, jnp.broadcast_to(tail_a, (tt, rest))], axis=1)
    mb = jnp.concatenate([kb, jnp.zeros((tt, rest), F32)], axis=1)
    mc = jnp.concatenate([kc, jnp.zeros((tt, rest), F32)], axis=1)
    misc = jnp.stack([ma, mb, mc])
    return jnp.stack([head, idxq, misc])


def _reorder_w_in(w_in):
    offs = {}
    o = 0
    for name, wdt in (("aq", 2048), ("ak", 512), ("av", 512), ("iq", 1024), ("ik", 64), ("iw", 16), ("gq", 1024),
                      ("gk", 1024), ("gv", 2048), ("gz", 2048), ("gb", 16), ("ga", 16), ("gl", 4096)):
        offs[name] = (o, wdt)
        o += wdt

    def seg(name):
        s, wdt = offs[name]
        return w_in[:, s:s + wdt]

    k = w_in.shape[0]
    misc = jnp.concatenate([seg("ik"), seg("iw"), seg("gb"), seg("ga"), jnp.zeros((k, LANE - 112), w_in.dtype)], axis=1)
    parts = [seg("aq"), seg("ak"), seg("av"), seg("iq"), seg("gq"), seg("gk"), seg("gv"), seg("gz"), seg("gl"), misc,
             jnp.zeros((k, PROJ_W - C_MISC - LANE), w_in.dtype)]
    return jnp.concatenate(parts, axis=1).astype(BF16)


def _order_key(score):
    bits = pltpu.bitcast(score, I32)
    return bits ^ ((bits >> 31) & 0x7FFFFFFF)


def _kth_largest_key(count_ge, n_rows, n_sel):
    def body(i, t):
        bit = lax.shift_left(jnp.int32(1), 31 - i)
        cand = t ^ bit
        return jnp.where(count_ge(cand) >= n_sel, cand, t)

    return lax.fori_loop(0, 32, body, jnp.full((n_rows, 1), INT_MIN, I32))


def _attn_prompt_kernel(q_ref, iq_ref, mq_ref, k_ref, v_ref, mk_ref, o_ref, key_scr, *, tq, kc, n_sel):
    qb = pl.program_id(1)
    q0 = qb * tq
    nkc = (q0 + tq + kc - 1) // kc
    qpos = q0 + lax.broadcasted_iota(I32, (tq, 1), 0)
    kiota = lax.broadcasted_iota(I32, (1, kc), 1)
    wq = mq_ref[...]

    def score_body(c, carry):
        ks = pl.multiple_of(c * kc, kc)
        kt = mk_ref[pl.ds(ks, kc), :][:, :IDX_DIM]
        acc = jnp.zeros((tq, kc), F32)
        for h in range(IDX_HEADS):
            s = lax.dot_general(iq_ref[:, h * IDX_DIM:(h + 1) * IDX_DIM], kt, NT_DIMS, preferred_element_type=F32)
            acc = acc + wq[:, L_IW + h:L_IW + h + 1] * jnp.maximum(s, 0.0)
        acc = jnp.where(ks + kiota <= qpos, acc, -jnp.inf)
        key_scr[c] = _order_key(acc)
        return carry

    lax.fori_loop(0, nkc, score_body, 0)

    def count_ge(cand):
        def body(c, acc):
            m = jnp.where(key_scr[c] >= cand, 1.0, 0.0)
            for s in range(kc // LANE):
                acc = acc + m[:, s * LANE:(s + 1) * LANE]
            return acc

        acc = lax.fori_loop(0, nkc, body, jnp.zeros((tq, LANE), F32))
        return jnp.sum(acc, axis=1, keepdims=True)

    thr = _kth_largest_key(count_ge, tq, float(n_sel))

    group = N_HEADS // N_KV_HEADS
    scale = HEAD_DIM ** -0.5
    qss = []
    for n in range(N_KV_HEADS):
        qs = jnp.concatenate(
            [q_ref[:, (n * group + g) * HEAD_DIM:(n * group + g + 1) * HEAD_DIM] for g in range(group)], axis=0)
        qss.append((qs.astype(F32) * scale).astype(BF16))

    def attn_body(c, carry):
        ks = pl.multiple_of(c * kc, kc)
        sel = (key_scr[c] >= thr) & (ks + kiota <= qpos)
        bias = jnp.where(sel, 0.0, NEG_BIG)[None]
        new = []
        for n in range(N_KV_HEADS):
            m_i, l_i, acc = carry[n]
            kn = k_ref[pl.ds(ks, kc), n * HEAD_DIM:(n + 1) * HEAD_DIM]
            vn = v_ref[pl.ds(ks, kc), n * HEAD_DIM:(n + 1) * HEAD_DIM]
            s = lax.dot_general(qss[n], kn, NT_DIMS, preferred_element_type=F32)
            s = (s.reshape(group, tq, kc) + bias).reshape(group * tq, kc)
            m_new = jnp.maximum(m_i, jnp.max(s, axis=1, keepdims=True))
            alpha = jnp.exp(m_i - m_new)
            p = jnp.exp(s - m_new)
            l_new = alpha * l_i + jnp.sum(p, axis=1, keepdims=True)
            acc = alpha * acc + jnp.dot(p.astype(BF16), vn, preferred_element_type=F32)
            new.append((m_new, l_new, acc))
        return tuple(new)

    init = tuple((jnp.full((group * tq, 1), NEG_BIG, F32), jnp.zeros((group * tq, 1), F32),
                  jnp.zeros((group * tq, HEAD_DIM), F32)) for _ in range(N_KV_HEADS))
    fin = lax.fori_loop(0, nkc, attn_body, init)
    for n in range(N_KV_HEADS):
        _, l_f, acc_f = fin[n]
        out = acc_f / l_f
        for g in range(group):
            o_ref[:, (n * group + g) * HEAD_DIM:(n * group + g + 1) * HEAD_DIM] = out[g * tq:(g + 1) * tq].astype(BF16)


def _attn_prompt(p32, p16, bsz, t, tq=128, kc=512):
    nq = t // tq
    kc = min(kc, t)
    n_sel = min(TOPK_MAX, t // 4)
    kern = functools.partial(_attn_prompt_kernel, tq=tq, kc=kc, n_sel=n_sel)
    return pl.pallas_call(
        kern,
        grid=(bsz, nq),
        in_specs=[pl.BlockSpec((tq, N_HEADS * HEAD_DIM), lambda b, i: (b * nq + i, 0)),
                  pl.BlockSpec((tq, IDX_HEADS * IDX_DIM), lambda b, i: (b * nq + i, C_IQ // (IDX_HEADS * IDX_DIM))),
                  pl.BlockSpec((tq, LANE), lambda b, i: (b * nq + i, MISC_BLK)),
                  pl.BlockSpec((t, N_KV_HEADS * HEAD_DIM), lambda b, i: (b, C_AK // (N_KV_HEADS * HEAD_DIM))),
                  pl.BlockSpec((t, N_KV_HEADS * HEAD_DIM), lambda b, i: (b, C_AV // (N_KV_HEADS * HEAD_DIM))),
                  pl.BlockSpec((t, LANE), lambda b, i: (b, MISC_BLK))],
        out_specs=pl.BlockSpec((tq, N_HEADS * HEAD_DIM), lambda b, i: (b * nq + i, 0)),
        out_shape=jax.ShapeDtypeStruct((bsz * t, N_HEADS * HEAD_DIM), BF16),
        scratch_shapes=[pltpu.VMEM((t // kc, tq, kc), I32)],
        compiler_params=_cparams(("parallel", "arbitrary")),
        name="attn_prompt",
    )(p16, p16, p32, p16, p16, p16)


def _attn_sample_kernel(pt_ref, iq_ref, wc_ref, q_ref, kn_ref, vn_ref, ikn_ref, *rest, n_pages, n_sel, past):
    kidx_refs = rest[:n_pages]
    k_refs = rest[n_pages:2 * n_pages]
    v_refs = rest[2 * n_pages:3 * n_pages]
    o_ref, sc_scr, selx_scr, s_scr = rest[3 * n_pages:]
    del pt_ref
    rows, nkv = 8, N_KV_HEADS
    n_keys = (n_pages + 1) * PAGE_SIZE
    pw = PAGE_SIZE * nkv
    iq = iq_ref[...].astype(BF16)
    wc = wc_ref[...]

    def idx_score(kt, dims):
        s = lax.dot_general(iq, kt.astype(BF16), dims, preferred_element_type=F32)
        x = jnp.maximum(s, 0.0) * wc
        return jnp.sum(x.reshape(IDX_HEADS, rows, PAGE_SIZE), axis=0)

    nn_dims = (((1,), (0,)), ((), ()))
    for p in range(n_pages):
        sc_scr[:, p * PAGE_SIZE:(p + 1) * PAGE_SIZE] = idx_score(kidx_refs[p][...], nn_dims)
    sc_scr[:, n_pages * PAGE_SIZE:] = idx_score(ikn_ref[...], NT_DIMS)

    qpos = past + lax.broadcasted_iota(I32, (rows, 1), 0)
    kpos = lax.broadcasted_iota(I32, (1, n_keys), 1)
    causal = kpos <= qpos
    keys = _order_key(jnp.where(causal, sc_scr[...], -jnp.inf))

    def count_ge(cand):
        return jnp.sum(jnp.where(keys >= cand, 1.0, 0.0), axis=1, keepdims=True)

    thr = _kth_largest_key(count_ge, rows, float(n_sel))
    sel = jnp.where((keys >= thr) & causal, 1.0, 0.0).astype(BF16)
    expand = jnp.where((lax.broadcasted_iota(I32, (PAGE_SIZE, pw), 1) >> 2)
                       == lax.broadcasted_iota(I32, (PAGE_SIZE, pw), 0), 1.0, 0.0).astype(BF16)
    for p in range(n_pages + 1):
        selx_scr[:, p * pw:(p + 1) * pw] = jnp.dot(sel[:, p * PAGE_SIZE:(p + 1) * PAGE_SIZE], expand,
                                                   preferred_element_type=F32)
    n_q = N_HEADS * rows
    own_head = ((lax.broadcasted_iota(I32, (n_q, pw), 0) >> 5)
                == (lax.broadcasted_iota(I32, (n_q, pw), 1) & (nkv - 1))).reshape(N_HEADS, rows, pw)
    q = (q_ref[...] * (HEAD_DIM ** -0.5)).astype(BF16)
    m_parts = []
    for p in range(n_pages + 1):
        kb = (k_refs[p] if p < n_pages else kn_ref)[...].astype(BF16)
        s = lax.dot_general(q, kb, NT_DIMS, preferred_element_type=F32)
        ok = own_head & (selx_scr[:, p * pw:(p + 1) * pw] > 0.5)[None]
        s = jnp.where(ok, s.reshape(N_HEADS, rows, pw), NEG_BIG).reshape(n_q, pw)
        s_scr[:, p * pw:(p + 1) * pw] = s
        m_parts.append(jnp.max(s, axis=1, keepdims=True))
    m = m_parts[0]
    for mp in m_parts[1:]:
        m = jnp.maximum(m, mp)
    l_parts, o_parts = [], []
    for p in range(n_pages + 1):
        vb = (v_refs[p] if p < n_pages else vn_ref)[...].astype(BF16)
        pr = jnp.exp(s_scr[:, p * pw:(p + 1) * pw] - m)
        l_parts.append(jnp.sum(pr, axis=1, keepdims=True))
        o_parts.append(jnp.dot(pr.astype(BF16), vb, preferred_element_type=F32))
    l = l_parts[0]
    acc = o_parts[0]
    for lp, op in zip(l_parts[1:], o_parts[1:]):
        l = l + lp
        acc = acc + op
    o_ref[...] = acc / l


def _attn_sample(p32, cache_k, cache_v, cache_kidx, page_table, ts):
    db, n_pages = page_table.shape
    past = n_pages * PAGE_SIZE
    rows = 8
    group = N_HEADS // N_KV_HEADS
    ps = p32.reshape(db, ts, PROJ_W)
    pad_t = ((0, 0), (0, 0), (0, rows - ts), (0, 0))
    iq = ps[:, :, C_IQ:C_IQ + IDX_HEADS * IDX_DIM].reshape(db, ts, IDX_HEADS, IDX_DIM).transpose(0, 2, 1, 3)
    iq = jnp.pad(iq, pad_t).reshape(db, IDX_HEADS * rows, IDX_DIM)
    wc = ps[:, :, C_MISC + L_IW:C_MISC + L_IW + IDX_HEADS].transpose(0, 2, 1)
    wc = jnp.pad(wc, ((0, 0), (0, 0), (0, rows - ts))).reshape(db, IDX_HEADS * rows, 1)
    q = ps[:, :, :N_HEADS * HEAD_DIM].reshape(db, ts, N_HEADS, HEAD_DIM).transpose(0, 2, 1, 3)
    q = jnp.pad(q, pad_t).reshape(db, N_HEADS * rows, HEAD_DIM)
    pad_k = ((0, 0), (0, PAGE_SIZE - ts), (0, 0))
    pw = PAGE_SIZE * N_KV_HEADS
    kn = jnp.pad(ps[:, :, C_AK:C_AK + N_KV_HEADS * HEAD_DIM], pad_k).reshape(db, pw, HEAD_DIM)
    vn = jnp.pad(ps[:, :, C_AV:C_AV + N_KV_HEADS * HEAD_DIM], pad_k).reshape(db, pw, HEAD_DIM)
    ikn = jnp.pad(ps[:, :, C_MISC:C_MISC + IDX_DIM], pad_k)
    n_sel = min(TOPK_MAX, (past + ts) // 4)
    n_phys = cache_k.shape[1]
    ck = cache_k.reshape(1, n_phys, pw, HEAD_DIM)
    cv = cache_v.reshape(1, n_phys, pw, HEAD_DIM)

    def kv_page(p):
        return pl.BlockSpec((None, None, pw, HEAD_DIM), lambda d, pt, p=p: (0, pt[d * n_pages + p], 0, 0))

    ckt = jnp.swapaxes(cache_kidx, 2, 3)

    def idx_page(p):
        return pl.BlockSpec((None, None, IDX_DIM, PAGE_SIZE), lambda d, pt, p=p: (0, pt[d * n_pages + p], 0, 0))

    def per_d(r, w):
        return pl.BlockSpec((None, r, w), lambda d, pt: (d, 0, 0))

    in_specs = ([per_d(IDX_HEADS * rows, IDX_DIM), per_d(IDX_HEADS * rows, 1), per_d(N_HEADS * rows, HEAD_DIM),
                 per_d(pw, HEAD_DIM), per_d(pw, HEAD_DIM), per_d(PAGE_SIZE, IDX_DIM)]
                + [idx_page(p) for p in range(n_pages)]
                + [kv_page(p) for p in range(n_pages)]
                + [kv_page(p) for p in range(n_pages)])
    n_keys = (n_pages + 1) * PAGE_SIZE
    grid_spec = pltpu.PrefetchScalarGridSpec(
        num_scalar_prefetch=1, grid=(db,), in_specs=in_specs,
        out_specs=per_d(N_HEADS * rows, HEAD_DIM),
        scratch_shapes=[pltpu.VMEM((rows, n_keys), F32), pltpu.VMEM((rows, n_keys * N_KV_HEADS), F32),
                        pltpu.VMEM((N_HEADS * rows, n_keys * N_KV_HEADS), F32)])
    kern = functools.partial(_attn_sample_kernel, n_pages=n_pages, n_sel=n_sel, past=past)
    o = pl.pallas_call(
        kern, grid_spec=grid_spec,
        out_shape=jax.ShapeDtypeStruct((db, N_HEADS * rows, HEAD_DIM), F32),
        compiler_params=_cparams(("arbitrary",)),
        name="attn_sample",
    )(page_table.reshape(-1), iq, wc, q, kn, vn, ikn,
      *([ckt] * n_pages), *([ck] * n_pages), *([cv] * n_pages))
    o = o.reshape(db, N_HEADS, rows, HEAD_DIM)[:, :, :ts].transpose(0, 2, 1, 3)
    return o.reshape(db * ts, N_HEADS * HEAD_DIM).astype(BF16)


def _dot3s(ah, al, bh, bl, dims=None):
    if dims is None:
        d = lambda x, y: jnp.dot(x, y, preferred_element_type=F32)
    else:
        d = lambda x, y: lax.dot_general(x, y, dims, preferred_element_type=F32)
    return d(ah, bh) + d(al, bh) + d(ah, bl)


def _cumsum_rows(sel_bf16, g_rows):
    gb = jnp.broadcast_to(g_rows, (g_rows.shape[0], LANE))
    g_hi, g_lo = _split_bf16(gb)
    g_lo2 = (gb - g_hi.astype(F32) - g_lo.astype(F32)).astype(BF16)
    d = lambda y: jnp.dot(sel_bf16, y, preferred_element_type=F32)
    return d(g_hi) + d(g_lo) + d(g_lo2)


def _l2n(x):
    return x * lax.rsqrt(jnp.sum(x * x, axis=-1, keepdims=True) + NORM_EPS)


def _gate_params(misc, hp, live):
    x = misc + hp[1:2, :]
    softplus = jnp.maximum(x, 0.0) + jnp.log(1.0 + jnp.exp(-jnp.abs(x)))
    g_all = -jnp.exp(hp[0:1, :]) * softplus
    beta_all = _sigmoid(misc)
    if live is not None:
        g_all = jnp.where(live, g_all, 0.0)
        beta_all = jnp.where(live, beta_all, 0.0)
    return g_all, beta_all


def _gdn_kernel(q_ref, k_ref, v_ref, z_ref, m_ref, wq_ref, wk_ref, wv_ref, cq_ref, ck_ref, cv_ref, hp_ref, nw_ref,
                s0_ref, o_ref, sf_ref, s_scr, hq_scr, hk_scr, hv_scr, xc_scr, *, c, nh):
    g_idx = pl.program_id(1)
    tb = pl.program_id(2)
    n_tb = pl.num_programs(2)
    dk, dv = GDN_DK, GDN_DV

    @pl.when(tb == 0)
    def _():
        s_scr[...] = s0_ref[...]
        hq_scr[...] = cq_ref[...]
        hk_scr[...] = ck_ref[...]
        hv_scr[...] = cv_ref[...]

    def conv_silu(x_ref, w_ref, h_scr):
        width = x_ref.shape[1]
        x = x_ref[...]
        xc_scr[0:8, 0:width] = h_scr[...]
        xc_scr[8:8 + c, 0:width] = x
        acc = jnp.zeros((c, width), F32)
        for j in range(CONV_W):
            acc = acc + xc_scr[8 - (CONV_W - 1) + j:8 - (CONV_W - 1) + j + c, 0:width] * w_ref[j:j + 1, :]
        h_scr[...] = x[c - 8:c, :]
        return acc * _sigmoid(acc)

    qa = conv_silu(q_ref, wq_ref, hq_scr)
    ka = conv_silu(k_ref, wk_ref, hk_scr)
    va = conv_silu(v_ref, wv_ref, hv_scr)
    g_all, beta_all = _gate_params(m_ref[...], hp_ref[...], None)
    lane = lax.broadcasted_iota(I32, (1, LANE), 1)
    ri = lax.broadcasted_iota(I32, (c, c), 0)
    ci = lax.broadcasted_iota(I32, (c, c), 1)
    causal = ri >= ci
    strict = ri > ci
    tri = jnp.where(causal, 1.0, 0.0).astype(BF16)
    eye = jnp.where(ri == ci, 1.0, 0.0)

    def same_blk(log_b):
        return (ri >> log_b) == (ci >> log_b)

    heads = range(2 * nh)
    ks = [_l2n(ka[:, hh * dk:(hh + 1) * dk]) for hh in range(nh)]
    qss = [_l2n(qa[:, hh * dk:(hh + 1) * dk]) * (dk ** -0.5) for hh in range(nh)]
    kk_qk = [lax.dot_general(jnp.concatenate([ks[hh], qss[hh]], axis=0).astype(BF16), ks[hh].astype(BF16), NT_DIMS,
                             preferred_element_type=F32) for hh in range(nh)]
    g_cols = [jnp.sum(jnp.where(lane == L_GA + 2 * nh * g_idx + hl, g_all, 0.0), axis=1, keepdims=True)
              for hl in heads]
    betas = [jnp.sum(jnp.where(lane == L_GB + 2 * nh * g_idx + hl, beta_all, 0.0), axis=1, keepdims=True)
             for hl in heads]
    gcs = [_cumsum_rows(tri, g) for g in g_cols]
    decays = [jnp.exp(jnp.where(causal, gc - gc.T, -jnp.inf)) for gc in gcs]
    a_mats = [jnp.where(strict, kk_qk[hl // 2][0:c] * betas[hl] * decays[hl], 0.0) for hl in heads]
    nms = [jnp.where(same_blk(3), -a, 0.0) for a in a_mats]
    pms = [eye + n for n in nms]
    for _ in range(2):
        nms = [_dot3(n, n) for n in nms]
        pms = [p + _dot3(p, n) for p, n in zip(pms, nms)]
    for lb in range(3, (c - 1).bit_length()):
        join = same_blk(lb + 1) & jnp.logical_not(same_blk(lb))
        t1s = [_dot3(jnp.where(join, a, 0.0), p) for a, p in zip(a_mats, pms)]
        pms = [p - _dot3(p, t1) for p, t1 in zip(pms, t1s)]
    us = [_dot3(pms[hl], va[:, hl * dv:(hl + 1) * dv] * betas[hl]) for hl in heads]
    ws = [_dot3(pms[hl], ks[hl // 2] * betas[hl] * jnp.exp(gcs[hl])) for hl in heads]
    qks = [jnp.where(causal, kk_qk[hl // 2][c:2 * c] * decays[hl], 0.0) for hl in heads]
    s_in = [s_scr[hl] for hl in heads]
    v_news = [us[hl] - _dot3(ws[hl], s_in[hl]) for hl in heads]
    os_ = [jnp.dot((qss[hl // 2] * jnp.exp(gcs[hl])).astype(BF16), s_in[hl].astype(BF16), preferred_element_type=F32)
           + jnp.dot(qks[hl].astype(BF16), v_news[hl].astype(BF16), preferred_element_type=F32) for hl in heads]
    for hl in heads:
        g_last = gcs[hl][c - 1:c, :]
        kdec = ks[hl // 2] * jnp.exp(g_last - gcs[hl])
        s_scr[hl] = s_in[hl] * jnp.exp(g_last) + _dot3(kdec.T, v_news[hl])
    for hl in heads:
        o = os_[hl]
        z = z_ref[:, hl * dv:(hl + 1) * dv]
        on = o * lax.rsqrt(jnp.mean(o * o, axis=-1, keepdims=True) + NORM_EPS) * nw_ref[...]
        o_ref[:, hl * dv:(hl + 1) * dv] = (on * (z * _sigmoid(z))).astype(BF16)

    @pl.when(tb == n_tb - 1)
    def _():
        sf_ref[...] = s_scr[...]


def _gdn_hp(a_log, dt_bias):
    hp = jnp.zeros((8, LANE), F32)
    return hp.at[0, L_GA:L_GA + GDN_V_HEADS].set(a_log.astype(F32)).at[1, L_GA:L_GA + GDN_V_HEADS].set(
        dt_bias.astype(F32))


def _gdn(p32, conv0, s0, conv_w, a_log, dt_bias, norm_w, bsz, t, c=128, nh=4):
    n_tb = t // c
    n_g = GDN_QK_HEADS // nh
    qw, vw = nh * GDN_DK, 2 * nh * GDN_DV
    row = lambda b, g, i: b * n_tb + i
    in_specs = [
        pl.BlockSpec((c, qw), lambda b, g, i: (row(b, g, i), C_GQ // qw + g)),
        pl.BlockSpec((c, qw), lambda b, g, i: (row(b, g, i), C_GK // qw + g)),
        pl.BlockSpec((c, vw), lambda b, g, i: (row(b, g, i), C_GV // vw + g)),
        pl.BlockSpec((c, vw), lambda b, g, i: (row(b, g, i), C_GZ // vw + g)),
        pl.BlockSpec((c, LANE), lambda b, g, i: (row(b, g, i), MISC_BLK)),
        pl.BlockSpec((CONV_W, qw), lambda b, g, i: (0, g)),
        pl.BlockSpec((CONV_W, qw), lambda b, g, i: (0, n_g + g)),
        pl.BlockSpec((CONV_W, vw), lambda b, g, i: (0, n_g + g)),
        pl.BlockSpec((None, 8, qw), lambda b, g, i: (b, 0, g)),
        pl.BlockSpec((None, 8, qw), lambda b, g, i: (b, 0, n_g + g)),
        pl.BlockSpec((None, 8, vw), lambda b, g, i: (b, 0, n_g + g)),
        pl.BlockSpec((8, LANE), lambda b, g, i: (0, 0)),
        pl.BlockSpec((1, GDN_DV), lambda b, g, i: (0, 0)),
        pl.BlockSpec((None, 2 * nh, GDN_DK, GDN_DV), lambda b, g, i: (b, g, 0, 0)),
    ]
    out_specs = [pl.BlockSpec((c, vw), lambda b, g, i: (row(b, g, i), g)),
                 pl.BlockSpec((None, 2 * nh, GDN_DK, GDN_DV), lambda b, g, i: (b, g, 0, 0))]
    return pl.pallas_call(
        functools.partial(_gdn_kernel, c=c, nh=nh),
        grid=(bsz, n_g, n_tb),
        in_specs=in_specs, out_specs=out_specs,
        out_shape=[jax.ShapeDtypeStruct((bsz * t, GDN_V_HEADS * GDN_DV), BF16),
                   jax.ShapeDtypeStruct((bsz, GDN_V_HEADS, GDN_DK, GDN_DV), F32)],
        scratch_shapes=[pltpu.VMEM((2 * nh, GDN_DK, GDN_DV), F32), pltpu.VMEM((8, qw), F32), pltpu.VMEM((8, qw), F32),
                        pltpu.VMEM((8, vw), F32), pltpu.VMEM((c + 8, vw), F32)],
        compiler_params=_cparams(("parallel", "parallel", "arbitrary")),
        name="gdn",
    )(p32, p32, p32, p32, p32, conv_w, conv_w, conv_w, conv0, conv0, conv0, _gdn_hp(a_log, dt_bias),
      norm_w.reshape(1, GDN_DV), s0)


def _gdn_sample_kernel(x_ref, z_ref, m_ref, w_ref, hp_ref, nw_ref, s0_ref, o_ref, sf_ref, *, ts):
    rows, nvh, dk, dv = 8, GDN_V_HEADS, GDN_DK, GDN_DV
    qd = GDN_QK_HEADS * dk
    x8 = x_ref[...]
    w = w_ref[...]
    acc = x8 * w[0:1, :]
    for j in range(1, CONV_W):
        acc = acc + pltpu.roll(x8, rows - j, 0) * w[j:j + 1, :]
    act = acc * _sigmoid(acc)
    live = lax.broadcasted_iota(I32, (rows, 1), 0) < ts
    g_all, beta_all = _gate_params(m_ref[...], hp_ref[...], live)
    qn = [_l2n(act[:, g * dk:(g + 1) * dk]) * (dk ** -0.5) for g in range(GDN_QK_HEADS)]
    kn = [_l2n(act[:, qd + g * dk:qd + (g + 1) * dk]) for g in range(GDN_QK_HEADS)]
    rep = nvh // GDN_QK_HEADS
    cat = lambda parts: jnp.concatenate(parts, axis=0)
    k = cat([kn[h // rep] for h in range(nvh)])
    qs = cat([qn[h // rep] for h in range(nvh)])
    v = cat([act[:, 2 * qd + h * dv:2 * qd + (h + 1) * dv] for h in range(nvh)])
    z = cat([z_ref[:, h * dv:(h + 1) * dv] for h in range(nvh)])
    beta = cat([beta_all[:, L_GB + h:L_GB + h + 1] for h in range(nvh)])
    g_col = cat([g_all[:, L_GA + h:L_GA + h + 1] for h in range(nvh)])
    n = nvh * rows
    ri = lax.broadcasted_iota(I32, (n, n), 0)
    ci = lax.broadcasted_iota(I32, (n, n), 1)
    same = (ri >> 3) == (ci >> 3)
    causal = same & (ri >= ci)
    strict = same & (ri > ci)
    eye = jnp.where(ri == ci, 1.0, 0.0)
    gc = _cumsum_rows(jnp.where(causal, 1.0, 0.0).astype(BF16), g_col)
    gl = _cumsum_rows(jnp.where(same, 1.0, 0.0).astype(BF16), g_col)
    decay = jnp.exp(jnp.where(causal, gc - gc.T, -jnp.inf))
    kb = k * beta
    a_mat = jnp.where(strict, _dot3(kb, k, NT_DIMS) * decay, 0.0)
    nm = -a_mat
    pm = eye + nm
    for _ in range(2):
        nm = _dot3(nm, nm)
        pm = pm + _dot3(pm, nm)
    u = _dot3(pm, v * beta)
    w_rows = _dot3(pm, kb * jnp.exp(gc))
    qk = jnp.where(causal, _dot3(qs, k, NT_DIMS) * decay, 0.0)
    qg = qs * jnp.exp(gc)
    kdt = (k * jnp.exp(gl - gc)).T
    s_in = [s0_ref[h] for h in range(nvh)]
    v_new, o1 = [], []
    for h in range(nvh):
        sl = slice(h * rows, (h + 1) * rows)
        r = _dot3(cat([w_rows[sl], qg[sl]]), s_in[h])
        v_new.append(u[sl] - r[0:rows])
        o1.append(r[rows:2 * rows])
    v_new = cat(v_new)
    o = cat(o1) + _dot3(qk, v_new)
    vh, vl = _split_bf16(v_new)
    lane = lax.broadcasted_iota(I32, (1, n), 1)
    for h in range(nvh):
        kh, kl = _split_bf16(jnp.where((lane >> 3) == h, kdt, 0.0))
        sf_ref[h] = s_in[h] * jnp.exp(gl[h * rows:h * rows + 1, :]) + _dot3s(kh, kl, vh, vl)
    on = o * lax.rsqrt(jnp.mean(o * o, axis=-1, keepdims=True) + NORM_EPS) * nw_ref[...]
    o_ref[...] = on * (z * _sigmoid(z))


def _gdn_sample(p32, state_conv, s0, conv_w, a_log, dt_bias, norm_w, db, ts):
    rows = 8
    conv_dim = conv_w.shape[-1]
    ps = p32.reshape(db, ts, PROJ_W)
    x8 = jnp.concatenate([state_conv, ps[:, :, C_GQ:C_GQ + conv_dim],
                          jnp.zeros((db, rows - ts - (CONV_W - 1), conv_dim), F32)], axis=1)
    pad_t = ((0, 0), (0, rows - ts), (0, 0))
    z8 = jnp.pad(ps[:, :, C_GZ:C_GZ + GDN_V_HEADS * GDN_DV], pad_t)
    m8 = jnp.pad(ps[:, :, C_MISC:C_MISC + LANE], pad_t)
    n = GDN_V_HEADS * rows
    per_d = lambda w: pl.BlockSpec((None, rows, w), lambda d: (d, 0, 0))
    state = pl.BlockSpec((None, GDN_V_HEADS, GDN_DK, GDN_DV), lambda d: (d, 0, 0, 0))
    o, sf = pl.pallas_call(
        functools.partial(_gdn_sample_kernel, ts=ts),
        grid=(db,),
        in_specs=[per_d(conv_dim), per_d(GDN_V_HEADS * GDN_DV), per_d(LANE),
                  pl.BlockSpec((CONV_W, conv_dim), lambda d: (0, 0)),
                  pl.BlockSpec((8, LANE), lambda d: (0, 0)),
                  pl.BlockSpec((1, GDN_DV), lambda d: (0, 0)), state],
        out_specs=[pl.BlockSpec((None, n, GDN_DV), lambda d: (d, 0, 0)), state],
        out_shape=[jax.ShapeDtypeStruct((db, n, GDN_DV), F32),
                   jax.ShapeDtypeStruct((db, GDN_V_HEADS, GDN_DK, GDN_DV), F32)],
        compiler_params=_cparams(("parallel",)),
        name="gdn_sample",
    )(x8, z8, m8, conv_w, _gdn_hp(a_log, dt_bias), norm_w.reshape(1, GDN_DV), s0)
    o = o.reshape(db, GDN_V_HEADS, rows, GDN_DV)[:, :, :ts].transpose(0, 2, 1, 3)
    return o.reshape(db * ts, GDN_V_HEADS * GDN_DV).astype(BF16), sf


def _merge_kernel(a_ref, g_ref, wa_ref, wg_ref, ga_ref, gg_ref, o_ref):
    pa = jnp.dot(a_ref[...], wa_ref[...], preferred_element_type=F32)
    pg = jnp.dot(g_ref[...], wg_ref[...], preferred_element_type=F32)
    o_ref[...] = (_sigmoid(ga_ref[...]) * pa + _sigmoid(gg_ref[...]) * pg).astype(BF16)


def _merge(attn_o, gdn_o, w_oa, w_og, p32, tm):
    m = attn_o.shape[0]
    tn = 512
    return pl.pallas_call(
        _merge_kernel,
        grid=(m // tm, D_MODEL // tn),
        in_specs=[pl.BlockSpec((tm, D_MODEL), lambda i, j: (i, 0)),
                  pl.BlockSpec((tm, D_MODEL), lambda i, j: (i, 0)),
                  pl.BlockSpec((D_MODEL, tn), lambda i, j: (0, j)),
                  pl.BlockSpec((D_MODEL, tn), lambda i, j: (0, j)),
                  pl.BlockSpec((tm, tn), lambda i, j: (i, C_GL // tn + j)),
                  pl.BlockSpec((tm, tn), lambda i, j: (i, (C_GL + D_MODEL) // tn + j))],
        out_specs=pl.BlockSpec((tm, tn), lambda i, j: (i, j)),
        out_shape=jax.ShapeDtypeStruct((m, D_MODEL), BF16),
        compiler_params=_cparams(("parallel", "arbitrary")),
        name="merge",
    )(attn_o, gdn_o, w_oa, w_og, p32, p32)


def _outproj_kernel(mg_ref, w_ref, x_ref, g1_ref, sc_ref, sh_ref, gn_ref, wr_ref, br_ref,
                    x1_ref, h2_ref, ti_ref, tg_ref):
    x1 = x_ref[...] + g1_ref[...] * jnp.dot(mg_ref[...], w_ref[...], preferred_element_type=F32)
    x1_ref[...] = x1
    ms = jnp.mean(x1 * x1, axis=-1, keepdims=True)
    h2 = x1 * lax.rsqrt(ms + NORM_EPS) * gn_ref[...] * (1.0 + sc_ref[...]) + sh_ref[...]
    h2_ref[...] = h2
    logits = _dot3(h2, wr_ref[...]) + br_ref[...]
    tm = logits.shape[0]
    lane = lax.broadcasted_iota(I32, (tm, LANE), 1)
    lanef = lane.astype(F32)
    l = jnp.where(lane < N_EXPERTS, logits, -jnp.inf)
    vals, idxs = [], []
    for _ in range(TOP_K):
        mx = jnp.max(l, axis=1, keepdims=True)
        ix = jnp.min(jnp.where(l == mx, lanef, float(LANE)), axis=1, keepdims=True)
        vals.append(mx)
        idxs.append(ix)
        l = jnp.where(lanef == ix, -jnp.inf, l)
    es = [jnp.exp(v - vals[0]) for v in vals]
    den = es[0]
    for e in es[1:]:
        den = den + e
    ti = jnp.zeros((tm, LANE), F32)
    tg = jnp.zeros((tm, LANE), F32)
    for kk in range(TOP_K):
        ti = jnp.where(lane == kk, idxs[kk], ti)
        tg = jnp.where(lane == kk, es[kk] / den, tg)
    ti_ref[...] = ti.astype(I32)
    tg_ref[...] = tg


def _outproj(merged, w_out, x, gate1, scale2, shift2, g_norm2, w_router, b_router, rows_per_mod, tm):
    m = x.shape[0]
    mod_rows = gate1.shape[1]
    tiles_per_mod = rows_per_mod // tm
    if mod_rows == 1:
        mod_spec = pl.BlockSpec((None, 1, D_MODEL), lambda i: (i // tiles_per_mod, 0, 0))
    else:
        mod_spec = pl.BlockSpec((None, tm, D_MODEL), lambda i: (i // tiles_per_mod, i % tiles_per_mod, 0))
    wr = jnp.pad(w_router.astype(F32), ((0, 0), (0, LANE - N_EXPERTS)))
    br = jnp.pad(b_router.astype(F32), (0, LANE - N_EXPERTS)).reshape(1, LANE)
    row = pl.BlockSpec((tm, D_MODEL), lambda i: (i, 0))
    small = pl.BlockSpec((tm, LANE), lambda i: (i, 0))
    return pl.pallas_call(
        _outproj_kernel,
        grid=(m // tm,),
        in_specs=[row, pl.BlockSpec((D_MODEL, D_MODEL), lambda i: (0, 0)), row, mod_spec, mod_spec, mod_spec,
                  pl.BlockSpec((1, D_MODEL), lambda i: (0, 0)),
                  pl.BlockSpec((D_MODEL, LANE), lambda i: (0, 0)),
                  pl.BlockSpec((1, LANE), lambda i: (0, 0))],
        out_specs=[row, row, small, small],
        out_shape=[jax.ShapeDtypeStruct((m, D_MODEL), F32), jax.ShapeDtypeStruct((m, D_MODEL), F32),
                   jax.ShapeDtypeStruct((m, LANE), I32), jax.ShapeDtypeStruct((m, LANE), F32)],
        compiler_params=_cparams(("parallel",)),
        name="outproj",
    )(merged, w_out, x, gate1, scale2, shift2, g_norm2.reshape(1, D_MODEL), wr, br)


GATHER_ROWS = 256


def _row_copy(src_ref, o_ref, sem, src_row, dst_row):
    return pltpu.make_async_copy(src_ref.at[pl.ds(src_row, 1)], o_ref.at[pl.ds(dst_row, 1)], sem)


def _gather_kernel(live_ref, idx_ref, src_ref, o_ref, *scratch):
    sem = scratch[-1]
    dst = scratch[0] if len(scratch) == 2 else o_ref
    n = o_ref.shape[0]
    live = live_ref[pl.program_id(0)] != 0

    @pl.when(live)
    def _():
        def issue(r, carry):
            _row_copy(src_ref, dst, sem, idx_ref[0, r], r).start()
            return carry

        def wait(r, carry):
            _row_copy(src_ref, dst, sem, 0, r).wait()
            return carry

        lax.fori_loop(0, n, issue, 0, unroll=8)
        lax.fori_loop(0, n, wait, 0, unroll=8)
        if dst is not o_ref:
            o_ref[...] = dst[...].astype(o_ref.dtype)

    @pl.when(jnp.logical_not(live))
    def _():
        o_ref[...] = jnp.zeros_like(o_ref)


def _gather_rows(src, idx, live, out_dtype):
    n_valid, w = idx.shape[0], src.shape[1]
    n_steps = -(-n_valid // GATHER_ROWS)
    n = n_steps * GATHER_ROWS
    idx = jnp.pad(idx, (0, n - n_valid))
    staging = [] if out_dtype == src.dtype else [pltpu.VMEM((GATHER_ROWS, w), src.dtype)]
    grid_spec = pltpu.PrefetchScalarGridSpec(
        num_scalar_prefetch=1, grid=(n_steps,),
        in_specs=[pl.BlockSpec((None, 1, GATHER_ROWS), lambda i, lv: (i, 0, 0), memory_space=pltpu.SMEM),
                  pl.BlockSpec(memory_space=pl.ANY)],
        out_specs=pl.BlockSpec((GATHER_ROWS, w), lambda i, lv: (i, 0)),
        scratch_shapes=staging + [pltpu.SemaphoreType.DMA(())])
    out = pl.pallas_call(
        _gather_kernel, grid_spec=grid_spec,
        out_shape=jax.ShapeDtypeStruct((n, w), out_dtype),
        compiler_params=pltpu.CompilerParams(dimension_semantics=("arbitrary",), vmem_limit_bytes=VMEM_LIMIT,
                                             disable_bounds_checks=True),
        name="gather_rows",
    )(live, idx.reshape(n_steps, 1, GATHER_ROWS), src)
    return out if n == n_valid else out[:n_valid]


MOE_GROUP = 1536
MOE_SUB = 256


def _moe_kernel(ge_ref, ns_ref, x_ref, wg_ref, wu_ref, wd_ref, bg_ref, bu_ref, bd_ref, o_ref, h_scr, *, n_f, tf):
    g_id = pl.program_id(0)
    j = pl.program_id(1)
    ns = ns_ref[g_id]
    del ge_ref
    n_rows = o_ref.shape[0]

    for k in range(1, n_rows // MOE_SUB + 1):
        rows = k * MOE_SUB

        @pl.when((ns == k) & (j < n_f))
        def _(rows=rows):
            x = x_ref[0:rows, :]
            g = jnp.dot(x, wg_ref[...].astype(BF16), preferred_element_type=F32) + bg_ref[...]
            u = jnp.dot(x, wu_ref[...].astype(BF16), preferred_element_type=F32) + bu_ref[...]
            gate = jnp.minimum(g, SWIGLU_LIMIT)
            up = jnp.clip(u, -SWIGLU_LIMIT, SWIGLU_LIMIT)
            h_scr[j, 0:rows, :] = ((up + 1.0) * gate * _sigmoid(SWIGLU_ALPHA * gate)).astype(BF16)

        @pl.when((ns == k) & (j >= n_f))
        def _(rows=rows):
            acc = jnp.dot(h_scr[0, 0:rows, :], wd_ref[0:tf, :].astype(BF16), preferred_element_type=F32)
            for kk in range(1, n_f):
                acc = acc + jnp.dot(h_scr[kk, 0:rows, :], wd_ref[kk * tf:(kk + 1) * tf, :].astype(BF16),
                                    preferred_element_type=F32)
            o_ref[0:rows, :] = acc + bd_ref[...]
            if rows < n_rows:
                o_ref[rows:, :] = jnp.zeros((n_rows - rows, o_ref.shape[1]), F32)

    @pl.when((ns == 0) & (j >= n_f))
    def _():
        o_ref[...] = jnp.zeros_like(o_ref)


def _moe_ffn(xs, grp_e, n_sub, w_gu, b_gu, w_dn, b_dn, tf):
    n_rows = xs.shape[0]
    n_groups = n_rows // MOE_GROUP
    n_f = D_FF // tf
    tn = tf
    n_o = D_MODEL // tn
    bgu = b_gu.reshape(N_EXPERTS, 1, 2 * D_FF)
    bdn = b_dn.reshape(N_EXPERTS, 1, D_MODEL)

    def jf(g, j, ns):
        return jnp.where(ns[g] > 0, jnp.minimum(j, n_f - 1), 0)

    def jo(g, j, ns):
        return jnp.where(ns[g] > 0, jnp.maximum(j - n_f, 0), 0)

    grid_spec = pltpu.PrefetchScalarGridSpec(
        num_scalar_prefetch=2, grid=(n_groups, n_f + n_o),
        in_specs=[pl.BlockSpec((MOE_GROUP, D_MODEL), lambda g, j, ge, ns: (g, 0)),
                  pl.BlockSpec((None, None, D_MODEL, tf), lambda g, j, ge, ns: (0, ge[g], 0, jf(g, j, ns))),
                  pl.BlockSpec((None, None, D_MODEL, tf), lambda g, j, ge, ns: (0, ge[g], 0, n_f + jf(g, j, ns))),
                  pl.BlockSpec((None, None, D_FF, tn), lambda g, j, ge, ns: (0, ge[g], 0, jo(g, j, ns))),
                  pl.BlockSpec((None, 1, tf), lambda g, j, ge, ns: (ge[g], 0, jf(g, j, ns))),
                  pl.BlockSpec((None, 1, tf), lambda g, j, ge, ns: (ge[g], 0, n_f + jf(g, j, ns))),
                  pl.BlockSpec((None, 1, tn), lambda g, j, ge, ns: (ge[g], 0, jo(g, j, ns)))],
        out_specs=pl.BlockSpec((MOE_GROUP, tn), lambda g, j, ge, ns: (g, jnp.maximum(j - n_f, 0))),
        scratch_shapes=[pltpu.VMEM((n_f, MOE_GROUP, tf), BF16)])
    return pl.pallas_call(
        functools.partial(_moe_kernel, n_f=n_f, tf=tf), grid_spec=grid_spec,
        out_shape=jax.ShapeDtypeStruct((n_rows, D_MODEL), F32),
        compiler_params=pltpu.CompilerParams(dimension_semantics=("parallel", "arbitrary"),
                                             vmem_limit_bytes=MOE_VMEM_LIMIT),
        name="moe_ffn",
    )(grp_e, n_sub, xs, w_gu, w_gu, w_dn, bgu, bgu, bdn)


def _final_kernel(x1_ref, yg_ref, tg_ref, g2_ref, gf_ref, o_ref):
    tg = tg_ref[...]
    ffn = jnp.zeros_like(x1_ref)
    for kk in range(TOP_K):
        ffn = ffn + tg[:, kk:kk + 1] * yg_ref[kk]
    x2 = x1_ref[...] + g2_ref[...] * ffn
    ms = jnp.mean(x2 * x2, axis=-1, keepdims=True)
    o_ref[...] = x2 * lax.rsqrt(ms + NORM_EPS) * gf_ref[...]


def _final(x1, yg, row0, tg, gate2, g_final, rows_per_mod, tm):
    m = x1.shape[0]
    blk0 = row0 // tm
    mod_rows = gate2.shape[1]
    tiles_per_mod = rows_per_mod // tm
    if mod_rows == 1:
        mod_spec = pl.BlockSpec((None, 1, D_MODEL), lambda i: (i // tiles_per_mod, 0, 0))
    else:
        mod_spec = pl.BlockSpec((None, tm, D_MODEL), lambda i: (i // tiles_per_mod, i % tiles_per_mod, 0))
    row = pl.BlockSpec((tm, D_MODEL), lambda i: (i, 0))
    return pl.pallas_call(
        _final_kernel,
        grid=(m // tm,),
        in_specs=[row, pl.BlockSpec((TOP_K, tm, D_MODEL), lambda i: (0, blk0 + i, 0)),
                  pl.BlockSpec((tm, LANE), lambda i: (i, 0)), mod_spec,
                  pl.BlockSpec((1, D_MODEL), lambda i: (0, 0))],
        out_specs=row,
        out_shape=jax.ShapeDtypeStruct((m, D_MODEL), F32),
        compiler_params=_cparams(("parallel",)),
        name="final",
    )(x1, yg, tg, gate2, g_final.reshape(1, D_MODEL))


def _route(topi):
    tt = topi.shape[0]
    gs, sub = MOE_GROUP, MOE_SUB
    sel = jnp.sum((topi[:, :, None] == jnp.arange(N_EXPERTS, dtype=I32)[None, None, :]).astype(I32), axis=1)
    counts = jnp.sum(sel, axis=0)
    grp_per_e = (counts + gs - 1) // gs
    grp_end = jnp.cumsum(grp_per_e)
    grp_start = grp_end - grp_per_e
    rank = jnp.cumsum(sel, axis=0) - sel
    dest = jnp.take_along_axis((grp_start * gs)[None, :] + rank, topi, axis=1)
    n_groups = -(-(tt * TOP_K + N_EXPERTS * (gs - 1)) // gs)
    n_rows = n_groups * gs
    tok = jnp.repeat(jnp.arange(tt, dtype=I32), TOP_K)
    row_tok = (jnp.arange(n_rows, dtype=I32) % tt).at[dest.reshape(-1)].set(tok)
    g_ids = jnp.arange(n_groups, dtype=I32)
    grp_e = jnp.minimum(jnp.searchsorted(grp_end, g_ids, side="right"), N_EXPERTS - 1).astype(I32)
    rows_in = jnp.clip(counts[grp_e] - (g_ids - grp_start[grp_e]) * gs, 0, gs)
    rows_in = jnp.where(g_ids < grp_end[-1], rows_in, 0)
    n_sub = ((rows_in + sub - 1) // sub).astype(I32)
    per = gs // sub
    live = (jnp.arange(n_groups * per, dtype=I32) % per < jnp.repeat(n_sub, per)).astype(I32)
    return dest, row_tok, grp_e, n_sub, live


def kernel(x_prompt, x_sample, c_prompt, c_sample, cache_k, cache_v, cache_kidx, state_gdn, state_conv, page_table,
           w_ada, b_ada, g_norm1, g_norm2, g_final, w_in, gdn_conv_w, gdn_a_log, gdn_dt_bias, gdn_norm_w,
           w_o_attn, w_o_gdn, w_out, w_router, b_router, w_gu, b_gu, w_dn, b_dn):
    bsz, t, d = x_prompt.shape
    db, ts, _ = x_sample.shape
    depth = w_ada.shape[0]
    assert depth == 1 and d == D_MODEL
    past = page_table.shape[1] * PAGE_SIZE
    conv_dim = gdn_conv_w.shape[-1]

    n_c = bsz + db
    n_cp = -(-n_c // 8) * 8
    c_all = jnp.pad(jnp.concatenate([c_prompt, c_sample], axis=0), ((0, n_cp - n_c), (0, 0)))
    mod = _ada(c_all, w_ada, b_ada[0])
    mod_p = mod[:bsz].reshape(bsz, 1, 6, d)
    mod_s = jnp.repeat(mod[bsz:n_c], ts, axis=0).reshape(1, db * ts, 6, d)
    mods = {"p": [mod_p[:, :, i] for i in range(6)], "s": [mod_s[:, :, i] for i in range(6)]}

    w_r = _reorder_w_in(w_in[0])
    w_oa = w_o_attn[0].astype(BF16)
    w_og = w_o_gdn[0].astype(BF16)
    w_o = w_out[0].astype(BF16)

    xp = x_prompt.reshape(bsz * t, d)
    xs = x_sample.reshape(db * ts, d)
    tm_p = min(512, t)
    tm_s = db * ts

    tabs_p = _rot_tables(jnp.arange(t, dtype=I32))
    tabs_s = _rot_tables(jnp.tile(past + jnp.arange(ts, dtype=I32), db))
    p32_p, p16_p = _inproj(xp, g_norm1[0], mods["p"][1], mods["p"][0], w_r, tabs_p, t, tm_p)
    p32_s, _ = _inproj(xs, g_norm1[0], mods["s"][1], mods["s"][0], w_r, tabs_s, db * ts, tm_s)

    attn_p = _attn_prompt(p32_p, p16_p, bsz, t)
    attn_s = _attn_sample(p32_s, cache_k, cache_v, cache_kidx, page_table, ts)

    conv0_p = jnp.zeros((bsz, 8, conv_dim), F32)
    s0_p = jnp.zeros((bsz, GDN_V_HEADS, GDN_DK, GDN_DV), F32)
    gdn_p, sfin_p = _gdn(p32_p, conv0_p, s0_p, gdn_conv_w[0], gdn_a_log[0], gdn_dt_bias[0], gdn_norm_w[0], bsz, t)
    gdn_s, sfin_s = _gdn_sample(p32_s, state_conv[0], state_gdn[0], gdn_conv_w[0], gdn_a_log[0], gdn_dt_bias[0],
                                gdn_norm_w[0], db, ts)

    outs = {}
    for name, attn_o, gdn_o, p32, x, rows_per_mod, tm in (("p", attn_p, gdn_p, p32_p, xp, t, min(256, t)),
                                                          ("s", attn_s, gdn_s, p32_s, xs, db * ts, min(256, db * ts))):
        merged = _merge(attn_o, gdn_o, w_oa, w_og, p32, tm)
        m = mods[name]
        outs[name] = _outproj(merged, w_o, x, m[2], m[4], m[3], g_norm2[0], w_router[0], b_router[0], rows_per_mod, tm)

    h2 = jnp.concatenate([outs["p"][1], outs["s"][1]], axis=0)
    topi = jnp.concatenate([outs["p"][2], outs["s"][2]], axis=0)[:, :TOP_K]
    dest, row_tok, grp_e, n_sub, live = _route(topi)
    xs = _gather_rows(h2, row_tok, live, BF16)
    yb = _moe_ffn(xs, grp_e, n_sub, w_gu, b_gu[0], w_dn, b_dn[0], tf=256)
    live_y = jnp.ones((-(-dest.size // GATHER_ROWS),), I32)
    yg = _gather_rows(yb, dest.T.reshape(-1), live_y, F32).reshape(TOP_K, h2.shape[0], d)

    n_p = bsz * t
    y_p = _final(outs["p"][0], yg, 0, outs["p"][3], mods["p"][5], g_final, t, min(256, t))
    y_s = _final(outs["s"][0], yg, n_p, outs["s"][3], mods["s"][5], g_final, db * ts, min(256, db * ts))

    kv = N_KV_HEADS * HEAD_DIM
    new_k_p = p32_p[:, C_AK:C_AK + kv].reshape(1, bsz, t, N_KV_HEADS, HEAD_DIM)
    new_v_p = p32_p[:, C_AV:C_AV + kv].reshape(1, bsz, t, N_KV_HEADS, HEAD_DIM)
    new_ki_p = p32_p[:, C_MISC:C_MISC + IDX_DIM].reshape(1, bsz, t, IDX_DIM)
    new_k_s = p32_s[:, C_AK:C_AK + kv].reshape(1, db, ts, N_KV_HEADS, HEAD_DIM)
    new_v_s = p32_s[:, C_AV:C_AV + kv].reshape(1, db, ts, N_KV_HEADS, HEAD_DIM)
    new_ki_s = p32_s[:, C_MISC:C_MISC + IDX_DIM].reshape(1, db, ts, IDX_DIM)
    conv_p = p32_p.reshape(bsz, t, PROJ_W)[:, t - (CONV_W - 1):, C_GQ:C_GQ + conv_dim][None]
    pre_s = p32_s[:, C_GQ:C_GQ + conv_dim].reshape(db, ts, conv_dim)
    conv_s = jnp.concatenate([state_conv[0], pre_s], axis=1)[:, ts:][None]
    return (y_p.reshape(bsz, t, d), y_s.reshape(db, ts, d), new_k_p, new_v_p, new_ki_p, conv_p, sfin_p[None],
            new_k_s, new_v_s, new_ki_s, conv_s, sfin_s[None])
```

```python
import functools

import jax
import jax.numpy as jnp
from jax import lax
from jax.experimental import pallas as pl
from jax.experimental.pallas import tpu as pltpu

F32 = jnp.float32
BF16 = jnp.bfloat16
I32 = jnp.int32

D_MODEL = 2048
N_HEADS = 16
N_KV_HEADS = 4
HEAD_DIM = 128
ROPE_THETA = 500000.0
ROT_FRACTION_DIV = 4
IDX_HEADS = 16
IDX_DIM = 64
TOPK_MAX = 256
GDN_QK_HEADS = 8
GDN_V_HEADS = 16
GDN_DK = 128
GDN_DV = 128
CONV_W = 4
N_EXPERTS = 32
TOP_K = 4
D_FF = D_MODEL
SWIGLU_LIMIT = 7.0
SWIGLU_ALPHA = 1.702
NORM_EPS = 1e-6
PAGE_SIZE = 128
LANE = 128

C_AQ, C_AK, C_AV, C_IQ = 0, 2048, 2560, 3072
C_GQ, C_GK, C_GV, C_GZ, C_GL = 4096, 5120, 6144, 8192, 10240
C_MISC = 14336
L_IW, L_GB, L_GA = 64, 80, 96
PROJ_TN = 512
PROJ_W = 14848
MISC_BLK = C_MISC // LANE

VMEM_LIMIT = 48 * 1024 * 1024
MOE_VMEM_LIMIT = 56 * 1024 * 1024
NEG_BIG = -1e30
NEG_BF16 = -(2.0 ** 100)
INT_MIN = -2147483648

NT_DIMS = (((1,), (1,)), ((), ()))


def _cparams(sem):
    return pltpu.CompilerParams(dimension_semantics=sem, vmem_limit_bytes=VMEM_LIMIT)


def _split_bf16(a):
    hi = a.astype(BF16)
    lo = (a - hi.astype(F32)).astype(BF16)
    return hi, lo


def _dot3(a, b, dims=None):
    ah, al = _split_bf16(a)
    bh, bl = _split_bf16(b)
    if dims is None:
        d = lambda x, y: jnp.dot(x, y, preferred_element_type=F32)
    else:
        d = lambda x, y: lax.dot_general(x, y, dims, preferred_element_type=F32)
    return d(ah, bh) + d(al, bh) + d(ah, bl)


def _sigmoid(x):
    return 1.0 / (1.0 + jnp.exp(-x))


def _ada_kernel(c_ref, w_ref, b_ref, o_ref):
    c = c_ref[...]
    a = (c * _sigmoid(c)).astype(BF16)
    o_ref[...] = jnp.dot(a, w_ref[...].astype(BF16), preferred_element_type=F32) + b_ref[...]


def _ada(c, w, b):
    m, k = c.shape
    n = w.shape[2]
    tn = 1024
    return pl.pallas_call(
        _ada_kernel,
        grid=(n // tn,),
        in_specs=[pl.BlockSpec((m, k), lambda j: (0, 0)),
                  pl.BlockSpec((None, k, tn), lambda j: (0, 0, j)),
                  pl.BlockSpec((1, tn), lambda j: (0, j))],
        out_specs=pl.BlockSpec((m, tn), lambda j: (0, j)),
        out_shape=jax.ShapeDtypeStruct((m, n), F32),
        compiler_params=_cparams(("arbitrary",)),
        name="ada",
    )(c, w, b.reshape(1, n))


def _rot_slab(x, tabs, half):
    a, b, c = tabs
    return x * a + pltpu.roll(x, LANE - half, 1) * b + pltpu.roll(x, half, 1) * c


def _inproj_kernel(x_ref, g_ref, sc_ref, sh_ref, w_ref, tab_ref, o32_ref, o16_ref, h_scr):
    j = pl.program_id(1)

    @pl.when(j == 0)
    def _():
        x = x_ref[...]
        ms = jnp.mean(x * x, axis=-1, keepdims=True)
        y = x * lax.rsqrt(ms + NORM_EPS) * g_ref[...]
        h_scr[...] = (y * (1.0 + sc_ref[...]) + sh_ref[...]).astype(BF16)

    acc = jnp.dot(h_scr[...], w_ref[...], preferred_element_type=F32)
    n_slab = PROJ_TN // LANE

    def store(fn):
        for s in range(n_slab):
            v = fn(acc[:, s * LANE:(s + 1) * LANE])
            o32_ref[:, s * LANE:(s + 1) * LANE] = v
            o16_ref[:, s * LANE:(s + 1) * LANE] = v.astype(BF16)

    rot_head = j < 5
    rot_idx = ((j >= 6) & (j < 8)) | (j == C_MISC // PROJ_TN)

    @pl.when(rot_head)
    def _():
        tabs = (tab_ref[0], tab_ref[1], tab_ref[2])
        store(lambda v: _rot_slab(v, tabs, HEAD_DIM // ROT_FRACTION_DIV // 2))

    @pl.when(rot_idx)
    def _():
        tabs = (tab_ref[0], tab_ref[1], tab_ref[2])
        store(lambda v: _rot_slab(v, tabs, IDX_DIM // ROT_FRACTION_DIV // 2))

    @pl.when(jnp.logical_not(rot_head | rot_idx))
    def _():
        store(lambda v: v)


def _tab_type(j):
    return jnp.where(j < 6, 0, jnp.where(j < 28, 1, 2))


def _inproj(x, g_norm, scale, shift, w_r, tabs, rows_per_mod, tm):
    m = x.shape[0]
    tt = tabs.shape[2]
    n_t = tt // tm
    mod_rows = scale.shape[1]
    tiles_per_mod = rows_per_mod // tm
    if mod_rows == 1:
        mod_spec = pl.BlockSpec((None, 1, D_MODEL), lambda i, j: (i // tiles_per_mod, 0, 0))
    else:
        mod_spec = pl.BlockSpec((None, tm, D_MODEL), lambda i, j: (i // tiles_per_mod, i % tiles_per_mod, 0))
    return pl.pallas_call(
        _inproj_kernel,
        grid=(m // tm, PROJ_W // PROJ_TN),
        in_specs=[pl.BlockSpec((tm, D_MODEL), lambda i, j: (i, 0)),
                  pl.BlockSpec((1, D_MODEL), lambda i, j: (0, 0)),
                  mod_spec, mod_spec,
                  pl.BlockSpec((D_MODEL, PROJ_TN), lambda i, j: (0, j)),
                  pl.BlockSpec((None, 3, tm, LANE), lambda i, j: (_tab_type(j), 0, i % n_t, 0))],
        out_specs=[pl.BlockSpec((tm, PROJ_TN), lambda i, j: (i, j)),
                   pl.BlockSpec((tm, PROJ_TN), lambda i, j: (i, j))],
        out_shape=[jax.ShapeDtypeStruct((m, PROJ_W), F32), jax.ShapeDtypeStruct((m, PROJ_W), BF16)],
        scratch_shapes=[pltpu.VMEM((tm, D_MODEL), BF16)],
        compiler_params=_cparams(("parallel", "arbitrary")),
        name="inproj",
    )(x, g_norm.reshape(1, D_MODEL), scale, shift, w_r, tabs)


def _rot_tables(pos):
    tt = pos.shape[0]
    posf = pos.astype(F32)

    def tab(rot, period, scale):
        half = rot // 2
        inv = ROPE_THETA ** (-jnp.arange(half, dtype=F32) * (2.0 / rot))
        ang = posf[:, None] * inv[None, :]
        cos, sin = jnp.cos(ang), jnp.sin(ang)
        zh = jnp.zeros((tt, half), F32)
        rest = period - rot
        a = jnp.concatenate([cos, cos, jnp.ones((tt, rest), F32)], axis=1) * scale
        b = jnp.concatenate([-sin, zh, jnp.zeros((tt, rest), F32)], axis=1) * scale
        c = jnp.concatenate([zh, sin, jnp.zeros((tt, rest), F32)], axis=1) * scale
        return a, b, c

    head = jnp.stack([jnp.tile(t, (1, LANE // HEAD_DIM)) for t in tab(HEAD_DIM // ROT_FRACTION_DIV, HEAD_DIM, 1.0)])
    idxq = jnp.stack([jnp.tile(t, (1, LANE // IDX_DIM))
                      for t in tab(IDX_DIM // ROT_FRACTION_DIV, IDX_DIM, IDX_DIM ** -0.5)])
    ka, kb, kc = tab(IDX_DIM // ROT_FRACTION_DIV, IDX_DIM, 1.0)
    rest = LANE - IDX_DIM
    lane = jnp.arange(rest)
    tail_a = jnp.where(lane < (L_GB - L_IW), IDX_HEADS ** -0.5, 1.0).astype(F32)
    ma = jnp.concatenate([ka, jnp.broadcast_to(tail_a, (tt, rest))], axis=1)
    mb = jnp.concatenate([kb, jnp.zeros((tt, rest), F32)], axis=1)
    mc = jnp.concatenate([kc, jnp.zeros((tt, rest), F32)], axis=1)
    misc = jnp.stack([ma, mb, mc])
    return jnp.stack([head, idxq, misc])


def _reorder_w_in(w_in):
    offs = {}
    o = 0
    for name, wdt in (("aq", 2048), ("ak", 512), ("av", 512), ("iq", 1024), ("ik", 64), ("iw", 16), ("gq", 1024),
                      ("gk", 1024), ("gv", 2048), ("gz", 2048), ("gb", 16), ("ga", 16), ("gl", 4096)):
        offs[name] = (o, wdt)
        o += wdt

    def seg(name):
        s, wdt = offs[name]
        return w_in[:, s:s + wdt]

    k = w_in.shape[0]
    misc = jnp.concatenate([seg("ik"), seg("iw"), seg("gb"), seg("ga"), jnp.zeros((k, LANE - 112), w_in.dtype)], axis=1)
    parts = [seg("aq"), seg("ak"), seg("av"), seg("iq"), seg("gq"), seg("gk"), seg("gv"), seg("gz"), seg("gl"), misc,
             jnp.zeros((k, PROJ_W - C_MISC - LANE), w_in.dtype)]
    return jnp.concatenate(parts, axis=1).astype(BF16)


def _order_key(score):
    bits = pltpu.bitcast(score, I32)
    return bits ^ ((bits >> 31) & 0x7FFFFFFF)


def _kth_largest_key(count_ge, n_rows, n_sel):
    def body(i, t):
        bit = lax.shift_left(jnp.int32(1), 31 - i)
        cand = t ^ bit
        return jnp.where(count_ge(cand) >= n_sel, cand, t)

    return lax.fori_loop(0, 32, body, jnp.full((n_rows, 1), INT_MIN, I32))


RADIX_BITS = 4


def _kth_largest_key_radix(keys, n_sel):
    n_rows = keys.shape[0]
    n_cand = (1 << RADIX_BITS) - 1
    digit = lax.broadcasted_iota(I32, (n_cand, n_rows, 1), 0) + 1
    t_u = jnp.zeros((n_rows, 1), I32)
    for step in range(32 // RADIX_BITS):
        shift = 32 - RADIX_BITS * (step + 1)
        cand = (t_u[None] | lax.shift_left(digit, shift)) ^ INT_MIN
        cnt = jnp.sum(jnp.where(keys[None] >= cand, 1.0, 0.0), axis=2, keepdims=True)
        keep = jnp.sum(jnp.where(cnt >= n_sel, 1, 0), axis=0)
        t_u = t_u | lax.shift_left(keep, shift)
    return t_u ^ INT_MIN


def _attn_prompt_kernel(q_ref, iq_ref, mq_ref, k_ref, v_ref, mk_ref, o_ref, key_scr, *, tq, kc, n_sel):
    qb = pl.program_id(1)
    q0 = qb * tq
    nkc = (q0 + tq + kc - 1) // kc
    qpos = q0 + lax.broadcasted_iota(I32, (tq, 1), 0)
    kiota = lax.broadcasted_iota(I32, (1, kc), 1)
    wq = mq_ref[...]

    def score_body(c, carry):
        ks = pl.multiple_of(c * kc, kc)
        kt = mk_ref[pl.ds(ks, kc), :][:, :IDX_DIM]
        acc = jnp.zeros((tq, kc), F32)
        for h in range(IDX_HEADS):
            s = lax.dot_general(iq_ref[:, h * IDX_DIM:(h + 1) * IDX_DIM], kt, NT_DIMS, preferred_element_type=F32)
            acc = acc + wq[:, L_IW + h:L_IW + h + 1] * jnp.maximum(s, 0.0)
        acc = jnp.where(ks + kiota <= qpos, acc, -jnp.inf)
        key_scr[c] = _order_key(acc)
        return carry

    lax.fori_loop(0, nkc, score_body, 0)

    def count_ge(cand):
        def body(c, acc):
            m = jnp.where(key_scr[c] >= cand, 1.0, 0.0)
            for s in range(kc // LANE):
                acc = acc + m[:, s * LANE:(s + 1) * LANE]
            return acc

        acc = lax.fori_loop(0, nkc, body, jnp.zeros((tq, LANE), F32))
        return jnp.sum(acc, axis=1, keepdims=True)

    thr = _kth_largest_key(count_ge, tq, float(n_sel))

    group = N_HEADS // N_KV_HEADS
    scale = HEAD_DIM ** -0.5
    qss = []
    for n in range(N_KV_HEADS):
        qs = jnp.concatenate(
            [q_ref[:, (n * group + g) * HEAD_DIM:(n * group + g + 1) * HEAD_DIM] for g in range(group)], axis=0)
        qss.append((qs.astype(F32) * scale).astype(BF16))

    def attn_body(c, carry):
        ks = pl.multiple_of(c * kc, kc)
        sel = (key_scr[c] >= thr) & (ks + kiota <= qpos)
        bias = jnp.where(sel, 0.0, NEG_BF16).astype(BF16)[None]
        new = []
        for n in range(N_KV_HEADS):
            m_i, acc = carry[n]
            kn = k_ref[pl.ds(ks, kc), n * HEAD_DIM:(n + 1) * HEAD_DIM]
            vn = jnp.concatenate([v_ref[pl.ds(ks, kc), n * HEAD_DIM:(n + 1) * HEAD_DIM], ones_blk], axis=1)
            s = lax.dot_general(qss[n], kn, NT_DIMS, preferred_element_type=F32).astype(BF16)
            s = (s.reshape(group, tq, kc) + bias).reshape(group * tq, kc)
            m_new = jnp.maximum(m_i, jnp.max(s, axis=1, keepdims=True).astype(F32))
            alpha = jnp.exp(m_i - m_new)
            p = jnp.exp(s - m_new.astype(BF16))
            acc = alpha * acc + jnp.dot(p, vn, preferred_element_type=F32)
            new.append((m_new, acc))
        return tuple(new)

    ones_blk = jnp.ones((kc, HEAD_DIM), BF16)
    init = tuple((jnp.full((group * tq, 1), NEG_BF16, F32), jnp.zeros((group * tq, 2 * HEAD_DIM), F32))
                 for _ in range(N_KV_HEADS))
    fin = lax.fori_loop(0, nkc, attn_body, init)
    for n in range(N_KV_HEADS):
        _, acc_f = fin[n]
        out = acc_f[:, :HEAD_DIM] / acc_f[:, HEAD_DIM:HEAD_DIM + 1]
        for g in range(group):
            o_ref[:, (n * group + g) * HEAD_DIM:(n * group + g + 1) * HEAD_DIM] = out[g * tq:(g + 1) * tq].astype(BF16)


def _attn_prompt(p32, p16, bsz, t, tq=128, kc=512):
    nq = t // tq
    kc = min(kc, t)
    n_sel = min(TOPK_MAX, t // 4)
    kern = functools.partial(_attn_prompt_kernel, tq=tq, kc=kc, n_sel=n_sel)
    return pl.pallas_call(
        kern,
        grid=(bsz, nq),
        in_specs=[pl.BlockSpec((tq, N_HEADS * HEAD_DIM), lambda b, i: (b * nq + i, 0)),
                  pl.BlockSpec((tq, IDX_HEADS * IDX_DIM), lambda b, i: (b * nq + i, C_IQ // (IDX_HEADS * IDX_DIM))),
                  pl.BlockSpec((tq, LANE), lambda b, i: (b * nq + i, MISC_BLK)),
                  pl.BlockSpec((t, N_KV_HEADS * HEAD_DIM), lambda b, i: (b, C_AK // (N_KV_HEADS * HEAD_DIM))),
                  pl.BlockSpec((t, N_KV_HEADS * HEAD_DIM), lambda b, i: (b, C_AV // (N_KV_HEADS * HEAD_DIM))),
                  pl.BlockSpec((t, LANE), lambda b, i: (b, MISC_BLK))],
        out_specs=pl.BlockSpec((tq, N_HEADS * HEAD_DIM), lambda b, i: (b * nq + i, 0)),
        out_shape=jax.ShapeDtypeStruct((bsz * t, N_HEADS * HEAD_DIM), BF16),
        scratch_shapes=[pltpu.VMEM((t // kc, tq, kc), I32)],
        compiler_params=_cparams(("parallel", "arbitrary")),
        name="attn_prompt",
    )(p16, p16, p32, p16, p16, p16)


def _attn_sample_kernel(pt_ref, iq_ref, wc_ref, q_ref, kn_ref, vn_ref, ikn_ref, *rest, n_pages, n_sel, past):
    kidx_refs = rest[:n_pages]
    k_refs = rest[n_pages:2 * n_pages]
    v_refs = rest[2 * n_pages:3 * n_pages]
    o_ref, sc_scr, selx_scr, s_scr = rest[3 * n_pages:]
    del pt_ref
    rows, nkv = 8, N_KV_HEADS
    n_keys = (n_pages + 1) * PAGE_SIZE
    pw = PAGE_SIZE * nkv
    iq = iq_ref[...].astype(BF16)
    wc = wc_ref[...]

    def idx_score(kt, dims):
        s = lax.dot_general(iq, kt.astype(BF16), dims, preferred_element_type=F32)
        x = jnp.maximum(s, 0.0) * wc
        return jnp.sum(x.reshape(IDX_HEADS, rows, PAGE_SIZE), axis=0)

    nn_dims = (((1,), (0,)), ((), ()))
    for p in range(n_pages):
        sc_scr[:, p * PAGE_SIZE:(p + 1) * PAGE_SIZE] = idx_score(kidx_refs[p][...], nn_dims)
    sc_scr[:, n_pages * PAGE_SIZE:] = idx_score(ikn_ref[...], NT_DIMS)

    qpos = past + lax.broadcasted_iota(I32, (rows, 1), 0)
    kpos = lax.broadcasted_iota(I32, (1, n_keys), 1)
    causal = kpos <= qpos
    keys = _order_key(jnp.where(causal, sc_scr[...], -jnp.inf))

    thr = _kth_largest_key_radix(keys, float(n_sel))
    sel = jnp.where((keys >= thr) & causal, 1.0, 0.0).astype(BF16)
    expand = jnp.where((lax.broadcasted_iota(I32, (PAGE_SIZE, pw), 1) >> 2)
                       == lax.broadcasted_iota(I32, (PAGE_SIZE, pw), 0), 1.0, 0.0).astype(BF16)
    for p in range(n_pages + 1):
        selx_scr[:, p * pw:(p + 1) * pw] = jnp.dot(sel[:, p * PAGE_SIZE:(p + 1) * PAGE_SIZE], expand,
                                                   preferred_element_type=F32)
    n_q = N_HEADS * rows
    own_head = ((lax.broadcasted_iota(I32, (n_q, pw), 0) >> 5)
                == (lax.broadcasted_iota(I32, (n_q, pw), 1) & (nkv - 1))).reshape(N_HEADS, rows, pw)
    q = (q_ref[...] * (HEAD_DIM ** -0.5)).astype(BF16)
    m_parts = []
    for p in range(n_pages + 1):
        kb = (k_refs[p] if p < n_pages else kn_ref)[...].astype(BF16)
        s = lax.dot_general(q, kb, NT_DIMS, preferred_element_type=F32)
        ok = own_head & (selx_scr[:, p * pw:(p + 1) * pw] > 0.5)[None]
        s = jnp.where(ok, s.reshape(N_HEADS, rows, pw), NEG_BIG).reshape(n_q, pw)
        s_scr[:, p * pw:(p + 1) * pw] = s
        m_parts.append(jnp.max(s, axis=1, keepdims=True))
    m = m_parts[0]
    for mp in m_parts[1:]:
        m = jnp.maximum(m, mp)
    l_parts, o_parts = [], []
    for p in range(n_pages + 1):
        vb = (v_refs[p] if p < n_pages else vn_ref)[...].astype(BF16)
        pr = jnp.exp(s_scr[:, p * pw:(p + 1) * pw] - m)
        l_parts.append(jnp.sum(pr, axis=1, keepdims=True))
        o_parts.append(jnp.dot(pr.astype(BF16), vb, preferred_element_type=F32))
    l = l_parts[0]
    acc = o_parts[0]
    for lp, op in zip(l_parts[1:], o_parts[1:]):
        l = l + lp
        acc = acc + op
    o_ref[...] = acc / l


def _attn_sample(p32, cache_k, cache_v, cache_kidx, page_table, ts):
    db, n_pages = page_table.shape
    past = n_pages * PAGE_SIZE
    rows = 8
    group = N_HEADS // N_KV_HEADS
    ps = p32.reshape(db, ts, PROJ_W)
    pad_t = ((0, 0), (0, 0), (0, rows - ts), (0, 0))
    iq = ps[:, :, C_IQ:C_IQ + IDX_HEADS * IDX_DIM].reshape(db, ts, IDX_HEADS, IDX_DIM).transpose(0, 2, 1, 3)
    iq = jnp.pad(iq, pad_t).reshape(db, IDX_HEADS * rows, IDX_DIM)
    wc = ps[:, :, C_MISC + L_IW:C_MISC + L_IW + IDX_HEADS].transpose(0, 2, 1)
    wc = jnp.pad(wc, ((0, 0), (0, 0), (0, rows - ts))).reshape(db, IDX_HEADS * rows, 1)
    q = ps[:, :, :N_HEADS * HEAD_DIM].reshape(db, ts, N_HEADS, HEAD_DIM).transpose(0, 2, 1, 3)
    q = jnp.pad(q, pad_t).reshape(db, N_HEADS * rows, HEAD_DIM)
    pad_k = ((0, 0), (0, PAGE_SIZE - ts), (0, 0))
    pw = PAGE_SIZE * N_KV_HEADS
    kn = jnp.pad(ps[:, :, C_AK:C_AK + N_KV_HEADS * HEAD_DIM], pad_k).reshape(db, pw, HEAD_DIM)
    vn = jnp.pad(ps[:, :, C_AV:C_AV + N_KV_HEADS * HEAD_DIM], pad_k).reshape(db, pw, HEAD_DIM)
    ikn = jnp.pad(ps[:, :, C_MISC:C_MISC + IDX_DIM], pad_k)
    n_sel = min(TOPK_MAX, (past + ts) // 4)
    n_phys = cache_k.shape[1]
    ck = cache_k.reshape(1, n_phys, pw, HEAD_DIM)
    cv = cache_v.reshape(1, n_phys, pw, HEAD_DIM)

    def kv_page(p):
        return pl.BlockSpec((None, None, pw, HEAD_DIM), lambda d, pt, p=p: (0, pt[d * n_pages + p], 0, 0))

    ckt = jnp.swapaxes(cache_kidx, 2, 3)

    def idx_page(p):
        return pl.BlockSpec((None, None, IDX_DIM, PAGE_SIZE), lambda d, pt, p=p: (0, pt[d * n_pages + p], 0, 0))

    def per_d(r, w):
        return pl.BlockSpec((None, r, w), lambda d, pt: (d, 0, 0))

    in_specs = ([per_d(IDX_HEADS * rows, IDX_DIM), per_d(IDX_HEADS * rows, 1), per_d(N_HEADS * rows, HEAD_DIM),
                 per_d(pw, HEAD_DIM), per_d(pw, HEAD_DIM), per_d(PAGE_SIZE, IDX_DIM)]
                + [idx_page(p) for p in range(n_pages)]
                + [kv_page(p) for p in range(n_pages)]
                + [kv_page(p) for p in range(n_pages)])
    n_keys = (n_pages + 1) * PAGE_SIZE
    grid_spec = pltpu.PrefetchScalarGridSpec(
        num_scalar_prefetch=1, grid=(db,), in_specs=in_specs,
        out_specs=per_d(N_HEADS * rows, HEAD_DIM),
        scratch_shapes=[pltpu.VMEM((rows, n_keys), F32), pltpu.VMEM((rows, n_keys * N_KV_HEADS), F32),
                        pltpu.VMEM((N_HEADS * rows, n_keys * N_KV_HEADS), F32)])
    kern = functools.partial(_attn_sample_kernel, n_pages=n_pages, n_sel=n_sel, past=past)
    o = pl.pallas_call(
        kern, grid_spec=grid_spec,
        out_shape=jax.ShapeDtypeStruct((db, N_HEADS * rows, HEAD_DIM), F32),
        compiler_params=_cparams(("arbitrary",)),
        name="attn_sample",
    )(page_table.reshape(-1), iq, wc, q, kn, vn, ikn,
      *([ckt] * n_pages), *([ck] * n_pages), *([cv] * n_pages))
    o = o.reshape(db, N_HEADS, rows, HEAD_DIM)[:, :, :ts].transpose(0, 2, 1, 3)
    return o.reshape(db * ts, N_HEADS * HEAD_DIM).astype(BF16)


def _dot3s(ah, al, bh, bl, dims=None):
    if dims is None:
        d = lambda x, y: jnp.dot(x, y, preferred_element_type=F32)
    else:
        d = lambda x, y: lax.dot_general(x, y, dims, preferred_element_type=F32)
    return d(ah, bh) + d(al, bh) + d(ah, bl)


def _cumsum_rows(sel_bf16, g_rows):
    gb = jnp.broadcast_to(g_rows, (g_rows.shape[0], LANE))
    g_hi, g_lo = _split_bf16(gb)
    g_lo2 = (gb - g_hi.astype(F32) - g_lo.astype(F32)).astype(BF16)
    d = lambda y: jnp.dot(sel_bf16, y, preferred_element_type=F32)
    return d(g_hi) + d(g_lo) + d(g_lo2)


def _l2n(x):
    return x * lax.rsqrt(jnp.sum(x * x, axis=-1, keepdims=True) + NORM_EPS)


def _gate_params(misc, hp, live):
    x = misc + hp[1:2, :]
    softplus = jnp.maximum(x, 0.0) + jnp.log(1.0 + jnp.exp(-jnp.abs(x)))
    g_all = -jnp.exp(hp[0:1, :]) * softplus
    beta_all = _sigmoid(misc)
    if live is not None:
        g_all = jnp.where(live, g_all, 0.0)
        beta_all = jnp.where(live, beta_all, 0.0)
    return g_all, beta_all


def _gdn_kernel(q_ref, k_ref, v_ref, z_ref, m_ref, wq_ref, wk_ref, wv_ref, cq_ref, ck_ref, cv_ref, hp_ref, nw_ref,
                s0_ref, o_ref, sf_ref, s_scr, hq_scr, hk_scr, hv_scr, xc_scr, *, c, nh):
    g_idx = pl.program_id(1)
    tb = pl.program_id(2)
    n_tb = pl.num_programs(2)
    dk, dv = GDN_DK, GDN_DV

    @pl.when(tb == 0)
    def _():
        s_scr[...] = s0_ref[...]
        hq_scr[...] = cq_ref[...]
        hk_scr[...] = ck_ref[...]
        hv_scr[...] = cv_ref[...]

    def conv_silu(x_ref, w_ref, h_scr):
        width = x_ref.shape[1]
        x = x_ref[...]
        xc_scr[0:8, 0:width] = h_scr[...]
        xc_scr[8:8 + c, 0:width] = x
        acc = jnp.zeros((c, width), F32)
        for j in range(CONV_W):
            acc = acc + xc_scr[8 - (CONV_W - 1) + j:8 - (CONV_W - 1) + j + c, 0:width] * w_ref[j:j + 1, :]
        h_scr[...] = x[c - 8:c, :]
        return acc * _sigmoid(acc)

    qa = conv_silu(q_ref, wq_ref, hq_scr)
    ka = conv_silu(k_ref, wk_ref, hk_scr)
    va = conv_silu(v_ref, wv_ref, hv_scr)
    g_all, beta_all = _gate_params(m_ref[...], hp_ref[...], None)
    lane = lax.broadcasted_iota(I32, (1, LANE), 1)
    ri = lax.broadcasted_iota(I32, (c, c), 0)
    ci = lax.broadcasted_iota(I32, (c, c), 1)
    causal = ri >= ci
    strict = ri > ci
    tri = jnp.where(causal, 1.0, 0.0).astype(BF16)
    eye = jnp.where(ri == ci, 1.0, 0.0)

    def same_blk(log_b):
        return (ri >> log_b) == (ci >> log_b)

    heads = range(2 * nh)
    ks = [_l2n(ka[:, hh * dk:(hh + 1) * dk]) for hh in range(nh)]
    qss = [_l2n(qa[:, hh * dk:(hh + 1) * dk]) * (dk ** -0.5) for hh in range(nh)]
    kk_qk = [lax.dot_general(jnp.concatenate([ks[hh], qss[hh]], axis=0).astype(BF16), ks[hh].astype(BF16), NT_DIMS,
                             preferred_element_type=F32) for hh in range(nh)]
    g_cols = [jnp.sum(jnp.where(lane == L_GA + 2 * nh * g_idx + hl, g_all, 0.0), axis=1, keepdims=True)
              for hl in heads]
    betas = [jnp.sum(jnp.where(lane == L_GB + 2 * nh * g_idx + hl, beta_all, 0.0), axis=1, keepdims=True)
             for hl in heads]
    gcs = [_cumsum_rows(tri, g) for g in g_cols]
    decays = [jnp.exp(jnp.where(causal, gc - gc.T, -jnp.inf)) for gc in gcs]
    a_mats = [jnp.where(strict, kk_qk[hl // 2][0:c] * betas[hl] * decays[hl], 0.0) for hl in heads]
    nms = [jnp.where(same_blk(3), -a, 0.0) for a in a_mats]
    pms = [eye + n for n in nms]
    for _ in range(2):
        nms = [_dot3(n, n) for n in nms]
        pms = [p + _dot3(p, n) for p, n in zip(pms, nms)]
    for lb in range(3, (c - 1).bit_length()):
        join = same_blk(lb + 1) & jnp.logical_not(same_blk(lb))
        t1s = [_dot3(jnp.where(join, a, 0.0), p) for a, p in zip(a_mats, pms)]
        pms = [p - _dot3(p, t1) for p, t1 in zip(pms, t1s)]
    us = [_dot3(pms[hl], va[:, hl * dv:(hl + 1) * dv] * betas[hl]) for hl in heads]
    ws = [_dot3(pms[hl], ks[hl // 2] * betas[hl] * jnp.exp(gcs[hl])) for hl in heads]
    qks = [jnp.where(causal, kk_qk[hl // 2][c:2 * c] * decays[hl], 0.0) for hl in heads]
    s_in = [s_scr[hl] for hl in heads]
    v_news = [us[hl] - _dot3(ws[hl], s_in[hl]) for hl in heads]
    os_ = [jnp.dot((qss[hl // 2] * jnp.exp(gcs[hl])).astype(BF16), s_in[hl].astype(BF16), preferred_element_type=F32)
           + jnp.dot(qks[hl].astype(BF16), v_news[hl].astype(BF16), preferred_element_type=F32) for hl in heads]
    for hl in heads:
        g_last = gcs[hl][c - 1:c, :]
        kdec = ks[hl // 2] * jnp.exp(g_last - gcs[hl])
        s_scr[hl] = s_in[hl] * jnp.exp(g_last) + _dot3(kdec.T, v_news[hl])
    for hl in heads:
        o = os_[hl]
        z = z_ref[:, hl * dv:(hl + 1) * dv]
        on = o * lax.rsqrt(jnp.mean(o * o, axis=-1, keepdims=True) + NORM_EPS) * nw_ref[...]
        o_ref[:, hl * dv:(hl + 1) * dv] = (on * (z * _sigmoid(z))).astype(BF16)

    @pl.when(tb == n_tb - 1)
    def _():
        sf_ref[...] = s_scr[...]


def _gdn_hp(a_log, dt_bias):
    hp = jnp.zeros((8, LANE), F32)
    return hp.at[0, L_GA:L_GA + GDN_V_HEADS].set(a_log.astype(F32)).at[1, L_GA:L_GA + GDN_V_HEADS].set(
        dt_bias.astype(F32))


def _gdn(p32, conv0, s0, conv_w, a_log, dt_bias, norm_w, bsz, t, c=128, nh=4):
    n_tb = t // c
    n_g = GDN_QK_HEADS // nh
    qw, vw = nh * GDN_DK, 2 * nh * GDN_DV
    row = lambda b, g, i: b * n_tb + i
    in_specs = [
        pl.BlockSpec((c, qw), lambda b, g, i: (row(b, g, i), C_GQ // qw + g)),
        pl.BlockSpec((c, qw), lambda b, g, i: (row(b, g, i), C_GK // qw + g)),
        pl.BlockSpec((c, vw), lambda b, g, i: (row(b, g, i), C_GV // vw + g)),
        pl.BlockSpec((c, vw), lambda b, g, i: (row(b, g, i), C_GZ // vw + g)),
        pl.BlockSpec((c, LANE), lambda b, g, i: (row(b, g, i), MISC_BLK)),
        pl.BlockSpec((CONV_W, qw), lambda b, g, i: (0, g)),
        pl.BlockSpec((CONV_W, qw), lambda b, g, i: (0, n_g + g)),
        pl.BlockSpec((CONV_W, vw), lambda b, g, i: (0, n_g + g)),
        pl.BlockSpec((None, 8, qw), lambda b, g, i: (b, 0, g)),
        pl.BlockSpec((None, 8, qw), lambda b, g, i: (b, 0, n_g + g)),
        pl.BlockSpec((None, 8, vw), lambda b, g, i: (b, 0, n_g + g)),
        pl.BlockSpec((8, LANE), lambda b, g, i: (0, 0)),
        pl.BlockSpec((1, GDN_DV), lambda b, g, i: (0, 0)),
        pl.BlockSpec((None, 2 * nh, GDN_DK, GDN_DV), lambda b, g, i: (b, g, 0, 0)),
    ]
    out_specs = [pl.BlockSpec((c, vw), lambda b, g, i: (row(b, g, i), g)),
                 pl.BlockSpec((None, 2 * nh, GDN_DK, GDN_DV), lambda b, g, i: (b, g, 0, 0))]
    return pl.pallas_call(
        functools.partial(_gdn_kernel, c=c, nh=nh),
        grid=(bsz, n_g, n_tb),
        in_specs=in_specs, out_specs=out_specs,
        out_shape=[jax.ShapeDtypeStruct((bsz * t, GDN_V_HEADS * GDN_DV), BF16),
                   jax.ShapeDtypeStruct((bsz, GDN_V_HEADS, GDN_DK, GDN_DV), F32)],
        scratch_shapes=[pltpu.VMEM((2 * nh, GDN_DK, GDN_DV), F32), pltpu.VMEM((8, qw), F32), pltpu.VMEM((8, qw), F32),
                        pltpu.VMEM((8, vw), F32), pltpu.VMEM((c + 8, vw), F32)],
        compiler_params=_cparams(("parallel", "parallel", "arbitrary")),
        name="gdn",
    )(p32, p32, p32, p32, p32, conv_w, conv_w, conv_w, conv0, conv0, conv0, _gdn_hp(a_log, dt_bias),
      norm_w.reshape(1, GDN_DV), s0)


def _gdn_sample_kernel(x_ref, z_ref, m_ref, w_ref, hp_ref, nw_ref, s0_ref, o_ref, sf_ref, *, ts):
    rows, nvh, dk, dv = 8, GDN_V_HEADS, GDN_DK, GDN_DV
    qd = GDN_QK_HEADS * dk
    x8 = x_ref[...]
    w = w_ref[...]
    acc = x8 * w[0:1, :]
    for j in range(1, CONV_W):
        acc = acc + pltpu.roll(x8, rows - j, 0) * w[j:j + 1, :]
    act = acc * _sigmoid(acc)
    live = lax.broadcasted_iota(I32, (rows, 1), 0) < ts
    g_all, beta_all = _gate_params(m_ref[...], hp_ref[...], live)
    qn = [_l2n(act[:, g * dk:(g + 1) * dk]) * (dk ** -0.5) for g in range(GDN_QK_HEADS)]
    kn = [_l2n(act[:, qd + g * dk:qd + (g + 1) * dk]) for g in range(GDN_QK_HEADS)]
    rep = nvh // GDN_QK_HEADS
    cat = lambda parts: jnp.concatenate(parts, axis=0)
    k = cat([kn[h // rep] for h in range(nvh)])
    qs = cat([qn[h // rep] for h in range(nvh)])
    v = cat([act[:, 2 * qd + h * dv:2 * qd + (h + 1) * dv] for h in range(nvh)])
    z = cat([z_ref[:, h * dv:(h + 1) * dv] for h in range(nvh)])
    beta = cat([beta_all[:, L_GB + h:L_GB + h + 1] for h in range(nvh)])
    g_col = cat([g_all[:, L_GA + h:L_GA + h + 1] for h in range(nvh)])
    n = nvh * rows
    ri = lax.broadcasted_iota(I32, (n, n), 0)
    ci = lax.broadcasted_iota(I32, (n, n), 1)
    same = (ri >> 3) == (ci >> 3)
    causal = same & (ri >= ci)
    strict = same & (ri > ci)
    eye = jnp.where(ri == ci, 1.0, 0.0)
    gc = _cumsum_rows(jnp.where(causal, 1.0, 0.0).astype(BF16), g_col)
    gl = _cumsum_rows(jnp.where(same, 1.0, 0.0).astype(BF16), g_col)
    decay = jnp.exp(jnp.where(causal, gc - gc.T, -jnp.inf))
    kb = k * beta
    a_mat = jnp.where(strict, _dot3(kb, k, NT_DIMS) * decay, 0.0)
    nm = -a_mat
    pm = eye + nm
    for _ in range(2):
        nm = _dot3(nm, nm)
        pm = pm + _dot3(pm, nm)
    u = _dot3(pm, v * beta)
    w_rows = _dot3(pm, kb * jnp.exp(gc))
    qk = jnp.where(causal, _dot3(qs, k, NT_DIMS) * decay, 0.0)
    qg = qs * jnp.exp(gc)
    kdt = (k * jnp.exp(gl - gc)).T
    s_in = [s0_ref[h] for h in range(nvh)]
    v_new, o1 = [], []
    for h in range(nvh):
        sl = slice(h * rows, (h + 1) * rows)
        r = _dot3(cat([w_rows[sl], qg[sl]]), s_in[h])
        v_new.append(u[sl] - r[0:rows])
        o1.append(r[rows:2 * rows])
    v_new = cat(v_new)
    o = cat(o1) + _dot3(qk, v_new)
    vh, vl = _split_bf16(v_new)
    lane = lax.broadcasted_iota(I32, (1, n), 1)
    for h in range(nvh):
        kh, kl = _split_bf16(jnp.where((lane >> 3) == h, kdt, 0.0))
        sf_ref[h] = s_in[h] * jnp.exp(gl[h * rows:h * rows + 1, :]) + _dot3s(kh, kl, vh, vl)
    on = o * lax.rsqrt(jnp.mean(o * o, axis=-1, keepdims=True) + NORM_EPS) * nw_ref[...]
    o_ref[...] = on * (z * _sigmoid(z))


def _gdn_sample(p32, state_conv, s0, conv_w, a_log, dt_bias, norm_w, db, ts):
    rows = 8
    conv_dim = conv_w.shape[-1]
    ps = p32.reshape(db, ts, PROJ_W)
    x8 = jnp.concatenate([state_conv, ps[:, :, C_GQ:C_GQ + conv_dim],
                          jnp.zeros((db, rows - ts - (CONV_W - 1), conv_dim), F32)], axis=1)
    pad_t = ((0, 0), (0, rows - ts), (0, 0))
    z8 = jnp.pad(ps[:, :, C_GZ:C_GZ + GDN_V_HEADS * GDN_DV], pad_t)
    m8 = jnp.pad(ps[:, :, C_MISC:C_MISC + LANE], pad_t)
    n = GDN_V_HEADS * rows
    per_d = lambda w: pl.BlockSpec((None, rows, w), lambda d: (d, 0, 0))
    state = pl.BlockSpec((None, GDN_V_HEADS, GDN_DK, GDN_DV), lambda d: (d, 0, 0, 0))
    o, sf = pl.pallas_call(
        functools.partial(_gdn_sample_kernel, ts=ts),
        grid=(db,),
        in_specs=[per_d(conv_dim), per_d(GDN_V_HEADS * GDN_DV), per_d(LANE),
                  pl.BlockSpec((CONV_W, conv_dim), lambda d: (0, 0)),
                  pl.BlockSpec((8, LANE), lambda d: (0, 0)),
                  pl.BlockSpec((1, GDN_DV), lambda d: (0, 0)), state],
        out_specs=[pl.BlockSpec((None, n, GDN_DV), lambda d: (d, 0, 0)), state],
        out_shape=[jax.ShapeDtypeStruct((db, n, GDN_DV), F32),
                   jax.ShapeDtypeStruct((db, GDN_V_HEADS, GDN_DK, GDN_DV), F32)],
        compiler_params=_cparams(("parallel",)),
        name="gdn_sample",
    )(x8, z8, m8, conv_w, _gdn_hp(a_log, dt_bias), norm_w.reshape(1, GDN_DV), s0)
    o = o.reshape(db, GDN_V_HEADS, rows, GDN_DV)[:, :, :ts].transpose(0, 2, 1, 3)
    return o.reshape(db * ts, GDN_V_HEADS * GDN_DV).astype(BF16), sf


def _merge_kernel(a_ref, g_ref, wa_ref, wg_ref, ga_ref, gg_ref, o_ref):
    pa = jnp.dot(a_ref[...], wa_ref[...], preferred_element_type=F32)
    pg = jnp.dot(g_ref[...], wg_ref[...], preferred_element_type=F32)
    o_ref[...] = (_sigmoid(ga_ref[...]) * pa + _sigmoid(gg_ref[...]) * pg).astype(BF16)


def _merge(attn_o, gdn_o, w_oa, w_og, p32, tm):
    m = attn_o.shape[0]
    tn = 512
    return pl.pallas_call(
        _merge_kernel,
        grid=(m // tm, D_MODEL // tn),
        in_specs=[pl.BlockSpec((tm, D_MODEL), lambda i, j: (i, 0)),
                  pl.BlockSpec((tm, D_MODEL), lambda i, j: (i, 0)),
                  pl.BlockSpec((D_MODEL, tn), lambda i, j: (0, j)),
                  pl.BlockSpec((D_MODEL, tn), lambda i, j: (0, j)),
                  pl.BlockSpec((tm, tn), lambda i, j: (i, C_GL // tn + j)),
                  pl.BlockSpec((tm, tn), lambda i, j: (i, (C_GL + D_MODEL) // tn + j))],
        out_specs=pl.BlockSpec((tm, tn), lambda i, j: (i, j)),
        out_shape=jax.ShapeDtypeStruct((m, D_MODEL), BF16),
        compiler_params=_cparams(("parallel", "arbitrary")),
        name="merge",
    )(attn_o, gdn_o, w_oa, w_og, p32, p32)


def _outproj_kernel(mg_ref, w_ref, x_ref, g1_ref, sc_ref, sh_ref, gn_ref, wr_ref, br_ref,
                    x1_ref, h2_ref, ti_ref, tg_ref):
    x1 = x_ref[...] + g1_ref[...] * jnp.dot(mg_ref[...], w_ref[...], preferred_element_type=F32)
    x1_ref[...] = x1
    ms = jnp.mean(x1 * x1, axis=-1, keepdims=True)
    h2 = x1 * lax.rsqrt(ms + NORM_EPS) * gn_ref[...] * (1.0 + sc_ref[...]) + sh_ref[...]
    h2_ref[...] = h2
    logits = _dot3(h2, wr_ref[...]) + br_ref[...]
    tm = logits.shape[0]
    lane = lax.broadcasted_iota(I32, (tm, LANE), 1)
    lanef = lane.astype(F32)
    l = jnp.where(lane < N_EXPERTS, logits, -jnp.inf)
    vals, idxs = [], []
    for _ in range(TOP_K):
        mx = jnp.max(l, axis=1, keepdims=True)
        ix = jnp.min(jnp.where(l == mx, lanef, float(LANE)), axis=1, keepdims=True)
        vals.append(mx)
        idxs.append(ix)
        l = jnp.where(lanef == ix, -jnp.inf, l)
    es = [jnp.exp(v - vals[0]) for v in vals]
    den = es[0]
    for e in es[1:]:
        den = den + e
    ti = jnp.zeros((tm, LANE), F32)
    tg = jnp.zeros((tm, LANE), F32)
    for kk in range(TOP_K):
        ti = jnp.where(lane == kk, idxs[kk], ti)
        tg = jnp.where(lane == kk, es[kk] / den, tg)
    ti_ref[...] = ti.astype(I32)
    tg_ref[...] = tg


def _outproj(merged, w_out, x, gate1, scale2, shift2, g_norm2, w_router, b_router, rows_per_mod, tm):
    m = x.shape[0]
    mod_rows = gate1.shape[1]
    tiles_per_mod = rows_per_mod // tm
    if mod_rows == 1:
        mod_spec = pl.BlockSpec((None, 1, D_MODEL), lambda i: (i // tiles_per_mod, 0, 0))
    else:
        mod_spec = pl.BlockSpec((None, tm, D_MODEL), lambda i: (i // tiles_per_mod, i % tiles_per_mod, 0))
    wr = jnp.pad(w_router.astype(F32), ((0, 0), (0, LANE - N_EXPERTS)))
    br = jnp.pad(b_router.astype(F32), (0, LANE - N_EXPERTS)).reshape(1, LANE)
    row = pl.BlockSpec((tm, D_MODEL), lambda i: (i, 0))
    small = pl.BlockSpec((tm, LANE), lambda i: (i, 0))
    return pl.pallas_call(
        _outproj_kernel,
        grid=(m // tm,),
        in_specs=[row, pl.BlockSpec((D_MODEL, D_MODEL), lambda i: (0, 0)), row, mod_spec, mod_spec, mod_spec,
                  pl.BlockSpec((1, D_MODEL), lambda i: (0, 0)),
                  pl.BlockSpec((D_MODEL, LANE), lambda i: (0, 0)),
                  pl.BlockSpec((1, LANE), lambda i: (0, 0))],
        out_specs=[row, row, small, small],
        out_shape=[jax.ShapeDtypeStruct((m, D_MODEL), F32), jax.ShapeDtypeStruct((m, D_MODEL), F32),
                   jax.ShapeDtypeStruct((m, LANE), I32), jax.ShapeDtypeStruct((m, LANE), F32)],
        compiler_params=_cparams(("parallel",)),
        name="outproj",
    )(merged, w_out, x, gate1, scale2, shift2, g_norm2.reshape(1, D_MODEL), wr, br)


GATHER_ROWS = 256


def _row_copy(src_ref, o_ref, sem, src_row, dst_row):
    return pltpu.make_async_copy(src_ref.at[pl.ds(src_row, 1)], o_ref.at[pl.ds(dst_row, 1)], sem)


def _gather_kernel(live_ref, idx_ref, src_ref, o_ref, *scratch):
    sem = scratch[-1]
    dst = scratch[0] if len(scratch) == 2 else o_ref
    n = o_ref.shape[0]
    live = live_ref[pl.program_id(0)] != 0

    @pl.when(live)
    def _():
        def issue(r, carry):
            _row_copy(src_ref, dst, sem, idx_ref[0, r], r).start()
            return carry

        def wait(r, carry):
            _row_copy(src_ref, dst, sem, 0, r).wait()
            return carry

        lax.fori_loop(0, n, issue, 0, unroll=8)
        lax.fori_loop(0, n, wait, 0, unroll=8)
        if dst is not o_ref:
            o_ref[...] = dst[...].astype(o_ref.dtype)

    @pl.when(jnp.logical_not(live))
    def _():
        o_ref[...] = jnp.zeros_like(o_ref)


def _gather_rows(src, idx, live, out_dtype):
    n_valid, w = idx.shape[0], src.shape[1]
    n_steps = -(-n_valid // GATHER_ROWS)
    n = n_steps * GATHER_ROWS
    idx = jnp.pad(idx, (0, n - n_valid))
    staging = [] if out_dtype == src.dtype else [pltpu.VMEM((GATHER_ROWS, w), src.dtype)]
    grid_spec = pltpu.PrefetchScalarGridSpec(
        num_scalar_prefetch=1, grid=(n_steps,),
        in_specs=[pl.BlockSpec((None, 1, GATHER_ROWS), lambda i, lv: (i, 0, 0), memory_space=pltpu.SMEM),
                  pl.BlockSpec(memory_space=pl.ANY)],
        out_specs=pl.BlockSpec((GATHER_ROWS, w), lambda i, lv: (i, 0)),
        scratch_shapes=staging + [pltpu.SemaphoreType.DMA(())])
    out = pl.pallas_call(
        _gather_kernel, grid_spec=grid_spec,
        out_shape=jax.ShapeDtypeStruct((n, w), out_dtype),
        compiler_params=pltpu.CompilerParams(dimension_semantics=("arbitrary",), vmem_limit_bytes=VMEM_LIMIT,
                                             disable_bounds_checks=True),
        name="gather_rows",
    )(live, idx.reshape(n_steps, 1, GATHER_ROWS), src)
    return out if n == n_valid else out[:n_valid]


MOE_GROUP = 1536
MOE_SUB = 256


def _moe_kernel(ge_ref, ns_ref, x_ref, wg_ref, wu_ref, wd_ref, bg_ref, bu_ref, bd_ref, o_ref, h_scr, *, n_f, tf):
    g_id = pl.program_id(0)
    j = pl.program_id(1)
    ns = ns_ref[g_id]
    del ge_ref
    n_rows = o_ref.shape[0]

    for k in range(1, n_rows // MOE_SUB + 1):
        rows = k * MOE_SUB

        @pl.when((ns == k) & (j < n_f))
        def _(rows=rows):
            x = x_ref[0:rows, :]
            g = jnp.dot(x, wg_ref[...].astype(BF16), preferred_element_type=F32) + bg_ref[...]
            u = jnp.dot(x, wu_ref[...].astype(BF16), preferred_element_type=F32) + bu_ref[...]
            gate = jnp.minimum(g, SWIGLU_LIMIT)
            up = jnp.clip(u, -SWIGLU_LIMIT, SWIGLU_LIMIT)
            h_scr[j, 0:rows, :] = ((up + 1.0) * gate * _sigmoid(SWIGLU_ALPHA * gate)).astype(BF16)

        @pl.when((ns == k) & (j >= n_f))
        def _(rows=rows):
            acc = jnp.dot(h_scr[0, 0:rows, :], wd_ref[0:tf, :].astype(BF16), preferred_element_type=F32)
            for kk in range(1, n_f):
                acc = acc + jnp.dot(h_scr[kk, 0:rows, :], wd_ref[kk * tf:(kk + 1) * tf, :].astype(BF16),
                                    preferred_element_type=F32)
            o_ref[0:rows, :] = acc + bd_ref[...]
            if rows < n_rows:
                o_ref[rows:, :] = jnp.zeros((n_rows - rows, o_ref.shape[1]), F32)

    @pl.when((ns == 0) & (j >= n_f))
    def _():
        o_ref[...] = jnp.zeros_like(o_ref)


def _moe_ffn(xs, grp_e, n_sub, w_gu, b_gu, w_dn, b_dn, tf):
    n_rows = xs.shape[0]
    n_groups = n_rows // MOE_GROUP
    n_f = D_FF // tf
    tn = tf
    n_o = D_MODEL // tn
    bgu = b_gu.reshape(N_EXPERTS, 1, 2 * D_FF)
    bdn = b_dn.reshape(N_EXPERTS, 1, D_MODEL)

    def jf(g, j, ns):
        return jnp.where(ns[g] > 0, jnp.minimum(j, n_f - 1), 0)

    def jo(g, j, ns):
        return jnp.where(ns[g] > 0, jnp.maximum(j - n_f, 0), 0)

    grid_spec = pltpu.PrefetchScalarGridSpec(
        num_scalar_prefetch=2, grid=(n_groups, n_f + n_o),
        in_specs=[pl.BlockSpec((MOE_GROUP, D_MODEL), lambda g, j, ge, ns: (jnp.where(ns[g] > 0, g, 0), 0)),
                  pl.BlockSpec((None, None, D_MODEL, tf), lambda g, j, ge, ns: (0, ge[g], 0, jf(g, j, ns))),
                  pl.BlockSpec((None, None, D_MODEL, tf), lambda g, j, ge, ns: (0, ge[g], 0, n_f + jf(g, j, ns))),
                  pl.BlockSpec((None, None, D_FF, tn), lambda g, j, ge, ns: (0, ge[g], 0, jo(g, j, ns))),
                  pl.BlockSpec((None, 1, tf), lambda g, j, ge, ns: (ge[g], 0, jf(g, j, ns))),
                  pl.BlockSpec((None, 1, tf), lambda g, j, ge, ns: (ge[g], 0, n_f + jf(g, j, ns))),
                  pl.BlockSpec((None, 1, tn), lambda g, j, ge, ns: (ge[g], 0, jo(g, j, ns)))],
        out_specs=pl.BlockSpec((MOE_GROUP, tn), lambda g, j, ge, ns: (g, jnp.maximum(j - n_f, 0))),
        scratch_shapes=[pltpu.VMEM((n_f, MOE_GROUP, tf), BF16)])
    return pl.pallas_call(
        functools.partial(_moe_kernel, n_f=n_f, tf=tf), grid_spec=grid_spec,
        out_shape=jax.ShapeDtypeStruct((n_rows, D_MODEL), F32),
        compiler_params=pltpu.CompilerParams(dimension_semantics=("arbitrary", "arbitrary"),
                                             vmem_limit_bytes=MOE_VMEM_LIMIT),
        name="moe_ffn",
    )(grp_e, n_sub, xs, w_gu, w_gu, w_dn, bgu, bgu, bdn)


def _final_kernel(x1_ref, yg_ref, tg_ref, g2_ref, gf_ref, o_ref):
    tg = tg_ref[...]
    ffn = jnp.zeros_like(x1_ref)
    for kk in range(TOP_K):
        ffn = ffn + tg[:, kk:kk + 1] * yg_ref[kk]
    x2 = x1_ref[...] + g2_ref[...] * ffn
    ms = jnp.mean(x2 * x2, axis=-1, keepdims=True)
    o_ref[...] = x2 * lax.rsqrt(ms + NORM_EPS) * gf_ref[...]


def _final(x1, yg, row0, tg, gate2, g_final, rows_per_mod, tm):
    m = x1.shape[0]
    blk0 = row0 // tm
    mod_rows = gate2.shape[1]
    tiles_per_mod = rows_per_mod // tm
    if mod_rows == 1:
        mod_spec = pl.BlockSpec((None, 1, D_MODEL), lambda i: (i // tiles_per_mod, 0, 0))
    else:
        mod_spec = pl.BlockSpec((None, tm, D_MODEL), lambda i: (i // tiles_per_mod, i % tiles_per_mod, 0))
    row = pl.BlockSpec((tm, D_MODEL), lambda i: (i, 0))
    return pl.pallas_call(
        _final_kernel,
        grid=(m // tm,),
        in_specs=[row, pl.BlockSpec((TOP_K, tm, D_MODEL), lambda i: (0, blk0 + i, 0)),
                  pl.BlockSpec((tm, LANE), lambda i: (i, 0)), mod_spec,
                  pl.BlockSpec((1, D_MODEL), lambda i: (0, 0))],
        out_specs=row,
        out_shape=jax.ShapeDtypeStruct((m, D_MODEL), F32),
        compiler_params=_cparams(("parallel",)),
        name="final",
    )(x1, yg, tg, gate2, g_final.reshape(1, D_MODEL))


def _route(topi):
    tt = topi.shape[0]
    gs, sub = MOE_GROUP, MOE_SUB
    sel = jnp.sum((topi[:, :, None] == jnp.arange(N_EXPERTS, dtype=I32)[None, None, :]).astype(I32), axis=1)
    counts = jnp.sum(sel, axis=0)
    grp_per_e = (counts + gs - 1) // gs
    grp_end = jnp.cumsum(grp_per_e)
    grp_start = grp_end - grp_per_e
    rank = jnp.cumsum(sel, axis=0) - sel
    dest = jnp.take_along_axis((grp_start * gs)[None, :] + rank, topi, axis=1)
    n_groups = -(-(tt * TOP_K + N_EXPERTS * (gs - 1)) // gs)
    n_rows = n_groups * gs
    tok = jnp.repeat(jnp.arange(tt, dtype=I32), TOP_K)
    row_tok = (jnp.arange(n_rows, dtype=I32) % tt).at[dest.reshape(-1)].set(tok)
    g_ids = jnp.arange(n_groups, dtype=I32)
    grp_e = jnp.minimum(jnp.searchsorted(grp_end, g_ids, side="right"), N_EXPERTS - 1).astype(I32)
    rows_in = jnp.clip(counts[grp_e] - (g_ids - grp_start[grp_e]) * gs, 0, gs)
    rows_in = jnp.where(g_ids < grp_end[-1], rows_in, 0)
    n_sub = ((rows_in + sub - 1) // sub).astype(I32)
    per = gs // sub
    live = (jnp.arange(n_groups * per, dtype=I32) % per < jnp.repeat(n_sub, per)).astype(I32)
    return dest, row_tok, grp_e, n_sub, live


def kernel(x_prompt, x_sample, c_prompt, c_sample, cache_k, cache_v, cache_kidx, state_gdn, state_conv, page_table,
           w_ada, b_ada, g_norm1, g_norm2, g_final, w_in, gdn_conv_w, gdn_a_log, gdn_dt_bias, gdn_norm_w,
           w_o_attn, w_o_gdn, w_out, w_router, b_router, w_gu, b_gu, w_dn, b_dn):
    bsz, t, d = x_prompt.shape
    db, ts, _ = x_sample.shape
    depth = w_ada.shape[0]
    assert depth == 1 and d == D_MODEL
    past = page_table.shape[1] * PAGE_SIZE
    conv_dim = gdn_conv_w.shape[-1]

    n_c = bsz + db
    n_cp = -(-n_c // 8) * 8
    c_all = jnp.pad(jnp.concatenate([c_prompt, c_sample], axis=0), ((0, n_cp - n_c), (0, 0)))
    mod = _ada(c_all, w_ada, b_ada[0])
    mod_p = mod[:bsz].reshape(bsz, 1, 6, d)
    mod_s = jnp.repeat(mod[bsz:n_c], ts, axis=0).reshape(1, db * ts, 6, d)
    mods = {"p": [mod_p[:, :, i] for i in range(6)], "s": [mod_s[:, :, i] for i in range(6)]}

    w_r = _reorder_w_in(w_in[0])
    w_oa = w_o_attn[0].astype(BF16)
    w_og = w_o_gdn[0].astype(BF16)
    w_o = w_out[0].astype(BF16)

    xp = x_prompt.reshape(bsz * t, d)
    xs = x_sample.reshape(db * ts, d)
    tm_p = min(512, t)
    tm_s = db * ts

    tabs_p = _rot_tables(jnp.arange(t, dtype=I32))
    tabs_s = _rot_tables(jnp.tile(past + jnp.arange(ts, dtype=I32), db))
    p32_p, p16_p = _inproj(xp, g_norm1[0], mods["p"][1], mods["p"][0], w_r, tabs_p, t, tm_p)
    p32_s, _ = _inproj(xs, g_norm1[0], mods["s"][1], mods["s"][0], w_r, tabs_s, db * ts, tm_s)

    attn_p = _attn_prompt(p32_p, p16_p, bsz, t)
    attn_s = _attn_sample(p32_s, cache_k, cache_v, cache_kidx, page_table, ts)

    conv0_p = jnp.zeros((bsz, 8, conv_dim), F32)
    s0_p = jnp.zeros((bsz, GDN_V_HEADS, GDN_DK, GDN_DV), F32)
    gdn_p, sfin_p = _gdn(p32_p, conv0_p, s0_p, gdn_conv_w[0], gdn_a_log[0], gdn_dt_bias[0], gdn_norm_w[0], bsz, t)
    gdn_s, sfin_s = _gdn_sample(p32_s, state_conv[0], state_gdn[0], gdn_conv_w[0], gdn_a_log[0], gdn_dt_bias[0],
                                gdn_norm_w[0], db, ts)

    outs = {}
    for name, attn_o, gdn_o, p32, x, rows_per_mod, tm in (("p", attn_p, gdn_p, p32_p, xp, t, min(256, t)),
                                                          ("s", attn_s, gdn_s, p32_s, xs, db * ts, min(256, db * ts))):
        merged = _merge(attn_o, gdn_o, w_oa, w_og, p32, min(512, rows_per_mod))
        m = mods[name]
        outs[name] = _outproj(merged, w_o, x, m[2], m[4], m[3], g_norm2[0], w_router[0], b_router[0], rows_per_mod, tm)

    h2 = jnp.concatenate([outs["p"][1], outs["s"][1]], axis=0)
    topi = jnp.concatenate([outs["p"][2], outs["s"][2]], axis=0)[:, :TOP_K]
    dest, row_tok, grp_e, n_sub, live = _route(topi)
    xs = _gather_rows(h2, row_tok, live, BF16)
    yb = _moe_ffn(xs, grp_e, n_sub, w_gu, b_gu[0], w_dn, b_dn[0], tf=256)
    live_y = jnp.ones((-(-dest.size // GATHER_ROWS),), I32)
    yg = _gather_rows(yb, dest.T.reshape(-1), live_y, F32).reshape(TOP_K, h2.shape[0], d)

    n_p = bsz * t
    y_p = _final(outs["p"][0], yg, 0, outs["p"][3], mods["p"][5], g_final, t, min(256, t))
    y_s = _final(outs["s"][0], yg, n_p, outs["s"][3], mods["s"][5], g_final, db * ts, min(256, db * ts))

    kv = N_KV_HEADS * HEAD_DIM
    new_k_p = p32_p[:, C_AK:C_AK + kv].reshape(1, bsz, t, N_KV_HEADS, HEAD_DIM)
    new_v_p = p32_p[:, C_AV:C_AV + kv].reshape(1, bsz, t, N_KV_HEADS, HEAD_DIM)
    new_ki_p = p32_p[:, C_MISC:C_MISC + IDX_DIM].reshape(1, bsz, t, IDX_DIM)
    new_k_s = p32_s[:, C_AK:C_AK + kv].reshape(1, db, ts, N_KV_HEADS, HEAD_DIM)
    new_v_s = p32_s[:, C_AV:C_AV + kv].reshape(1, db, ts, N_KV_HEADS, HEAD_DIM)
    new_ki_s = p32_s[:, C_MISC:C_MISC + IDX_DIM].reshape(1, db, ts, IDX_DIM)
    conv_p = p32_p.reshape(bsz, t, PROJ_W)[:, t - (CONV_W - 1):, C_GQ:C_GQ + conv_dim][None]
    pre_s = p32_s[:, C_GQ:C_GQ + conv_dim].reshape(db, ts, conv_dim)
    conv_s = jnp.concatenate([state_conv[0], pre_s], axis=1)[:, ts:][None]
    return (y_p.reshape(bsz, t, d), y_s.reshape(db, ts, d), new_k_p, new_v_p, new_ki_p, conv_p, sfin_p[None],
            new_k_s, new_v_s, new_ki_s, conv_s, sfin_s[None])
```

```python
import functools

import jax
import jax.numpy as jnp
from jax import lax
from jax.experimental import pallas as pl
from jax.experimental.pallas import tpu as pltpu

F32 = jnp.float32
BF16 = jnp.bfloat16
I32 = jnp.int32

D_MODEL = 2048
N_HEADS = 16
N_KV_HEADS = 4
HEAD_DIM = 128
ROPE_THETA = 500000.0
ROT_FRACTION_DIV = 4
IDX_HEADS = 16
IDX_DIM = 64
TOPK_MAX = 256
GDN_QK_HEADS = 8
GDN_V_HEADS = 16
GDN_DK = 128
GDN_DV = 128
CONV_W = 4
N_EXPERTS = 32
TOP_K = 4
D_FF = D_MODEL
SWIGLU_LIMIT = 7.0
SWIGLU_ALPHA = 1.702
NORM_EPS = 1e-6
PAGE_SIZE = 128
LANE = 128

C_AQ, C_AK, C_AV, C_IQ = 0, 2048, 2560, 3072
C_GQ, C_GK, C_GV, C_GZ, C_GL = 4096, 5120, 6144, 8192, 10240
C_MISC = 14336
L_IW, L_GB, L_GA = 64, 80, 96
PROJ_TN = 512
PROJ_W = 14848
MISC_BLK = C_MISC // LANE

VMEM_LIMIT = 48 * 1024 * 1024
MOE_VMEM_LIMIT = 56 * 1024 * 1024
NEG_BIG = -1e30
NEG_BF16 = -(2.0 ** 100)
INT_MIN = -2147483648

NT_DIMS = (((1,), (1,)), ((), ()))


def _cparams(sem):
    return pltpu.CompilerParams(dimension_semantics=sem, vmem_limit_bytes=VMEM_LIMIT)


def _split_bf16(a):
    hi = a.astype(BF16)
    lo = (a - hi.astype(F32)).astype(BF16)
    return hi, lo


def _dot3(a, b, dims=None):
    ah, al = _split_bf16(a)
    bh, bl = _split_bf16(b)
    if dims is None:
        d = lambda x, y: jnp.dot(x, y, preferred_element_type=F32)
    else:
        d = lambda x, y: lax.dot_general(x, y, dims, preferred_element_type=F32)
    return d(ah, bh) + d(al, bh) + d(ah, bl)


def _sigmoid(x):
    return 1.0 / (1.0 + jnp.exp(-x))


def _ada_kernel(c_ref, w_ref, b_ref, o_ref):
    c = c_ref[...]
    a = (c * _sigmoid(c)).astype(BF16)
    o_ref[...] = jnp.dot(a, w_ref[...].astype(BF16), preferred_element_type=F32) + b_ref[...]


def _ada(c, w, b):
    m, k = c.shape
    n = w.shape[2]
    tn = 1024
    return pl.pallas_call(
        _ada_kernel,
        grid=(n // tn,),
        in_specs=[pl.BlockSpec((m, k), lambda j: (0, 0)),
                  pl.BlockSpec((None, k, tn), lambda j: (0, 0, j)),
                  pl.BlockSpec((1, tn), lambda j: (0, j))],
        out_specs=pl.BlockSpec((m, tn), lambda j: (0, j)),
        out_shape=jax.ShapeDtypeStruct((m, n), F32),
        compiler_params=_cparams(("arbitrary",)),
        name="ada",
    )(c, w, b.reshape(1, n))


def _rot_slab(x, tabs, half):
    a, b, c = tabs
    return x * a + pltpu.roll(x, LANE - half, 1) * b + pltpu.roll(x, half, 1) * c


def _inproj_kernel(x_ref, g_ref, sc_ref, sh_ref, w_ref, tab_ref, o32_ref, o16_ref, h_scr):
    j = pl.program_id(1)

    @pl.when(j == 0)
    def _():
        x = x_ref[...]
        ms = jnp.mean(x * x, axis=-1, keepdims=True)
        y = x * lax.rsqrt(ms + NORM_EPS) * g_ref[...]
        h_scr[...] = (y * (1.0 + sc_ref[...]) + sh_ref[...]).astype(BF16)

    acc = jnp.dot(h_scr[...], w_ref[...], preferred_element_type=F32)
    n_slab = PROJ_TN // LANE

    def store(fn):
        for s in range(n_slab):
            v = fn(acc[:, s * LANE:(s + 1) * LANE])
            o32_ref[:, s * LANE:(s + 1) * LANE] = v
            o16_ref[:, s * LANE:(s + 1) * LANE] = v.astype(BF16)

    rot_head = j < 5
    rot_idx = ((j >= 6) & (j < 8)) | (j == C_MISC // PROJ_TN)

    @pl.when(rot_head)
    def _():
        tabs = (tab_ref[0], tab_ref[1], tab_ref[2])
        store(lambda v: _rot_slab(v, tabs, HEAD_DIM // ROT_FRACTION_DIV // 2))

    @pl.when(rot_idx)
    def _():
        tabs = (tab_ref[0], tab_ref[1], tab_ref[2])
        store(lambda v: _rot_slab(v, tabs, IDX_DIM // ROT_FRACTION_DIV // 2))

    @pl.when(jnp.logical_not(rot_head | rot_idx))
    def _():
        store(lambda v: v)


def _tab_type(j):
    return jnp.where(j < 6, 0, jnp.where(j < 28, 1, 2))


def _inproj(x, g_norm, scale, shift, w_r, tabs, rows_per_mod, tm):
    m = x.shape[0]
    tt = tabs.shape[2]
    n_t = tt // tm
    mod_rows = scale.shape[1]
    tiles_per_mod = rows_per_mod // tm
    if mod_rows == 1:
        mod_spec = pl.BlockSpec((None, 1, D_MODEL), lambda i, j: (i // tiles_per_mod, 0, 0))
    else:
        mod_spec = pl.BlockSpec((None, tm, D_MODEL), lambda i, j: (i // tiles_per_mod, i % tiles_per_mod, 0))
    return pl.pallas_call(
        _inproj_kernel,
        grid=(m // tm, PROJ_W // PROJ_TN),
        in_specs=[pl.BlockSpec((tm, D_MODEL), lambda i, j: (i, 0)),
                  pl.BlockSpec((1, D_MODEL), lambda i, j: (0, 0)),
                  mod_spec, mod_spec,
                  pl.BlockSpec((D_MODEL, PROJ_TN), lambda i, j: (0, j)),
                  pl.BlockSpec((None, 3, tm, LANE), lambda i, j: (_tab_type(j), 0, i % n_t, 0))],
        out_specs=[pl.BlockSpec((tm, PROJ_TN), lambda i, j: (i, j)),
                   pl.BlockSpec((tm, PROJ_TN), lambda i, j: (i, j))],
        out_shape=[jax.ShapeDtypeStruct((m, PROJ_W), F32), jax.ShapeDtypeStruct((m, PROJ_W), BF16)],
        scratch_shapes=[pltpu.VMEM((tm, D_MODEL), BF16)],
        compiler_params=_cparams(("parallel", "arbitrary")),
        name="inproj",
    )(x, g_norm.reshape(1, D_MODEL), scale, shift, w_r, tabs)


def _rot_tables(pos):
    tt = pos.shape[0]
    posf = pos.astype(F32)

    def tab(rot, period, scale):
        half = rot // 2
        inv = ROPE_THETA ** (-jnp.arange(half, dtype=F32) * (2.0 / rot))
        ang = posf[:, None] * inv[None, :]
        cos, sin = jnp.cos(ang), jnp.sin(ang)
        zh = jnp.zeros((tt, half), F32)
        rest = period - rot
        a = jnp.concatenate([cos, cos, jnp.ones((tt, rest), F32)], axis=1) * scale
        b = jnp.concatenate([-sin, zh, jnp.zeros((tt, rest), F32)], axis=1) * scale
        c = jnp.concatenate([zh, sin, jnp.zeros((tt, rest), F32)], axis=1) * scale
        return a, b, c

    head = jnp.stack([jnp.tile(t, (1, LANE // HEAD_DIM)) for t in tab(HEAD_DIM // ROT_FRACTION_DIV, HEAD_DIM, 1.0)])
    idxq = jnp.stack([jnp.tile(t, (1, LANE // IDX_DIM))
                      for t in tab(IDX_DIM // ROT_FRACTION_DIV, IDX_DIM, IDX_DIM ** -0.5)])
    ka, kb, kc = tab(IDX_DIM // ROT_FRACTION_DIV, IDX_DIM, 1.0)
    rest = LANE - IDX_DIM
    lane = jnp.arange(rest)
    tail_a = jnp.where(lane < (L_GB - L_IW), IDX_HEADS ** -0.5, 1.0).astype(F32)
    ma = jnp.concatenate([ka, jnp.broadcast_to(tail_a, (tt, rest))], axis=1)
    mb = jnp.concatenate([kb, jnp.zeros((tt, rest), F32)], axis=1)
    mc = jnp.concatenate([kc, jnp.zeros((tt, rest), F32)], axis=1)
    misc = jnp.stack([ma, mb, mc])
    return jnp.stack([head, idxq, misc])


def _reorder_w_in(w_in):
    offs = {}
    o = 0
    for name, wdt in (("aq", 2048), ("ak", 512), ("av", 512), ("iq", 1024), ("ik", 64), ("iw", 16), ("gq", 1024),
                      ("gk", 1024), ("gv", 2048), ("gz", 2048), ("gb", 16), ("ga", 16), ("gl", 4096)):
        offs[name] = (o, wdt)
        o += wdt

    def seg(name):
        s, wdt = offs[name]
        return w_in[:, s:s + wdt]

    k = w_in.shape[0]
    misc = jnp.concatenate([seg("ik"), seg("iw"), seg("gb"), seg("ga"), jnp.zeros((k, LANE - 112), w_in.dtype)], axis=1)
    parts = [seg("aq"), seg("ak"), seg("av"), seg("iq"), seg("gq"), seg("gk"), seg("gv"), seg("gz"), seg("gl"), misc,
             jnp.zeros((k, PROJ_W - C_MISC - LANE), w_in.dtype)]
    return jnp.concatenate(parts, axis=1).astype(BF16)


def _order_key(score):
    bits = pltpu.bitcast(score, I32)
    return bits ^ ((bits >> 31) & 0x7FFFFFFF)


def _kth_largest_key(count_ge, n_rows, n_sel):
    def body(i, t):
        bit = lax.shift_left(jnp.int32(1), 31 - i)
        cand = t ^ bit
        return jnp.where(count_ge(cand) >= n_sel, cand, t)

    return lax.fori_loop(0, 32, body, jnp.full((n_rows, 1), INT_MIN, I32))


RADIX_BITS = 4


def _kth_largest_key_radix(keys, n_sel):
    n_rows = keys.shape[0]
    n_cand = (1 << RADIX_BITS) - 1
    digit = lax.broadcasted_iota(I32, (n_cand, n_rows, 1), 0) + 1
    t_u = jnp.zeros((n_rows, 1), I32)
    for step in range(32 // RADIX_BITS):
        shift = 32 - RADIX_BITS * (step + 1)
        cand = (t_u[None] | lax.shift_left(digit, shift)) ^ INT_MIN
        cnt = jnp.sum(jnp.where(keys[None] >= cand, 1.0, 0.0), axis=2, keepdims=True)
        keep = jnp.sum(jnp.where(cnt >= n_sel, 1, 0), axis=0)
        t_u = t_u | lax.shift_left(keep, shift)
    return t_u ^ INT_MIN


def _attn_prompt_kernel(q_ref, iq_ref, mq_ref, k_ref, v_ref, mk_ref, o_ref, key_scr, *, tq, kc, n_sel):
    qb = pl.program_id(1)
    q0 = qb * tq
    nkc = (q0 + tq + kc - 1) // kc
    qpos = q0 + lax.broadcasted_iota(I32, (tq, 1), 0)
    kiota = lax.broadcasted_iota(I32, (1, kc), 1)
    wq = mq_ref[...]

    def score_body(c, carry):
        ks = pl.multiple_of(c * kc, kc)
        kt = mk_ref[pl.ds(ks, kc), :][:, :IDX_DIM]
        acc = jnp.zeros((tq, kc), F32)
        for h in range(IDX_HEADS):
            s = lax.dot_general(iq_ref[:, h * IDX_DIM:(h + 1) * IDX_DIM], kt, NT_DIMS, preferred_element_type=F32)
            acc = acc + wq[:, L_IW + h:L_IW + h + 1] * jnp.maximum(s, 0.0)
        acc = jnp.where(ks + kiota <= qpos, acc, -jnp.inf)
        key_scr[c] = _order_key(acc)
        return carry

    lax.fori_loop(0, nkc, score_body, 0)

    def count(cmp, cand):
        def body(c, acc):
            m = jnp.where(cmp(key_scr[c], cand), 1.0, 0.0)
            for s in range(kc // LANE):
                acc = acc + m[:, s * LANE:(s + 1) * LANE]
            return acc

        acc = lax.fori_loop(0, nkc, body, jnp.zeros((tq, LANE), F32))
        return jnp.sum(acc, axis=1, keepdims=True)

    count_ge = functools.partial(count, lambda a, b: a >= b)
    thr = _kth_largest_key(count_ge, tq, float(n_sel))

    n_tie_ok = float(n_sel) - count(lambda a, b: a > b, thr)

    @pl.when(jnp.max(count_ge(thr)) > float(n_sel))
    def _():
        before = (lax.broadcasted_iota(I32, (kc, kc), 0) < lax.broadcasted_iota(I32, (kc, kc), 1))
        before = jnp.where(before, 1.0, 0.0).astype(BF16)

        def demote(c, seen):
            key = key_scr[c]
            eq = key == thr
            eqf = jnp.where(eq, 1.0, 0.0)
            rank = seen + jnp.dot(eqf.astype(BF16), before, preferred_element_type=F32)
            key_scr[c] = jnp.where(eq & (rank >= n_tie_ok), INT_MIN, key)
            return seen + jnp.sum(eqf, axis=1, keepdims=True)

        lax.fori_loop(0, nkc, demote, jnp.zeros((tq, 1), F32))

    group = N_HEADS // N_KV_HEADS
    scale = HEAD_DIM ** -0.5
    qss = []
    for n in range(N_KV_HEADS):
        qs = jnp.concatenate(
            [q_ref[:, (n * group + g) * HEAD_DIM:(n * group + g + 1) * HEAD_DIM] for g in range(group)], axis=0)
        qss.append((qs.astype(F32) * scale).astype(BF16))

    def attn_body(c, carry):
        ks = pl.multiple_of(c * kc, kc)
        sel = (key_scr[c] >= thr) & (ks + kiota <= qpos)
        bias = jnp.where(sel, 0.0, NEG_BF16).astype(BF16)[None]
        new = []
        for n in range(N_KV_HEADS):
            m_i, acc = carry[n]
            kn = k_ref[pl.ds(ks, kc), n * HEAD_DIM:(n + 1) * HEAD_DIM]
            vn = jnp.concatenate([v_ref[pl.ds(ks, kc), n * HEAD_DIM:(n + 1) * HEAD_DIM], ones_blk], axis=1)
            s = lax.dot_general(qss[n], kn, NT_DIMS, preferred_element_type=F32).astype(BF16)
            s = (s.reshape(group, tq, kc) + bias).reshape(group * tq, kc)
            m_new = jnp.maximum(m_i, jnp.max(s, axis=1, keepdims=True).astype(F32))
            alpha = jnp.exp(m_i - m_new)
            p = jnp.exp(s - m_new.astype(BF16))
            acc = alpha * acc + jnp.dot(p, vn, preferred_element_type=F32)
            new.append((m_new, acc))
        return tuple(new)

    ones_blk = jnp.ones((kc, HEAD_DIM), BF16)
    init = tuple((jnp.full((group * tq, 1), NEG_BF16, F32), jnp.zeros((group * tq, 2 * HEAD_DIM), F32))
                 for _ in range(N_KV_HEADS))
    fin = lax.fori_loop(0, nkc, attn_body, init)
    for n in range(N_KV_HEADS):
        _, acc_f = fin[n]
        out = acc_f[:, :HEAD_DIM] / acc_f[:, HEAD_DIM:HEAD_DIM + 1]
        for g in range(group):
            o_ref[:, (n * group + g) * HEAD_DIM:(n * group + g + 1) * HEAD_DIM] = out[g * tq:(g + 1) * tq].astype(BF16)


def _attn_prompt(p32, p16, bsz, t, tq=128, kc=512):
    nq = t // tq
    kc = min(kc, t)
    n_sel = min(TOPK_MAX, t // 4)
    kern = functools.partial(_attn_prompt_kernel, tq=tq, kc=kc, n_sel=n_sel)
    return pl.pallas_call(
        kern,
        grid=(bsz, nq),
        in_specs=[pl.BlockSpec((tq, N_HEADS * HEAD_DIM), lambda b, i: (b * nq + i, 0)),
                  pl.BlockSpec((tq, IDX_HEADS * IDX_DIM), lambda b, i: (b * nq + i, C_IQ // (IDX_HEADS * IDX_DIM))),
                  pl.BlockSpec((tq, LANE), lambda b, i: (b * nq + i, MISC_BLK)),
                  pl.BlockSpec((t, N_KV_HEADS * HEAD_DIM), lambda b, i: (b, C_AK // (N_KV_HEADS * HEAD_DIM))),
                  pl.BlockSpec((t, N_KV_HEADS * HEAD_DIM), lambda b, i: (b, C_AV // (N_KV_HEADS * HEAD_DIM))),
                  pl.BlockSpec((t, LANE), lambda b, i: (b, MISC_BLK))],
        out_specs=pl.BlockSpec((tq, N_HEADS * HEAD_DIM), lambda b, i: (b * nq + i, 0)),
        out_shape=jax.ShapeDtypeStruct((bsz * t, N_HEADS * HEAD_DIM), BF16),
        scratch_shapes=[pltpu.VMEM((t // kc, tq, kc), I32)],
        compiler_params=_cparams(("parallel", "arbitrary")),
        name="attn_prompt",
    )(p16, p16, p32, p16, p16, p16)


def _attn_sample_kernel(pt_ref, iq_ref, wc_ref, q_ref, kn_ref, vn_ref, ikn_ref, *rest, n_pages, n_sel, past):
    kidx_refs = rest[:n_pages]
    k_refs = rest[n_pages:2 * n_pages]
    v_refs = rest[2 * n_pages:3 * n_pages]
    o_ref, sc_scr, selx_scr, s_scr = rest[3 * n_pages:]
    del pt_ref
    rows, nkv = 8, N_KV_HEADS
    n_keys = (n_pages + 1) * PAGE_SIZE
    pw = PAGE_SIZE * nkv
    iq = iq_ref[...].astype(BF16)
    wc = wc_ref[...]

    def idx_score(kt, dims):
        s = lax.dot_general(iq, kt.astype(BF16), dims, preferred_element_type=F32)
        x = jnp.maximum(s, 0.0) * wc
        return jnp.sum(x.reshape(IDX_HEADS, rows, PAGE_SIZE), axis=0)

    nn_dims = (((1,), (0,)), ((), ()))
    for p in range(n_pages):
        sc_scr[:, p * PAGE_SIZE:(p + 1) * PAGE_SIZE] = idx_score(kidx_refs[p][...], nn_dims)
    sc_scr[:, n_pages * PAGE_SIZE:] = idx_score(ikn_ref[...], NT_DIMS)

    qpos = past + lax.broadcasted_iota(I32, (rows, 1), 0)
    kpos = lax.broadcasted_iota(I32, (1, n_keys), 1)
    causal = kpos <= qpos
    keys = _order_key(jnp.where(causal, sc_scr[...], -jnp.inf))

    thr = _kth_largest_key_radix(keys, float(n_sel))
    above = keys > thr
    tied = jnp.where(keys == thr, 1.0, 0.0)
    n_tie_ok = float(n_sel) - jnp.sum(jnp.where(above, 1.0, 0.0), axis=1, keepdims=True)
    before = (lax.broadcasted_iota(I32, (PAGE_SIZE, PAGE_SIZE), 0) < lax.broadcasted_iota(I32, (PAGE_SIZE, PAGE_SIZE), 1))
    before = jnp.where(before, 1.0, 0.0).astype(BF16)
    seen = jnp.zeros((rows, 1), F32)
    ranks = []
    for p in range(n_pages + 1):
        tp = tied[:, p * PAGE_SIZE:(p + 1) * PAGE_SIZE]
        ranks.append(seen + jnp.dot(tp.astype(BF16), before, preferred_element_type=F32))
        seen = seen + jnp.sum(tp, axis=1, keepdims=True)
    rank = jnp.concatenate(ranks, axis=1)
    sel = jnp.where((above | ((tied > 0.5) & (rank < n_tie_ok))) & causal, 1.0, 0.0).astype(BF16)
    expand = jnp.where((lax.broadcasted_iota(I32, (PAGE_SIZE, pw), 1) >> 2)
                       == lax.broadcasted_iota(I32, (PAGE_SIZE, pw), 0), 1.0, 0.0).astype(BF16)
    for p in range(n_pages + 1):
        selx_scr[:, p * pw:(p + 1) * pw] = jnp.dot(sel[:, p * PAGE_SIZE:(p + 1) * PAGE_SIZE], expand,
                                                   preferred_element_type=F32)
    n_q = N_HEADS * rows
    own_head = ((lax.broadcasted_iota(I32, (n_q, pw), 0) >> 5)
                == (lax.broadcasted_iota(I32, (n_q, pw), 1) & (nkv - 1))).reshape(N_HEADS, rows, pw)
    q = (q_ref[...] * (HEAD_DIM ** -0.5)).astype(BF16)
    m_parts = []
    for p in range(n_pages + 1):
        kb = (k_refs[p] if p < n_pages else kn_ref)[...].astype(BF16)
        s = lax.dot_general(q, kb, NT_DIMS, preferred_element_type=F32)
        ok = own_head & (selx_scr[:, p * pw:(p + 1) * pw] > 0.5)[None]
        s = jnp.where(ok, s.reshape(N_HEADS, rows, pw), NEG_BIG).reshape(n_q, pw)
        s_scr[:, p * pw:(p + 1) * pw] = s
        m_parts.append(jnp.max(s, axis=1, keepdims=True))
    m = m_parts[0]
    for mp in m_parts[1:]:
        m = jnp.maximum(m, mp)
    l_parts, o_parts = [], []
    for p in range(n_pages + 1):
        vb = (v_refs[p] if p < n_pages else vn_ref)[...].astype(BF16)
        pr = jnp.exp(s_scr[:, p * pw:(p + 1) * pw] - m)
        l_parts.append(jnp.sum(pr, axis=1, keepdims=True))
        o_parts.append(jnp.dot(pr.astype(BF16), vb, preferred_element_type=F32))
    l = l_parts[0]
    acc = o_parts[0]
    for lp, op in zip(l_parts[1:], o_parts[1:]):
        l = l + lp
        acc = acc + op
    o_ref[...] = acc / l


def _attn_sample(p32, cache_k, cache_v, cache_kidx, page_table, ts):
    db, n_pages = page_table.shape
    past = n_pages * PAGE_SIZE
    rows = 8
    group = N_HEADS // N_KV_HEADS
    ps = p32.reshape(db, ts, PROJ_W)
    pad_t = ((0, 0), (0, 0), (0, rows - ts), (0, 0))
    iq = ps[:, :, C_IQ:C_IQ + IDX_HEADS * IDX_DIM].reshape(db, ts, IDX_HEADS, IDX_DIM).transpose(0, 2, 1, 3)
    iq = jnp.pad(iq, pad_t).reshape(db, IDX_HEADS * rows, IDX_DIM)
    wc = ps[:, :, C_MISC + L_IW:C_MISC + L_IW + IDX_HEADS].transpose(0, 2, 1)
    wc = jnp.pad(wc, ((0, 0), (0, 0), (0, rows - ts))).reshape(db, IDX_HEADS * rows, 1)
    q = ps[:, :, :N_HEADS * HEAD_DIM].reshape(db, ts, N_HEADS, HEAD_DIM).transpose(0, 2, 1, 3)
    q = jnp.pad(q, pad_t).reshape(db, N_HEADS * rows, HEAD_DIM)
    pad_k = ((0, 0), (0, PAGE_SIZE - ts), (0, 0))
    pw = PAGE_SIZE * N_KV_HEADS
    kn = jnp.pad(ps[:, :, C_AK:C_AK + N_KV_HEADS * HEAD_DIM], pad_k).reshape(db, pw, HEAD_DIM)
    vn = jnp.pad(ps[:, :, C_AV:C_AV + N_KV_HEADS * HEAD_DIM], pad_k).reshape(db, pw, HEAD_DIM)
    ikn = jnp.pad(ps[:, :, C_MISC:C_MISC + IDX_DIM], pad_k)
    n_sel = min(TOPK_MAX, (past + ts) // 4)
    n_phys = cache_k.shape[1]
    ck = cache_k.reshape(1, n_phys, pw, HEAD_DIM)
    cv = cache_v.reshape(1, n_phys, pw, HEAD_DIM)

    def kv_page(p):
        return pl.BlockSpec((None, None, pw, HEAD_DIM), lambda d, pt, p=p: (0, pt[d * n_pages + p], 0, 0))

    ckt = jnp.swapaxes(cache_kidx, 2, 3)

    def idx_page(p):
        return pl.BlockSpec((None, None, IDX_DIM, PAGE_SIZE), lambda d, pt, p=p: (0, pt[d * n_pages + p], 0, 0))

    def per_d(r, w):
        return pl.BlockSpec((None, r, w), lambda d, pt: (d, 0, 0))

    in_specs = ([per_d(IDX_HEADS * rows, IDX_DIM), per_d(IDX_HEADS * rows, 1), per_d(N_HEADS * rows, HEAD_DIM),
                 per_d(pw, HEAD_DIM), per_d(pw, HEAD_DIM), per_d(PAGE_SIZE, IDX_DIM)]
                + [idx_page(p) for p in range(n_pages)]
                + [kv_page(p) for p in range(n_pages)]
                + [kv_page(p) for p in range(n_pages)])
    n_keys = (n_pages + 1) * PAGE_SIZE
    grid_spec = pltpu.PrefetchScalarGridSpec(
        num_scalar_prefetch=1, grid=(db,), in_specs=in_specs,
        out_specs=per_d(N_HEADS * rows, HEAD_DIM),
        scratch_shapes=[pltpu.VMEM((rows, n_keys), F32), pltpu.VMEM((rows, n_keys * N_KV_HEADS), F32),
                        pltpu.VMEM((N_HEADS * rows, n_keys * N_KV_HEADS), F32)])
    kern = functools.partial(_attn_sample_kernel, n_pages=n_pages, n_sel=n_sel, past=past)
    o = pl.pallas_call(
        kern, grid_spec=grid_spec,
        out_shape=jax.ShapeDtypeStruct((db, N_HEADS * rows, HEAD_DIM), F32),
        compiler_params=_cparams(("arbitrary",)),
        name="attn_sample",
    )(page_table.reshape(-1), iq, wc, q, kn, vn, ikn,
      *([ckt] * n_pages), *([ck] * n_pages), *([cv] * n_pages))
    o = o.reshape(db, N_HEADS, rows, HEAD_DIM)[:, :, :ts].transpose(0, 2, 1, 3)
    return o.reshape(db * ts, N_HEADS * HEAD_DIM).astype(BF16)


def _dot3s(ah, al, bh, bl, dims=None):
    if dims is None:
        d = lambda x, y: jnp.dot(x, y, preferred_element_type=F32)
    else:
        d = lambda x, y: lax.dot_general(x, y, dims, preferred_element_type=F32)
    return d(ah, bh) + d(al, bh) + d(ah, bl)


def _cumsum_rows(sel_bf16, g_rows):
    gb = jnp.broadcast_to(g_rows, (g_rows.shape[0], LANE))
    g_hi, g_lo = _split_bf16(gb)
    g_lo2 = (gb - g_hi.astype(F32) - g_lo.astype(F32)).astype(BF16)
    d = lambda y: jnp.dot(sel_bf16, y, preferred_element_type=F32)
    return d(g_hi) + d(g_lo) + d(g_lo2)


def _l2n(x):
    return x * lax.rsqrt(jnp.sum(x * x, axis=-1, keepdims=True) + NORM_EPS)


def _gate_params(misc, hp, live):
    x = misc + hp[1:2, :]
    softplus = jnp.maximum(x, 0.0) + jnp.log(1.0 + jnp.exp(-jnp.abs(x)))
    g_all = -jnp.exp(hp[0:1, :]) * softplus
    beta_all = _sigmoid(misc)
    if live is not None:
        g_all = jnp.where(live, g_all, 0.0)
        beta_all = jnp.where(live, beta_all, 0.0)
    return g_all, beta_all


def _gdn_kernel(q_ref, k_ref, v_ref, z_ref, m_ref, wq_ref, wk_ref, wv_ref, cq_ref, ck_ref, cv_ref, hp_ref, nw_ref,
                s0_ref, o_ref, sf_ref, s_scr, hq_scr, hk_scr, hv_scr, xc_scr, *, c, nh):
    g_idx = pl.program_id(1)
    tb = pl.program_id(2)
    n_tb = pl.num_programs(2)
    dk, dv = GDN_DK, GDN_DV

    @pl.when(tb == 0)
    def _():
        s_scr[...] = s0_ref[...]
        hq_scr[...] = cq_ref[...]
        hk_scr[...] = ck_ref[...]
        hv_scr[...] = cv_ref[...]

    def conv_silu(x_ref, w_ref, h_scr):
        width = x_ref.shape[1]
        x = x_ref[...]
        xc_scr[0:8, 0:width] = h_scr[...]
        xc_scr[8:8 + c, 0:width] = x
        acc = jnp.zeros((c, width), F32)
        for j in range(CONV_W):
            acc = acc + xc_scr[8 - (CONV_W - 1) + j:8 - (CONV_W - 1) + j + c, 0:width] * w_ref[j:j + 1, :]
        h_scr[...] = x[c - 8:c, :]
        return acc * _sigmoid(acc)

    qa = conv_silu(q_ref, wq_ref, hq_scr)
    ka = conv_silu(k_ref, wk_ref, hk_scr)
    va = conv_silu(v_ref, wv_ref, hv_scr)
    g_all, beta_all = _gate_params(m_ref[...], hp_ref[...], None)
    lane = lax.broadcasted_iota(I32, (1, LANE), 1)
    ri = lax.broadcasted_iota(I32, (c, c), 0)
    ci = lax.broadcasted_iota(I32, (c, c), 1)
    causal = ri >= ci
    strict = ri > ci
    tri = jnp.where(causal, 1.0, 0.0).astype(BF16)
    eye = jnp.where(ri == ci, 1.0, 0.0)

    def same_blk(log_b):
        return (ri >> log_b) == (ci >> log_b)

    heads = range(2 * nh)
    ks = [_l2n(ka[:, hh * dk:(hh + 1) * dk]) for hh in range(nh)]
    qss = [_l2n(qa[:, hh * dk:(hh + 1) * dk]) * (dk ** -0.5) for hh in range(nh)]
    kk_qk = [lax.dot_general(jnp.concatenate([ks[hh], qss[hh]], axis=0).astype(BF16), ks[hh].astype(BF16), NT_DIMS,
                             preferred_element_type=F32) for hh in range(nh)]
    g_cols = [jnp.sum(jnp.where(lane == L_GA + 2 * nh * g_idx + hl, g_all, 0.0), axis=1, keepdims=True)
              for hl in heads]
    betas = [jnp.sum(jnp.where(lane == L_GB + 2 * nh * g_idx + hl, beta_all, 0.0), axis=1, keepdims=True)
             for hl in heads]
    gcs = [_cumsum_rows(tri, g) for g in g_cols]
    decays = [jnp.exp(jnp.where(causal, gc - gc.T, -jnp.inf)) for gc in gcs]
    a_mats = [jnp.where(strict, kk_qk[hl // 2][0:c] * betas[hl] * decays[hl], 0.0) for hl in heads]
    nms = [jnp.where(same_blk(3), -a, 0.0) for a in a_mats]
    pms = [eye + n for n in nms]
    for _ in range(2):
        nms = [_dot3(n, n) for n in nms]
        pms = [p + _dot3(p, n) for p, n in zip(pms, nms)]
    for lb in range(3, (c - 1).bit_length()):
        join = same_blk(lb + 1) & jnp.logical_not(same_blk(lb))
        t1s = [_dot3(jnp.where(join, a, 0.0), p) for a, p in zip(a_mats, pms)]
        pms = [p - _dot3(p, t1) for p, t1 in zip(pms, t1s)]
    us = [_dot3(pms[hl], va[:, hl * dv:(hl + 1) * dv] * betas[hl]) for hl in heads]
    ws = [_dot3(pms[hl], ks[hl // 2] * betas[hl] * jnp.exp(gcs[hl])) for hl in heads]
    qks = [jnp.where(causal, kk_qk[hl // 2][c:2 * c] * decays[hl], 0.0) for hl in heads]
    s_in = [s_scr[hl] for hl in heads]
    v_news = [us[hl] - _dot3(ws[hl], s_in[hl]) for hl in heads]
    os_ = [jnp.dot((qss[hl // 2] * jnp.exp(gcs[hl])).astype(BF16), s_in[hl].astype(BF16), preferred_element_type=F32)
           + jnp.dot(qks[hl].astype(BF16), v_news[hl].astype(BF16), preferred_element_type=F32) for hl in heads]
    for hl in heads:
        g_last = gcs[hl][c - 1:c, :]
        kdec = ks[hl // 2] * jnp.exp(g_last - gcs[hl])
        s_scr[hl] = s_in[hl] * jnp.exp(g_last) + _dot3(kdec.T, v_news[hl])
    for hl in heads:
        o = os_[hl]
        z = z_ref[:, hl * dv:(hl + 1) * dv]
        on = o * lax.rsqrt(jnp.mean(o * o, axis=-1, keepdims=True) + NORM_EPS) * nw_ref[...]
        o_ref[:, hl * dv:(hl + 1) * dv] = (on * (z * _sigmoid(z))).astype(BF16)

    @pl.when(tb == n_tb - 1)
    def _():
        sf_ref[...] = s_scr[...]


def _gdn_hp(a_log, dt_bias):
    hp = jnp.zeros((8, LANE), F32)
    return hp.at[0, L_GA:L_GA + GDN_V_HEADS].set(a_log.astype(F32)).at[1, L_GA:L_GA + GDN_V_HEADS].set(
        dt_bias.astype(F32))


def _gdn(p32, conv0, s0, conv_w, a_log, dt_bias, norm_w, bsz, t, c=128, nh=4):
    n_tb = t // c
    n_g = GDN_QK_HEADS // nh
    qw, vw = nh * GDN_DK, 2 * nh * GDN_DV
    row = lambda b, g, i: b * n_tb + i
    in_specs = [
        pl.BlockSpec((c, qw), lambda b, g, i: (row(b, g, i), C_GQ // qw + g)),
        pl.BlockSpec((c, qw), lambda b, g, i: (row(b, g, i), C_GK // qw + g)),
        pl.BlockSpec((c, vw), lambda b, g, i: (row(b, g, i), C_GV // vw + g)),
        pl.BlockSpec((c, vw), lambda b, g, i: (row(b, g, i), C_GZ // vw + g)),
        pl.BlockSpec((c, LANE), lambda b, g, i: (row(b, g, i), MISC_BLK)),
        pl.BlockSpec((CONV_W, qw), lambda b, g, i: (0, g)),
        pl.BlockSpec((CONV_W, qw), lambda b, g, i: (0, n_g + g)),
        pl.BlockSpec((CONV_W, vw), lambda b, g, i: (0, n_g + g)),
        pl.BlockSpec((None, 8, qw), lambda b, g, i: (b, 0, g)),
        pl.BlockSpec((None, 8, qw), lambda b, g, i: (b, 0, n_g + g)),
        pl.BlockSpec((None, 8, vw), lambda b, g, i: (b, 0, n_g + g)),
        pl.BlockSpec((8, LANE), lambda b, g, i: (0, 0)),
        pl.BlockSpec((1, GDN_DV), lambda b, g, i: (0, 0)),
        pl.BlockSpec((None, 2 * nh, GDN_DK, GDN_DV), lambda b, g, i: (b, g, 0, 0)),
    ]
    out_specs = [pl.BlockSpec((c, vw), lambda b, g, i: (row(b, g, i), g)),
                 pl.BlockSpec((None, 2 * nh, GDN_DK, GDN_DV), lambda b, g, i: (b, g, 0, 0))]
    return pl.pallas_call(
        functools.partial(_gdn_kernel, c=c, nh=nh),
        grid=(bsz, n_g, n_tb),
        in_specs=in_specs, out_specs=out_specs,
        out_shape=[jax.ShapeDtypeStruct((bsz * t, GDN_V_HEADS * GDN_DV), BF16),
                   jax.ShapeDtypeStruct((bsz, GDN_V_HEADS, GDN_DK, GDN_DV), F32)],
        scratch_shapes=[pltpu.VMEM((2 * nh, GDN_DK, GDN_DV), F32), pltpu.VMEM((8, qw), F32), pltpu.VMEM((8, qw), F32),
                        pltpu.VMEM((8, vw), F32), pltpu.VMEM((c + 8, vw), F32)],
        compiler_params=_cparams(("parallel", "parallel", "arbitrary")),
        name="gdn",
    )(p32, p32, p32, p32, p32, conv_w, conv_w, conv_w, conv0, conv0, conv0, _gdn_hp(a_log, dt_bias),
      norm_w.reshape(1, GDN_DV), s0)


def _gdn_sample_kernel(x_ref, z_ref, m_ref, w_ref, hp_ref, nw_ref, s0_ref, o_ref, sf_ref, *, ts):
    rows, nvh, dk, dv = 8, GDN_V_HEADS, GDN_DK, GDN_DV
    qd = GDN_QK_HEADS * dk
    x8 = x_ref[...]
    w = w_ref[...]
    acc = x8 * w[0:1, :]
    for j in range(1, CONV_W):
        acc = acc + pltpu.roll(x8, rows - j, 0) * w[j:j + 1, :]
    act = acc * _sigmoid(acc)
    live = lax.broadcasted_iota(I32, (rows, 1), 0) < ts
    g_all, beta_all = _gate_params(m_ref[...], hp_ref[...], live)
    qn = [_l2n(act[:, g * dk:(g + 1) * dk]) * (dk ** -0.5) for g in range(GDN_QK_HEADS)]
    kn = [_l2n(act[:, qd + g * dk:qd + (g + 1) * dk]) for g in range(GDN_QK_HEADS)]
    rep = nvh // GDN_QK_HEADS
    cat = lambda parts: jnp.concatenate(parts, axis=0)
    k = cat([kn[h // rep] for h in range(nvh)])
    qs = cat([qn[h // rep] for h in range(nvh)])
    v = cat([act[:, 2 * qd + h * dv:2 * qd + (h + 1) * dv] for h in range(nvh)])
    z = cat([z_ref[:, h * dv:(h + 1) * dv] for h in range(nvh)])
    beta = cat([beta_all[:, L_GB + h:L_GB + h + 1] for h in range(nvh)])
    g_col = cat([g_all[:, L_GA + h:L_GA + h + 1] for h in range(nvh)])
    n = nvh * rows
    ri = lax.broadcasted_iota(I32, (n, n), 0)
    ci = lax.broadcasted_iota(I32, (n, n), 1)
    same = (ri >> 3) == (ci >> 3)
    causal = same & (ri >= ci)
    strict = same & (ri > ci)
    eye = jnp.where(ri == ci, 1.0, 0.0)
    gc = _cumsum_rows(jnp.where(causal, 1.0, 0.0).astype(BF16), g_col)
    gl = _cumsum_rows(jnp.where(same, 1.0, 0.0).astype(BF16), g_col)
    decay = jnp.exp(jnp.where(causal, gc - gc.T, -jnp.inf))
    kb = k * beta
    a_mat = jnp.where(strict, _dot3(kb, k, NT_DIMS) * decay, 0.0)
    nm = -a_mat
    pm = eye + nm
    for _ in range(2):
        nm = _dot3(nm, nm)
        pm = pm + _dot3(pm, nm)
    u = _dot3(pm, v * beta)
    w_rows = _dot3(pm, kb * jnp.exp(gc))
    qk = jnp.where(causal, _dot3(qs, k, NT_DIMS) * decay, 0.0)
    qg = qs * jnp.exp(gc)
    kdt = (k * jnp.exp(gl - gc)).T
    s_in = [s0_ref[h] for h in range(nvh)]
    v_new, o1 = [], []
    for h in range(nvh):
        sl = slice(h * rows, (h + 1) * rows)
        r = _dot3(cat([w_rows[sl], qg[sl]]), s_in[h])
        v_new.append(u[sl] - r[0:rows])
        o1.append(r[rows:2 * rows])
    v_new = cat(v_new)
    o = cat(o1) + _dot3(qk, v_new)
    vh, vl = _split_bf16(v_new)
    lane = lax.broadcasted_iota(I32, (1, n), 1)
    for h in range(nvh):
        kh, kl = _split_bf16(jnp.where((lane >> 3) == h, kdt, 0.0))
        sf_ref[h] = s_in[h] * jnp.exp(gl[h * rows:h * rows + 1, :]) + _dot3s(kh, kl, vh, vl)
    on = o * lax.rsqrt(jnp.mean(o * o, axis=-1, keepdims=True) + NORM_EPS) * nw_ref[...]
    o_ref[...] = on * (z * _sigmoid(z))


def _gdn_sample(p32, state_conv, s0, conv_w, a_log, dt_bias, norm_w, db, ts):
    rows = 8
    conv_dim = conv_w.shape[-1]
    ps = p32.reshape(db, ts, PROJ_W)
    x8 = jnp.concatenate([state_conv, ps[:, :, C_GQ:C_GQ + conv_dim],
                          jnp.zeros((db, rows - ts - (CONV_W - 1), conv_dim), F32)], axis=1)
    pad_t = ((0, 0), (0, rows - ts), (0, 0))
    z8 = jnp.pad(ps[:, :, C_GZ:C_GZ + GDN_V_HEADS * GDN_DV], pad_t)
    m8 = jnp.pad(ps[:, :, C_MISC:C_MISC + LANE], pad_t)
    n = GDN_V_HEADS * rows
    per_d = lambda w: pl.BlockSpec((None, rows, w), lambda d: (d, 0, 0))
    state = pl.BlockSpec((None, GDN_V_HEADS, GDN_DK, GDN_DV), lambda d: (d, 0, 0, 0))
    o, sf = pl.pallas_call(
        functools.partial(_gdn_sample_kernel, ts=ts),
        grid=(db,),
        in_specs=[per_d(conv_dim), per_d(GDN_V_HEADS * GDN_DV), per_d(LANE),
                  pl.BlockSpec((CONV_W, conv_dim), lambda d: (0, 0)),
                  pl.BlockSpec((8, LANE), lambda d: (0, 0)),
                  pl.BlockSpec((1, GDN_DV), lambda d: (0, 0)), state],
        out_specs=[pl.BlockSpec((None, n, GDN_DV), lambda d: (d, 0, 0)), state],
        out_shape=[jax.ShapeDtypeStruct((db, n, GDN_DV), F32),
                   jax.ShapeDtypeStruct((db, GDN_V_HEADS, GDN_DK, GDN_DV), F32)],
        compiler_params=_cparams(("parallel",)),
        name="gdn_sample",
    )(x8, z8, m8, conv_w, _gdn_hp(a_log, dt_bias), norm_w.reshape(1, GDN_DV), s0)
    o = o.reshape(db, GDN_V_HEADS, rows, GDN_DV)[:, :, :ts].transpose(0, 2, 1, 3)
    return o.reshape(db * ts, GDN_V_HEADS * GDN_DV).astype(BF16), sf


def _merge_kernel(a_ref, g_ref, wa_ref, wg_ref, ga_ref, gg_ref, o_ref):
    pa = jnp.dot(a_ref[...], wa_ref[...], preferred_element_type=F32)
    pg = jnp.dot(g_ref[...], wg_ref[...], preferred_element_type=F32)
    o_ref[...] = (_sigmoid(ga_ref[...]) * pa + _sigmoid(gg_ref[...]) * pg).astype(BF16)


def _merge(attn_o, gdn_o, w_oa, w_og, p32, tm):
    m = attn_o.shape[0]
    tn = 512
    return pl.pallas_call(
        _merge_kernel,
        grid=(m // tm, D_MODEL // tn),
        in_specs=[pl.BlockSpec((tm, D_MODEL), lambda i, j: (i, 0)),
                  pl.BlockSpec((tm, D_MODEL), lambda i, j: (i, 0)),
                  pl.BlockSpec((D_MODEL, tn), lambda i, j: (0, j)),
                  pl.BlockSpec((D_MODEL, tn), lambda i, j: (0, j)),
                  pl.BlockSpec((tm, tn), lambda i, j: (i, C_GL // tn + j)),
                  pl.BlockSpec((tm, tn), lambda i, j: (i, (C_GL + D_MODEL) // tn + j))],
        out_specs=pl.BlockSpec((tm, tn), lambda i, j: (i, j)),
        out_shape=jax.ShapeDtypeStruct((m, D_MODEL), BF16),
        compiler_params=_cparams(("parallel", "arbitrary")),
        name="merge",
    )(attn_o, gdn_o, w_oa, w_og, p32, p32)


def _outproj_kernel(mg_ref, w_ref, x_ref, g1_ref, sc_ref, sh_ref, gn_ref, wr_ref, br_ref,
                    x1_ref, h2_ref, ti_ref, tg_ref):
    x1 = x_ref[...] + g1_ref[...] * jnp.dot(mg_ref[...], w_ref[...], preferred_element_type=F32)
    x1_ref[...] = x1
    ms = jnp.mean(x1 * x1, axis=-1, keepdims=True)
    h2 = x1 * lax.rsqrt(ms + NORM_EPS) * gn_ref[...] * (1.0 + sc_ref[...]) + sh_ref[...]
    h2_ref[...] = h2
    logits = _dot3(h2, wr_ref[...]) + br_ref[...]
    tm = logits.shape[0]
    lane = lax.broadcasted_iota(I32, (tm, LANE), 1)
    lanef = lane.astype(F32)
    l = jnp.where(lane < N_EXPERTS, logits, -jnp.inf)
    vals, idxs = [], []
    for _ in range(TOP_K):
        mx = jnp.max(l, axis=1, keepdims=True)
        ix = jnp.min(jnp.where(l == mx, lanef, float(LANE)), axis=1, keepdims=True)
        vals.append(mx)
        idxs.append(ix)
        l = jnp.where(lanef == ix, -jnp.inf, l)
    es = [jnp.exp(v - vals[0]) for v in vals]
    den = es[0]
    for e in es[1:]:
        den = den + e
    ti = jnp.zeros((tm, LANE), F32)
    tg = jnp.zeros((tm, LANE), F32)
    for kk in range(TOP_K):
        ti = jnp.where(lane == kk, idxs[kk], ti)
        tg = jnp.where(lane == kk, es[kk] / den, tg)
    ti_ref[...] = ti.astype(I32)
    tg_ref[...] = tg


def _outproj(merged, w_out, x, gate1, scale2, shift2, g_norm2, w_router, b_router, rows_per_mod, tm):
    m = x.shape[0]
    mod_rows = gate1.shape[1]
    tiles_per_mod = rows_per_mod // tm
    if mod_rows == 1:
        mod_spec = pl.BlockSpec((None, 1, D_MODEL), lambda i: (i // tiles_per_mod, 0, 0))
    else:
        mod_spec = pl.BlockSpec((None, tm, D_MODEL), lambda i: (i // tiles_per_mod, i % tiles_per_mod, 0))
    wr = jnp.pad(w_router.astype(F32), ((0, 0), (0, LANE - N_EXPERTS)))
    br = jnp.pad(b_router.astype(F32), (0, LANE - N_EXPERTS)).reshape(1, LANE)
    row = pl.BlockSpec((tm, D_MODEL), lambda i: (i, 0))
    small = pl.BlockSpec((tm, LANE), lambda i: (i, 0))
    return pl.pallas_call(
        _outproj_kernel,
        grid=(m // tm,),
        in_specs=[row, pl.BlockSpec((D_MODEL, D_MODEL), lambda i: (0, 0)), row, mod_spec, mod_spec, mod_spec,
                  pl.BlockSpec((1, D_MODEL), lambda i: (0, 0)),
                  pl.BlockSpec((D_MODEL, LANE), lambda i: (0, 0)),
                  pl.BlockSpec((1, LANE), lambda i: (0, 0))],
        out_specs=[row, row, small, small],
        out_shape=[jax.ShapeDtypeStruct((m, D_MODEL), F32), jax.ShapeDtypeStruct((m, D_MODEL), F32),
                   jax.ShapeDtypeStruct((m, LANE), I32), jax.ShapeDtypeStruct((m, LANE), F32)],
        compiler_params=_cparams(("parallel",)),
        name="outproj",
    )(merged, w_out, x, gate1, scale2, shift2, g_norm2.reshape(1, D_MODEL), wr, br)


GATHER_ROWS = 256


def _row_copy(src_ref, o_ref, sem, src_row, dst_row):
    return pltpu.make_async_copy(src_ref.at[pl.ds(src_row, 1)], o_ref.at[pl.ds(dst_row, 1)], sem)


def _gather_kernel(live_ref, idx_ref, src_ref, o_ref, *scratch):
    sem = scratch[-1]
    dst = scratch[0] if len(scratch) == 2 else o_ref
    n = o_ref.shape[0]
    live = live_ref[pl.program_id(0)] != 0

    @pl.when(live)
    def _():
        def issue(r, carry):
            _row_copy(src_ref, dst, sem, idx_ref[0, r], r).start()
            return carry

        def wait(r, carry):
            _row_copy(src_ref, dst, sem, 0, r).wait()
            return carry

        lax.fori_loop(0, n, issue, 0, unroll=8)
        lax.fori_loop(0, n, wait, 0, unroll=8)
        if dst is not o_ref:
            o_ref[...] = dst[...].astype(o_ref.dtype)

    @pl.when(jnp.logical_not(live))
    def _():
        o_ref[...] = jnp.zeros_like(o_ref)


def _gather_rows(src, idx, live, out_dtype):
    n_valid, w = idx.shape[0], src.shape[1]
    n_steps = -(-n_valid // GATHER_ROWS)
    n = n_steps * GATHER_ROWS
    idx = jnp.pad(idx, (0, n - n_valid))
    staging = [] if out_dtype == src.dtype else [pltpu.VMEM((GATHER_ROWS, w), src.dtype)]
    grid_spec = pltpu.PrefetchScalarGridSpec(
        num_scalar_prefetch=1, grid=(n_steps,),
        in_specs=[pl.BlockSpec((None, 1, GATHER_ROWS), lambda i, lv: (i, 0, 0), memory_space=pltpu.SMEM),
                  pl.BlockSpec(memory_space=pl.ANY)],
        out_specs=pl.BlockSpec((GATHER_ROWS, w), lambda i, lv: (i, 0)),
        scratch_shapes=staging + [pltpu.SemaphoreType.DMA(())])
    out = pl.pallas_call(
        _gather_kernel, grid_spec=grid_spec,
        out_shape=jax.ShapeDtypeStruct((n, w), out_dtype),
        compiler_params=pltpu.CompilerParams(dimension_semantics=("arbitrary",), vmem_limit_bytes=VMEM_LIMIT,
                                             disable_bounds_checks=True),
        name="gather_rows",
    )(live, idx.reshape(n_steps, 1, GATHER_ROWS), src)
    return out if n == n_valid else out[:n_valid]


MOE_GROUP = 1536
MOE_SUB = 256


def _moe_kernel(ge_ref, ns_ref, x_ref, wg_ref, wu_ref, wd_ref, bg_ref, bu_ref, bd_ref, o_ref, h_scr, *, n_f, tf):
    g_id = pl.program_id(0)
    j = pl.program_id(1)
    ns = ns_ref[g_id]
    del ge_ref
    n_rows = o_ref.shape[0]

    for k in range(1, n_rows // MOE_SUB + 1):
        rows = k * MOE_SUB

        @pl.when((ns == k) & (j < n_f))
        def _(rows=rows):
            x = x_ref[0:rows, :]
            g = jnp.dot(x, wg_ref[...].astype(BF16), preferred_element_type=F32) + bg_ref[...]
            u = jnp.dot(x, wu_ref[...].astype(BF16), preferred_element_type=F32) + bu_ref[...]
            gate = jnp.minimum(g, SWIGLU_LIMIT)
            up = jnp.clip(u, -SWIGLU_LIMIT, SWIGLU_LIMIT)
            h_scr[j, 0:rows, :] = ((up + 1.0) * gate * _sigmoid(SWIGLU_ALPHA * gate)).astype(BF16)

        @pl.when((ns == k) & (j >= n_f))
        def _(rows=rows):
            acc = jnp.dot(h_scr[0, 0:rows, :], wd_ref[0:tf, :].astype(BF16), preferred_element_type=F32)
            for kk in range(1, n_f):
                acc = acc + jnp.dot(h_scr[kk, 0:rows, :], wd_ref[kk * tf:(kk + 1) * tf, :].astype(BF16),
                                    preferred_element_type=F32)
            o_ref[0:rows, :] = acc + bd_ref[...]
            if rows < n_rows:
                o_ref[rows:, :] = jnp.zeros((n_rows - rows, o_ref.shape[1]), F32)

    @pl.when((ns == 0) & (j >= n_f))
    def _():
        o_ref[...] = jnp.zeros_like(o_ref)


def _moe_ffn(xs, grp_e, n_sub, w_gu, b_gu, w_dn, b_dn, tf):
    n_rows = xs.shape[0]
    n_groups = n_rows // MOE_GROUP
    n_f = D_FF // tf
    tn = tf
    n_o = D_MODEL // tn
    bgu = b_gu.reshape(N_EXPERTS, 1, 2 * D_FF)
    bdn = b_dn.reshape(N_EXPERTS, 1, D_MODEL)

    def jf(g, j, ns):
        return jnp.where(ns[g] > 0, jnp.minimum(j, n_f - 1), 0)

    def jo(g, j, ns):
        return jnp.where(ns[g] > 0, jnp.maximum(j - n_f, 0), 0)

    grid_spec = pltpu.PrefetchScalarGridSpec(
        num_scalar_prefetch=2, grid=(n_groups, n_f + n_o),
        in_specs=[pl.BlockSpec((MOE_GROUP, D_MODEL), lambda g, j, ge, ns: (jnp.where(ns[g] > 0, g, 0), 0)),
                  pl.BlockSpec((None, None, D_MODEL, tf), lambda g, j, ge, ns: (0, ge[g], 0, jf(g, j, ns))),
                  pl.BlockSpec((None, None, D_MODEL, tf), lambda g, j, ge, ns: (0, ge[g], 0, n_f + jf(g, j, ns))),
                  pl.BlockSpec((None, None, D_FF, tn), lambda g, j, ge, ns: (0, ge[g], 0, jo(g, j, ns))),
                  pl.BlockSpec((None, 1, tf), lambda g, j, ge, ns: (ge[g], 0, jf(g, j, ns))),
                  pl.BlockSpec((None, 1, tf), lambda g, j, ge, ns: (ge[g], 0, n_f + jf(g, j, ns))),
                  pl.BlockSpec((None, 1, tn), lambda g, j, ge, ns: (ge[g], 0, jo(g, j, ns)))],
        out_specs=pl.BlockSpec((MOE_GROUP, tn), lambda g, j, ge, ns: (g, jnp.maximum(j - n_f, 0))),
        scratch_shapes=[pltpu.VMEM((n_f, MOE_GROUP, tf), BF16)])
    return pl.pallas_call(
        functools.partial(_moe_kernel, n_f=n_f, tf=tf), grid_spec=grid_spec,
        out_shape=jax.ShapeDtypeStruct((n_rows, D_MODEL), F32),
        compiler_params=pltpu.CompilerParams(dimension_semantics=("arbitrary", "arbitrary"),
                                             vmem_limit_bytes=MOE_VMEM_LIMIT),
        name="moe_ffn",
    )(grp_e, n_sub, xs, w_gu, w_gu, w_dn, bgu, bgu, bdn)


def _final_kernel(x1_ref, yg_ref, tg_ref, g2_ref, gf_ref, o_ref):
    tg = tg_ref[...]
    ffn = jnp.zeros_like(x1_ref)
    for kk in range(TOP_K):
        ffn = ffn + tg[:, kk:kk + 1] * yg_ref[kk]
    x2 = x1_ref[...] + g2_ref[...] * ffn
    ms = jnp.mean(x2 * x2, axis=-1, keepdims=True)
    o_ref[...] = x2 * lax.rsqrt(ms + NORM_EPS) * gf_ref[...]


def _final(x1, yg, row0, tg, gate2, g_final, rows_per_mod, tm):
    m = x1.shape[0]
    blk0 = row0 // tm
    mod_rows = gate2.shape[1]
    tiles_per_mod = rows_per_mod // tm
    if mod_rows == 1:
        mod_spec = pl.BlockSpec((None, 1, D_MODEL), lambda i: (i // tiles_per_mod, 0, 0))
    else:
        mod_spec = pl.BlockSpec((None, tm, D_MODEL), lambda i: (i // tiles_per_mod, i % tiles_per_mod, 0))
    row = pl.BlockSpec((tm, D_MODEL), lambda i: (i, 0))
    return pl.pallas_call(
        _final_kernel,
        grid=(m // tm,),
        in_specs=[row, pl.BlockSpec((TOP_K, tm, D_MODEL), lambda i: (0, blk0 + i, 0)),
                  pl.BlockSpec((tm, LANE), lambda i: (i, 0)), mod_spec,
                  pl.BlockSpec((1, D_MODEL), lambda i: (0, 0))],
        out_specs=row,
        out_shape=jax.ShapeDtypeStruct((m, D_MODEL), F32),
        compiler_params=_cparams(("parallel",)),
        name="final",
    )(x1, yg, tg, gate2, g_final.reshape(1, D_MODEL))


def _route(topi):
    tt = topi.shape[0]
    gs, sub = MOE_GROUP, MOE_SUB
    sel = jnp.sum((topi[:, :, None] == jnp.arange(N_EXPERTS, dtype=I32)[None, None, :]).astype(I32), axis=1)
    counts = jnp.sum(sel, axis=0)
    grp_per_e = (counts + gs - 1) // gs
    grp_end = jnp.cumsum(grp_per_e)
    grp_start = grp_end - grp_per_e
    rank = jnp.cumsum(sel, axis=0) - sel
    dest = jnp.take_along_axis((grp_start * gs)[None, :] + rank, topi, axis=1)
    n_groups = -(-(tt * TOP_K + N_EXPERTS * (gs - 1)) // gs)
    n_rows = n_groups * gs
    tok = jnp.repeat(jnp.arange(tt, dtype=I32), TOP_K)
    row_tok = (jnp.arange(n_rows, dtype=I32) % tt).at[dest.reshape(-1)].set(tok)
    g_ids = jnp.arange(n_groups, dtype=I32)
    grp_e = jnp.minimum(jnp.searchsorted(grp_end, g_ids, side="right"), N_EXPERTS - 1).astype(I32)
    rows_in = jnp.clip(counts[grp_e] - (g_ids - grp_start[grp_e]) * gs, 0, gs)
    rows_in = jnp.where(g_ids < grp_end[-1], rows_in, 0)
    n_sub = ((rows_in + sub - 1) // sub).astype(I32)
    per = gs // sub
    live = (jnp.arange(n_groups * per, dtype=I32) % per < jnp.repeat(n_sub, per)).astype(I32)
    return dest, row_tok, grp_e, n_sub, live


def kernel(x_prompt, x_sample, c_prompt, c_sample, cache_k, cache_v, cache_kidx, state_gdn, state_conv, page_table,
           w_ada, b_ada, g_norm1, g_norm2, g_final, w_in, gdn_conv_w, gdn_a_log, gdn_dt_bias, gdn_norm_w,
           w_o_attn, w_o_gdn, w_out, w_router, b_router, w_gu, b_gu, w_dn, b_dn):
    bsz, t, d = x_prompt.shape
    db, ts, _ = x_sample.shape
    depth = w_ada.shape[0]
    assert depth == 1 and d == D_MODEL
    past = page_table.shape[1] * PAGE_SIZE
    conv_dim = gdn_conv_w.shape[-1]

    n_c = bsz + db
    n_cp = -(-n_c // 8) * 8
    c_all = jnp.pad(jnp.concatenate([c_prompt, c_sample], axis=0), ((0, n_cp - n_c), (0, 0)))
    mod = _ada(c_all, w_ada, b_ada[0])
    mod_p = mod[:bsz].reshape(bsz, 1, 6, d)
    mod_s = jnp.repeat(mod[bsz:n_c], ts, axis=0).reshape(1, db * ts, 6, d)
    mods = {"p": [mod_p[:, :, i] for i in range(6)], "s": [mod_s[:, :, i] for i in range(6)]}

    w_r = _reorder_w_in(w_in[0])
    w_oa = w_o_attn[0].astype(BF16)
    w_og = w_o_gdn[0].astype(BF16)
    w_o = w_out[0].astype(BF16)

    xp = x_prompt.reshape(bsz * t, d)
    xs = x_sample.reshape(db * ts, d)
    tm_p = min(512, t)
    tm_s = db * ts

    tabs_p = _rot_tables(jnp.arange(t, dtype=I32))
    tabs_s = _rot_tables(jnp.tile(past + jnp.arange(ts, dtype=I32), db))
    p32_p, p16_p = _inproj(xp, g_norm1[0], mods["p"][1], mods["p"][0], w_r, tabs_p, t, tm_p)
    p32_s, _ = _inproj(xs, g_norm1[0], mods["s"][1], mods["s"][0], w_r, tabs_s, db * ts, tm_s)

    attn_p = _attn_prompt(p32_p, p16_p, bsz, t)
    attn_s = _attn_sample(p32_s, cache_k, cache_v, cache_kidx, page_table, ts)

    conv0_p = jnp.zeros((bsz, 8, conv_dim), F32)
    s0_p = jnp.zeros((bsz, GDN_V_HEADS, GDN_DK, GDN_DV), F32)
    gdn_p, sfin_p = _gdn(p32_p, conv0_p, s0_p, gdn_conv_w[0], gdn_a_log[0], gdn_dt_bias[0], gdn_norm_w[0], bsz, t)
    gdn_s, sfin_s = _gdn_sample(p32_s, state_conv[0], state_gdn[0], gdn_conv_w[0], gdn_a_log[0], gdn_dt_bias[0],
                                gdn_norm_w[0], db, ts)

    outs = {}
    for name, attn_o, gdn_o, p32, x, rows_per_mod, tm in (("p", attn_p, gdn_p, p32_p, xp, t, min(256, t)),
                                                          ("s", attn_s, gdn_s, p32_s, xs, db * ts, min(256, db * ts))):
        merged = _merge(attn_o, gdn_o, w_oa, w_og, p32, min(512, rows_per_mod))
        m = mods[name]
        outs[name] = _outproj(merged, w_o, x, m[2], m[4], m[3], g_norm2[0], w_router[0], b_router[0], rows_per_mod, tm)

    h2 = jnp.concatenate([outs["p"][1], outs["s"][1]], axis=0)
    topi = jnp.concatenate([outs["p"][2], outs["s"][2]], axis=0)[:, :TOP_K]
    dest, row_tok, grp_e, n_sub, live = _route(topi)
    xs = _gather_rows(h2, row_tok, live, BF16)
    yb = _moe_ffn(xs, grp_e, n_sub, w_gu, b_gu[0], w_dn, b_dn[0], tf=256)
    live_y = jnp.ones((-(-dest.size // GATHER_ROWS),), I32)
    yg = _gather_rows(yb, dest.T.reshape(-1), live_y, F32).reshape(TOP_K, h2.shape[0], d)

    n_p = bsz * t
    y_p = _final(outs["p"][0], yg, 0, outs["p"][3], mods["p"][5], g_final, t, min(256, t))
    y_s = _final(outs["s"][0], yg, n_p, outs["s"][3], mods["s"][5], g_final, db * ts, min(256, db * ts))

    kv = N_KV_HEADS * HEAD_DIM
    new_k_p = p32_p[:, C_AK:C_AK + kv].reshape(1, bsz, t, N_KV_HEADS, HEAD_DIM)
    new_v_p = p32_p[:, C_AV:C_AV + kv].reshape(1, bsz, t, N_KV_HEADS, HEAD_DIM)
    new_ki_p = p32_p[:, C_MISC:C_MISC + IDX_DIM].reshape(1, bsz, t, IDX_DIM)
    new_k_s = p32_s[:, C_AK:C_AK + kv].reshape(1, db, ts, N_KV_HEADS, HEAD_DIM)
    new_v_s = p32_s[:, C_AV:C_AV + kv].reshape(1, db, ts, N_KV_HEADS, HEAD_DIM)
    new_ki_s = p32_s[:, C_MISC:C_MISC + IDX_DIM].reshape(1, db, ts, IDX_DIM)
    conv_p = p32_p.reshape(bsz, t, PROJ_W)[:, t - (CONV_W - 1):, C_GQ:C_GQ + conv_dim][None]
    pre_s = p32_s[:, C_GQ:C_GQ + conv_dim].reshape(db, ts, conv_dim)
    conv_s = jnp.concatenate([state_conv[0], pre_s], axis=1)[:, ts:][None]
    return (y_p.reshape(bsz, t, d), y_s.reshape(db, ts, d), new_k_p, new_v_p, new_ki_p, conv_p, sfin_p[None],
            new_k_s, new_v_s, new_ki_s, conv_s, sfin_s[None])
```

```python
import functools

import jax
import jax.numpy as jnp
from jax import lax
from jax.experimental import pallas as pl
from jax.experimental.pallas import tpu as pltpu

F32 = jnp.float32
BF16 = jnp.bfloat16
I32 = jnp.int32

D_MODEL = 2048
N_HEADS = 16
N_KV_HEADS = 4
HEAD_DIM = 128
ROPE_THETA = 500000.0
ROT_FRACTION_DIV = 4
IDX_HEADS = 16
IDX_DIM = 64
TOPK_MAX = 256
GDN_QK_HEADS = 8
GDN_V_HEADS = 16
GDN_DK = 128
GDN_DV = 128
CONV_W = 4
N_EXPERTS = 32
TOP_K = 4
D_FF = D_MODEL
SWIGLU_LIMIT = 7.0
SWIGLU_ALPHA = 1.702
NORM_EPS = 1e-6
PAGE_SIZE = 128
LANE = 128

C_AQ, C_AK, C_AV, C_IQ = 0, 2048, 2560, 3072
C_GQ, C_GK, C_GV, C_GZ, C_GL = 4096, 5120, 6144, 8192, 10240
C_MISC = 14336
L_IW, L_GB, L_GA = 64, 80, 96
PROJ_TN = 512
PROJ_W = 14848
MISC_BLK = C_MISC // LANE

VMEM_LIMIT = 48 * 1024 * 1024
MOE_VMEM_LIMIT = 56 * 1024 * 1024
NEG_BIG = -1e30
NEG_BF16 = -(2.0 ** 100)
INT_MIN = -2147483648

NT_DIMS = (((1,), (1,)), ((), ()))


def _cparams(sem):
    return pltpu.CompilerParams(dimension_semantics=sem, vmem_limit_bytes=VMEM_LIMIT)


def _split_bf16(a):
    hi = a.astype(BF16)
    lo = (a - hi.astype(F32)).astype(BF16)
    return hi, lo


def _dot3(a, b, dims=None):
    ah, al = _split_bf16(a)
    bh, bl = _split_bf16(b)
    if dims is None:
        d = lambda x, y: jnp.dot(x, y, preferred_element_type=F32)
    else:
        d = lambda x, y: lax.dot_general(x, y, dims, preferred_element_type=F32)
    return d(ah, bh) + d(al, bh) + d(ah, bl)


def _sigmoid(x):
    return 1.0 / (1.0 + jnp.exp(-x))


def _ada_kernel(c_ref, w_ref, b_ref, o_ref):
    c = c_ref[...]
    a = (c * _sigmoid(c)).astype(BF16)
    o_ref[...] = jnp.dot(a, w_ref[...].astype(BF16), preferred_element_type=F32) + b_ref[...]


def _ada(c, w, b):
    m, k = c.shape
    n = w.shape[2]
    tn = 1024
    return pl.pallas_call(
        _ada_kernel,
        grid=(n // tn,),
        in_specs=[pl.BlockSpec((m, k), lambda j: (0, 0)),
                  pl.BlockSpec((None, k, tn), lambda j: (0, 0, j)),
                  pl.BlockSpec((1, tn), lambda j: (0, j))],
        out_specs=pl.BlockSpec((m, tn), lambda j: (0, j)),
        out_shape=jax.ShapeDtypeStruct((m, n), F32),
        compiler_params=_cparams(("arbitrary",)),
        name="ada",
    )(c, w, b.reshape(1, n))


def _rot_slab(x, tabs, half):
    a, b, c = tabs
    return x * a + pltpu.roll(x, LANE - half, 1) * b + pltpu.roll(x, half, 1) * c


def _inproj_kernel(x_ref, g_ref, sc_ref, sh_ref, w_ref, tab_ref, o32_ref, o16_ref, h_scr):
    j = pl.program_id(1)

    @pl.when(j == 0)
    def _():
        x = x_ref[...]
        ms = jnp.mean(x * x, axis=-1, keepdims=True)
        y = x * lax.rsqrt(ms + NORM_EPS) * g_ref[...]
        h_scr[...] = (y * (1.0 + sc_ref[...]) + sh_ref[...]).astype(BF16)

    acc = jnp.dot(h_scr[...], w_ref[...], preferred_element_type=F32)
    n_slab = PROJ_TN // LANE

    def store(fn):
        for s in range(n_slab):
            v = fn(acc[:, s * LANE:(s + 1) * LANE])
            o32_ref[:, s * LANE:(s + 1) * LANE] = v
            o16_ref[:, s * LANE:(s + 1) * LANE] = v.astype(BF16)

    rot_head = j < 5
    rot_idx = ((j >= 6) & (j < 8)) | (j == C_MISC // PROJ_TN)

    @pl.when(rot_head)
    def _():
        tabs = (tab_ref[0], tab_ref[1], tab_ref[2])
        store(lambda v: _rot_slab(v, tabs, HEAD_DIM // ROT_FRACTION_DIV // 2))

    @pl.when(rot_idx)
    def _():
        tabs = (tab_ref[0], tab_ref[1], tab_ref[2])
        store(lambda v: _rot_slab(v, tabs, IDX_DIM // ROT_FRACTION_DIV // 2))

    @pl.when(jnp.logical_not(rot_head | rot_idx))
    def _():
        store(lambda v: v)


def _tab_type(j):
    return jnp.where(j < 6, 0, jnp.where(j < 28, 1, 2))


MOD_SHIFT1, MOD_SCALE1, MOD_GATE1, MOD_SHIFT2, MOD_SCALE2, MOD_GATE2 = range(6)


def _mod_spec(mod, col, rows_per_mod, tm, grid_rank):
    tiles_per_mod = rows_per_mod // tm
    if mod.shape[1] == 1:
        blk, pick = (None, 1, D_MODEL), lambda i: (i // tiles_per_mod, 0, col)
    else:
        blk, pick = (None, tm, D_MODEL), lambda i: (i // tiles_per_mod, i % tiles_per_mod, col)
    return pl.BlockSpec(blk, (lambda i, j: pick(i)) if grid_rank == 2 else pick)


def _inproj(x, g_norm, mod, w_r, tabs, rows_per_mod, tm):
    m = x.shape[0]
    tt = tabs.shape[2]
    n_t = tt // tm
    return pl.pallas_call(
        _inproj_kernel,
        grid=(m // tm, PROJ_W // PROJ_TN),
        in_specs=[pl.BlockSpec((tm, D_MODEL), lambda i, j: (i, 0)),
                  pl.BlockSpec((1, D_MODEL), lambda i, j: (0, 0)),
                  _mod_spec(mod, MOD_SCALE1, rows_per_mod, tm, 2), _mod_spec(mod, MOD_SHIFT1, rows_per_mod, tm, 2),
                  pl.BlockSpec((D_MODEL, PROJ_TN), lambda i, j: (0, j)),
                  pl.BlockSpec((None, 3, tm, LANE), lambda i, j: (_tab_type(j), 0, i % n_t, 0))],
        out_specs=[pl.BlockSpec((tm, PROJ_TN), lambda i, j: (i, j)),
                   pl.BlockSpec((tm, PROJ_TN), lambda i, j: (i, j))],
        out_shape=[jax.ShapeDtypeStruct((m, PROJ_W), F32), jax.ShapeDtypeStruct((m, PROJ_W), BF16)],
        scratch_shapes=[pltpu.VMEM((tm, D_MODEL), BF16)],
        compiler_params=_cparams(("parallel", "arbitrary")),
        name="inproj",
    )(x, g_norm.reshape(1, D_MODEL), mod, mod, w_r, tabs)


def _rot_tables(pos):
    tt = pos.shape[0]
    posf = pos.astype(F32)

    def tab(rot, period, scale):
        half = rot // 2
        inv = ROPE_THETA ** (-jnp.arange(half, dtype=F32) * (2.0 / rot))
        ang = posf[:, None] * inv[None, :]
        cos, sin = jnp.cos(ang), jnp.sin(ang)
        zh = jnp.zeros((tt, half), F32)
        rest = period - rot
        a = jnp.concatenate([cos, cos, jnp.ones((tt, rest), F32)], axis=1) * scale
        b = jnp.concatenate([-sin, zh, jnp.zeros((tt, rest), F32)], axis=1) * scale
        c = jnp.concatenate([zh, sin, jnp.zeros((tt, rest), F32)], axis=1) * scale
        return a, b, c

    head = jnp.stack([jnp.tile(t, (1, LANE // HEAD_DIM)) for t in tab(HEAD_DIM // ROT_FRACTION_DIV, HEAD_DIM, 1.0)])
    idxq = jnp.stack([jnp.tile(t, (1, LANE // IDX_DIM))
                      for t in tab(IDX_DIM // ROT_FRACTION_DIV, IDX_DIM, IDX_DIM ** -0.5)])
    ka, kb, kc = tab(IDX_DIM // ROT_FRACTION_DIV, IDX_DIM, 1.0)
    rest = LANE - IDX_DIM
    lane = jnp.arange(rest)
    tail_a = jnp.where(lane < (L_GB - L_IW), IDX_HEADS ** -0.5, 1.0).astype(F32)
    ma = jnp.concatenate([ka, jnp.broadcast_to(tail_a, (tt, rest))], axis=1)
    mb = jnp.concatenate([kb, jnp.zeros((tt, rest), F32)], axis=1)
    mc = jnp.concatenate([kc, jnp.zeros((tt, rest), F32)], axis=1)
    misc = jnp.stack([ma, mb, mc])
    return jnp.stack([head, idxq, misc])


def _reorder_w_in(w_in):
    offs = {}
    o = 0
    for name, wdt in (("aq", 2048), ("ak", 512), ("av", 512), ("iq", 1024), ("ik", 64), ("iw", 16), ("gq", 1024),
                      ("gk", 1024), ("gv", 2048), ("gz", 2048), ("gb", 16), ("ga", 16), ("gl", 4096)):
        offs[name] = (o, wdt)
        o += wdt

    def seg(name):
        s, wdt = offs[name]
        return w_in[:, s:s + wdt]

    k = w_in.shape[0]
    misc = jnp.concatenate([seg("ik"), seg("iw"), seg("gb"), seg("ga"), jnp.zeros((k, LANE - 112), w_in.dtype)], axis=1)
    parts = [seg("aq"), seg("ak"), seg("av"), seg("iq"), seg("gq"), seg("gk"), seg("gv"), seg("gz"), seg("gl"), misc,
             jnp.zeros((k, PROJ_W - C_MISC - LANE), w_in.dtype)]
    return jnp.concatenate(parts, axis=1).astype(BF16)


def _order_key(score):
    bits = pltpu.bitcast(score, I32)
    return bits ^ ((bits >> 31) & 0x7FFFFFFF)


def _kth_largest_key(count_ge, n_rows, n_sel):
    def body(i, t):
        bit = lax.shift_left(jnp.int32(1), 31 - i)
        cand = t ^ bit
        return jnp.where(count_ge(cand) >= n_sel, cand, t)

    return lax.fori_loop(0, 32, body, jnp.full((n_rows, 1), INT_MIN, I32))


RADIX_BITS = 4


def _kth_largest_key_radix(keys, n_sel):
    n_rows = keys.shape[0]
    n_cand = (1 << RADIX_BITS) - 1
    digit = lax.broadcasted_iota(I32, (n_cand, n_rows, 1), 0) + 1
    t_u = jnp.zeros((n_rows, 1), I32)
    for step in range(32 // RADIX_BITS):
        shift = 32 - RADIX_BITS * (step + 1)
        cand = (t_u[None] | lax.shift_left(digit, shift)) ^ INT_MIN
        cnt = jnp.sum(jnp.where(keys[None] >= cand, 1.0, 0.0), axis=2, keepdims=True)
        keep = jnp.sum(jnp.where(cnt >= n_sel, 1, 0), axis=0)
        t_u = t_u | lax.shift_left(keep, shift)
    return t_u ^ INT_MIN


def _attn_prompt_kernel(q_ref, iq_ref, mq_ref, k_ref, v_ref, mk_ref, o_ref, key_scr, *, tq, kc, n_sel):
    qb = pl.program_id(1)
    q0 = qb * tq
    nkc = (q0 + tq + kc - 1) // kc
    qpos = q0 + lax.broadcasted_iota(I32, (tq, 1), 0)
    kiota = lax.broadcasted_iota(I32, (1, kc), 1)
    wq = mq_ref[...]

    def score_body(c, carry):
        ks = pl.multiple_of(c * kc, kc)
        kt = mk_ref[pl.ds(ks, kc), :][:, :IDX_DIM]
        acc = jnp.zeros((tq, kc), F32)
        for h in range(IDX_HEADS):
            s = lax.dot_general(iq_ref[:, h * IDX_DIM:(h + 1) * IDX_DIM], kt, NT_DIMS, preferred_element_type=F32)
            acc = acc + wq[:, L_IW + h:L_IW + h + 1] * jnp.maximum(s, 0.0)
        acc = jnp.where(ks + kiota <= qpos, acc, -jnp.inf)
        key_scr[c] = _order_key(acc)
        return carry

    lax.fori_loop(0, nkc, score_body, 0)

    def count(cmp, cand):
        def body(c, acc):
            m = jnp.where(cmp(key_scr[c], cand), 1.0, 0.0)
            for s in range(kc // LANE):
                acc = acc + m[:, s * LANE:(s + 1) * LANE]
            return acc

        acc = lax.fori_loop(0, nkc, body, jnp.zeros((tq, LANE), F32))
        return jnp.sum(acc, axis=1, keepdims=True)

    count_ge = functools.partial(count, lambda a, b: a >= b)
    thr = _kth_largest_key(count_ge, tq, float(n_sel))

    n_tie_ok = float(n_sel) - count(lambda a, b: a > b, thr)

    @pl.when(jnp.max(count_ge(thr)) > float(n_sel))
    def _():
        before = (lax.broadcasted_iota(I32, (kc, kc), 0) < lax.broadcasted_iota(I32, (kc, kc), 1))
        before = jnp.where(before, 1.0, 0.0).astype(BF16)

        def demote(c, seen):
            key = key_scr[c]
            eq = key == thr
            eqf = jnp.where(eq, 1.0, 0.0)
            rank = seen + jnp.dot(eqf.astype(BF16), before, preferred_element_type=F32)
            key_scr[c] = jnp.where(eq & (rank >= n_tie_ok), INT_MIN, key)
            return seen + jnp.sum(eqf, axis=1, keepdims=True)

        lax.fori_loop(0, nkc, demote, jnp.zeros((tq, 1), F32))

    group = N_HEADS // N_KV_HEADS
    scale = HEAD_DIM ** -0.5
    qss = []
    for n in range(N_KV_HEADS):
        qs = jnp.concatenate(
            [q_ref[:, (n * group + g) * HEAD_DIM:(n * group + g + 1) * HEAD_DIM] for g in range(group)], axis=0)
        qss.append((qs.astype(F32) * scale).astype(BF16))

    def attn_body(c, carry):
        ks = pl.multiple_of(c * kc, kc)
        sel = (key_scr[c] >= thr) & (ks + kiota <= qpos)
        bias = jnp.where(sel, 0.0, NEG_BF16).astype(BF16)[None]
        new = []
        for n in range(N_KV_HEADS):
            m_i, acc = carry[n]
            kn = k_ref[pl.ds(ks, kc), n * HEAD_DIM:(n + 1) * HEAD_DIM]
            vn = jnp.concatenate([v_ref[pl.ds(ks, kc), n * HEAD_DIM:(n + 1) * HEAD_DIM], ones_blk], axis=1)
            s = lax.dot_general(qss[n], kn, NT_DIMS, preferred_element_type=F32).astype(BF16)
            s = (s.reshape(group, tq, kc) + bias).reshape(group * tq, kc)
            m_new = jnp.maximum(m_i, jnp.max(s, axis=1, keepdims=True).astype(F32))
            alpha = jnp.exp(m_i - m_new)
            p = jnp.exp(s - m_new.astype(BF16))
            acc = alpha * acc + jnp.dot(p, vn, preferred_element_type=F32)
            new.append((m_new, acc))
        return tuple(new)

    ones_blk = jnp.ones((kc, HEAD_DIM), BF16)
    init = tuple((jnp.full((group * tq, 1), NEG_BF16, F32), jnp.zeros((group * tq, 2 * HEAD_DIM), F32))
                 for _ in range(N_KV_HEADS))
    fin = lax.fori_loop(0, nkc, attn_body, init)
    for n in range(N_KV_HEADS):
        _, acc_f = fin[n]
        out = acc_f[:, :HEAD_DIM] / acc_f[:, HEAD_DIM:HEAD_DIM + 1]
        for g in range(group):
            o_ref[:, (n * group + g) * HEAD_DIM:(n * group + g + 1) * HEAD_DIM] = out[g * tq:(g + 1) * tq].astype(BF16)


def _attn_prompt(p32, p16, bsz, t, tq=128, kc=512):
    nq = t // tq
    kc = min(kc, t)
    n_sel = min(TOPK_MAX, t // 4)
    kern = functools.partial(_attn_prompt_kernel, tq=tq, kc=kc, n_sel=n_sel)
    return pl.pallas_call(
        kern,
        grid=(bsz, nq),
        in_specs=[pl.BlockSpec((tq, N_HEADS * HEAD_DIM), lambda b, i: (b * nq + i, 0)),
                  pl.BlockSpec((tq, IDX_HEADS * IDX_DIM), lambda b, i: (b * nq + i, C_IQ // (IDX_HEADS * IDX_DIM))),
                  pl.BlockSpec((tq, LANE), lambda b, i: (b * nq + i, MISC_BLK)),
                  pl.BlockSpec((t, N_KV_HEADS * HEAD_DIM), lambda b, i: (b, C_AK // (N_KV_HEADS * HEAD_DIM))),
                  pl.BlockSpec((t, N_KV_HEADS * HEAD_DIM), lambda b, i: (b, C_AV // (N_KV_HEADS * HEAD_DIM))),
                  pl.BlockSpec((t, LANE), lambda b, i: (b, MISC_BLK))],
        out_specs=pl.BlockSpec((tq, N_HEADS * HEAD_DIM), lambda b, i: (b * nq + i, 0)),
        out_shape=jax.ShapeDtypeStruct((bsz * t, N_HEADS * HEAD_DIM), BF16),
        scratch_shapes=[pltpu.VMEM((t // kc, tq, kc), I32)],
        compiler_params=_cparams(("parallel", "arbitrary")),
        name="attn_prompt",
    )(p16, p16, p32, p16, p16, p16)


def _attn_sample_kernel(pt_ref, iq_ref, wc_ref, q_ref, kn_ref, vn_ref, ikn_ref, *rest, n_pages, n_sel, past):
    kidx_refs = rest[:n_pages]
    k_refs = rest[n_pages:2 * n_pages]
    v_refs = rest[2 * n_pages:3 * n_pages]
    o_ref, sc_scr, selx_scr, s_scr = rest[3 * n_pages:]
    del pt_ref
    rows, nkv = 8, N_KV_HEADS
    n_keys = (n_pages + 1) * PAGE_SIZE
    pw = PAGE_SIZE * nkv
    iq = iq_ref[...].astype(BF16)
    wc = wc_ref[...]

    def idx_score(kt, dims):
        s = lax.dot_general(iq, kt.astype(BF16), dims, preferred_element_type=F32)
        x = jnp.maximum(s, 0.0) * wc
        return jnp.sum(x.reshape(IDX_HEADS, rows, PAGE_SIZE), axis=0)

    nn_dims = (((1,), (0,)), ((), ()))
    for p in range(n_pages):
        sc_scr[:, p * PAGE_SIZE:(p + 1) * PAGE_SIZE] = idx_score(kidx_refs[p][...], nn_dims)
    sc_scr[:, n_pages * PAGE_SIZE:] = idx_score(ikn_ref[...], NT_DIMS)

    qpos = past + lax.broadcasted_iota(I32, (rows, 1), 0)
    kpos = lax.broadcasted_iota(I32, (1, n_keys), 1)
    causal = kpos <= qpos
    keys = _order_key(jnp.where(causal, sc_scr[...], -jnp.inf))

    thr = _kth_largest_key_radix(keys, float(n_sel))
    above = keys > thr
    tied = jnp.where(keys == thr, 1.0, 0.0)
    n_tie_ok = float(n_sel) - jnp.sum(jnp.where(above, 1.0, 0.0), axis=1, keepdims=True)
    before = (lax.broadcasted_iota(I32, (PAGE_SIZE, PAGE_SIZE), 0) < lax.broadcasted_iota(I32, (PAGE_SIZE, PAGE_SIZE), 1))
    before = jnp.where(before, 1.0, 0.0).astype(BF16)
    seen = jnp.zeros((rows, 1), F32)
    ranks = []
    for p in range(n_pages + 1):
        tp = tied[:, p * PAGE_SIZE:(p + 1) * PAGE_SIZE]
        ranks.append(seen + jnp.dot(tp.astype(BF16), before, preferred_element_type=F32))
        seen = seen + jnp.sum(tp, axis=1, keepdims=True)
    rank = jnp.concatenate(ranks, axis=1)
    sel = jnp.where((above | ((tied > 0.5) & (rank < n_tie_ok))) & causal, 1.0, 0.0).astype(BF16)
    expand = jnp.where((lax.broadcasted_iota(I32, (PAGE_SIZE, pw), 1) >> 2)
                       == lax.broadcasted_iota(I32, (PAGE_SIZE, pw), 0), 1.0, 0.0).astype(BF16)
    for p in range(n_pages + 1):
        selx_scr[:, p * pw:(p + 1) * pw] = jnp.dot(sel[:, p * PAGE_SIZE:(p + 1) * PAGE_SIZE], expand,
                                                   preferred_element_type=F32)
    n_q = N_HEADS * rows
    own_head = ((lax.broadcasted_iota(I32, (n_q, pw), 0) >> 5)
                == (lax.broadcasted_iota(I32, (n_q, pw), 1) & (nkv - 1))).reshape(N_HEADS, rows, pw)
    q = (q_ref[...] * (HEAD_DIM ** -0.5)).astype(BF16)
    m_parts = []
    for p in range(n_pages + 1):
        kb = (k_refs[p] if p < n_pages else kn_ref)[...].astype(BF16)
        s = lax.dot_general(q, kb, NT_DIMS, preferred_element_type=F32)
        ok = own_head & (selx_scr[:, p * pw:(p + 1) * pw] > 0.5)[None]
        s = jnp.where(ok, s.reshape(N_HEADS, rows, pw), NEG_BIG).reshape(n_q, pw)
        s_scr[:, p * pw:(p + 1) * pw] = s
        m_parts.append(jnp.max(s, axis=1, keepdims=True))
    m = m_parts[0]
    for mp in m_parts[1:]:
        m = jnp.maximum(m, mp)
    l_parts, o_parts = [], []
    for p in range(n_pages + 1):
        vb = (v_refs[p] if p < n_pages else vn_ref)[...].astype(BF16)
        pr = jnp.exp(s_scr[:, p * pw:(p + 1) * pw] - m)
        l_parts.append(jnp.sum(pr, axis=1, keepdims=True))
        o_parts.append(jnp.dot(pr.astype(BF16), vb, preferred_element_type=F32))
    l = l_parts[0]
    acc = o_parts[0]
    for lp, op in zip(l_parts[1:], o_parts[1:]):
        l = l + lp
        acc = acc + op
    o_ref[...] = acc / l


def _attn_sample(p32, cache_k, cache_v, cache_kidx, page_table, ts):
    db, n_pages = page_table.shape
    past = n_pages * PAGE_SIZE
    rows = 8
    group = N_HEADS // N_KV_HEADS
    ps = p32.reshape(db, ts, PROJ_W)
    pad_t = ((0, 0), (0, 0), (0, rows - ts), (0, 0))
    iq = ps[:, :, C_IQ:C_IQ + IDX_HEADS * IDX_DIM].reshape(db, ts, IDX_HEADS, IDX_DIM).transpose(0, 2, 1, 3)
    iq = jnp.pad(iq, pad_t).reshape(db, IDX_HEADS * rows, IDX_DIM)
    wc = ps[:, :, C_MISC + L_IW:C_MISC + L_IW + IDX_HEADS].transpose(0, 2, 1)
    wc = jnp.pad(wc, ((0, 0), (0, 0), (0, rows - ts))).reshape(db, IDX_HEADS * rows, 1)
    q = ps[:, :, :N_HEADS * HEAD_DIM].reshape(db, ts, N_HEADS, HEAD_DIM).transpose(0, 2, 1, 3)
    q = jnp.pad(q, pad_t).reshape(db, N_HEADS * rows, HEAD_DIM)
    pad_k = ((0, 0), (0, PAGE_SIZE - ts), (0, 0))
    pw = PAGE_SIZE * N_KV_HEADS
    kn = jnp.pad(ps[:, :, C_AK:C_AK + N_KV_HEADS * HEAD_DIM], pad_k).reshape(db, pw, HEAD_DIM)
    vn = jnp.pad(ps[:, :, C_AV:C_AV + N_KV_HEADS * HEAD_DIM], pad_k).reshape(db, pw, HEAD_DIM)
    ikn = jnp.pad(ps[:, :, C_MISC:C_MISC + IDX_DIM], pad_k)
    n_sel = min(TOPK_MAX, (past + ts) // 4)
    n_phys = cache_k.shape[1]
    ck = cache_k.reshape(1, n_phys, pw, HEAD_DIM)
    cv = cache_v.reshape(1, n_phys, pw, HEAD_DIM)

    def kv_page(p):
        return pl.BlockSpec((None, None, pw, HEAD_DIM), lambda d, pt, p=p: (0, pt[d * n_pages + p], 0, 0))

    ckt = jnp.swapaxes(cache_kidx, 2, 3)

    def idx_page(p):
        return pl.BlockSpec((None, None, IDX_DIM, PAGE_SIZE), lambda d, pt, p=p: (0, pt[d * n_pages + p], 0, 0))

    def per_d(r, w):
        return pl.BlockSpec((None, r, w), lambda d, pt: (d, 0, 0))

    in_specs = ([per_d(IDX_HEADS * rows, IDX_DIM), per_d(IDX_HEADS * rows, 1), per_d(N_HEADS * rows, HEAD_DIM),
                 per_d(pw, HEAD_DIM), per_d(pw, HEAD_DIM), per_d(PAGE_SIZE, IDX_DIM)]
                + [idx_page(p) for p in range(n_pages)]
                + [kv_page(p) for p in range(n_pages)]
                + [kv_page(p) for p in range(n_pages)])
    n_keys = (n_pages + 1) * PAGE_SIZE
    grid_spec = pltpu.PrefetchScalarGridSpec(
        num_scalar_prefetch=1, grid=(db,), in_specs=in_specs,
        out_specs=per_d(N_HEADS * rows, HEAD_DIM),
        scratch_shapes=[pltpu.VMEM((rows, n_keys), F32), pltpu.VMEM((rows, n_keys * N_KV_HEADS), F32),
                        pltpu.VMEM((N_HEADS * rows, n_keys * N_KV_HEADS), F32)])
    kern = functools.partial(_attn_sample_kernel, n_pages=n_pages, n_sel=n_sel, past=past)
    o = pl.pallas_call(
        kern, grid_spec=grid_spec,
        out_shape=jax.ShapeDtypeStruct((db, N_HEADS * rows, HEAD_DIM), F32),
        compiler_params=_cparams(("arbitrary",)),
        name="attn_sample",
    )(page_table.reshape(-1), iq, wc, q, kn, vn, ikn,
      *([ckt] * n_pages), *([ck] * n_pages), *([cv] * n_pages))
    o = o.reshape(db, N_HEADS, rows, HEAD_DIM)[:, :, :ts].transpose(0, 2, 1, 3)
    return o.reshape(db * ts, N_HEADS * HEAD_DIM).astype(BF16)


def _dot3s(ah, al, bh, bl, dims=None):
    if dims is None:
        d = lambda x, y: jnp.dot(x, y, preferred_element_type=F32)
    else:
        d = lambda x, y: lax.dot_general(x, y, dims, preferred_element_type=F32)
    return d(ah, bh) + d(al, bh) + d(ah, bl)


def _cumsum_rows(sel_bf16, g_rows):
    gb = jnp.broadcast_to(g_rows, (g_rows.shape[0], LANE))
    g_hi, g_lo = _split_bf16(gb)
    g_lo2 = (gb - g_hi.astype(F32) - g_lo.astype(F32)).astype(BF16)
    d = lambda y: jnp.dot(sel_bf16, y, preferred_element_type=F32)
    return d(g_hi) + d(g_lo) + d(g_lo2)


def _l2n(x):
    return x * lax.rsqrt(jnp.sum(x * x, axis=-1, keepdims=True) + NORM_EPS)


def _gate_params(misc, hp, live):
    x = misc + hp[1:2, :]
    softplus = jnp.maximum(x, 0.0) + jnp.log(1.0 + jnp.exp(-jnp.abs(x)))
    g_all = -jnp.exp(hp[0:1, :]) * softplus
    beta_all = _sigmoid(misc)
    if live is not None:
        g_all = jnp.where(live, g_all, 0.0)
        beta_all = jnp.where(live, beta_all, 0.0)
    return g_all, beta_all


def _gdn_kernel(q_ref, k_ref, v_ref, z_ref, m_ref, wq_ref, wk_ref, wv_ref, cq_ref, ck_ref, cv_ref, hp_ref, nw_ref,
                s0_ref, o_ref, sf_ref, s_scr, hq_scr, hk_scr, hv_scr, xc_scr, *, c, nh):
    g_idx = pl.program_id(1)
    tb = pl.program_id(2)
    n_tb = pl.num_programs(2)
    dk, dv = GDN_DK, GDN_DV

    @pl.when(tb == 0)
    def _():
        s_scr[...] = s0_ref[...]
        hq_scr[...] = cq_ref[...]
        hk_scr[...] = ck_ref[...]
        hv_scr[...] = cv_ref[...]

    def conv_silu(x_ref, w_ref, h_scr):
        width = x_ref.shape[1]
        x = x_ref[...]
        xc_scr[0:8, 0:width] = h_scr[...]
        xc_scr[8:8 + c, 0:width] = x
        acc = jnp.zeros((c, width), F32)
        for j in range(CONV_W):
            acc = acc + xc_scr[8 - (CONV_W - 1) + j:8 - (CONV_W - 1) + j + c, 0:width] * w_ref[j:j + 1, :]
        h_scr[...] = x[c - 8:c, :]
        return acc * _sigmoid(acc)

    qa = conv_silu(q_ref, wq_ref, hq_scr)
    ka = conv_silu(k_ref, wk_ref, hk_scr)
    va = conv_silu(v_ref, wv_ref, hv_scr)
    g_all, beta_all = _gate_params(m_ref[...], hp_ref[...], None)
    lane = lax.broadcasted_iota(I32, (1, LANE), 1)
    ri = lax.broadcasted_iota(I32, (c, c), 0)
    ci = lax.broadcasted_iota(I32, (c, c), 1)
    causal = ri >= ci
    strict = ri > ci
    tri = jnp.where(causal, 1.0, 0.0).astype(BF16)
    eye = jnp.where(ri == ci, 1.0, 0.0)

    def same_blk(log_b):
        return (ri >> log_b) == (ci >> log_b)

    heads = range(2 * nh)
    ks = [_l2n(ka[:, hh * dk:(hh + 1) * dk]) for hh in range(nh)]
    qss = [_l2n(qa[:, hh * dk:(hh + 1) * dk]) * (dk ** -0.5) for hh in range(nh)]
    kk_qk = [lax.dot_general(jnp.concatenate([ks[hh], qss[hh]], axis=0).astype(BF16), ks[hh].astype(BF16), NT_DIMS,
                             preferred_element_type=F32) for hh in range(nh)]
    g_cols = [jnp.sum(jnp.where(lane == L_GA + 2 * nh * g_idx + hl, g_all, 0.0), axis=1, keepdims=True)
              for hl in heads]
    betas = [jnp.sum(jnp.where(lane == L_GB + 2 * nh * g_idx + hl, beta_all, 0.0), axis=1, keepdims=True)
             for hl in heads]
    gcs = [_cumsum_rows(tri, g) for g in g_cols]
    decays = [jnp.exp(jnp.where(causal, gc - gc.T, -jnp.inf)) for gc in gcs]
    a_mats = [jnp.where(strict, kk_qk[hl // 2][0:c] * betas[hl] * decays[hl], 0.0) for hl in heads]
    nms = [jnp.where(same_blk(3), -a, 0.0) for a in a_mats]
    pms = [eye + n for n in nms]
    for _ in range(2):
        nms = [_dot3(n, n) for n in nms]
        pms = [p + _dot3(p, n) for p, n in zip(pms, nms)]
    for lb in range(3, (c - 1).bit_length()):
        join = same_blk(lb + 1) & jnp.logical_not(same_blk(lb))
        t1s = [_dot3(jnp.where(join, a, 0.0), p) for a, p in zip(a_mats, pms)]
        pms = [p - _dot3(p, t1) for p, t1 in zip(pms, t1s)]
    us = [_dot3(pms[hl], va[:, hl * dv:(hl + 1) * dv] * betas[hl]) for hl in heads]
    ws = [_dot3(pms[hl], ks[hl // 2] * betas[hl] * jnp.exp(gcs[hl])) for hl in heads]
    qks = [jnp.where(causal, kk_qk[hl // 2][c:2 * c] * decays[hl], 0.0) for hl in heads]
    s_in = [s_scr[hl] for hl in heads]
    v_news = [us[hl] - _dot3(ws[hl], s_in[hl]) for hl in heads]
    os_ = [jnp.dot((qss[hl // 2] * jnp.exp(gcs[hl])).astype(BF16), s_in[hl].astype(BF16), preferred_element_type=F32)
           + jnp.dot(qks[hl].astype(BF16), v_news[hl].astype(BF16), preferred_element_type=F32) for hl in heads]
    for hl in heads:
        g_last = gcs[hl][c - 1:c, :]
        kdec = ks[hl // 2] * jnp.exp(g_last - gcs[hl])
        s_scr[hl] = s_in[hl] * jnp.exp(g_last) + _dot3(kdec.T, v_news[hl])
    for hl in heads:
        o = os_[hl]
        z = z_ref[:, hl * dv:(hl + 1) * dv]
        on = o * lax.rsqrt(jnp.mean(o * o, axis=-1, keepdims=True) + NORM_EPS) * nw_ref[...]
        o_ref[:, hl * dv:(hl + 1) * dv] = (on * (z * _sigmoid(z))).astype(BF16)

    @pl.when(tb == n_tb - 1)
    def _():
        sf_ref[...] = s_scr[...]


def _gdn_hp(a_log, dt_bias):
    hp = jnp.zeros((8, LANE), F32)
    return hp.at[0, L_GA:L_GA + GDN_V_HEADS].set(a_log.astype(F32)).at[1, L_GA:L_GA + GDN_V_HEADS].set(
        dt_bias.astype(F32))


def _gdn(p32, conv0, s0, conv_w, a_log, dt_bias, norm_w, bsz, t, c=128, nh=4):
    n_tb = t // c
    n_g = GDN_QK_HEADS // nh
    qw, vw = nh * GDN_DK, 2 * nh * GDN_DV
    row = lambda b, g, i: b * n_tb + i
    in_specs = [
        pl.BlockSpec((c, qw), lambda b, g, i: (row(b, g, i), C_GQ // qw + g)),
        pl.BlockSpec((c, qw), lambda b, g, i: (row(b, g, i), C_GK // qw + g)),
        pl.BlockSpec((c, vw), lambda b, g, i: (row(b, g, i), C_GV // vw + g)),
        pl.BlockSpec((c, vw), lambda b, g, i: (row(b, g, i), C_GZ // vw + g)),
        pl.BlockSpec((c, LANE), lambda b, g, i: (row(b, g, i), MISC_BLK)),
        pl.BlockSpec((CONV_W, qw), lambda b, g, i: (0, g)),
        pl.BlockSpec((CONV_W, qw), lambda b, g, i: (0, n_g + g)),
        pl.BlockSpec((CONV_W, vw), lambda b, g, i: (0, n_g + g)),
        pl.BlockSpec((None, 8, qw), lambda b, g, i: (b, 0, g)),
        pl.BlockSpec((None, 8, qw), lambda b, g, i: (b, 0, n_g + g)),
        pl.BlockSpec((None, 8, vw), lambda b, g, i: (b, 0, n_g + g)),
        pl.BlockSpec((8, LANE), lambda b, g, i: (0, 0)),
        pl.BlockSpec((1, GDN_DV), lambda b, g, i: (0, 0)),
        pl.BlockSpec((None, 2 * nh, GDN_DK, GDN_DV), lambda b, g, i: (b, g, 0, 0)),
    ]
    out_specs = [pl.BlockSpec((c, vw), lambda b, g, i: (row(b, g, i), g)),
                 pl.BlockSpec((None, 2 * nh, GDN_DK, GDN_DV), lambda b, g, i: (b, g, 0, 0))]
    return pl.pallas_call(
        functools.partial(_gdn_kernel, c=c, nh=nh),
        grid=(bsz, n_g, n_tb),
        in_specs=in_specs, out_specs=out_specs,
        out_shape=[jax.ShapeDtypeStruct((bsz * t, GDN_V_HEADS * GDN_DV), BF16),
                   jax.ShapeDtypeStruct((bsz, GDN_V_HEADS, GDN_DK, GDN_DV), F32)],
        scratch_shapes=[pltpu.VMEM((2 * nh, GDN_DK, GDN_DV), F32), pltpu.VMEM((8, qw), F32), pltpu.VMEM((8, qw), F32),
                        pltpu.VMEM((8, vw), F32), pltpu.VMEM((c + 8, vw), F32)],
        compiler_params=_cparams(("parallel", "parallel", "arbitrary")),
        name="gdn",
    )(p32, p32, p32, p32, p32, conv_w, conv_w, conv_w, conv0, conv0, conv0, _gdn_hp(a_log, dt_bias),
      norm_w.reshape(1, GDN_DV), s0)


def _gdn_sample_kernel(x_ref, z_ref, m_ref, w_ref, hp_ref, nw_ref, s0_ref, o_ref, sf_ref, *, ts):
    rows, nvh, dk, dv = 8, GDN_V_HEADS, GDN_DK, GDN_DV
    qd = GDN_QK_HEADS * dk
    x8 = x_ref[...]
    w = w_ref[...]
    acc = x8 * w[0:1, :]
    for j in range(1, CONV_W):
        acc = acc + pltpu.roll(x8, rows - j, 0) * w[j:j + 1, :]
    act = acc * _sigmoid(acc)
    live = lax.broadcasted_iota(I32, (rows, 1), 0) < ts
    g_all, beta_all = _gate_params(m_ref[...], hp_ref[...], live)
    qn = [_l2n(act[:, g * dk:(g + 1) * dk]) * (dk ** -0.5) for g in range(GDN_QK_HEADS)]
    kn = [_l2n(act[:, qd + g * dk:qd + (g + 1) * dk]) for g in range(GDN_QK_HEADS)]
    rep = nvh // GDN_QK_HEADS
    cat = lambda parts: jnp.concatenate(parts, axis=0)
    k = cat([kn[h // rep] for h in range(nvh)])
    qs = cat([qn[h // rep] for h in range(nvh)])
    v = cat([act[:, 2 * qd + h * dv:2 * qd + (h + 1) * dv] for h in range(nvh)])
    z = cat([z_ref[:, h * dv:(h + 1) * dv] for h in range(nvh)])
    beta = cat([beta_all[:, L_GB + h:L_GB + h + 1] for h in range(nvh)])
    g_col = cat([g_all[:, L_GA + h:L_GA + h + 1] for h in range(nvh)])
    n = nvh * rows
    ri = lax.broadcasted_iota(I32, (n, n), 0)
    ci = lax.broadcasted_iota(I32, (n, n), 1)
    same = (ri >> 3) == (ci >> 3)
    causal = same & (ri >= ci)
    strict = same & (ri > ci)
    eye = jnp.where(ri == ci, 1.0, 0.0)
    gc = _cumsum_rows(jnp.where(causal, 1.0, 0.0).astype(BF16), g_col)
    gl = _cumsum_rows(jnp.where(same, 1.0, 0.0).astype(BF16), g_col)
    decay = jnp.exp(jnp.where(causal, gc - gc.T, -jnp.inf))
    kb = k * beta
    a_mat = jnp.where(strict, _dot3(kb, k, NT_DIMS) * decay, 0.0)
    nm = -a_mat
    pm = eye + nm
    for _ in range(2):
        nm = _dot3(nm, nm)
        pm = pm + _dot3(pm, nm)
    u = _dot3(pm, v * beta)
    w_rows = _dot3(pm, kb * jnp.exp(gc))
    qk = jnp.where(causal, _dot3(qs, k, NT_DIMS) * decay, 0.0)
    qg = qs * jnp.exp(gc)
    kdt = (k * jnp.exp(gl - gc)).T
    s_in = [s0_ref[h] for h in range(nvh)]
    v_new, o1 = [], []
    for h in range(nvh):
        sl = slice(h * rows, (h + 1) * rows)
        r = _dot3(cat([w_rows[sl], qg[sl]]), s_in[h])
        v_new.append(u[sl] - r[0:rows])
        o1.append(r[rows:2 * rows])
    v_new = cat(v_new)
    o = cat(o1) + _dot3(qk, v_new)
    vh, vl = _split_bf16(v_new)
    lane = lax.broadcasted_iota(I32, (1, n), 1)
    for h in range(nvh):
        kh, kl = _split_bf16(jnp.where((lane >> 3) == h, kdt, 0.0))
        sf_ref[h] = s_in[h] * jnp.exp(gl[h * rows:h * rows + 1, :]) + _dot3s(kh, kl, vh, vl)
    on = o * lax.rsqrt(jnp.mean(o * o, axis=-1, keepdims=True) + NORM_EPS) * nw_ref[...]
    o_ref[...] = on * (z * _sigmoid(z))


def _gdn_sample(p32, state_conv, s0, conv_w, a_log, dt_bias, norm_w, db, ts):
    rows = 8
    conv_dim = conv_w.shape[-1]
    ps = p32.reshape(db, ts, PROJ_W)
    x8 = jnp.concatenate([state_conv, ps[:, :, C_GQ:C_GQ + conv_dim],
                          jnp.zeros((db, rows - ts - (CONV_W - 1), conv_dim), F32)], axis=1)
    pad_t = ((0, 0), (0, rows - ts), (0, 0))
    z8 = jnp.pad(ps[:, :, C_GZ:C_GZ + GDN_V_HEADS * GDN_DV], pad_t)
    m8 = jnp.pad(ps[:, :, C_MISC:C_MISC + LANE], pad_t)
    n = GDN_V_HEADS * rows
    per_d = lambda w: pl.BlockSpec((None, rows, w), lambda d: (d, 0, 0))
    state = pl.BlockSpec((None, GDN_V_HEADS, GDN_DK, GDN_DV), lambda d: (d, 0, 0, 0))
    o, sf = pl.pallas_call(
        functools.partial(_gdn_sample_kernel, ts=ts),
        grid=(db,),
        in_specs=[per_d(conv_dim), per_d(GDN_V_HEADS * GDN_DV), per_d(LANE),
                  pl.BlockSpec((CONV_W, conv_dim), lambda d: (0, 0)),
                  pl.BlockSpec((8, LANE), lambda d: (0, 0)),
                  pl.BlockSpec((1, GDN_DV), lambda d: (0, 0)), state],
        out_specs=[pl.BlockSpec((None, n, GDN_DV), lambda d: (d, 0, 0)), state],
        out_shape=[jax.ShapeDtypeStruct((db, n, GDN_DV), F32),
                   jax.ShapeDtypeStruct((db, GDN_V_HEADS, GDN_DK, GDN_DV), F32)],
        compiler_params=_cparams(("parallel",)),
        name="gdn_sample",
    )(x8, z8, m8, conv_w, _gdn_hp(a_log, dt_bias), norm_w.reshape(1, GDN_DV), s0)
    o = o.reshape(db, GDN_V_HEADS, rows, GDN_DV)[:, :, :ts].transpose(0, 2, 1, 3)
    return o.reshape(db * ts, GDN_V_HEADS * GDN_DV).astype(BF16), sf


def _merge_kernel(a_ref, g_ref, wa_ref, wg_ref, ga_ref, gg_ref, o_ref):
    pa = jnp.dot(a_ref[...], wa_ref[...], preferred_element_type=F32)
    pg = jnp.dot(g_ref[...], wg_ref[...], preferred_element_type=F32)
    o_ref[...] = (_sigmoid(ga_ref[...]) * pa + _sigmoid(gg_ref[...]) * pg).astype(BF16)


def _merge(attn_o, gdn_o, w_oa, w_og, p32, tm):
    m = attn_o.shape[0]
    tn = 512
    return pl.pallas_call(
        _merge_kernel,
        grid=(m // tm, D_MODEL // tn),
        in_specs=[pl.BlockSpec((tm, D_MODEL), lambda i, j: (i, 0)),
                  pl.BlockSpec((tm, D_MODEL), lambda i, j: (i, 0)),
                  pl.BlockSpec((D_MODEL, tn), lambda i, j: (0, j)),
                  pl.BlockSpec((D_MODEL, tn), lambda i, j: (0, j)),
                  pl.BlockSpec((tm, tn), lambda i, j: (i, C_GL // tn + j)),
                  pl.BlockSpec((tm, tn), lambda i, j: (i, (C_GL + D_MODEL) // tn + j))],
        out_specs=pl.BlockSpec((tm, tn), lambda i, j: (i, j)),
        out_shape=jax.ShapeDtypeStruct((m, D_MODEL), BF16),
        compiler_params=_cparams(("parallel", "arbitrary")),
        name="merge",
    )(attn_o, gdn_o, w_oa, w_og, p32, p32)


def _outproj_kernel(mg_ref, w_ref, x_ref, g1_ref, sc_ref, sh_ref, gn_ref, wr_ref, br_ref,
                    x1_ref, h2_ref, ti_ref, tg_ref):
    x1 = x_ref[...] + g1_ref[...] * jnp.dot(mg_ref[...], w_ref[...], preferred_element_type=F32)
    x1_ref[...] = x1
    ms = jnp.mean(x1 * x1, axis=-1, keepdims=True)
    h2 = x1 * lax.rsqrt(ms + NORM_EPS) * gn_ref[...] * (1.0 + sc_ref[...]) + sh_ref[...]
    h2_ref[...] = h2
    logits = _dot3(h2, wr_ref[...]) + br_ref[...]
    tm = logits.shape[0]
    lane = lax.broadcasted_iota(I32, (tm, LANE), 1)
    lanef = lane.astype(F32)
    l = jnp.where(lane < N_EXPERTS, logits, -jnp.inf)
    vals, idxs = [], []
    for _ in range(TOP_K):
        mx = jnp.max(l, axis=1, keepdims=True)
        ix = jnp.min(jnp.where(l == mx, lanef, float(LANE)), axis=1, keepdims=True)
        vals.append(mx)
        idxs.append(ix)
        l = jnp.where(lanef == ix, -jnp.inf, l)
    es = [jnp.exp(v - vals[0]) for v in vals]
    den = es[0]
    for e in es[1:]:
        den = den + e
    ti = jnp.zeros((tm, LANE), F32)
    tg = jnp.zeros((tm, LANE), F32)
    for kk in range(TOP_K):
        ti = jnp.where(lane == kk, idxs[kk], ti)
        tg = jnp.where(lane == kk, es[kk] / den, tg)
    ti_ref[...] = ti.astype(I32)
    tg_ref[...] = tg


def _outproj(merged, w_out, x, mod, g_norm2, w_router, b_router, rows_per_mod, tm):
    m = x.shape[0]
    mod_specs = [_mod_spec(mod, c, rows_per_mod, tm, 1) for c in (MOD_GATE1, MOD_SCALE2, MOD_SHIFT2)]
    wr = jnp.pad(w_router.astype(F32), ((0, 0), (0, LANE - N_EXPERTS)))
    br = jnp.pad(b_router.astype(F32), (0, LANE - N_EXPERTS)).reshape(1, LANE)
    row = pl.BlockSpec((tm, D_MODEL), lambda i: (i, 0))
    small = pl.BlockSpec((tm, LANE), lambda i: (i, 0))
    return pl.pallas_call(
        _outproj_kernel,
        grid=(m // tm,),
        in_specs=[row, pl.BlockSpec((D_MODEL, D_MODEL), lambda i: (0, 0)), row, *mod_specs,
                  pl.BlockSpec((1, D_MODEL), lambda i: (0, 0)),
                  pl.BlockSpec((D_MODEL, LANE), lambda i: (0, 0)),
                  pl.BlockSpec((1, LANE), lambda i: (0, 0))],
        out_specs=[row, row, small, small],
        out_shape=[jax.ShapeDtypeStruct((m, D_MODEL), F32), jax.ShapeDtypeStruct((m, D_MODEL), F32),
                   jax.ShapeDtypeStruct((m, LANE), I32), jax.ShapeDtypeStruct((m, LANE), F32)],
        compiler_params=_cparams(("parallel",)),
        name="outproj",
    )(merged, w_out, x, mod, mod, mod, g_norm2.reshape(1, D_MODEL), wr, br)


GATHER_ROWS = 256


def _row_copy(src_ref, o_ref, sem, src_row, dst_row):
    return pltpu.make_async_copy(src_ref.at[pl.ds(src_row, 1)], o_ref.at[pl.ds(dst_row, 1)], sem)


def _gather_kernel(live_ref, idx_ref, src_ref, o_ref, *scratch):
    sem = scratch[-1]
    dst = scratch[0] if len(scratch) == 2 else o_ref
    n = o_ref.shape[0]
    live = live_ref[pl.program_id(0)] != 0

    @pl.when(live)
    def _():
        def issue(r, carry):
            _row_copy(src_ref, dst, sem, idx_ref[0, r], r).start()
            return carry

        lax.fori_loop(0, n, issue, 0, unroll=8)
        pltpu.make_async_copy(src_ref.at[pl.ds(0, n)], dst, sem).wait()
        if dst is not o_ref:
            o_ref[...] = dst[...].astype(o_ref.dtype)

    @pl.when(jnp.logical_not(live))
    def _():
        o_ref[...] = jnp.zeros_like(o_ref)


def _gather_rows(src, idx, live, out_dtype):
    n_valid, w = idx.shape[0], src.shape[1]
    n_steps = -(-n_valid // GATHER_ROWS)
    n = n_steps * GATHER_ROWS
    idx = jnp.pad(idx, (0, n - n_valid))
    staging = [] if out_dtype == src.dtype else [pltpu.VMEM((GATHER_ROWS, w), src.dtype)]
    grid_spec = pltpu.PrefetchScalarGridSpec(
        num_scalar_prefetch=1, grid=(n_steps,),
        in_specs=[pl.BlockSpec((None, 1, GATHER_ROWS), lambda i, lv: (i, 0, 0), memory_space=pltpu.SMEM),
                  pl.BlockSpec(memory_space=pl.ANY)],
        out_specs=pl.BlockSpec((GATHER_ROWS, w), lambda i, lv: (i, 0)),
        scratch_shapes=staging + [pltpu.SemaphoreType.DMA(())])
    out = pl.pallas_call(
        _gather_kernel, grid_spec=grid_spec,
        out_shape=jax.ShapeDtypeStruct((n, w), out_dtype),
        compiler_params=pltpu.CompilerParams(dimension_semantics=("arbitrary",), vmem_limit_bytes=VMEM_LIMIT,
                                             disable_bounds_checks=True),
        name="gather_rows",
    )(live, idx.reshape(n_steps, 1, GATHER_ROWS), src)
    return out if n == n_valid else out[:n_valid]


MOE_GROUP = 1536
MOE_SUB = 256


def _moe_kernel(ge_ref, ns_ref, x_ref, wg_ref, wu_ref, wd_ref, bg_ref, bu_ref, bd_ref, o_ref, h_scr, *, n_f, tf):
    g_id = pl.program_id(0)
    j = pl.program_id(1)
    ns = ns_ref[g_id]
    del ge_ref
    n_rows = o_ref.shape[0]

    for k in range(1, n_rows // MOE_SUB + 1):
        rows = k * MOE_SUB

        @pl.when((ns == k) & (j < n_f))
        def _(rows=rows):
            x = x_ref[0:rows, :]
            g = jnp.dot(x, wg_ref[...].astype(BF16), preferred_element_type=F32) + bg_ref[...]
            u = jnp.dot(x, wu_ref[...].astype(BF16), preferred_element_type=F32) + bu_ref[...]
            gate = jnp.minimum(g, SWIGLU_LIMIT)
            up = jnp.clip(u, -SWIGLU_LIMIT, SWIGLU_LIMIT)
            h_scr[j, 0:rows, :] = ((up + 1.0) * gate * _sigmoid(SWIGLU_ALPHA * gate)).astype(BF16)

        @pl.when((ns == k) & (j >= n_f))
        def _(rows=rows):
            acc = jnp.dot(h_scr[0, 0:rows, :], wd_ref[0:tf, :].astype(BF16), preferred_element_type=F32)
            for kk in range(1, n_f):
                acc = acc + jnp.dot(h_scr[kk, 0:rows, :], wd_ref[kk * tf:(kk + 1) * tf, :].astype(BF16),
                                    preferred_element_type=F32)
            o_ref[0:rows, :] = acc + bd_ref[...]
            if rows < n_rows:
                o_ref[rows:, :] = jnp.zeros((n_rows - rows, o_ref.shape[1]), F32)

    @pl.when((ns == 0) & (j >= n_f))
    def _():
        o_ref[...] = jnp.zeros_like(o_ref)


def _moe_ffn(xs, grp_e, n_sub, w_gu, b_gu, w_dn, b_dn, tf):
    n_rows = xs.shape[0]
    n_groups = n_rows // MOE_GROUP
    n_f = D_FF // tf
    tn = tf
    n_o = D_MODEL // tn
    bgu = b_gu.reshape(N_EXPERTS, 1, 2 * D_FF)
    bdn = b_dn.reshape(N_EXPERTS, 1, D_MODEL)

    def jf(g, j, ns):
        return jnp.where(ns[g] > 0, jnp.minimum(j, n_f - 1), 0)

    def jo(g, j, ns):
        return jnp.where(ns[g] > 0, jnp.maximum(j - n_f, 0), 0)

    grid_spec = pltpu.PrefetchScalarGridSpec(
        num_scalar_prefetch=2, grid=(n_groups, n_f + n_o),
        in_specs=[pl.BlockSpec((MOE_GROUP, D_MODEL), lambda g, j, ge, ns: (jnp.where(ns[g] > 0, g, 0), 0)),
                  pl.BlockSpec((None, None, D_MODEL, tf), lambda g, j, ge, ns: (0, ge[g], 0, jf(g, j, ns))),
                  pl.BlockSpec((None, None, D_MODEL, tf), lambda g, j, ge, ns: (0, ge[g], 0, n_f + jf(g, j, ns))),
                  pl.BlockSpec((None, None, D_FF, tn), lambda g, j, ge, ns: (0, ge[g], 0, jo(g, j, ns))),
                  pl.BlockSpec((None, 1, tf), lambda g, j, ge, ns: (ge[g], 0, jf(g, j, ns))),
                  pl.BlockSpec((None, 1, tf), lambda g, j, ge, ns: (ge[g], 0, n_f + jf(g, j, ns))),
                  pl.BlockSpec((None, 1, tn), lambda g, j, ge, ns: (ge[g], 0, jo(g, j, ns)))],
        out_specs=pl.BlockSpec((MOE_GROUP, tn), lambda g, j, ge, ns: (g, jnp.maximum(j - n_f, 0))),
        scratch_shapes=[pltpu.VMEM((n_f, MOE_GROUP, tf), BF16)])
    return pl.pallas_call(
        functools.partial(_moe_kernel, n_f=n_f, tf=tf), grid_spec=grid_spec,
        out_shape=jax.ShapeDtypeStruct((n_rows, D_MODEL), F32),
        compiler_params=pltpu.CompilerParams(dimension_semantics=("arbitrary", "arbitrary"),
                                             vmem_limit_bytes=MOE_VMEM_LIMIT),
        name="moe_ffn",
    )(grp_e, n_sub, xs, w_gu, w_gu, w_dn, bgu, bgu, bdn)


def _final_kernel(x1_ref, yg_ref, tg_ref, g2_ref, gf_ref, o_ref):
    tg = tg_ref[...]
    ffn = jnp.zeros_like(x1_ref)
    for kk in range(TOP_K):
        ffn = ffn + tg[:, kk:kk + 1] * yg_ref[kk]
    x2 = x1_ref[...] + g2_ref[...] * ffn
    ms = jnp.mean(x2 * x2, axis=-1, keepdims=True)
    o_ref[...] = x2 * lax.rsqrt(ms + NORM_EPS) * gf_ref[...]


def _final(x1, yg, row0, tg, mod, g_final, rows_per_mod, tm):
    m = x1.shape[0]
    blk0 = row0 // tm
    mod_spec = _mod_spec(mod, MOD_GATE2, rows_per_mod, tm, 1)
    row = pl.BlockSpec((tm, D_MODEL), lambda i: (i, 0))
    return pl.pallas_call(
        _final_kernel,
        grid=(m // tm,),
        in_specs=[row, pl.BlockSpec((TOP_K, tm, D_MODEL), lambda i: (0, blk0 + i, 0)),
                  pl.BlockSpec((tm, LANE), lambda i: (i, 0)), mod_spec,
                  pl.BlockSpec((1, D_MODEL), lambda i: (0, 0))],
        out_specs=row,
        out_shape=jax.ShapeDtypeStruct((m, D_MODEL), F32),
        compiler_params=_cparams(("parallel",)),
        name="final",
    )(x1, yg, tg, mod, g_final.reshape(1, D_MODEL))


def _route(topi):
    tt = topi.shape[0]
    gs, sub = MOE_GROUP, MOE_SUB
    sel = jnp.sum((topi[:, :, None] == jnp.arange(N_EXPERTS, dtype=I32)[None, None, :]).astype(I32), axis=1)
    counts = jnp.sum(sel, axis=0)
    grp_per_e = (counts + gs - 1) // gs
    grp_end = jnp.cumsum(grp_per_e)
    grp_start = grp_end - grp_per_e
    rank = jnp.cumsum(sel, axis=0) - sel
    dest = jnp.take_along_axis((grp_start * gs)[None, :] + rank, topi, axis=1)
    n_groups = -(-(tt * TOP_K + N_EXPERTS * (gs - 1)) // gs)
    n_rows = n_groups * gs
    tok = jnp.repeat(jnp.arange(tt, dtype=I32), TOP_K)
    row_tok = (jnp.arange(n_rows, dtype=I32) % tt).at[dest.reshape(-1)].set(tok)
    g_ids = jnp.arange(n_groups, dtype=I32)
    grp_e = jnp.minimum(jnp.searchsorted(grp_end, g_ids, side="right"), N_EXPERTS - 1).astype(I32)
    rows_in = jnp.clip(counts[grp_e] - (g_ids - grp_start[grp_e]) * gs, 0, gs)
    rows_in = jnp.where(g_ids < grp_end[-1], rows_in, 0)
    n_sub = ((rows_in + sub - 1) // sub).astype(I32)
    per = gs // sub
    live = (jnp.arange(n_groups * per, dtype=I32) % per < jnp.repeat(n_sub, per)).astype(I32)
    return dest, row_tok, grp_e, n_sub, live


def kernel(x_prompt, x_sample, c_prompt, c_sample, cache_k, cache_v, cache_kidx, state_gdn, state_conv, page_table,
           w_ada, b_ada, g_norm1, g_norm2, g_final, w_in, gdn_conv_w, gdn_a_log, gdn_dt_bias, gdn_norm_w,
           w_o_attn, w_o_gdn, w_out, w_router, b_router, w_gu, b_gu, w_dn, b_dn):
    bsz, t, d = x_prompt.shape
    db, ts, _ = x_sample.shape
    depth = w_ada.shape[0]
    assert depth == 1 and d == D_MODEL
    past = page_table.shape[1] * PAGE_SIZE
    conv_dim = gdn_conv_w.shape[-1]

    n_c = bsz + db
    n_cp = -(-n_c // 8) * 8
    c_all = jnp.pad(jnp.concatenate([c_prompt, c_sample], axis=0), ((0, n_cp - n_c), (0, 0)))
    mod = _ada(c_all, w_ada, b_ada[0])
    mods = {"p": mod[:bsz].reshape(bsz, 1, 6 * d),
            "s": jnp.repeat(mod[bsz:n_c], ts, axis=0).reshape(1, db * ts, 6 * d)}

    w_r = _reorder_w_in(w_in[0])
    w_oa = w_o_attn[0].astype(BF16)
    w_og = w_o_gdn[0].astype(BF16)
    w_o = w_out[0].astype(BF16)

    xp = x_prompt.reshape(bsz * t, d)
    xs = x_sample.reshape(db * ts, d)
    tm_p = min(512, t)
    tm_s = db * ts

    tabs_p = _rot_tables(jnp.arange(t, dtype=I32))
    tabs_s = _rot_tables(jnp.tile(past + jnp.arange(ts, dtype=I32), db))
    p32_p, p16_p = _inproj(xp, g_norm1[0], mods["p"], w_r, tabs_p, t, tm_p)
    p32_s, _ = _inproj(xs, g_norm1[0], mods["s"], w_r, tabs_s, db * ts, tm_s)

    attn_p = _attn_prompt(p32_p, p16_p, bsz, t)
    attn_s = _attn_sample(p32_s, cache_k, cache_v, cache_kidx, page_table, ts)

    conv0_p = jnp.zeros((bsz, 8, conv_dim), F32)
    s0_p = jnp.zeros((bsz, GDN_V_HEADS, GDN_DK, GDN_DV), F32)
    gdn_p, sfin_p = _gdn(p32_p, conv0_p, s0_p, gdn_conv_w[0], gdn_a_log[0], gdn_dt_bias[0], gdn_norm_w[0], bsz, t)
    gdn_s, sfin_s = _gdn_sample(p32_s, state_conv[0], state_gdn[0], gdn_conv_w[0], gdn_a_log[0], gdn_dt_bias[0],
                                gdn_norm_w[0], db, ts)

    outs = {}
    for name, attn_o, gdn_o, p32, x, rows_per_mod, tm in (("p", attn_p, gdn_p, p32_p, xp, t, min(256, t)),
                                                          ("s", attn_s, gdn_s, p32_s, xs, db * ts, min(256, db * ts))):
        merged = _merge(attn_o, gdn_o, w_oa, w_og, p32, min(512, rows_per_mod))
        outs[name] = _outproj(merged, w_o, x, mods[name], g_norm2[0], w_router[0], b_router[0], rows_per_mod, tm)

    h2 = jnp.concatenate([outs["p"][1], outs["s"][1]], axis=0)
    topi = jnp.concatenate([outs["p"][2], outs["s"][2]], axis=0)[:, :TOP_K]
    dest, row_tok, grp_e, n_sub, live = _route(topi)
    xs = _gather_rows(h2, row_tok, live, BF16)
    yb = _moe_ffn(xs, grp_e, n_sub, w_gu, b_gu[0], w_dn, b_dn[0], tf=256)
    live_y = jnp.ones((-(-dest.size // GATHER_ROWS),), I32)
    yg = _gather_rows(yb, dest.T.reshape(-1), live_y, F32).reshape(TOP_K, h2.shape[0], d)

    n_p = bsz * t
    y_p = _final(outs["p"][0], yg, 0, outs["p"][3], mods["p"], g_final, t, min(256, t))
    y_s = _final(outs["s"][0], yg, n_p, outs["s"][3], mods["s"], g_final, db * ts, min(256, db * ts))

    kv = N_KV_HEADS * HEAD_DIM
    new_k_p = p32_p[:, C_AK:C_AK + kv].reshape(1, bsz, t, N_KV_HEADS, HEAD_DIM)
    new_v_p = p32_p[:, C_AV:C_AV + kv].reshape(1, bsz, t, N_KV_HEADS, HEAD_DIM)
    new_ki_p = p32_p[:, C_MISC:C_MISC + IDX_DIM].reshape(1, bsz, t, IDX_DIM)
    new_k_s = p32_s[:, C_AK:C_AK + kv].reshape(1, db, ts, N_KV_HEADS, HEAD_DIM)
    new_v_s = p32_s[:, C_AV:C_AV + kv].reshape(1, db, ts, N_KV_HEADS, HEAD_DIM)
    new_ki_s = p32_s[:, C_MISC:C_MISC + IDX_DIM].reshape(1, db, ts, IDX_DIM)
    conv_p = p32_p.reshape(bsz, t, PROJ_W)[:, t - (CONV_W - 1):, C_GQ:C_GQ + conv_dim][None]
    pre_s = p32_s[:, C_GQ:C_GQ + conv_dim].reshape(db, ts, conv_dim)
    conv_s = jnp.concatenate([state_conv[0], pre_s], axis=1)[:, ts:][None]
    return (y_p.reshape(bsz, t, d), y_s.reshape(db, ts, d), new_k_p, new_v_p, new_ki_p, conv_p, sfin_p[None],
            new_k_s, new_v_s, new_ki_s, conv_s, sfin_s[None])
```

```python
import functools

import jax
import jax.numpy as jnp
from jax import lax
from jax.experimental import pallas as pl
from jax.experimental.pallas import tpu as pltpu

F32 = jnp.float32
BF16 = jnp.bfloat16
I32 = jnp.int32

D_MODEL = 2048
N_HEADS = 16
N_KV_HEADS = 4
HEAD_DIM = 128
ROPE_THETA = 500000.0
ROT_FRACTION_DIV = 4
IDX_HEADS = 16
IDX_DIM = 64
TOPK_MAX = 256
GDN_QK_HEADS = 8
GDN_V_HEADS = 16
GDN_DK = 128
GDN_DV = 128
CONV_W = 4
N_EXPERTS = 32
TOP_K = 4
D_FF = D_MODEL
SWIGLU_LIMIT = 7.0
SWIGLU_ALPHA = 1.702
NORM_EPS = 1e-6
PAGE_SIZE = 128
LANE = 128

C_AQ, C_AK, C_AV, C_IQ = 0, 2048, 2560, 3072
C_GQ, C_GK, C_GV, C_GZ, C_GL = 4096, 5120, 6144, 8192, 10240
C_MISC = 14336
L_IW, L_GB, L_GA = 64, 80, 96
PROJ_TN = 512
PROJ_W = 14848
MISC_BLK = C_MISC // LANE

VMEM_LIMIT = 48 * 1024 * 1024
MOE_VMEM_LIMIT = 56 * 1024 * 1024
NEG_BIG = -1e30
NEG_BF16 = -(2.0 ** 100)
INT_MIN = -2147483648

NT_DIMS = (((1,), (1,)), ((), ()))


def _cparams(sem):
    return pltpu.CompilerParams(dimension_semantics=sem, vmem_limit_bytes=VMEM_LIMIT)


def _split_bf16(a):
    hi = a.astype(BF16)
    lo = (a - hi.astype(F32)).astype(BF16)
    return hi, lo


def _dot3(a, b, dims=None):
    ah, al = _split_bf16(a)
    bh, bl = _split_bf16(b)
    if dims is None:
        d = lambda x, y: jnp.dot(x, y, preferred_element_type=F32)
    else:
        d = lambda x, y: lax.dot_general(x, y, dims, preferred_element_type=F32)
    return d(ah, bh) + d(al, bh) + d(ah, bl)


def _sigmoid(x):
    return 1.0 / (1.0 + jnp.exp(-x))


def _ada_kernel(c_ref, w_ref, b_ref, o_ref):
    c = c_ref[...]
    a = (c * _sigmoid(c)).astype(BF16)
    o_ref[...] = jnp.dot(a, w_ref[...].astype(BF16), preferred_element_type=F32) + b_ref[...]


def _ada(c, w, b):
    m, k = c.shape
    n = w.shape[2]
    tn = 1024
    return pl.pallas_call(
        _ada_kernel,
        grid=(n // tn,),
        in_specs=[pl.BlockSpec((m, k), lambda j: (0, 0)),
                  pl.BlockSpec((None, k, tn), lambda j: (0, 0, j)),
                  pl.BlockSpec((1, tn), lambda j: (0, j))],
        out_specs=pl.BlockSpec((m, tn), lambda j: (0, j)),
        out_shape=jax.ShapeDtypeStruct((m, n), F32),
        compiler_params=_cparams(("arbitrary",)),
        name="ada",
    )(c, w, b.reshape(1, n))


def _rot_slab(x, tabs, half):
    a, b, c = tabs
    return x * a + pltpu.roll(x, LANE - half, 1) * b + pltpu.roll(x, half, 1) * c


def _inproj_kernel(x_ref, g_ref, sc_ref, sh_ref, w_ref, tab_ref, o32_ref, o16_ref, h_scr):
    j = pl.program_id(1)

    @pl.when(j == 0)
    def _():
        x = x_ref[...]
        ms = jnp.mean(x * x, axis=-1, keepdims=True)
        y = x * lax.rsqrt(ms + NORM_EPS) * g_ref[...]
        h_scr[...] = (y * (1.0 + sc_ref[...]) + sh_ref[...]).astype(BF16)

    acc = jnp.dot(h_scr[...], w_ref[...], preferred_element_type=F32)
    n_slab = PROJ_TN // LANE

    def store(fn):
        for s in range(n_slab):
            v = fn(acc[:, s * LANE:(s + 1) * LANE])
            o32_ref[:, s * LANE:(s + 1) * LANE] = v
            o16_ref[:, s * LANE:(s + 1) * LANE] = v.astype(BF16)

    rot_head = j < 5
    rot_idx = ((j >= 6) & (j < 8)) | (j == C_MISC // PROJ_TN)

    @pl.when(rot_head)
    def _():
        tabs = (tab_ref[0], tab_ref[1], tab_ref[2])
        store(lambda v: _rot_slab(v, tabs, HEAD_DIM // ROT_FRACTION_DIV // 2))

    @pl.when(rot_idx)
    def _():
        tabs = (tab_ref[0], tab_ref[1], tab_ref[2])
        store(lambda v: _rot_slab(v, tabs, IDX_DIM // ROT_FRACTION_DIV // 2))

    @pl.when(jnp.logical_not(rot_head | rot_idx))
    def _():
        store(lambda v: v)


def _tab_type(j):
    return jnp.where(j < 6, 0, jnp.where(j < 28, 1, 2))


MOD_SHIFT1, MOD_SCALE1, MOD_GATE1, MOD_SHIFT2, MOD_SCALE2, MOD_GATE2 = range(6)


def _mod_spec(mod, col, rows_per_mod, tm, grid_rank):
    tiles_per_mod = rows_per_mod // tm
    if mod.shape[1] == 1:
        blk, pick = (None, 1, D_MODEL), lambda i: (i // tiles_per_mod, 0, col)
    else:
        blk, pick = (None, tm, D_MODEL), lambda i: (i // tiles_per_mod, i % tiles_per_mod, col)
    return pl.BlockSpec(blk, (lambda i, j: pick(i)) if grid_rank == 2 else pick)


def _inproj(x, g_norm, mod, w_r, tabs, rows_per_mod, tm):
    m = x.shape[0]
    tt = tabs.shape[2]
    n_t = tt // tm
    return pl.pallas_call(
        _inproj_kernel,
        grid=(m // tm, PROJ_W // PROJ_TN),
        in_specs=[pl.BlockSpec((tm, D_MODEL), lambda i, j: (i, 0)),
                  pl.BlockSpec((1, D_MODEL), lambda i, j: (0, 0)),
                  _mod_spec(mod, MOD_SCALE1, rows_per_mod, tm, 2), _mod_spec(mod, MOD_SHIFT1, rows_per_mod, tm, 2),
                  pl.BlockSpec((D_MODEL, PROJ_TN), lambda i, j: (0, j)),
                  pl.BlockSpec((None, 3, tm, LANE), lambda i, j: (_tab_type(j), 0, i % n_t, 0))],
        out_specs=[pl.BlockSpec((tm, PROJ_TN), lambda i, j: (i, j)),
                   pl.BlockSpec((tm, PROJ_TN), lambda i, j: (i, j))],
        out_shape=[jax.ShapeDtypeStruct((m, PROJ_W), F32), jax.ShapeDtypeStruct((m, PROJ_W), BF16)],
        scratch_shapes=[pltpu.VMEM((tm, D_MODEL), BF16)],
        compiler_params=_cparams(("parallel", "arbitrary")),
        name="inproj",
    )(x, g_norm.reshape(1, D_MODEL), mod, mod, w_r, tabs)


def _rot_tables(pos):
    tt = pos.shape[0]
    posf = pos.astype(F32)

    def tab(rot, period, scale):
        half = rot // 2
        inv = ROPE_THETA ** (-jnp.arange(half, dtype=F32) * (2.0 / rot))
        ang = posf[:, None] * inv[None, :]
        cos, sin = jnp.cos(ang), jnp.sin(ang)
        zh = jnp.zeros((tt, half), F32)
        rest = period - rot
        a = jnp.concatenate([cos, cos, jnp.ones((tt, rest), F32)], axis=1) * scale
        b = jnp.concatenate([-sin, zh, jnp.zeros((tt, rest), F32)], axis=1) * scale
        c = jnp.concatenate([zh, sin, jnp.zeros((tt, rest), F32)], axis=1) * scale
        return a, b, c

    head = jnp.stack([jnp.tile(t, (1, LANE // HEAD_DIM)) for t in tab(HEAD_DIM // ROT_FRACTION_DIV, HEAD_DIM, 1.0)])
    idxq = jnp.stack([jnp.tile(t, (1, LANE // IDX_DIM))
                      for t in tab(IDX_DIM // ROT_FRACTION_DIV, IDX_DIM, IDX_DIM ** -0.5)])
    ka, kb, kc = tab(IDX_DIM // ROT_FRACTION_DIV, IDX_DIM, 1.0)
    rest = LANE - IDX_DIM
    lane = jnp.arange(rest)
    tail_a = jnp.where(lane < (L_GB - L_IW), IDX_HEADS ** -0.5, 1.0).astype(F32)
    ma = jnp.concatenate([ka, jnp.broadcast_to(tail_a, (tt, rest))], axis=1)
    mb = jnp.concatenate([kb, jnp.zeros((tt, rest), F32)], axis=1)
    mc = jnp.concatenate([kc, jnp.zeros((tt, rest), F32)], axis=1)
    misc = jnp.stack([ma, mb, mc])
    return jnp.stack([head, idxq, misc])


def _reorder_w_in(w_in):
    offs = {}
    o = 0
    for name, wdt in (("aq", 2048), ("ak", 512), ("av", 512), ("iq", 1024), ("ik", 64), ("iw", 16), ("gq", 1024),
                      ("gk", 1024), ("gv", 2048), ("gz", 2048), ("gb", 16), ("ga", 16), ("gl", 4096)):
        offs[name] = (o, wdt)
        o += wdt

    def seg(name):
        s, wdt = offs[name]
        return w_in[:, s:s + wdt]

    k = w_in.shape[0]
    misc = jnp.concatenate([seg("ik"), seg("iw"), seg("gb"), seg("ga"), jnp.zeros((k, LANE - 112), w_in.dtype)], axis=1)
    parts = [seg("aq"), seg("ak"), seg("av"), seg("iq"), seg("gq"), seg("gk"), seg("gv"), seg("gz"), seg("gl"), misc,
             jnp.zeros((k, PROJ_W - C_MISC - LANE), w_in.dtype)]
    return jnp.concatenate(parts, axis=1).astype(BF16)


def _order_key(score):
    bits = pltpu.bitcast(score, I32)
    return bits ^ ((bits >> 31) & 0x7FFFFFFF)


def _kth_largest_key(count_ge, n_rows, n_sel):
    def body(i, t):
        bit = lax.shift_left(jnp.int32(1), 31 - i)
        cand = t ^ bit
        return jnp.where(count_ge(cand) >= n_sel, cand, t)

    return lax.fori_loop(0, 32, body, jnp.full((n_rows, 1), INT_MIN, I32))


RADIX_BITS = 4


def _kth_largest_key_radix(keys, n_sel):
    n_rows = keys.shape[0]
    n_cand = (1 << RADIX_BITS) - 1
    digit = lax.broadcasted_iota(I32, (n_cand, n_rows, 1), 0) + 1
    t_u = jnp.zeros((n_rows, 1), I32)
    for step in range(32 // RADIX_BITS):
        shift = 32 - RADIX_BITS * (step + 1)
        cand = (t_u[None] | lax.shift_left(digit, shift)) ^ INT_MIN
        cnt = jnp.sum(jnp.where(keys[None] >= cand, 1.0, 0.0), axis=2, keepdims=True)
        keep = jnp.sum(jnp.where(cnt >= n_sel, 1, 0), axis=0)
        t_u = t_u | lax.shift_left(keep, shift)
    return t_u ^ INT_MIN


def _attn_prompt_kernel(q_ref, iq_ref, mq_ref, k_ref, v_ref, mk_ref, o_ref, key_scr, *, tq, kc, n_sel):
    qb = pl.program_id(1)
    q0 = qb * tq
    nkc = (q0 + tq + kc - 1) // kc
    qpos = q0 + lax.broadcasted_iota(I32, (tq, 1), 0)
    kiota = lax.broadcasted_iota(I32, (1, kc), 1)
    wq = mq_ref[...]

    def score_body(c, carry):
        ks = pl.multiple_of(c * kc, kc)
        kt = mk_ref[pl.ds(ks, kc), :][:, :IDX_DIM]
        acc = jnp.zeros((tq, kc), F32)
        for h in range(IDX_HEADS):
            s = lax.dot_general(iq_ref[:, h * IDX_DIM:(h + 1) * IDX_DIM], kt, NT_DIMS, preferred_element_type=F32)
            acc = acc + wq[:, L_IW + h:L_IW + h + 1] * jnp.maximum(s, 0.0)
        acc = jnp.where(ks + kiota <= qpos, acc, -jnp.inf)
        key_scr[c] = _order_key(acc)
        return carry

    lax.fori_loop(0, nkc, score_body, 0)

    def count(cmp, cand):
        def body(c, acc):
            m = jnp.where(cmp(key_scr[c], cand), 1.0, 0.0)
            for s in range(kc // LANE):
                acc = acc + m[:, s * LANE:(s + 1) * LANE]
            return acc

        acc = lax.fori_loop(0, nkc, body, jnp.zeros((tq, LANE), F32))
        return jnp.sum(acc, axis=1, keepdims=True)

    count_ge = functools.partial(count, lambda a, b: a >= b)
    thr = _kth_largest_key(count_ge, tq, float(n_sel))

    n_tie_ok = float(n_sel) - count(lambda a, b: a > b, thr)

    @pl.when(jnp.max(count_ge(thr)) > float(n_sel))
    def _():
        before = (lax.broadcasted_iota(I32, (kc, kc), 0) < lax.broadcasted_iota(I32, (kc, kc), 1))
        before = jnp.where(before, 1.0, 0.0).astype(BF16)

        def demote(c, seen):
            key = key_scr[c]
            eq = key == thr
            eqf = jnp.where(eq, 1.0, 0.0)
            rank = seen + jnp.dot(eqf.astype(BF16), before, preferred_element_type=F32)
            key_scr[c] = jnp.where(eq & (rank >= n_tie_ok), INT_MIN, key)
            return seen + jnp.sum(eqf, axis=1, keepdims=True)

        lax.fori_loop(0, nkc, demote, jnp.zeros((tq, 1), F32))

    group = N_HEADS // N_KV_HEADS
    scale = HEAD_DIM ** -0.5
    qss = []
    for n in range(N_KV_HEADS):
        qs = jnp.concatenate(
            [q_ref[:, (n * group + g) * HEAD_DIM:(n * group + g + 1) * HEAD_DIM] for g in range(group)], axis=0)
        qss.append((qs.astype(F32) * scale).astype(BF16))

    def attn_body(c, carry):
        ks = pl.multiple_of(c * kc, kc)
        sel = (key_scr[c] >= thr) & (ks + kiota <= qpos)
        bias = jnp.where(sel, 0.0, NEG_BF16).astype(BF16)[None]
        new = []
        for n in range(N_KV_HEADS):
            m_i, acc = carry[n]
            kn = k_ref[pl.ds(ks, kc), n * HEAD_DIM:(n + 1) * HEAD_DIM]
            vn = jnp.concatenate([v_ref[pl.ds(ks, kc), n * HEAD_DIM:(n + 1) * HEAD_DIM], ones_blk], axis=1)
            s = lax.dot_general(qss[n], kn, NT_DIMS, preferred_element_type=F32).astype(BF16)
            s = (s.reshape(group, tq, kc) + bias).reshape(group * tq, kc)
            m_new = jnp.maximum(m_i, jnp.max(s, axis=1, keepdims=True).astype(F32))
            alpha = jnp.exp(m_i - m_new)
            p = jnp.exp(s - m_new.astype(BF16))
            acc = alpha * acc + jnp.dot(p, vn, preferred_element_type=F32)
            new.append((m_new, acc))
        return tuple(new)

    ones_blk = jnp.ones((kc, HEAD_DIM), BF16)
    init = tuple((jnp.full((group * tq, 1), NEG_BF16, F32), jnp.zeros((group * tq, 2 * HEAD_DIM), F32))
                 for _ in range(N_KV_HEADS))
    fin = lax.fori_loop(0, nkc, attn_body, init)
    for n in range(N_KV_HEADS):
        _, acc_f = fin[n]
        out = acc_f[:, :HEAD_DIM] / acc_f[:, HEAD_DIM:HEAD_DIM + 1]
        for g in range(group):
            o_ref[:, (n * group + g) * HEAD_DIM:(n * group + g + 1) * HEAD_DIM] = out[g * tq:(g + 1) * tq].astype(BF16)


def _attn_prompt(p32, p16, bsz, t, tq=256, kc=512):
    nq = t // tq
    kc = min(kc, t)
    n_sel = min(TOPK_MAX, t // 4)
    kern = functools.partial(_attn_prompt_kernel, tq=tq, kc=kc, n_sel=n_sel)
    return pl.pallas_call(
        kern,
        grid=(bsz, nq),
        in_specs=[pl.BlockSpec((tq, N_HEADS * HEAD_DIM), lambda b, i: (b * nq + i, 0)),
                  pl.BlockSpec((tq, IDX_HEADS * IDX_DIM), lambda b, i: (b * nq + i, C_IQ // (IDX_HEADS * IDX_DIM))),
                  pl.BlockSpec((tq, LANE), lambda b, i: (b * nq + i, MISC_BLK)),
                  pl.BlockSpec((t, N_KV_HEADS * HEAD_DIM), lambda b, i: (b, C_AK // (N_KV_HEADS * HEAD_DIM))),
                  pl.BlockSpec((t, N_KV_HEADS * HEAD_DIM), lambda b, i: (b, C_AV // (N_KV_HEADS * HEAD_DIM))),
                  pl.BlockSpec((t, LANE), lambda b, i: (b, MISC_BLK))],
        out_specs=pl.BlockSpec((tq, N_HEADS * HEAD_DIM), lambda b, i: (b * nq + i, 0)),
        out_shape=jax.ShapeDtypeStruct((bsz * t, N_HEADS * HEAD_DIM), BF16),
        scratch_shapes=[pltpu.VMEM((t // kc, tq, kc), I32)],
        compiler_params=_cparams(("parallel", "arbitrary")),
        name="attn_prompt",
    )(p16, p16, p32, p16, p16, p16)


def _attn_sample_kernel(pt_ref, iq_ref, wc_ref, q_ref, kn_ref, vn_ref, ikn_ref, *rest, n_pages, n_sel, past):
    kidx_refs = rest[:n_pages]
    k_refs = rest[n_pages:2 * n_pages]
    v_refs = rest[2 * n_pages:3 * n_pages]
    o_ref, sc_scr, selx_scr, s_scr = rest[3 * n_pages:]
    del pt_ref
    rows, nkv = 8, N_KV_HEADS
    n_keys = (n_pages + 1) * PAGE_SIZE
    pw = PAGE_SIZE * nkv
    iq = iq_ref[...].astype(BF16)
    wc = wc_ref[...]

    def idx_score(kt, dims):
        s = lax.dot_general(iq, kt.astype(BF16), dims, preferred_element_type=F32)
        x = jnp.maximum(s, 0.0) * wc
        return jnp.sum(x.reshape(IDX_HEADS, rows, PAGE_SIZE), axis=0)

    nn_dims = (((1,), (0,)), ((), ()))
    for p in range(n_pages):
        sc_scr[:, p * PAGE_SIZE:(p + 1) * PAGE_SIZE] = idx_score(kidx_refs[p][...], nn_dims)
    sc_scr[:, n_pages * PAGE_SIZE:] = idx_score(ikn_ref[...], NT_DIMS)

    qpos = past + lax.broadcasted_iota(I32, (rows, 1), 0)
    kpos = lax.broadcasted_iota(I32, (1, n_keys), 1)
    causal = kpos <= qpos
    keys = _order_key(jnp.where(causal, sc_scr[...], -jnp.inf))

    thr = _kth_largest_key_radix(keys, float(n_sel))
    above = keys > thr
    tied = jnp.where(keys == thr, 1.0, 0.0)
    n_tie_ok = float(n_sel) - jnp.sum(jnp.where(above, 1.0, 0.0), axis=1, keepdims=True)
    before = (lax.broadcasted_iota(I32, (PAGE_SIZE, PAGE_SIZE), 0) < lax.broadcasted_iota(I32, (PAGE_SIZE, PAGE_SIZE), 1))
    before = jnp.where(before, 1.0, 0.0).astype(BF16)
    seen = jnp.zeros((rows, 1), F32)
    ranks = []
    for p in range(n_pages + 1):
        tp = tied[:, p * PAGE_SIZE:(p + 1) * PAGE_SIZE]
        ranks.append(seen + jnp.dot(tp.astype(BF16), before, preferred_element_type=F32))
        seen = seen + jnp.sum(tp, axis=1, keepdims=True)
    rank = jnp.concatenate(ranks, axis=1)
    sel = jnp.where((above | ((tied > 0.5) & (rank < n_tie_ok))) & causal, 1.0, 0.0).astype(BF16)
    expand = jnp.where((lax.broadcasted_iota(I32, (PAGE_SIZE, pw), 1) >> 2)
                       == lax.broadcasted_iota(I32, (PAGE_SIZE, pw), 0), 1.0, 0.0).astype(BF16)
    for p in range(n_pages + 1):
        selx_scr[:, p * pw:(p + 1) * pw] = jnp.dot(sel[:, p * PAGE_SIZE:(p + 1) * PAGE_SIZE], expand,
                                                   preferred_element_type=F32)
    n_q = N_HEADS * rows
    own_head = ((lax.broadcasted_iota(I32, (n_q, pw), 0) >> 5)
                == (lax.broadcasted_iota(I32, (n_q, pw), 1) & (nkv - 1))).reshape(N_HEADS, rows, pw)
    q = (q_ref[...] * (HEAD_DIM ** -0.5)).astype(BF16)
    m_parts = []
    for p in range(n_pages + 1):
        kb = (k_refs[p] if p < n_pages else kn_ref)[...].astype(BF16)
        s = lax.dot_general(q, kb, NT_DIMS, preferred_element_type=F32)
        ok = own_head & (selx_scr[:, p * pw:(p + 1) * pw] > 0.5)[None]
        s = jnp.where(ok, s.reshape(N_HEADS, rows, pw), NEG_BIG).reshape(n_q, pw)
        s_scr[:, p * pw:(p + 1) * pw] = s
        m_parts.append(jnp.max(s, axis=1, keepdims=True))
    m = m_parts[0]
    for mp in m_parts[1:]:
        m = jnp.maximum(m, mp)
    l_parts, o_parts = [], []
    for p in range(n_pages + 1):
        vb = (v_refs[p] if p < n_pages else vn_ref)[...].astype(BF16)
        pr = jnp.exp(s_scr[:, p * pw:(p + 1) * pw] - m)
        l_parts.append(jnp.sum(pr, axis=1, keepdims=True))
        o_parts.append(jnp.dot(pr.astype(BF16), vb, preferred_element_type=F32))
    l = l_parts[0]
    acc = o_parts[0]
    for lp, op in zip(l_parts[1:], o_parts[1:]):
        l = l + lp
        acc = acc + op
    o_ref[...] = acc / l


def _attn_sample(p32, cache_k, cache_v, cache_kidx, page_table, ts):
    db, n_pages = page_table.shape
    past = n_pages * PAGE_SIZE
    rows = 8
    group = N_HEADS // N_KV_HEADS
    ps = p32.reshape(db, ts, PROJ_W)
    pad_t = ((0, 0), (0, 0), (0, rows - ts), (0, 0))
    iq = ps[:, :, C_IQ:C_IQ + IDX_HEADS * IDX_DIM].reshape(db, ts, IDX_HEADS, IDX_DIM).transpose(0, 2, 1, 3)
    iq = jnp.pad(iq, pad_t).reshape(db, IDX_HEADS * rows, IDX_DIM)
    wc = ps[:, :, C_MISC + L_IW:C_MISC + L_IW + IDX_HEADS].transpose(0, 2, 1)
    wc = jnp.pad(wc, ((0, 0), (0, 0), (0, rows - ts))).reshape(db, IDX_HEADS * rows, 1)
    q = ps[:, :, :N_HEADS * HEAD_DIM].reshape(db, ts, N_HEADS, HEAD_DIM).transpose(0, 2, 1, 3)
    q = jnp.pad(q, pad_t).reshape(db, N_HEADS * rows, HEAD_DIM)
    pad_k = ((0, 0), (0, PAGE_SIZE - ts), (0, 0))
    pw = PAGE_SIZE * N_KV_HEADS
    kn = jnp.pad(ps[:, :, C_AK:C_AK + N_KV_HEADS * HEAD_DIM], pad_k).reshape(db, pw, HEAD_DIM)
    vn = jnp.pad(ps[:, :, C_AV:C_AV + N_KV_HEADS * HEAD_DIM], pad_k).reshape(db, pw, HEAD_DIM)
    ikn = jnp.pad(ps[:, :, C_MISC:C_MISC + IDX_DIM], pad_k)
    n_sel = min(TOPK_MAX, (past + ts) // 4)
    n_phys = cache_k.shape[1]
    ck = cache_k.reshape(1, n_phys, pw, HEAD_DIM)
    cv = cache_v.reshape(1, n_phys, pw, HEAD_DIM)

    def kv_page(p):
        return pl.BlockSpec((None, None, pw, HEAD_DIM), lambda d, pt, p=p: (0, pt[d * n_pages + p], 0, 0))

    ckt = jnp.swapaxes(cache_kidx, 2, 3)

    def idx_page(p):
        return pl.BlockSpec((None, None, IDX_DIM, PAGE_SIZE), lambda d, pt, p=p: (0, pt[d * n_pages + p], 0, 0))

    def per_d(r, w):
        return pl.BlockSpec((None, r, w), lambda d, pt: (d, 0, 0))

    in_specs = ([per_d(IDX_HEADS * rows, IDX_DIM), per_d(IDX_HEADS * rows, 1), per_d(N_HEADS * rows, HEAD_DIM),
                 per_d(pw, HEAD_DIM), per_d(pw, HEAD_DIM), per_d(PAGE_SIZE, IDX_DIM)]
                + [idx_page(p) for p in range(n_pages)]
                + [kv_page(p) for p in range(n_pages)]
                + [kv_page(p) for p in range(n_pages)])
    n_keys = (n_pages + 1) * PAGE_SIZE
    grid_spec = pltpu.PrefetchScalarGridSpec(
        num_scalar_prefetch=1, grid=(db,), in_specs=in_specs,
        out_specs=per_d(N_HEADS * rows, HEAD_DIM),
        scratch_shapes=[pltpu.VMEM((rows, n_keys), F32), pltpu.VMEM((rows, n_keys * N_KV_HEADS), F32),
                        pltpu.VMEM((N_HEADS * rows, n_keys * N_KV_HEADS), F32)])
    kern = functools.partial(_attn_sample_kernel, n_pages=n_pages, n_sel=n_sel, past=past)
    o = pl.pallas_call(
        kern, grid_spec=grid_spec,
        out_shape=jax.ShapeDtypeStruct((db, N_HEADS * rows, HEAD_DIM), F32),
        compiler_params=_cparams(("arbitrary",)),
        name="attn_sample",
    )(page_table.reshape(-1), iq, wc, q, kn, vn, ikn,
      *([ckt] * n_pages), *([ck] * n_pages), *([cv] * n_pages))
    o = o.reshape(db, N_HEADS, rows, HEAD_DIM)[:, :, :ts].transpose(0, 2, 1, 3)
    return o.reshape(db * ts, N_HEADS * HEAD_DIM).astype(BF16)


def _dot3s(ah, al, bh, bl, dims=None):
    if dims is None:
        d = lambda x, y: jnp.dot(x, y, preferred_element_type=F32)
    else:
        d = lambda x, y: lax.dot_general(x, y, dims, preferred_element_type=F32)
    return d(ah, bh) + d(al, bh) + d(ah, bl)


def _cumsum_rows(sel_bf16, g_rows):
    gb = jnp.broadcast_to(g_rows, (g_rows.shape[0], LANE))
    g_hi, g_lo = _split_bf16(gb)
    g_lo2 = (gb - g_hi.astype(F32) - g_lo.astype(F32)).astype(BF16)
    d = lambda y: jnp.dot(sel_bf16, y, preferred_element_type=F32)
    return d(g_hi) + d(g_lo) + d(g_lo2)


def _l2n(x):
    return x * lax.rsqrt(jnp.sum(x * x, axis=-1, keepdims=True) + NORM_EPS)


def _gate_params(misc, hp, live):
    x = misc + hp[1:2, :]
    softplus = jnp.maximum(x, 0.0) + jnp.log(1.0 + jnp.exp(-jnp.abs(x)))
    g_all = -jnp.exp(hp[0:1, :]) * softplus
    beta_all = _sigmoid(misc)
    if live is not None:
        g_all = jnp.where(live, g_all, 0.0)
        beta_all = jnp.where(live, beta_all, 0.0)
    return g_all, beta_all


def _gdn_kernel(q_ref, k_ref, v_ref, z_ref, m_ref, wq_ref, wk_ref, wv_ref, cq_ref, ck_ref, cv_ref, hp_ref, nw_ref,
                s0_ref, o_ref, sf_ref, s_scr, hq_scr, hk_scr, hv_scr, xc_scr, *, c, nh):
    g_idx = pl.program_id(1)
    tb = pl.program_id(2)
    n_tb = pl.num_programs(2)
    dk, dv = GDN_DK, GDN_DV

    @pl.when(tb == 0)
    def _():
        s_scr[...] = s0_ref[...]
        hq_scr[...] = cq_ref[...]
        hk_scr[...] = ck_ref[...]
        hv_scr[...] = cv_ref[...]

    def conv_silu(x_ref, w_ref, h_scr):
        width = x_ref.shape[1]
        x = x_ref[...]
        xc_scr[0:8, 0:width] = h_scr[...]
        xc_scr[8:8 + c, 0:width] = x
        acc = jnp.zeros((c, width), F32)
        for j in range(CONV_W):
            acc = acc + xc_scr[8 - (CONV_W - 1) + j:8 - (CONV_W - 1) + j + c, 0:width] * w_ref[j:j + 1, :]
        h_scr[...] = x[c - 8:c, :]
        return acc * _sigmoid(acc)

    qa = conv_silu(q_ref, wq_ref, hq_scr)
    ka = conv_silu(k_ref, wk_ref, hk_scr)
    va = conv_silu(v_ref, wv_ref, hv_scr)
    g_all, beta_all = _gate_params(m_ref[...], hp_ref[...], None)
    lane = lax.broadcasted_iota(I32, (1, LANE), 1)
    ri = lax.broadcasted_iota(I32, (c, c), 0)
    ci = lax.broadcasted_iota(I32, (c, c), 1)
    causal = ri >= ci
    strict = ri > ci
    tri = jnp.where(causal, 1.0, 0.0).astype(BF16)
    eye = jnp.where(ri == ci, 1.0, 0.0)

    def same_blk(log_b):
        return (ri >> log_b) == (ci >> log_b)

    heads = range(2 * nh)
    ks = [_l2n(ka[:, hh * dk:(hh + 1) * dk]) for hh in range(nh)]
    qss = [_l2n(qa[:, hh * dk:(hh + 1) * dk]) * (dk ** -0.5) for hh in range(nh)]
    kk_qk = [lax.dot_general(jnp.concatenate([ks[hh], qss[hh]], axis=0).astype(BF16), ks[hh].astype(BF16), NT_DIMS,
                             preferred_element_type=F32) for hh in range(nh)]
    g_cols = [jnp.sum(jnp.where(lane == L_GA + 2 * nh * g_idx + hl, g_all, 0.0), axis=1, keepdims=True)
              for hl in heads]
    betas = [jnp.sum(jnp.where(lane == L_GB + 2 * nh * g_idx + hl, beta_all, 0.0), axis=1, keepdims=True)
             for hl in heads]
    gcs = [_cumsum_rows(tri, g) for g in g_cols]
    decays = [jnp.exp(jnp.where(causal, gc - gc.T, -jnp.inf)) for gc in gcs]
    a_mats = [jnp.where(strict, kk_qk[hl // 2][0:c] * betas[hl] * decays[hl], 0.0) for hl in heads]
    nms = [jnp.where(same_blk(3), -a, 0.0) for a in a_mats]
    pms = [eye + n for n in nms]
    for _ in range(2):
        nms = [_dot3(n, n) for n in nms]
        pms = [p + _dot3(p, n) for p, n in zip(pms, nms)]
    for lb in range(3, (c - 1).bit_length()):
        join = same_blk(lb + 1) & jnp.logical_not(same_blk(lb))
        t1s = [_dot3(jnp.where(join, a, 0.0), p) for a, p in zip(a_mats, pms)]
        pms = [p - _dot3(p, t1) for p, t1 in zip(pms, t1s)]
    us = [_dot3(pms[hl], va[:, hl * dv:(hl + 1) * dv] * betas[hl]) for hl in heads]
    ws = [_dot3(pms[hl], ks[hl // 2] * betas[hl] * jnp.exp(gcs[hl])) for hl in heads]
    qks = [jnp.where(causal, kk_qk[hl // 2][c:2 * c] * decays[hl], 0.0) for hl in heads]
    s_in = [s_scr[hl] for hl in heads]
    v_news = [us[hl] - _dot3(ws[hl], s_in[hl]) for hl in heads]
    os_ = [jnp.dot((qss[hl // 2] * jnp.exp(gcs[hl])).astype(BF16), s_in[hl].astype(BF16), preferred_element_type=F32)
           + jnp.dot(qks[hl].astype(BF16), v_news[hl].astype(BF16), preferred_element_type=F32) for hl in heads]
    for hl in heads:
        g_last = gcs[hl][c - 1:c, :]
        kdec = ks[hl // 2] * jnp.exp(g_last - gcs[hl])
        s_scr[hl] = s_in[hl] * jnp.exp(g_last) + _dot3(kdec.T, v_news[hl])
    for hl in heads:
        o = os_[hl]
        z = z_ref[:, hl * dv:(hl + 1) * dv]
        on = o * lax.rsqrt(jnp.mean(o * o, axis=-1, keepdims=True) + NORM_EPS) * nw_ref[...]
        o_ref[:, hl * dv:(hl + 1) * dv] = (on * (z * _sigmoid(z))).astype(BF16)

    @pl.when(tb == n_tb - 1)
    def _():
        sf_ref[...] = s_scr[...]


def _gdn_hp(a_log, dt_bias):
    hp = jnp.zeros((8, LANE), F32)
    return hp.at[0, L_GA:L_GA + GDN_V_HEADS].set(a_log.astype(F32)).at[1, L_GA:L_GA + GDN_V_HEADS].set(
        dt_bias.astype(F32))


def _gdn(p32, conv0, s0, conv_w, a_log, dt_bias, norm_w, bsz, t, c=128, nh=4):
    n_tb = t // c
    n_g = GDN_QK_HEADS // nh
    qw, vw = nh * GDN_DK, 2 * nh * GDN_DV
    row = lambda b, g, i: b * n_tb + i
    in_specs = [
        pl.BlockSpec((c, qw), lambda b, g, i: (row(b, g, i), C_GQ // qw + g)),
        pl.BlockSpec((c, qw), lambda b, g, i: (row(b, g, i), C_GK // qw + g)),
        pl.BlockSpec((c, vw), lambda b, g, i: (row(b, g, i), C_GV // vw + g)),
        pl.BlockSpec((c, vw), lambda b, g, i: (row(b, g, i), C_GZ // vw + g)),
        pl.BlockSpec((c, LANE), lambda b, g, i: (row(b, g, i), MISC_BLK)),
        pl.BlockSpec((CONV_W, qw), lambda b, g, i: (0, g)),
        pl.BlockSpec((CONV_W, qw), lambda b, g, i: (0, n_g + g)),
        pl.BlockSpec((CONV_W, vw), lambda b, g, i: (0, n_g + g)),
        pl.BlockSpec((None, 8, qw), lambda b, g, i: (b, 0, g)),
        pl.BlockSpec((None, 8, qw), lambda b, g, i: (b, 0, n_g + g)),
        pl.BlockSpec((None, 8, vw), lambda b, g, i: (b, 0, n_g + g)),
        pl.BlockSpec((8, LANE), lambda b, g, i: (0, 0)),
        pl.BlockSpec((1, GDN_DV), lambda b, g, i: (0, 0)),
        pl.BlockSpec((None, 2 * nh, GDN_DK, GDN_DV), lambda b, g, i: (b, g, 0, 0)),
    ]
    out_specs = [pl.BlockSpec((c, vw), lambda b, g, i: (row(b, g, i), g)),
                 pl.BlockSpec((None, 2 * nh, GDN_DK, GDN_DV), lambda b, g, i: (b, g, 0, 0))]
    return pl.pallas_call(
        functools.partial(_gdn_kernel, c=c, nh=nh),
        grid=(bsz, n_g, n_tb),
        in_specs=in_specs, out_specs=out_specs,
        out_shape=[jax.ShapeDtypeStruct((bsz * t, GDN_V_HEADS * GDN_DV), BF16),
                   jax.ShapeDtypeStruct((bsz, GDN_V_HEADS, GDN_DK, GDN_DV), F32)],
        scratch_shapes=[pltpu.VMEM((2 * nh, GDN_DK, GDN_DV), F32), pltpu.VMEM((8, qw), F32), pltpu.VMEM((8, qw), F32),
                        pltpu.VMEM((8, vw), F32), pltpu.VMEM((c + 8, vw), F32)],
        compiler_params=_cparams(("parallel", "parallel", "arbitrary")),
        name="gdn",
    )(p32, p32, p32, p32, p32, conv_w, conv_w, conv_w, conv0, conv0, conv0, _gdn_hp(a_log, dt_bias),
      norm_w.reshape(1, GDN_DV), s0)


def _gdn_sample_kernel(x_ref, z_ref, m_ref, w_ref, hp_ref, nw_ref, s0_ref, o_ref, sf_ref, *, ts):
    rows, nvh, dk, dv = 8, GDN_V_HEADS, GDN_DK, GDN_DV
    qd = GDN_QK_HEADS * dk
    x8 = x_ref[...]
    w = w_ref[...]
    acc = x8 * w[0:1, :]
    for j in range(1, CONV_W):
        acc = acc + pltpu.roll(x8, rows - j, 0) * w[j:j + 1, :]
    act = acc * _sigmoid(acc)
    live = lax.broadcasted_iota(I32, (rows, 1), 0) < ts
    g_all, beta_all = _gate_params(m_ref[...], hp_ref[...], live)
    qn = [_l2n(act[:, g * dk:(g + 1) * dk]) * (dk ** -0.5) for g in range(GDN_QK_HEADS)]
    kn = [_l2n(act[:, qd + g * dk:qd + (g + 1) * dk]) for g in range(GDN_QK_HEADS)]
    rep = nvh // GDN_QK_HEADS
    cat = lambda parts: jnp.concatenate(parts, axis=0)
    k = cat([kn[h // rep] for h in range(nvh)])
    qs = cat([qn[h // rep] for h in range(nvh)])
    v = cat([act[:, 2 * qd + h * dv:2 * qd + (h + 1) * dv] for h in range(nvh)])
    z = cat([z_ref[:, h * dv:(h + 1) * dv] for h in range(nvh)])
    beta = cat([beta_all[:, L_GB + h:L_GB + h + 1] for h in range(nvh)])
    g_col = cat([g_all[:, L_GA + h:L_GA + h + 1] for h in range(nvh)])
    n = nvh * rows
    ri = lax.broadcasted_iota(I32, (n, n), 0)
    ci = lax.broadcasted_iota(I32, (n, n), 1)
    same = (ri >> 3) == (ci >> 3)
    causal = same & (ri >= ci)
    strict = same & (ri > ci)
    eye = jnp.where(ri == ci, 1.0, 0.0)
    gc = _cumsum_rows(jnp.where(causal, 1.0, 0.0).astype(BF16), g_col)
    gl = _cumsum_rows(jnp.where(same, 1.0, 0.0).astype(BF16), g_col)
    decay = jnp.exp(jnp.where(causal, gc - gc.T, -jnp.inf))
    kb = k * beta
    a_mat = jnp.where(strict, _dot3(kb, k, NT_DIMS) * decay, 0.0)
    nm = -a_mat
    pm = eye + nm
    for _ in range(2):
        nm = _dot3(nm, nm)
        pm = pm + _dot3(pm, nm)
    u = _dot3(pm, v * beta)
    w_rows = _dot3(pm, kb * jnp.exp(gc))
    qk = jnp.where(causal, _dot3(qs, k, NT_DIMS) * decay, 0.0)
    qg = qs * jnp.exp(gc)
    kdt = (k * jnp.exp(gl - gc)).T
    s_in = [s0_ref[h] for h in range(nvh)]
    v_new, o1 = [], []
    for h in range(nvh):
        sl = slice(h * rows, (h + 1) * rows)
        r = _dot3(cat([w_rows[sl], qg[sl]]), s_in[h])
        v_new.append(u[sl] - r[0:rows])
        o1.append(r[rows:2 * rows])
    v_new = cat(v_new)
    o = cat(o1) + _dot3(qk, v_new)
    vh, vl = _split_bf16(v_new)
    lane = lax.broadcasted_iota(I32, (1, n), 1)
    for h in range(nvh):
        kh, kl = _split_bf16(jnp.where((lane >> 3) == h, kdt, 0.0))
        sf_ref[h] = s_in[h] * jnp.exp(gl[h * rows:h * rows + 1, :]) + _dot3s(kh, kl, vh, vl)
    on = o * lax.rsqrt(jnp.mean(o * o, axis=-1, keepdims=True) + NORM_EPS) * nw_ref[...]
    o_ref[...] = on * (z * _sigmoid(z))


def _gdn_sample(p32, state_conv, s0, conv_w, a_log, dt_bias, norm_w, db, ts):
    rows = 8
    conv_dim = conv_w.shape[-1]
    ps = p32.reshape(db, ts, PROJ_W)
    x8 = jnp.concatenate([state_conv, ps[:, :, C_GQ:C_GQ + conv_dim],
                          jnp.zeros((db, rows - ts - (CONV_W - 1), conv_dim), F32)], axis=1)
    pad_t = ((0, 0), (0, rows - ts), (0, 0))
    z8 = jnp.pad(ps[:, :, C_GZ:C_GZ + GDN_V_HEADS * GDN_DV], pad_t)
    m8 = jnp.pad(ps[:, :, C_MISC:C_MISC + LANE], pad_t)
    n = GDN_V_HEADS * rows
    per_d = lambda w: pl.BlockSpec((None, rows, w), lambda d: (d, 0, 0))
    state = pl.BlockSpec((None, GDN_V_HEADS, GDN_DK, GDN_DV), lambda d: (d, 0, 0, 0))
    o, sf = pl.pallas_call(
        functools.partial(_gdn_sample_kernel, ts=ts),
        grid=(db,),
        in_specs=[per_d(conv_dim), per_d(GDN_V_HEADS * GDN_DV), per_d(LANE),
                  pl.BlockSpec((CONV_W, conv_dim), lambda d: (0, 0)),
                  pl.BlockSpec((8, LANE), lambda d: (0, 0)),
                  pl.BlockSpec((1, GDN_DV), lambda d: (0, 0)), state],
        out_specs=[pl.BlockSpec((None, n, GDN_DV), lambda d: (d, 0, 0)), state],
        out_shape=[jax.ShapeDtypeStruct((db, n, GDN_DV), F32),
                   jax.ShapeDtypeStruct((db, GDN_V_HEADS, GDN_DK, GDN_DV), F32)],
        compiler_params=_cparams(("parallel",)),
        name="gdn_sample",
    )(x8, z8, m8, conv_w, _gdn_hp(a_log, dt_bias), norm_w.reshape(1, GDN_DV), s0)
    o = o.reshape(db, GDN_V_HEADS, rows, GDN_DV)[:, :, :ts].transpose(0, 2, 1, 3)
    return o.reshape(db * ts, GDN_V_HEADS * GDN_DV).astype(BF16), sf


def _merge_kernel(a_ref, g_ref, wa_ref, wg_ref, ga_ref, gg_ref, o_ref):
    pa = jnp.dot(a_ref[...], wa_ref[...], preferred_element_type=F32)
    pg = jnp.dot(g_ref[...], wg_ref[...], preferred_element_type=F32)
    o_ref[...] = (_sigmoid(ga_ref[...]) * pa + _sigmoid(gg_ref[...]) * pg).astype(BF16)


def _merge(attn_o, gdn_o, w_oa, w_og, p32, tm):
    m = attn_o.shape[0]
    tn = 512
    return pl.pallas_call(
        _merge_kernel,
        grid=(m // tm, D_MODEL // tn),
        in_specs=[pl.BlockSpec((tm, D_MODEL), lambda i, j: (i, 0)),
                  pl.BlockSpec((tm, D_MODEL), lambda i, j: (i, 0)),
                  pl.BlockSpec((D_MODEL, tn), lambda i, j: (0, j)),
                  pl.BlockSpec((D_MODEL, tn), lambda i, j: (0, j)),
                  pl.BlockSpec((tm, tn), lambda i, j: (i, C_GL // tn + j)),
                  pl.BlockSpec((tm, tn), lambda i, j: (i, (C_GL + D_MODEL) // tn + j))],
        out_specs=pl.BlockSpec((tm, tn), lambda i, j: (i, j)),
        out_shape=jax.ShapeDtypeStruct((m, D_MODEL), BF16),
        compiler_params=_cparams(("parallel", "arbitrary")),
        name="merge",
    )(attn_o, gdn_o, w_oa, w_og, p32, p32)


def _outproj_kernel(mg_ref, w_ref, x_ref, g1_ref, sc_ref, sh_ref, gn_ref, wr_ref, br_ref,
                    x1_ref, h2_ref, ti_ref, tg_ref):
    x1 = x_ref[...] + g1_ref[...] * jnp.dot(mg_ref[...], w_ref[...], preferred_element_type=F32)
    x1_ref[...] = x1
    ms = jnp.mean(x1 * x1, axis=-1, keepdims=True)
    h2 = x1 * lax.rsqrt(ms + NORM_EPS) * gn_ref[...] * (1.0 + sc_ref[...]) + sh_ref[...]
    h2_ref[...] = h2
    logits = _dot3(h2, wr_ref[...]) + br_ref[...]
    tm = logits.shape[0]
    lane = lax.broadcasted_iota(I32, (tm, LANE), 1)
    lanef = lane.astype(F32)
    l = jnp.where(lane < N_EXPERTS, logits, -jnp.inf)
    vals, idxs = [], []
    for _ in range(TOP_K):
        mx = jnp.max(l, axis=1, keepdims=True)
        ix = jnp.min(jnp.where(l == mx, lanef, float(LANE)), axis=1, keepdims=True)
        vals.append(mx)
        idxs.append(ix)
        l = jnp.where(lanef == ix, -jnp.inf, l)
    es = [jnp.exp(v - vals[0]) for v in vals]
    den = es[0]
    for e in es[1:]:
        den = den + e
    ti = jnp.zeros((tm, LANE), F32)
    tg = jnp.zeros((tm, LANE), F32)
    for kk in range(TOP_K):
        ti = jnp.where(lane == kk, idxs[kk], ti)
        tg = jnp.where(lane == kk, es[kk] / den, tg)
    ti_ref[...] = ti.astype(I32)
    tg_ref[...] = tg


def _outproj(merged, w_out, x, mod, g_norm2, w_router, b_router, rows_per_mod, tm):
    m = x.shape[0]
    mod_specs = [_mod_spec(mod, c, rows_per_mod, tm, 1) for c in (MOD_GATE1, MOD_SCALE2, MOD_SHIFT2)]
    wr = jnp.pad(w_router.astype(F32), ((0, 0), (0, LANE - N_EXPERTS)))
    br = jnp.pad(b_router.astype(F32), (0, LANE - N_EXPERTS)).reshape(1, LANE)
    row = pl.BlockSpec((tm, D_MODEL), lambda i: (i, 0))
    small = pl.BlockSpec((tm, LANE), lambda i: (i, 0))
    return pl.pallas_call(
        _outproj_kernel,
        grid=(m // tm,),
        in_specs=[row, pl.BlockSpec((D_MODEL, D_MODEL), lambda i: (0, 0)), row, *mod_specs,
                  pl.BlockSpec((1, D_MODEL), lambda i: (0, 0)),
                  pl.BlockSpec((D_MODEL, LANE), lambda i: (0, 0)),
                  pl.BlockSpec((1, LANE), lambda i: (0, 0))],
        out_specs=[row, row, small, small],
        out_shape=[jax.ShapeDtypeStruct((m, D_MODEL), F32), jax.ShapeDtypeStruct((m, D_MODEL), F32),
                   jax.ShapeDtypeStruct((m, LANE), I32), jax.ShapeDtypeStruct((m, LANE), F32)],
        compiler_params=_cparams(("parallel",)),
        name="outproj",
    )(merged, w_out, x, mod, mod, mod, g_norm2.reshape(1, D_MODEL), wr, br)


GATHER_ROWS = 256


def _row_copy(src_ref, o_ref, sem, src_row, dst_row):
    return pltpu.make_async_copy(src_ref.at[pl.ds(src_row, 1)], o_ref.at[pl.ds(dst_row, 1)], sem)


def _gather_kernel(live_ref, idx_ref, src_ref, o_ref, *scratch):
    sem = scratch[-1]
    dst = scratch[0] if len(scratch) == 2 else o_ref
    n = o_ref.shape[0]
    live = live_ref[pl.program_id(0)] != 0

    @pl.when(live)
    def _():
        def issue(r, carry):
            _row_copy(src_ref, dst, sem, idx_ref[0, r], r).start()
            return carry

        lax.fori_loop(0, n, issue, 0, unroll=8)
        pltpu.make_async_copy(src_ref.at[pl.ds(0, n)], dst, sem).wait()
        if dst is not o_ref:
            o_ref[...] = dst[...].astype(o_ref.dtype)

    @pl.when(jnp.logical_not(live))
    def _():
        o_ref[...] = jnp.zeros_like(o_ref)


def _gather_rows(src, idx, live, out_dtype):
    n_valid, w = idx.shape[0], src.shape[1]
    n_steps = -(-n_valid // GATHER_ROWS)
    n = n_steps * GATHER_ROWS
    idx = jnp.pad(idx, (0, n - n_valid))
    staging = [] if out_dtype == src.dtype else [pltpu.VMEM((GATHER_ROWS, w), src.dtype)]
    grid_spec = pltpu.PrefetchScalarGridSpec(
        num_scalar_prefetch=1, grid=(n_steps,),
        in_specs=[pl.BlockSpec((None, 1, GATHER_ROWS), lambda i, lv: (i, 0, 0), memory_space=pltpu.SMEM),
                  pl.BlockSpec(memory_space=pl.ANY)],
        out_specs=pl.BlockSpec((GATHER_ROWS, w), lambda i, lv: (i, 0)),
        scratch_shapes=staging + [pltpu.SemaphoreType.DMA(())])
    out = pl.pallas_call(
        _gather_kernel, grid_spec=grid_spec,
        out_shape=jax.ShapeDtypeStruct((n, w), out_dtype),
        compiler_params=pltpu.CompilerParams(dimension_semantics=("arbitrary",), vmem_limit_bytes=VMEM_LIMIT,
                                             disable_bounds_checks=True),
        name="gather_rows",
    )(live, idx.reshape(n_steps, 1, GATHER_ROWS), src)
    return out if n == n_valid else out[:n_valid]


MOE_GROUP = 1536
MOE_SUB = 256


def _moe_kernel(ge_ref, ns_ref, x_ref, wg_ref, wu_ref, wd_ref, bg_ref, bu_ref, bd_ref, o_ref, h_scr, *, n_f, tf):
    g_id = pl.program_id(0)
    j = pl.program_id(1)
    ns = ns_ref[g_id]
    del ge_ref
    n_rows = o_ref.shape[0]

    for k in range(1, n_rows // MOE_SUB + 1):
        rows = k * MOE_SUB

        @pl.when((ns == k) & (j < n_f))
        def _(rows=rows):
            x = x_ref[0:rows, :]
            g = jnp.dot(x, wg_ref[...].astype(BF16), preferred_element_type=F32) + bg_ref[...]
            u = jnp.dot(x, wu_ref[...].astype(BF16), preferred_element_type=F32) + bu_ref[...]
            gate = jnp.minimum(g, SWIGLU_LIMIT)
            up = jnp.clip(u, -SWIGLU_LIMIT, SWIGLU_LIMIT)
            h_scr[j, 0:rows, :] = ((up + 1.0) * gate * _sigmoid(SWIGLU_ALPHA * gate)).astype(BF16)

        @pl.when((ns == k) & (j >= n_f))
        def _(rows=rows):
            acc = jnp.dot(h_scr[0, 0:rows, :], wd_ref[0:tf, :].astype(BF16), preferred_element_type=F32)
            for kk in range(1, n_f):
                acc = acc + jnp.dot(h_scr[kk, 0:rows, :], wd_ref[kk * tf:(kk + 1) * tf, :].astype(BF16),
                                    preferred_element_type=F32)
            o_ref[0:rows, :] = acc + bd_ref[...]
            if rows < n_rows:
                o_ref[rows:, :] = jnp.zeros((n_rows - rows, o_ref.shape[1]), F32)

    @pl.when((ns == 0) & (j >= n_f))
    def _():
        o_ref[...] = jnp.zeros_like(o_ref)


def _moe_ffn(xs, grp_e, n_sub, w_gu, b_gu, w_dn, b_dn, tf):
    n_rows = xs.shape[0]
    n_groups = n_rows // MOE_GROUP
    n_f = D_FF // tf
    tn = tf
    n_o = D_MODEL // tn
    bgu = b_gu.reshape(N_EXPERTS, 1, 2 * D_FF)
    bdn = b_dn.reshape(N_EXPERTS, 1, D_MODEL)

    def jf(g, j, ns):
        return jnp.where(ns[g] > 0, jnp.minimum(j, n_f - 1), 0)

    def jo(g, j, ns):
        return jnp.where(ns[g] > 0, jnp.maximum(j - n_f, 0), 0)

    grid_spec = pltpu.PrefetchScalarGridSpec(
        num_scalar_prefetch=2, grid=(n_groups, n_f + n_o),
        in_specs=[pl.BlockSpec((MOE_GROUP, D_MODEL), lambda g, j, ge, ns: (jnp.where(ns[g] > 0, g, 0), 0)),
                  pl.BlockSpec((None, None, D_MODEL, tf), lambda g, j, ge, ns: (0, ge[g], 0, jf(g, j, ns))),
                  pl.BlockSpec((None, None, D_MODEL, tf), lambda g, j, ge, ns: (0, ge[g], 0, n_f + jf(g, j, ns))),
                  pl.BlockSpec((None, None, D_FF, tn), lambda g, j, ge, ns: (0, ge[g], 0, jo(g, j, ns))),
                  pl.BlockSpec((None, 1, tf), lambda g, j, ge, ns: (ge[g], 0, jf(g, j, ns))),
                  pl.BlockSpec((None, 1, tf), lambda g, j, ge, ns: (ge[g], 0, n_f + jf(g, j, ns))),
                  pl.BlockSpec((None, 1, tn), lambda g, j, ge, ns: (ge[g], 0, jo(g, j, ns)))],
        out_specs=pl.BlockSpec((MOE_GROUP, tn), lambda g, j, ge, ns: (g, jnp.maximum(j - n_f, 0))),
        scratch_shapes=[pltpu.VMEM((n_f, MOE_GROUP, tf), BF16)])
    return pl.pallas_call(
        functools.partial(_moe_kernel, n_f=n_f, tf=tf), grid_spec=grid_spec,
        out_shape=jax.ShapeDtypeStruct((n_rows, D_MODEL), F32),
        compiler_params=pltpu.CompilerParams(dimension_semantics=("arbitrary", "arbitrary"),
                                             vmem_limit_bytes=MOE_VMEM_LIMIT),
        name="moe_ffn",
    )(grp_e, n_sub, xs, w_gu, w_gu, w_dn, bgu, bgu, bdn)


def _final_kernel(x1_ref, yg_ref, tg_ref, g2_ref, gf_ref, o_ref):
    tg = tg_ref[...]
    ffn = jnp.zeros_like(x1_ref)
    for kk in range(TOP_K):
        ffn = ffn + tg[:, kk:kk + 1] * yg_ref[kk]
    x2 = x1_ref[...] + g2_ref[...] * ffn
    ms = jnp.mean(x2 * x2, axis=-1, keepdims=True)
    o_ref[...] = x2 * lax.rsqrt(ms + NORM_EPS) * gf_ref[...]


def _final(x1, yg, row0, tg, mod, g_final, rows_per_mod, tm):
    m = x1.shape[0]
    blk0 = row0 // tm
    mod_spec = _mod_spec(mod, MOD_GATE2, rows_per_mod, tm, 1)
    row = pl.BlockSpec((tm, D_MODEL), lambda i: (i, 0))
    return pl.pallas_call(
        _final_kernel,
        grid=(m // tm,),
        in_specs=[row, pl.BlockSpec((TOP_K, tm, D_MODEL), lambda i: (0, blk0 + i, 0)),
                  pl.BlockSpec((tm, LANE), lambda i: (i, 0)), mod_spec,
                  pl.BlockSpec((1, D_MODEL), lambda i: (0, 0))],
        out_specs=row,
        out_shape=jax.ShapeDtypeStruct((m, D_MODEL), F32),
        compiler_params=_cparams(("parallel",)),
        name="final",
    )(x1, yg, tg, mod, g_final.reshape(1, D_MODEL))


def _route(topi):
    tt = topi.shape[0]
    gs, sub = MOE_GROUP, MOE_SUB
    sel = jnp.sum((topi[:, :, None] == jnp.arange(N_EXPERTS, dtype=I32)[None, None, :]).astype(I32), axis=1)
    counts = jnp.sum(sel, axis=0)
    grp_per_e = (counts + gs - 1) // gs
    grp_end = jnp.cumsum(grp_per_e)
    grp_start = grp_end - grp_per_e
    rank = jnp.cumsum(sel, axis=0) - sel
    dest = jnp.take_along_axis((grp_start * gs)[None, :] + rank, topi, axis=1)
    n_groups = -(-(tt * TOP_K + N_EXPERTS * (gs - 1)) // gs)
    n_rows = n_groups * gs
    tok = jnp.repeat(jnp.arange(tt, dtype=I32), TOP_K)
    row_tok = (jnp.arange(n_rows, dtype=I32) % tt).at[dest.reshape(-1)].set(tok)
    g_ids = jnp.arange(n_groups, dtype=I32)
    grp_e = jnp.minimum(jnp.searchsorted(grp_end, g_ids, side="right"), N_EXPERTS - 1).astype(I32)
    rows_in = jnp.clip(counts[grp_e] - (g_ids - grp_start[grp_e]) * gs, 0, gs)
    rows_in = jnp.where(g_ids < grp_end[-1], rows_in, 0)
    n_sub = ((rows_in + sub - 1) // sub).astype(I32)
    per = gs // sub
    live = (jnp.arange(n_groups * per, dtype=I32) % per < jnp.repeat(n_sub, per)).astype(I32)
    return dest, row_tok, grp_e, n_sub, live


def kernel(x_prompt, x_sample, c_prompt, c_sample, cache_k, cache_v, cache_kidx, state_gdn, state_conv, page_table,
           w_ada, b_ada, g_norm1, g_norm2, g_final, w_in, gdn_conv_w, gdn_a_log, gdn_dt_bias, gdn_norm_w,
           w_o_attn, w_o_gdn, w_out, w_router, b_router, w_gu, b_gu, w_dn, b_dn):
    bsz, t, d = x_prompt.shape
    db, ts, _ = x_sample.shape
    depth = w_ada.shape[0]
    assert depth == 1 and d == D_MODEL
    past = page_table.shape[1] * PAGE_SIZE
    conv_dim = gdn_conv_w.shape[-1]

    n_c = bsz + db
    n_cp = -(-n_c // 8) * 8
    c_all = jnp.pad(jnp.concatenate([c_prompt, c_sample], axis=0), ((0, n_cp - n_c), (0, 0)))
    mod = _ada(c_all, w_ada, b_ada[0])
    mods = {"p": mod[:bsz].reshape(bsz, 1, 6 * d),
            "s": jnp.repeat(mod[bsz:n_c], ts, axis=0).reshape(1, db * ts, 6 * d)}

    w_r = _reorder_w_in(w_in[0])
    w_oa = w_o_attn[0].astype(BF16)
    w_og = w_o_gdn[0].astype(BF16)
    w_o = w_out[0].astype(BF16)

    xp = x_prompt.reshape(bsz * t, d)
    xs = x_sample.reshape(db * ts, d)
    tm_p = min(512, t)
    tm_s = db * ts

    tabs_p = _rot_tables(jnp.arange(t, dtype=I32))
    tabs_s = _rot_tables(jnp.tile(past + jnp.arange(ts, dtype=I32), db))
    p32_p, p16_p = _inproj(xp, g_norm1[0], mods["p"], w_r, tabs_p, t, tm_p)
    p32_s, _ = _inproj(xs, g_norm1[0], mods["s"], w_r, tabs_s, db * ts, tm_s)

    attn_p = _attn_prompt(p32_p, p16_p, bsz, t)
    attn_s = _attn_sample(p32_s, cache_k, cache_v, cache_kidx, page_table, ts)

    conv0_p = jnp.zeros((bsz, 8, conv_dim), F32)
    s0_p = jnp.zeros((bsz, GDN_V_HEADS, GDN_DK, GDN_DV), F32)
    gdn_p, sfin_p = _gdn(p32_p, conv0_p, s0_p, gdn_conv_w[0], gdn_a_log[0], gdn_dt_bias[0], gdn_norm_w[0], bsz, t)
    gdn_s, sfin_s = _gdn_sample(p32_s, state_conv[0], state_gdn[0], gdn_conv_w[0], gdn_a_log[0], gdn_dt_bias[0],
                                gdn_norm_w[0], db, ts)

    outs = {}
    for name, attn_o, gdn_o, p32, x, rows_per_mod, tm in (("p", attn_p, gdn_p, p32_p, xp, t, min(256, t)),
                                                          ("s", attn_s, gdn_s, p32_s, xs, db * ts, min(256, db * ts))):
        merged = _merge(attn_o, gdn_o, w_oa, w_og, p32, min(512, rows_per_mod))
        outs[name] = _outproj(merged, w_o, x, mods[name], g_norm2[0], w_router[0], b_router[0], rows_per_mod, tm)

    h2 = jnp.concatenate([outs["p"][1], outs["s"][1]], axis=0)
    topi = jnp.concatenate([outs["p"][2], outs["s"][2]], axis=0)[:, :TOP_K]
    dest, row_tok, grp_e, n_sub, live = _route(topi)
    xs = _gather_rows(h2, row_tok, live, BF16)
    yb = _moe_ffn(xs, grp_e, n_sub, w_gu, b_gu[0], w_dn, b_dn[0], tf=256)
    live_y = jnp.ones((-(-dest.size // GATHER_ROWS),), I32)
    yg = _gather_rows(yb, dest.T.reshape(-1), live_y, F32).reshape(TOP_K, h2.shape[0], d)

    n_p = bsz * t
    y_p = _final(outs["p"][0], yg, 0, outs["p"][3], mods["p"], g_final, t, min(256, t))
    y_s = _final(outs["s"][0], yg, n_p, outs["s"][3], mods["s"], g_final, db * ts, min(256, db * ts))

    kv = N_KV_HEADS * HEAD_DIM
    new_k_p = p32_p[:, C_AK:C_AK + kv].reshape(1, bsz, t, N_KV_HEADS, HEAD_DIM)
    new_v_p = p32_p[:, C_AV:C_AV + kv].reshape(1, bsz, t, N_KV_HEADS, HEAD_DIM)
    new_ki_p = p32_p[:, C_MISC:C_MISC + IDX_DIM].reshape(1, bsz, t, IDX_DIM)
    new_k_s = p32_s[:, C_AK:C_AK + kv].reshape(1, db, ts, N_KV_HEADS, HEAD_DIM)
    new_v_s = p32_s[:, C_AV:C_AV + kv].reshape(1, db, ts, N_KV_HEADS, HEAD_DIM)
    new_ki_s = p32_s[:, C_MISC:C_MISC + IDX_DIM].reshape(1, db, ts, IDX_DIM)
    conv_p = p32_p.reshape(bsz, t, PROJ_W)[:, t - (CONV_W - 1):, C_GQ:C_GQ + conv_dim][None]
    pre_s = p32_s[:, C_GQ:C_GQ + conv_dim].reshape(db, ts, conv_dim)
    conv_s = jnp.concatenate([state_conv[0], pre_s], axis=1)[:, ts:][None]
    return (y_p.reshape(bsz, t, d), y_s.reshape(db, ts, d), new_k_p, new_v_p, new_ki_p, conv_p, sfin_p[None],
            new_k_s, new_v_s, new_ki_s, conv_s, sfin_s[None])
```

```python
import functools

import jax
import jax.numpy as jnp
from jax import lax
from jax.experimental import pallas as pl
from jax.experimental.pallas import tpu as pltpu

F32 = jnp.float32
BF16 = jnp.bfloat16
I32 = jnp.int32

D_MODEL = 2048
N_HEADS = 16
N_KV_HEADS = 4
HEAD_DIM = 128
ROPE_THETA = 500000.0
ROT_FRACTION_DIV = 4
IDX_HEADS = 16
IDX_DIM = 64
TOPK_MAX = 256
GDN_QK_HEADS = 8
GDN_V_HEADS = 16
GDN_DK = 128
GDN_DV = 128
CONV_W = 4
N_EXPERTS = 32
TOP_K = 4
D_FF = D_MODEL
SWIGLU_LIMIT = 7.0
SWIGLU_ALPHA = 1.702
NORM_EPS = 1e-6
PAGE_SIZE = 128
LANE = 128

C_AQ, C_AK, C_AV, C_IQ = 0, 2048, 2560, 3072
C_GQ, C_GK, C_GV, C_GZ, C_GL = 4096, 5120, 6144, 8192, 10240
C_MISC = 14336
L_IW, L_GB, L_GA = 64, 80, 96
PROJ_TN = 512
PROJ_W = 14848
MISC_BLK = C_MISC // LANE

VMEM_LIMIT = 48 * 1024 * 1024
MOE_VMEM_LIMIT = 56 * 1024 * 1024
NEG_BIG = -1e30
NEG_BF16 = -(2.0 ** 100)
INT_MIN = -2147483648

NT_DIMS = (((1,), (1,)), ((), ()))


def _cparams(sem):
    return pltpu.CompilerParams(dimension_semantics=sem, vmem_limit_bytes=VMEM_LIMIT)


def _split_bf16(a):
    hi = a.astype(BF16)
    lo = (a - hi.astype(F32)).astype(BF16)
    return hi, lo


def _dot3(a, b, dims=None):
    ah, al = _split_bf16(a)
    bh, bl = _split_bf16(b)
    if dims is None:
        d = lambda x, y: jnp.dot(x, y, preferred_element_type=F32)
    else:
        d = lambda x, y: lax.dot_general(x, y, dims, preferred_element_type=F32)
    return d(ah, bh) + d(al, bh) + d(ah, bl)


def _sigmoid(x):
    return 1.0 / (1.0 + jnp.exp(-x))


def _ada_kernel(c_ref, w_ref, b_ref, o_ref):
    c = c_ref[...]
    a = (c * _sigmoid(c)).astype(BF16)
    o_ref[...] = jnp.dot(a, w_ref[...].astype(BF16), preferred_element_type=F32) + b_ref[...]


def _ada(c, w, b):
    m, k = c.shape
    n = w.shape[2]
    tn = 1024
    return pl.pallas_call(
        _ada_kernel,
        grid=(n // tn,),
        in_specs=[pl.BlockSpec((m, k), lambda j: (0, 0)),
                  pl.BlockSpec((None, k, tn), lambda j: (0, 0, j)),
                  pl.BlockSpec((1, tn), lambda j: (0, j))],
        out_specs=pl.BlockSpec((m, tn), lambda j: (0, j)),
        out_shape=jax.ShapeDtypeStruct((m, n), F32),
        compiler_params=_cparams(("arbitrary",)),
        name="ada",
    )(c, w, b.reshape(1, n))


def _rot_slab(x, tabs, half):
    a, b, c = tabs
    return x * a + pltpu.roll(x, LANE - half, 1) * b + pltpu.roll(x, half, 1) * c


def _inproj_kernel(x_ref, g_ref, sc_ref, sh_ref, w_ref, tab_ref, o32_ref, o16_ref, h_scr):
    j = pl.program_id(1)

    @pl.when(j == 0)
    def _():
        x = x_ref[...]
        ms = jnp.mean(x * x, axis=-1, keepdims=True)
        y = x * lax.rsqrt(ms + NORM_EPS) * g_ref[...]
        h_scr[...] = (y * (1.0 + sc_ref[...]) + sh_ref[...]).astype(BF16)

    acc = jnp.dot(h_scr[...], w_ref[...], preferred_element_type=F32)
    n_slab = PROJ_TN // LANE

    def store(fn):
        for s in range(n_slab):
            v = fn(acc[:, s * LANE:(s + 1) * LANE])
            o32_ref[:, s * LANE:(s + 1) * LANE] = v
            o16_ref[:, s * LANE:(s + 1) * LANE] = v.astype(BF16)

    rot_head = j < 5
    rot_idx = ((j >= 6) & (j < 8)) | (j == C_MISC // PROJ_TN)

    @pl.when(rot_head)
    def _():
        tabs = (tab_ref[0], tab_ref[1], tab_ref[2])
        store(lambda v: _rot_slab(v, tabs, HEAD_DIM // ROT_FRACTION_DIV // 2))

    @pl.when(rot_idx)
    def _():
        tabs = (tab_ref[0], tab_ref[1], tab_ref[2])
        store(lambda v: _rot_slab(v, tabs, IDX_DIM // ROT_FRACTION_DIV // 2))

    @pl.when(jnp.logical_not(rot_head | rot_idx))
    def _():
        store(lambda v: v)


def _tab_type(j):
    return jnp.where(j < 6, 0, jnp.where(j < 28, 1, 2))


MOD_SHIFT1, MOD_SCALE1, MOD_GATE1, MOD_SHIFT2, MOD_SCALE2, MOD_GATE2 = range(6)


def _mod_spec(mod, col, rows_per_mod, tm, grid_rank):
    tiles_per_mod = rows_per_mod // tm
    if mod.shape[1] == 1:
        blk, pick = (None, 1, D_MODEL), lambda i: (i // tiles_per_mod, 0, col)
    else:
        blk, pick = (None, tm, D_MODEL), lambda i: (i // tiles_per_mod, i % tiles_per_mod, col)
    return pl.BlockSpec(blk, (lambda i, j: pick(i)) if grid_rank == 2 else pick)


def _inproj(x, g_norm, mod, w_r, tabs, rows_per_mod, tm):
    m = x.shape[0]
    tt = tabs.shape[2]
    n_t = tt // tm
    return pl.pallas_call(
        _inproj_kernel,
        grid=(m // tm, PROJ_W // PROJ_TN),
        in_specs=[pl.BlockSpec((tm, D_MODEL), lambda i, j: (i, 0)),
                  pl.BlockSpec((1, D_MODEL), lambda i, j: (0, 0)),
                  _mod_spec(mod, MOD_SCALE1, rows_per_mod, tm, 2), _mod_spec(mod, MOD_SHIFT1, rows_per_mod, tm, 2),
                  pl.BlockSpec((D_MODEL, PROJ_TN), lambda i, j: (0, j)),
                  pl.BlockSpec((None, 3, tm, LANE), lambda i, j: (_tab_type(j), 0, i % n_t, 0))],
        out_specs=[pl.BlockSpec((tm, PROJ_TN), lambda i, j: (i, j)),
                   pl.BlockSpec((tm, PROJ_TN), lambda i, j: (i, j))],
        out_shape=[jax.ShapeDtypeStruct((m, PROJ_W), F32), jax.ShapeDtypeStruct((m, PROJ_W), BF16)],
        scratch_shapes=[pltpu.VMEM((tm, D_MODEL), BF16)],
        compiler_params=_cparams(("parallel", "arbitrary")),
        name="inproj",
    )(x, g_norm.reshape(1, D_MODEL), mod, mod, w_r, tabs)


def _rot_tables(pos):
    tt = pos.shape[0]
    posf = pos.astype(F32)

    def tab(rot, period, scale):
        half = rot // 2
        inv = ROPE_THETA ** (-jnp.arange(half, dtype=F32) * (2.0 / rot))
        ang = posf[:, None] * inv[None, :]
        cos, sin = jnp.cos(ang), jnp.sin(ang)
        zh = jnp.zeros((tt, half), F32)
        rest = period - rot
        a = jnp.concatenate([cos, cos, jnp.ones((tt, rest), F32)], axis=1) * scale
        b = jnp.concatenate([-sin, zh, jnp.zeros((tt, rest), F32)], axis=1) * scale
        c = jnp.concatenate([zh, sin, jnp.zeros((tt, rest), F32)], axis=1) * scale
        return a, b, c

    head = jnp.stack([jnp.tile(t, (1, LANE // HEAD_DIM)) for t in tab(HEAD_DIM // ROT_FRACTION_DIV, HEAD_DIM, 1.0)])
    idxq = jnp.stack([jnp.tile(t, (1, LANE // IDX_DIM))
                      for t in tab(IDX_DIM // ROT_FRACTION_DIV, IDX_DIM, IDX_DIM ** -0.5)])
    ka, kb, kc = tab(IDX_DIM // ROT_FRACTION_DIV, IDX_DIM, 1.0)
    rest = LANE - IDX_DIM
    lane = jnp.arange(rest)
    tail_a = jnp.where(lane < (L_GB - L_IW), IDX_HEADS ** -0.5, 1.0).astype(F32)
    ma = jnp.concatenate([ka, jnp.broadcast_to(tail_a, (tt, rest))], axis=1)
    mb = jnp.concatenate([kb, jnp.zeros((tt, rest), F32)], axis=1)
    mc = jnp.concatenate([kc, jnp.zeros((tt, rest), F32)], axis=1)
    misc = jnp.stack([ma, mb, mc])
    return jnp.stack([head, idxq, misc])


def _reorder_w_in(w_in):
    offs = {}
    o = 0
    for name, wdt in (("aq", 2048), ("ak", 512), ("av", 512), ("iq", 1024), ("ik", 64), ("iw", 16), ("gq", 1024),
                      ("gk", 1024), ("gv", 2048), ("gz", 2048), ("gb", 16), ("ga", 16), ("gl", 4096)):
        offs[name] = (o, wdt)
        o += wdt

    def seg(name):
        s, wdt = offs[name]
        return w_in[:, s:s + wdt]

    k = w_in.shape[0]
    misc = jnp.concatenate([seg("ik"), seg("iw"), seg("gb"), seg("ga"), jnp.zeros((k, LANE - 112), w_in.dtype)], axis=1)
    parts = [seg("aq"), seg("ak"), seg("av"), seg("iq"), seg("gq"), seg("gk"), seg("gv"), seg("gz"), seg("gl"), misc,
             jnp.zeros((k, PROJ_W - C_MISC - LANE), w_in.dtype)]
    return jnp.concatenate(parts, axis=1).astype(BF16)


def _order_key(score):
    bits = pltpu.bitcast(score, I32)
    return bits ^ ((bits >> 31) & 0x7FFFFFFF)


def _kth_largest_key(count_ge, n_rows, n_sel):
    def body(i, t):
        bit = lax.shift_left(jnp.int32(1), 31 - i)
        cand = t ^ bit
        return jnp.where(count_ge(cand) >= n_sel, cand, t)

    return lax.fori_loop(0, 32, body, jnp.full((n_rows, 1), INT_MIN, I32))


RADIX_BITS = 4


def _kth_largest_key_radix(keys, n_sel):
    n_rows = keys.shape[0]
    n_cand = (1 << RADIX_BITS) - 1
    digit = lax.broadcasted_iota(I32, (n_cand, n_rows, 1), 0) + 1
    t_u = jnp.zeros((n_rows, 1), I32)
    for step in range(32 // RADIX_BITS):
        shift = 32 - RADIX_BITS * (step + 1)
        cand = (t_u[None] | lax.shift_left(digit, shift)) ^ INT_MIN
        cnt = jnp.sum(jnp.where(keys[None] >= cand, 1.0, 0.0), axis=2, keepdims=True)
        keep = jnp.sum(jnp.where(cnt >= n_sel, 1, 0), axis=0)
        t_u = t_u | lax.shift_left(keep, shift)
    return t_u ^ INT_MIN


def _attn_prompt_kernel(q_ref, iq_ref, mq_ref, k_ref, v_ref, mk_ref, o_ref, key_scr, *, tq, kc, n_sel):
    qb = pl.program_id(1)
    q0 = qb * tq
    nkc = (q0 + tq + kc - 1) // kc
    qpos = q0 + lax.broadcasted_iota(I32, (tq, 1), 0)
    kiota = lax.broadcasted_iota(I32, (1, kc), 1)
    wq = mq_ref[...]

    def score_body(c, carry):
        ks = pl.multiple_of(c * kc, kc)
        kt = mk_ref[pl.ds(ks, kc), :][:, :IDX_DIM]
        acc = jnp.zeros((tq, kc), F32)
        for h in range(IDX_HEADS):
            s = lax.dot_general(iq_ref[:, h * IDX_DIM:(h + 1) * IDX_DIM], kt, NT_DIMS, preferred_element_type=F32)
            acc = acc + wq[:, L_IW + h:L_IW + h + 1] * jnp.maximum(s, 0.0)
        acc = jnp.where(ks + kiota <= qpos, acc, -jnp.inf)
        key_scr[c] = _order_key(acc)
        return carry

    lax.fori_loop(0, nkc, score_body, 0)

    def count(cmp, cand):
        def body(c, acc):
            m = jnp.where(cmp(key_scr[c], cand), 1.0, 0.0)
            for s in range(kc // LANE):
                acc = acc + m[:, s * LANE:(s + 1) * LANE]
            return acc

        acc = lax.fori_loop(0, nkc, body, jnp.zeros((tq, LANE), F32))
        return jnp.sum(acc, axis=1, keepdims=True)

    count_ge = functools.partial(count, lambda a, b: a >= b)
    thr = _kth_largest_key(count_ge, tq, float(n_sel))

    n_tie_ok = float(n_sel) - count(lambda a, b: a > b, thr)

    @pl.when(jnp.max(count_ge(thr)) > float(n_sel))
    def _():
        before = (lax.broadcasted_iota(I32, (kc, kc), 0) < lax.broadcasted_iota(I32, (kc, kc), 1))
        before = jnp.where(before, 1.0, 0.0).astype(BF16)

        def demote(c, seen):
            key = key_scr[c]
            eq = key == thr
            eqf = jnp.where(eq, 1.0, 0.0)
            rank = seen + jnp.dot(eqf.astype(BF16), before, preferred_element_type=F32)
            key_scr[c] = jnp.where(eq & (rank >= n_tie_ok), INT_MIN, key)
            return seen + jnp.sum(eqf, axis=1, keepdims=True)

        lax.fori_loop(0, nkc, demote, jnp.zeros((tq, 1), F32))

    group = N_HEADS // N_KV_HEADS
    scale = HEAD_DIM ** -0.5
    qss = []
    for n in range(N_KV_HEADS):
        qs = jnp.concatenate(
            [q_ref[:, (n * group + g) * HEAD_DIM:(n * group + g + 1) * HEAD_DIM] for g in range(group)], axis=0)
        qss.append((qs.astype(F32) * scale).astype(BF16))

    def attn_body(c, carry):
        ks = pl.multiple_of(c * kc, kc)
        sel = (key_scr[c] >= thr) & (ks + kiota <= qpos)
        bias = jnp.where(sel, 0.0, NEG_BF16).astype(BF16)[None]
        new = []
        for n in range(N_KV_HEADS):
            m_i, acc = carry[n]
            kn = k_ref[pl.ds(ks, kc), n * HEAD_DIM:(n + 1) * HEAD_DIM]
            vn = jnp.concatenate([v_ref[pl.ds(ks, kc), n * HEAD_DIM:(n + 1) * HEAD_DIM], ones_blk], axis=1)
            s = lax.dot_general(qss[n], kn, NT_DIMS, preferred_element_type=F32).astype(BF16)
            s = (s.reshape(group, tq, kc) + bias).reshape(group * tq, kc)
            m_new = jnp.maximum(m_i, jnp.max(s, axis=1, keepdims=True).astype(F32))
            alpha = jnp.exp(m_i - m_new)
            p = jnp.exp(s - m_new.astype(BF16))
            acc = alpha * acc + jnp.dot(p, vn, preferred_element_type=F32)
            new.append((m_new, acc))
        return tuple(new)

    ones_blk = jnp.ones((kc, HEAD_DIM), BF16)
    init = tuple((jnp.full((group * tq, 1), NEG_BF16, F32), jnp.zeros((group * tq, 2 * HEAD_DIM), F32))
                 for _ in range(N_KV_HEADS))
    fin = lax.fori_loop(0, nkc, attn_body, init)
    for n in range(N_KV_HEADS):
        _, acc_f = fin[n]
        out = acc_f[:, :HEAD_DIM] / acc_f[:, HEAD_DIM:HEAD_DIM + 1]
        for g in range(group):
            o_ref[:, (n * group + g) * HEAD_DIM:(n * group + g + 1) * HEAD_DIM] = out[g * tq:(g + 1) * tq].astype(BF16)


def _attn_prompt(p32, p16, bsz, t, tq=256, kc=512):
    nq = t // tq
    kc = min(kc, t)
    n_sel = min(TOPK_MAX, t // 4)
    kern = functools.partial(_attn_prompt_kernel, tq=tq, kc=kc, n_sel=n_sel)
    return pl.pallas_call(
        kern,
        grid=(bsz, nq),
        in_specs=[pl.BlockSpec((tq, N_HEADS * HEAD_DIM), lambda b, i: (b * nq + i, 0)),
                  pl.BlockSpec((tq, IDX_HEADS * IDX_DIM), lambda b, i: (b * nq + i, C_IQ // (IDX_HEADS * IDX_DIM))),
                  pl.BlockSpec((tq, LANE), lambda b, i: (b * nq + i, MISC_BLK)),
                  pl.BlockSpec((t, N_KV_HEADS * HEAD_DIM), lambda b, i: (b, C_AK // (N_KV_HEADS * HEAD_DIM))),
                  pl.BlockSpec((t, N_KV_HEADS * HEAD_DIM), lambda b, i: (b, C_AV // (N_KV_HEADS * HEAD_DIM))),
                  pl.BlockSpec((t, LANE), lambda b, i: (b, MISC_BLK))],
        out_specs=pl.BlockSpec((tq, N_HEADS * HEAD_DIM), lambda b, i: (b * nq + i, 0)),
        out_shape=jax.ShapeDtypeStruct((bsz * t, N_HEADS * HEAD_DIM), BF16),
        scratch_shapes=[pltpu.VMEM((t // kc, tq, kc), I32)],
        compiler_params=_cparams(("parallel", "arbitrary")),
        name="attn_prompt",
    )(p16, p16, p32, p16, p16, p16)


def _attn_sample_kernel(pt_ref, iq_ref, wc_ref, q_ref, kn_ref, vn_ref, ikn_ref, *rest, n_pages, n_sel, past):
    kidx_refs = rest[:n_pages]
    k_refs = rest[n_pages:2 * n_pages]
    v_refs = rest[2 * n_pages:3 * n_pages]
    o_ref, sc_scr, selx_scr, s_scr = rest[3 * n_pages:]
    del pt_ref
    rows, nkv = 8, N_KV_HEADS
    n_keys = (n_pages + 1) * PAGE_SIZE
    pw = PAGE_SIZE * nkv
    iq = iq_ref[...].astype(BF16)
    wc = wc_ref[...]

    def idx_score(kt, dims):
        s = lax.dot_general(iq, kt.astype(BF16), dims, preferred_element_type=F32)
        x = jnp.maximum(s, 0.0) * wc
        return jnp.sum(x.reshape(IDX_HEADS, rows, PAGE_SIZE), axis=0)

    nn_dims = (((1,), (0,)), ((), ()))
    for p in range(n_pages):
        sc_scr[:, p * PAGE_SIZE:(p + 1) * PAGE_SIZE] = idx_score(kidx_refs[p][...], nn_dims)
    sc_scr[:, n_pages * PAGE_SIZE:] = idx_score(ikn_ref[...], NT_DIMS)

    qpos = past + lax.broadcasted_iota(I32, (rows, 1), 0)
    kpos = lax.broadcasted_iota(I32, (1, n_keys), 1)
    causal = kpos <= qpos
    keys = _order_key(jnp.where(causal, sc_scr[...], -jnp.inf))

    thr = _kth_largest_key_radix(keys, float(n_sel))
    above = keys > thr
    tied = jnp.where(keys == thr, 1.0, 0.0)
    n_tie_ok = float(n_sel) - jnp.sum(jnp.where(above, 1.0, 0.0), axis=1, keepdims=True)
    before = (lax.broadcasted_iota(I32, (PAGE_SIZE, PAGE_SIZE), 0) < lax.broadcasted_iota(I32, (PAGE_SIZE, PAGE_SIZE), 1))
    before = jnp.where(before, 1.0, 0.0).astype(BF16)
    seen = jnp.zeros((rows, 1), F32)
    ranks = []
    for p in range(n_pages + 1):
        tp = tied[:, p * PAGE_SIZE:(p + 1) * PAGE_SIZE]
        ranks.append(seen + jnp.dot(tp.astype(BF16), before, preferred_element_type=F32))
        seen = seen + jnp.sum(tp, axis=1, keepdims=True)
    rank = jnp.concatenate(ranks, axis=1)
    sel = jnp.where((above | ((tied > 0.5) & (rank < n_tie_ok))) & causal, 1.0, 0.0).astype(BF16)
    expand = jnp.where((lax.broadcasted_iota(I32, (PAGE_SIZE, pw), 1) >> 2)
                       == lax.broadcasted_iota(I32, (PAGE_SIZE, pw), 0), 1.0, 0.0).astype(BF16)
    for p in range(n_pages + 1):
        selx_scr[:, p * pw:(p + 1) * pw] = jnp.dot(sel[:, p * PAGE_SIZE:(p + 1) * PAGE_SIZE], expand,
                                                   preferred_element_type=F32)
    n_q = N_HEADS * rows
    own_head = ((lax.broadcasted_iota(I32, (n_q, pw), 0) >> 5)
                == (lax.broadcasted_iota(I32, (n_q, pw), 1) & (nkv - 1))).reshape(N_HEADS, rows, pw)
    q = (q_ref[...] * (HEAD_DIM ** -0.5)).astype(BF16)
    m_parts = []
    for p in range(n_pages + 1):
        kb = (k_refs[p] if p < n_pages else kn_ref)[...].astype(BF16)
        s = lax.dot_general(q, kb, NT_DIMS, preferred_element_type=F32)
        ok = own_head & (selx_scr[:, p * pw:(p + 1) * pw] > 0.5)[None]
        s = jnp.where(ok, s.reshape(N_HEADS, rows, pw), NEG_BIG).reshape(n_q, pw)
        s_scr[:, p * pw:(p + 1) * pw] = s
        m_parts.append(jnp.max(s, axis=1, keepdims=True))
    m = m_parts[0]
    for mp in m_parts[1:]:
        m = jnp.maximum(m, mp)
    l_parts, o_parts = [], []
    for p in range(n_pages + 1):
        vb = (v_refs[p] if p < n_pages else vn_ref)[...].astype(BF16)
        pr = jnp.exp(s_scr[:, p * pw:(p + 1) * pw] - m)
        l_parts.append(jnp.sum(pr, axis=1, keepdims=True))
        o_parts.append(jnp.dot(pr.astype(BF16), vb, preferred_element_type=F32))
    l = l_parts[0]
    acc = o_parts[0]
    for lp, op in zip(l_parts[1:], o_parts[1:]):
        l = l + lp
        acc = acc + op
    o_ref[...] = acc / l


def _attn_sample(p32, cache_k, cache_v, cache_kidx, page_table, ts):
    db, n_pages = page_table.shape
    past = n_pages * PAGE_SIZE
    rows = 8
    group = N_HEADS // N_KV_HEADS
    ps = p32.reshape(db, ts, PROJ_W)
    pad_t = ((0, 0), (0, 0), (0, rows - ts), (0, 0))
    iq = ps[:, :, C_IQ:C_IQ + IDX_HEADS * IDX_DIM].reshape(db, ts, IDX_HEADS, IDX_DIM).transpose(0, 2, 1, 3)
    iq = jnp.pad(iq, pad_t).reshape(db, IDX_HEADS * rows, IDX_DIM)
    wc = ps[:, :, C_MISC + L_IW:C_MISC + L_IW + IDX_HEADS].transpose(0, 2, 1)
    wc = jnp.pad(wc, ((0, 0), (0, 0), (0, rows - ts))).reshape(db, IDX_HEADS * rows, 1)
    q = ps[:, :, :N_HEADS * HEAD_DIM].reshape(db, ts, N_HEADS, HEAD_DIM).transpose(0, 2, 1, 3)
    q = jnp.pad(q, pad_t).reshape(db, N_HEADS * rows, HEAD_DIM)
    pad_k = ((0, 0), (0, PAGE_SIZE - ts), (0, 0))
    pw = PAGE_SIZE * N_KV_HEADS
    kn = jnp.pad(ps[:, :, C_AK:C_AK + N_KV_HEADS * HEAD_DIM], pad_k).reshape(db, pw, HEAD_DIM)
    vn = jnp.pad(ps[:, :, C_AV:C_AV + N_KV_HEADS * HEAD_DIM], pad_k).reshape(db, pw, HEAD_DIM)
    ikn = jnp.pad(ps[:, :, C_MISC:C_MISC + IDX_DIM], pad_k)
    n_sel = min(TOPK_MAX, (past + ts) // 4)
    n_phys = cache_k.shape[1]
    ck = cache_k.reshape(1, n_phys, pw, HEAD_DIM)
    cv = cache_v.reshape(1, n_phys, pw, HEAD_DIM)

    def kv_page(p):
        return pl.BlockSpec((None, None, pw, HEAD_DIM), lambda d, pt, p=p: (0, pt[d * n_pages + p], 0, 0))

    ckt = jnp.swapaxes(cache_kidx, 2, 3)

    def idx_page(p):
        return pl.BlockSpec((None, None, IDX_DIM, PAGE_SIZE), lambda d, pt, p=p: (0, pt[d * n_pages + p], 0, 0))

    def per_d(r, w):
        return pl.BlockSpec((None, r, w), lambda d, pt: (d, 0, 0))

    in_specs = ([per_d(IDX_HEADS * rows, IDX_DIM), per_d(IDX_HEADS * rows, 1), per_d(N_HEADS * rows, HEAD_DIM),
                 per_d(pw, HEAD_DIM), per_d(pw, HEAD_DIM), per_d(PAGE_SIZE, IDX_DIM)]
                + [idx_page(p) for p in range(n_pages)]
                + [kv_page(p) for p in range(n_pages)]
                + [kv_page(p) for p in range(n_pages)])
    n_keys = (n_pages + 1) * PAGE_SIZE
    grid_spec = pltpu.PrefetchScalarGridSpec(
        num_scalar_prefetch=1, grid=(db,), in_specs=in_specs,
        out_specs=per_d(N_HEADS * rows, HEAD_DIM),
        scratch_shapes=[pltpu.VMEM((rows, n_keys), F32), pltpu.VMEM((rows, n_keys * N_KV_HEADS), F32),
                        pltpu.VMEM((N_HEADS * rows, n_keys * N_KV_HEADS), F32)])
    kern = functools.partial(_attn_sample_kernel, n_pages=n_pages, n_sel=n_sel, past=past)
    o = pl.pallas_call(
        kern, grid_spec=grid_spec,
        out_shape=jax.ShapeDtypeStruct((db, N_HEADS * rows, HEAD_DIM), F32),
        compiler_params=_cparams(("arbitrary",)),
        name="attn_sample",
    )(page_table.reshape(-1), iq, wc, q, kn, vn, ikn,
      *([ckt] * n_pages), *([ck] * n_pages), *([cv] * n_pages))
    o = o.reshape(db, N_HEADS, rows, HEAD_DIM)[:, :, :ts].transpose(0, 2, 1, 3)
    return o.reshape(db * ts, N_HEADS * HEAD_DIM).astype(BF16)


def _dot3s(ah, al, bh, bl, dims=None):
    if dims is None:
        d = lambda x, y: jnp.dot(x, y, preferred_element_type=F32)
    else:
        d = lambda x, y: lax.dot_general(x, y, dims, preferred_element_type=F32)
    return d(ah, bh) + d(al, bh) + d(ah, bl)


def _cumsum_rows(sel_bf16, g_rows):
    gb = jnp.broadcast_to(g_rows, (g_rows.shape[0], LANE))
    g_hi, g_lo = _split_bf16(gb)
    g_lo2 = (gb - g_hi.astype(F32) - g_lo.astype(F32)).astype(BF16)
    d = lambda y: jnp.dot(sel_bf16, y, preferred_element_type=F32)
    return d(g_hi) + d(g_lo) + d(g_lo2)


def _l2n(x):
    return x * lax.rsqrt(jnp.sum(x * x, axis=-1, keepdims=True) + NORM_EPS)


def _gate_params(misc, hp, live):
    x = misc + hp[1:2, :]
    softplus = jnp.maximum(x, 0.0) + jnp.log(1.0 + jnp.exp(-jnp.abs(x)))
    g_all = -jnp.exp(hp[0:1, :]) * softplus
    beta_all = _sigmoid(misc)
    if live is not None:
        g_all = jnp.where(live, g_all, 0.0)
        beta_all = jnp.where(live, beta_all, 0.0)
    return g_all, beta_all


def _gdn_kernel(q_ref, k_ref, v_ref, z_ref, m_ref, wq_ref, wk_ref, wv_ref, cq_ref, ck_ref, cv_ref, hp_ref, nw_ref,
                s0_ref, o_ref, sf_ref, s_scr, hq_scr, hk_scr, hv_scr, xc_scr, *, c, nh):
    g_idx = pl.program_id(1)
    tb = pl.program_id(2)
    n_tb = pl.num_programs(2)
    dk, dv = GDN_DK, GDN_DV

    @pl.when(tb == 0)
    def _():
        s_scr[...] = s0_ref[...]
        hq_scr[...] = cq_ref[...]
        hk_scr[...] = ck_ref[...]
        hv_scr[...] = cv_ref[...]

    def conv_silu(x_ref, w_ref, h_scr):
        width = x_ref.shape[1]
        x = x_ref[...]
        xc_scr[0:8, 0:width] = h_scr[...]
        xc_scr[8:8 + c, 0:width] = x
        acc = jnp.zeros((c, width), F32)
        for j in range(CONV_W):
            acc = acc + xc_scr[8 - (CONV_W - 1) + j:8 - (CONV_W - 1) + j + c, 0:width] * w_ref[j:j + 1, :]
        h_scr[...] = x[c - 8:c, :]
        return acc * _sigmoid(acc)

    qa = conv_silu(q_ref, wq_ref, hq_scr)
    ka = conv_silu(k_ref, wk_ref, hk_scr)
    va = conv_silu(v_ref, wv_ref, hv_scr)
    g_all, beta_all = _gate_params(m_ref[...], hp_ref[...], None)
    lane = lax.broadcasted_iota(I32, (1, LANE), 1)
    ri = lax.broadcasted_iota(I32, (c, c), 0)
    ci = lax.broadcasted_iota(I32, (c, c), 1)
    causal = ri >= ci
    strict = ri > ci
    tri = jnp.where(causal, 1.0, 0.0).astype(BF16)
    eye = jnp.where(ri == ci, 1.0, 0.0)

    def same_blk(log_b):
        return (ri >> log_b) == (ci >> log_b)

    heads = range(2 * nh)
    ks = [_l2n(ka[:, hh * dk:(hh + 1) * dk]) for hh in range(nh)]
    qss = [_l2n(qa[:, hh * dk:(hh + 1) * dk]) * (dk ** -0.5) for hh in range(nh)]
    kk_qk = [lax.dot_general(jnp.concatenate([ks[hh], qss[hh]], axis=0).astype(BF16), ks[hh].astype(BF16), NT_DIMS,
                             preferred_element_type=F32) for hh in range(nh)]
    g_cols = [jnp.sum(jnp.where(lane == L_GA + 2 * nh * g_idx + hl, g_all, 0.0), axis=1, keepdims=True)
              for hl in heads]
    betas = [jnp.sum(jnp.where(lane == L_GB + 2 * nh * g_idx + hl, beta_all, 0.0), axis=1, keepdims=True)
             for hl in heads]
    gcs = [_cumsum_rows(tri, g) for g in g_cols]
    decays = [jnp.exp(jnp.where(causal, gc - gc.T, -jnp.inf)) for gc in gcs]
    a_mats = [jnp.where(strict, kk_qk[hl // 2][0:c] * betas[hl] * decays[hl], 0.0) for hl in heads]
    nms = [jnp.where(same_blk(3), -a, 0.0) for a in a_mats]
    pms = [eye + n for n in nms]
    for _ in range(2):
        nms = [_dot3(n, n) for n in nms]
        pms = [p + _dot3(p, n) for p, n in zip(pms, nms)]
    for lb in range(3, (c - 1).bit_length()):
        join = same_blk(lb + 1) & jnp.logical_not(same_blk(lb))
        t1s = [_dot3(jnp.where(join, a, 0.0), p) for a, p in zip(a_mats, pms)]
        pms = [p - _dot3(p, t1) for p, t1 in zip(pms, t1s)]
    us = [_dot3(pms[hl], va[:, hl * dv:(hl + 1) * dv] * betas[hl]) for hl in heads]
    ws = [_dot3(pms[hl], ks[hl // 2] * betas[hl] * jnp.exp(gcs[hl])) for hl in heads]
    qks = [jnp.where(causal, kk_qk[hl // 2][c:2 * c] * decays[hl], 0.0) for hl in heads]
    s_in = [s_scr[hl] for hl in heads]
    v_news = [us[hl] - _dot3(ws[hl], s_in[hl]) for hl in heads]
    os_ = [jnp.dot((qss[hl // 2] * jnp.exp(gcs[hl])).astype(BF16), s_in[hl].astype(BF16), preferred_element_type=F32)
           + jnp.dot(qks[hl].astype(BF16), v_news[hl].astype(BF16), preferred_element_type=F32) for hl in heads]
    for hl in heads:
        g_last = gcs[hl][c - 1:c, :]
        kdec = ks[hl // 2] * jnp.exp(g_last - gcs[hl])
        s_scr[hl] = s_in[hl] * jnp.exp(g_last) + _dot3(kdec.T, v_news[hl])
    for hl in heads:
        o = os_[hl]
        z = z_ref[:, hl * dv:(hl + 1) * dv]
        on = o * lax.rsqrt(jnp.mean(o * o, axis=-1, keepdims=True) + NORM_EPS) * nw_ref[...]
        o_ref[:, hl * dv:(hl + 1) * dv] = (on * (z * _sigmoid(z))).astype(BF16)

    @pl.when(tb == n_tb - 1)
    def _():
        sf_ref[...] = s_scr[...]


def _gdn_hp(a_log, dt_bias):
    hp = jnp.zeros((8, LANE), F32)
    return hp.at[0, L_GA:L_GA + GDN_V_HEADS].set(a_log.astype(F32)).at[1, L_GA:L_GA + GDN_V_HEADS].set(
        dt_bias.astype(F32))


def _gdn(p32, conv0, s0, conv_w, a_log, dt_bias, norm_w, bsz, t, c=128, nh=4):
    n_tb = t // c
    n_g = GDN_QK_HEADS // nh
    qw, vw = nh * GDN_DK, 2 * nh * GDN_DV
    row = lambda b, g, i: b * n_tb + i
    in_specs = [
        pl.BlockSpec((c, qw), lambda b, g, i: (row(b, g, i), C_GQ // qw + g)),
        pl.BlockSpec((c, qw), lambda b, g, i: (row(b, g, i), C_GK // qw + g)),
        pl.BlockSpec((c, vw), lambda b, g, i: (row(b, g, i), C_GV // vw + g)),
        pl.BlockSpec((c, vw), lambda b, g, i: (row(b, g, i), C_GZ // vw + g)),
        pl.BlockSpec((c, LANE), lambda b, g, i: (row(b, g, i), MISC_BLK)),
        pl.BlockSpec((CONV_W, qw), lambda b, g, i: (0, g)),
        pl.BlockSpec((CONV_W, qw), lambda b, g, i: (0, n_g + g)),
        pl.BlockSpec((CONV_W, vw), lambda b, g, i: (0, n_g + g)),
        pl.BlockSpec((None, 8, qw), lambda b, g, i: (b, 0, g)),
        pl.BlockSpec((None, 8, qw), lambda b, g, i: (b, 0, n_g + g)),
        pl.BlockSpec((None, 8, vw), lambda b, g, i: (b, 0, n_g + g)),
        pl.BlockSpec((8, LANE), lambda b, g, i: (0, 0)),
        pl.BlockSpec((1, GDN_DV), lambda b, g, i: (0, 0)),
        pl.BlockSpec((None, 2 * nh, GDN_DK, GDN_DV), lambda b, g, i: (b, g, 0, 0)),
    ]
    out_specs = [pl.BlockSpec((c, vw), lambda b, g, i: (row(b, g, i), g)),
                 pl.BlockSpec((None, 2 * nh, GDN_DK, GDN_DV), lambda b, g, i: (b, g, 0, 0))]
    return pl.pallas_call(
        functools.partial(_gdn_kernel, c=c, nh=nh),
        grid=(bsz, n_g, n_tb),
        in_specs=in_specs, out_specs=out_specs,
        out_shape=[jax.ShapeDtypeStruct((bsz * t, GDN_V_HEADS * GDN_DV), BF16),
                   jax.ShapeDtypeStruct((bsz, GDN_V_HEADS, GDN_DK, GDN_DV), F32)],
        scratch_shapes=[pltpu.VMEM((2 * nh, GDN_DK, GDN_DV), F32), pltpu.VMEM((8, qw), F32), pltpu.VMEM((8, qw), F32),
                        pltpu.VMEM((8, vw), F32), pltpu.VMEM((c + 8, vw), F32)],
        compiler_params=_cparams(("parallel", "parallel", "arbitrary")),
        name="gdn",
    )(p32, p32, p32, p32, p32, conv_w, conv_w, conv_w, conv0, conv0, conv0, _gdn_hp(a_log, dt_bias),
      norm_w.reshape(1, GDN_DV), s0)


def _gdn_sample_kernel(x_ref, z_ref, m_ref, w_ref, hp_ref, nw_ref, s0_ref, o_ref, sf_ref, *, ts):
    rows, nvh, dk, dv = 8, GDN_V_HEADS, GDN_DK, GDN_DV
    qd = GDN_QK_HEADS * dk
    x8 = x_ref[...]
    w = w_ref[...]
    acc = x8 * w[0:1, :]
    for j in range(1, CONV_W):
        acc = acc + pltpu.roll(x8, rows - j, 0) * w[j:j + 1, :]
    act = acc * _sigmoid(acc)
    live = lax.broadcasted_iota(I32, (rows, 1), 0) < ts
    g_all, beta_all = _gate_params(m_ref[...], hp_ref[...], live)
    qn = [_l2n(act[:, g * dk:(g + 1) * dk]) * (dk ** -0.5) for g in range(GDN_QK_HEADS)]
    kn = [_l2n(act[:, qd + g * dk:qd + (g + 1) * dk]) for g in range(GDN_QK_HEADS)]
    rep = nvh // GDN_QK_HEADS
    cat = lambda parts: jnp.concatenate(parts, axis=0)
    k = cat([kn[h // rep] for h in range(nvh)])
    qs = cat([qn[h // rep] for h in range(nvh)])
    v = cat([act[:, 2 * qd + h * dv:2 * qd + (h + 1) * dv] for h in range(nvh)])
    z = cat([z_ref[:, h * dv:(h + 1) * dv] for h in range(nvh)])
    beta = cat([beta_all[:, L_GB + h:L_GB + h + 1] for h in range(nvh)])
    g_col = cat([g_all[:, L_GA + h:L_GA + h + 1] for h in range(nvh)])
    n = nvh * rows
    ri = lax.broadcasted_iota(I32, (n, n), 0)
    ci = lax.broadcasted_iota(I32, (n, n), 1)
    same = (ri >> 3) == (ci >> 3)
    causal = same & (ri >= ci)
    strict = same & (ri > ci)
    eye = jnp.where(ri == ci, 1.0, 0.0)
    gc = _cumsum_rows(jnp.where(causal, 1.0, 0.0).astype(BF16), g_col)
    gl = _cumsum_rows(jnp.where(same, 1.0, 0.0).astype(BF16), g_col)
    decay = jnp.exp(jnp.where(causal, gc - gc.T, -jnp.inf))
    kb = k * beta
    a_mat = jnp.where(strict, _dot3(kb, k, NT_DIMS) * decay, 0.0)
    nm = -a_mat
    pm = eye + nm
    for _ in range(2):
        nm = _dot3(nm, nm)
        pm = pm + _dot3(pm, nm)
    u = _dot3(pm, v * beta)
    w_rows = _dot3(pm, kb * jnp.exp(gc))
    qk = jnp.where(causal, _dot3(qs, k, NT_DIMS) * decay, 0.0)
    qg = qs * jnp.exp(gc)
    kdt = (k * jnp.exp(gl - gc)).T
    s_in = [s0_ref[h] for h in range(nvh)]
    v_new, o1 = [], []
    for h in range(nvh):
        sl = slice(h * rows, (h + 1) * rows)
        r = _dot3(cat([w_rows[sl], qg[sl]]), s_in[h])
        v_new.append(u[sl] - r[0:rows])
        o1.append(r[rows:2 * rows])
    v_new = cat(v_new)
    o = cat(o1) + _dot3(qk, v_new)
    vh, vl = _split_bf16(v_new)
    lane = lax.broadcasted_iota(I32, (1, n), 1)
    for h in range(nvh):
        kh, kl = _split_bf16(jnp.where((lane >> 3) == h, kdt, 0.0))
        sf_ref[h] = s_in[h] * jnp.exp(gl[h * rows:h * rows + 1, :]) + _dot3s(kh, kl, vh, vl)
    on = o * lax.rsqrt(jnp.mean(o * o, axis=-1, keepdims=True) + NORM_EPS) * nw_ref[...]
    o_ref[...] = on * (z * _sigmoid(z))


def _gdn_sample(p32, state_conv, s0, conv_w, a_log, dt_bias, norm_w, db, ts):
    rows = 8
    conv_dim = conv_w.shape[-1]
    ps = p32.reshape(db, ts, PROJ_W)
    x8 = jnp.concatenate([state_conv, ps[:, :, C_GQ:C_GQ + conv_dim],
                          jnp.zeros((db, rows - ts - (CONV_W - 1), conv_dim), F32)], axis=1)
    pad_t = ((0, 0), (0, rows - ts), (0, 0))
    z8 = jnp.pad(ps[:, :, C_GZ:C_GZ + GDN_V_HEADS * GDN_DV], pad_t)
    m8 = jnp.pad(ps[:, :, C_MISC:C_MISC + LANE], pad_t)
    n = GDN_V_HEADS * rows
    per_d = lambda w: pl.BlockSpec((None, rows, w), lambda d: (d, 0, 0))
    state = pl.BlockSpec((None, GDN_V_HEADS, GDN_DK, GDN_DV), lambda d: (d, 0, 0, 0))
    o, sf = pl.pallas_call(
        functools.partial(_gdn_sample_kernel, ts=ts),
        grid=(db,),
        in_specs=[per_d(conv_dim), per_d(GDN_V_HEADS * GDN_DV), per_d(LANE),
                  pl.BlockSpec((CONV_W, conv_dim), lambda d: (0, 0)),
                  pl.BlockSpec((8, LANE), lambda d: (0, 0)),
                  pl.BlockSpec((1, GDN_DV), lambda d: (0, 0)), state],
        out_specs=[pl.BlockSpec((None, n, GDN_DV), lambda d: (d, 0, 0)), state],
        out_shape=[jax.ShapeDtypeStruct((db, n, GDN_DV), F32),
                   jax.ShapeDtypeStruct((db, GDN_V_HEADS, GDN_DK, GDN_DV), F32)],
        compiler_params=_cparams(("parallel",)),
        name="gdn_sample",
    )(x8, z8, m8, conv_w, _gdn_hp(a_log, dt_bias), norm_w.reshape(1, GDN_DV), s0)
    o = o.reshape(db, GDN_V_HEADS, rows, GDN_DV)[:, :, :ts].transpose(0, 2, 1, 3)
    return o.reshape(db * ts, GDN_V_HEADS * GDN_DV).astype(BF16), sf


def _merge_kernel(a_ref, g_ref, wa_ref, wg_ref, ga_ref, gg_ref, o_ref):
    pa = jnp.dot(a_ref[...], wa_ref[...], preferred_element_type=F32)
    pg = jnp.dot(g_ref[...], wg_ref[...], preferred_element_type=F32)
    o_ref[...] = (_sigmoid(ga_ref[...]) * pa + _sigmoid(gg_ref[...]) * pg).astype(BF16)


def _merge(attn_o, gdn_o, w_oa, w_og, p32, tm):
    m = attn_o.shape[0]
    tn = 512
    return pl.pallas_call(
        _merge_kernel,
        grid=(m // tm, D_MODEL // tn),
        in_specs=[pl.BlockSpec((tm, D_MODEL), lambda i, j: (i, 0)),
                  pl.BlockSpec((tm, D_MODEL), lambda i, j: (i, 0)),
                  pl.BlockSpec((D_MODEL, tn), lambda i, j: (0, j)),
                  pl.BlockSpec((D_MODEL, tn), lambda i, j: (0, j)),
                  pl.BlockSpec((tm, tn), lambda i, j: (i, C_GL // tn + j)),
                  pl.BlockSpec((tm, tn), lambda i, j: (i, (C_GL + D_MODEL) // tn + j))],
        out_specs=pl.BlockSpec((tm, tn), lambda i, j: (i, j)),
        out_shape=jax.ShapeDtypeStruct((m, D_MODEL), BF16),
        compiler_params=_cparams(("parallel", "arbitrary")),
        name="merge",
    )(attn_o, gdn_o, w_oa, w_og, p32, p32)


def _outproj_kernel(mg_ref, w_ref, x_ref, g1_ref, sc_ref, sh_ref, gn_ref, wr_ref, br_ref,
                    x1_ref, h2_ref, ti_ref, tg_ref):
    x1 = x_ref[...] + g1_ref[...] * jnp.dot(mg_ref[...], w_ref[...], preferred_element_type=F32)
    x1_ref[...] = x1
    ms = jnp.mean(x1 * x1, axis=-1, keepdims=True)
    h2 = x1 * lax.rsqrt(ms + NORM_EPS) * gn_ref[...] * (1.0 + sc_ref[...]) + sh_ref[...]
    h2_ref[...] = h2
    logits = _dot3(h2, wr_ref[...]) + br_ref[...]
    tm = logits.shape[0]
    lane = lax.broadcasted_iota(I32, (tm, LANE), 1)
    lanef = lane.astype(F32)
    l = jnp.where(lane < N_EXPERTS, logits, -jnp.inf)
    vals, idxs = [], []
    for _ in range(TOP_K):
        mx = jnp.max(l, axis=1, keepdims=True)
        ix = jnp.min(jnp.where(l == mx, lanef, float(LANE)), axis=1, keepdims=True)
        vals.append(mx)
        idxs.append(ix)
        l = jnp.where(lanef == ix, -jnp.inf, l)
    es = [jnp.exp(v - vals[0]) for v in vals]
    den = es[0]
    for e in es[1:]:
        den = den + e
    ti = jnp.zeros((tm, LANE), F32)
    tg = jnp.zeros((tm, LANE), F32)
    for kk in range(TOP_K):
        ti = jnp.where(lane == kk, idxs[kk], ti)
        tg = jnp.where(lane == kk, es[kk] / den, tg)
    ti_ref[...] = ti.astype(I32)
    tg_ref[...] = tg


def _outproj(merged, w_out, x, mod, g_norm2, w_router, b_router, rows_per_mod, tm):
    m = x.shape[0]
    mod_specs = [_mod_spec(mod, c, rows_per_mod, tm, 1) for c in (MOD_GATE1, MOD_SCALE2, MOD_SHIFT2)]
    wr = jnp.pad(w_router.astype(F32), ((0, 0), (0, LANE - N_EXPERTS)))
    br = jnp.pad(b_router.astype(F32), (0, LANE - N_EXPERTS)).reshape(1, LANE)
    row = pl.BlockSpec((tm, D_MODEL), lambda i: (i, 0))
    small = pl.BlockSpec((tm, LANE), lambda i: (i, 0))
    return pl.pallas_call(
        _outproj_kernel,
        grid=(m // tm,),
        in_specs=[row, pl.BlockSpec((D_MODEL, D_MODEL), lambda i: (0, 0)), row, *mod_specs,
                  pl.BlockSpec((1, D_MODEL), lambda i: (0, 0)),
                  pl.BlockSpec((D_MODEL, LANE), lambda i: (0, 0)),
                  pl.BlockSpec((1, LANE), lambda i: (0, 0))],
        out_specs=[row, row, small, small],
        out_shape=[jax.ShapeDtypeStruct((m, D_MODEL), F32), jax.ShapeDtypeStruct((m, D_MODEL), F32),
                   jax.ShapeDtypeStruct((m, LANE), I32), jax.ShapeDtypeStruct((m, LANE), F32)],
        compiler_params=_cparams(("parallel",)),
        name="outproj",
    )(merged, w_out, x, mod, mod, mod, g_norm2.reshape(1, D_MODEL), wr, br)


GATHER_ROWS = 256
GATHER_UNROLL = 8


def _row_copy(src_ref, o_ref, sem, src_row, dst_row):
    return pltpu.make_async_copy(src_ref.at[pl.ds(src_row, 1)], o_ref.at[pl.ds(dst_row, 1)], sem)


def _gather_kernel(live_ref, idx_ref, src_ref, o_ref, *scratch):
    sem = scratch[-1]
    dst = scratch[0] if len(scratch) == 2 else o_ref
    n = o_ref.shape[0]
    live = live_ref[pl.program_id(0)] != 0

    @pl.when(live)
    def _():
        def issue(g, carry):
            for k in range(GATHER_UNROLL):
                r = g * GATHER_UNROLL + k
                _row_copy(src_ref, dst, sem, idx_ref[0, r], r).start(priority=k % 2)
            return carry

        lax.fori_loop(0, n // GATHER_UNROLL, issue, 0)
        pltpu.make_async_copy(src_ref.at[pl.ds(0, n)], dst, sem).wait()
        if dst is not o_ref:
            o_ref[...] = dst[...].astype(o_ref.dtype)

    @pl.when(jnp.logical_not(live))
    def _():
        o_ref[...] = jnp.zeros_like(o_ref)


def _gather_rows(src, idx, live, out_dtype):
    n_valid, w = idx.shape[0], src.shape[1]
    n_steps = -(-n_valid // GATHER_ROWS)
    n = n_steps * GATHER_ROWS
    idx = jnp.pad(idx, (0, n - n_valid))
    staging = [] if out_dtype == src.dtype else [pltpu.VMEM((GATHER_ROWS, w), src.dtype)]
    grid_spec = pltpu.PrefetchScalarGridSpec(
        num_scalar_prefetch=1, grid=(n_steps,),
        in_specs=[pl.BlockSpec((None, 1, GATHER_ROWS), lambda i, lv: (i, 0, 0), memory_space=pltpu.SMEM),
                  pl.BlockSpec(memory_space=pl.ANY)],
        out_specs=pl.BlockSpec((GATHER_ROWS, w), lambda i, lv: (i, 0)),
        scratch_shapes=staging + [pltpu.SemaphoreType.DMA(())])
    out = pl.pallas_call(
        _gather_kernel, grid_spec=grid_spec,
        out_shape=jax.ShapeDtypeStruct((n, w), out_dtype),
        compiler_params=pltpu.CompilerParams(dimension_semantics=("arbitrary",), vmem_limit_bytes=VMEM_LIMIT,
                                             disable_bounds_checks=True),
        name="gather_rows",
    )(live, idx.reshape(n_steps, 1, GATHER_ROWS), src)
    return out if n == n_valid else out[:n_valid]


MOE_GROUP = 1536
MOE_SUB = 256


def _moe_kernel(ge_ref, ns_ref, x_ref, wg_ref, wu_ref, wd_ref, bg_ref, bu_ref, bd_ref, o_ref, h_scr, *, n_f, tf):
    g_id = pl.program_id(0)
    j = pl.program_id(1)
    ns = ns_ref[g_id]
    del ge_ref
    n_rows = o_ref.shape[0]

    for k in range(1, n_rows // MOE_SUB + 1):
        rows = k * MOE_SUB

        @pl.when((ns == k) & (j < n_f))
        def _(rows=rows):
            x = x_ref[0:rows, :]
            g = jnp.dot(x, wg_ref[...].astype(BF16), preferred_element_type=F32) + bg_ref[...]
            u = jnp.dot(x, wu_ref[...].astype(BF16), preferred_element_type=F32) + bu_ref[...]
            gate = jnp.minimum(g, SWIGLU_LIMIT)
            up = jnp.clip(u, -SWIGLU_LIMIT, SWIGLU_LIMIT)
            h_scr[j, 0:rows, :] = ((up + 1.0) * gate * _sigmoid(SWIGLU_ALPHA * gate)).astype(BF16)

        @pl.when((ns == k) & (j >= n_f))
        def _(rows=rows):
            acc = jnp.dot(h_scr[0, 0:rows, :], wd_ref[0:tf, :].astype(BF16), preferred_element_type=F32)
            for kk in range(1, n_f):
                acc = acc + jnp.dot(h_scr[kk, 0:rows, :], wd_ref[kk * tf:(kk + 1) * tf, :].astype(BF16),
                                    preferred_element_type=F32)
            o_ref[0:rows, :] = acc + bd_ref[...]
            if rows < n_rows:
                o_ref[rows:, :] = jnp.zeros((n_rows - rows, o_ref.shape[1]), F32)

    @pl.when((ns == 0) & (j >= n_f))
    def _():
        o_ref[...] = jnp.zeros_like(o_ref)


def _moe_ffn(xs, grp_e, n_sub, w_gu, b_gu, w_dn, b_dn, tf):
    n_rows = xs.shape[0]
    n_groups = n_rows // MOE_GROUP
    n_f = D_FF // tf
    tn = tf
    n_o = D_MODEL // tn
    bgu = b_gu.reshape(N_EXPERTS, 1, 2 * D_FF)
    bdn = b_dn.reshape(N_EXPERTS, 1, D_MODEL)

    def jf(g, j, ns):
        return jnp.where(ns[g] > 0, jnp.minimum(j, n_f - 1), 0)

    def jo(g, j, ns):
        return jnp.where(ns[g] > 0, jnp.maximum(j - n_f, 0), 0)

    grid_spec = pltpu.PrefetchScalarGridSpec(
        num_scalar_prefetch=2, grid=(n_groups, n_f + n_o),
        in_specs=[pl.BlockSpec((MOE_GROUP, D_MODEL), lambda g, j, ge, ns: (jnp.where(ns[g] > 0, g, 0), 0)),
                  pl.BlockSpec((None, None, D_MODEL, tf), lambda g, j, ge, ns: (0, ge[g], 0, jf(g, j, ns))),
                  pl.BlockSpec((None, None, D_MODEL, tf), lambda g, j, ge, ns: (0, ge[g], 0, n_f + jf(g, j, ns))),
                  pl.BlockSpec((None, None, D_FF, tn), lambda g, j, ge, ns: (0, ge[g], 0, jo(g, j, ns))),
                  pl.BlockSpec((None, 1, tf), lambda g, j, ge, ns: (ge[g], 0, jf(g, j, ns))),
                  pl.BlockSpec((None, 1, tf), lambda g, j, ge, ns: (ge[g], 0, n_f + jf(g, j, ns))),
                  pl.BlockSpec((None, 1, tn), lambda g, j, ge, ns: (ge[g], 0, jo(g, j, ns)))],
        out_specs=pl.BlockSpec((MOE_GROUP, tn), lambda g, j, ge, ns: (g, jnp.maximum(j - n_f, 0))),
        scratch_shapes=[pltpu.VMEM((n_f, MOE_GROUP, tf), BF16)])
    return pl.pallas_call(
        functools.partial(_moe_kernel, n_f=n_f, tf=tf), grid_spec=grid_spec,
        out_shape=jax.ShapeDtypeStruct((n_rows, D_MODEL), F32),
        compiler_params=pltpu.CompilerParams(dimension_semantics=("arbitrary", "arbitrary"),
                                             vmem_limit_bytes=MOE_VMEM_LIMIT),
        name="moe_ffn",
    )(grp_e, n_sub, xs, w_gu, w_gu, w_dn, bgu, bgu, bdn)


def _final_kernel(x1_ref, yg_ref, tg_ref, g2_ref, gf_ref, o_ref):
    tg = tg_ref[...]
    ffn = jnp.zeros_like(x1_ref)
    for kk in range(TOP_K):
        ffn = ffn + tg[:, kk:kk + 1] * yg_ref[kk]
    x2 = x1_ref[...] + g2_ref[...] * ffn
    ms = jnp.mean(x2 * x2, axis=-1, keepdims=True)
    o_ref[...] = x2 * lax.rsqrt(ms + NORM_EPS) * gf_ref[...]


def _final(x1, yg, row0, tg, mod, g_final, rows_per_mod, tm):
    m = x1.shape[0]
    blk0 = row0 // tm
    mod_spec = _mod_spec(mod, MOD_GATE2, rows_per_mod, tm, 1)
    row = pl.BlockSpec((tm, D_MODEL), lambda i: (i, 0))
    return pl.pallas_call(
        _final_kernel,
        grid=(m // tm,),
        in_specs=[row, pl.BlockSpec((TOP_K, tm, D_MODEL), lambda i: (0, blk0 + i, 0)),
                  pl.BlockSpec((tm, LANE), lambda i: (i, 0)), mod_spec,
                  pl.BlockSpec((1, D_MODEL), lambda i: (0, 0))],
        out_specs=row,
        out_shape=jax.ShapeDtypeStruct((m, D_MODEL), F32),
        compiler_params=_cparams(("parallel",)),
        name="final",
    )(x1, yg, tg, mod, g_final.reshape(1, D_MODEL))


def _route(topi):
    tt = topi.shape[0]
    gs, sub = MOE_GROUP, MOE_SUB
    sel = jnp.sum((topi[:, :, None] == jnp.arange(N_EXPERTS, dtype=I32)[None, None, :]).astype(I32), axis=1)
    counts = jnp.sum(sel, axis=0)
    grp_per_e = (counts + gs - 1) // gs
    grp_end = jnp.cumsum(grp_per_e)
    grp_start = grp_end - grp_per_e
    rank = jnp.cumsum(sel, axis=0) - sel
    dest = jnp.take_along_axis((grp_start * gs)[None, :] + rank, topi, axis=1)
    n_groups = -(-(tt * TOP_K + N_EXPERTS * (gs - 1)) // gs)
    n_rows = n_groups * gs
    tok = jnp.repeat(jnp.arange(tt, dtype=I32), TOP_K)
    row_tok = (jnp.arange(n_rows, dtype=I32) % tt).at[dest.reshape(-1)].set(tok)
    g_ids = jnp.arange(n_groups, dtype=I32)
    grp_e = jnp.minimum(jnp.searchsorted(grp_end, g_ids, side="right"), N_EXPERTS - 1).astype(I32)
    rows_in = jnp.clip(counts[grp_e] - (g_ids - grp_start[grp_e]) * gs, 0, gs)
    rows_in = jnp.where(g_ids < grp_end[-1], rows_in, 0)
    n_sub = ((rows_in + sub - 1) // sub).astype(I32)
    per = gs // sub
    live = (jnp.arange(n_groups * per, dtype=I32) % per < jnp.repeat(n_sub, per)).astype(I32)
    return dest, row_tok, grp_e, n_sub, live


def kernel(x_prompt, x_sample, c_prompt, c_sample, cache_k, cache_v, cache_kidx, state_gdn, state_conv, page_table,
           w_ada, b_ada, g_norm1, g_norm2, g_final, w_in, gdn_conv_w, gdn_a_log, gdn_dt_bias, gdn_norm_w,
           w_o_attn, w_o_gdn, w_out, w_router, b_router, w_gu, b_gu, w_dn, b_dn):
    bsz, t, d = x_prompt.shape
    db, ts, _ = x_sample.shape
    depth = w_ada.shape[0]
    assert depth == 1 and d == D_MODEL
    past = page_table.shape[1] * PAGE_SIZE
    conv_dim = gdn_conv_w.shape[-1]

    n_c = bsz + db
    n_cp = -(-n_c // 8) * 8
    c_all = jnp.pad(jnp.concatenate([c_prompt, c_sample], axis=0), ((0, n_cp - n_c), (0, 0)))
    mod = _ada(c_all, w_ada, b_ada[0])
    mods = {"p": mod[:bsz].reshape(bsz, 1, 6 * d),
            "s": jnp.repeat(mod[bsz:n_c], ts, axis=0).reshape(1, db * ts, 6 * d)}

    w_r = _reorder_w_in(w_in[0])
    w_oa = w_o_attn[0].astype(BF16)
    w_og = w_o_gdn[0].astype(BF16)
    w_o = w_out[0].astype(BF16)

    xp = x_prompt.reshape(bsz * t, d)
    xs = x_sample.reshape(db * ts, d)
    tm_p = min(512, t)
    tm_s = db * ts

    tabs_p = _rot_tables(jnp.arange(t, dtype=I32))
    tabs_s = _rot_tables(jnp.tile(past + jnp.arange(ts, dtype=I32), db))
    p32_p, p16_p = _inproj(xp, g_norm1[0], mods["p"], w_r, tabs_p, t, tm_p)
    p32_s, _ = _inproj(xs, g_norm1[0], mods["s"], w_r, tabs_s, db * ts, tm_s)

    attn_p = _attn_prompt(p32_p, p16_p, bsz, t)
    attn_s = _attn_sample(p32_s, cache_k, cache_v, cache_kidx, page_table, ts)

    conv0_p = jnp.zeros((bsz, 8, conv_dim), F32)
    s0_p = jnp.zeros((bsz, GDN_V_HEADS, GDN_DK, GDN_DV), F32)
    gdn_p, sfin_p = _gdn(p32_p, conv0_p, s0_p, gdn_conv_w[0], gdn_a_log[0], gdn_dt_bias[0], gdn_norm_w[0], bsz, t)
    gdn_s, sfin_s = _gdn_sample(p32_s, state_conv[0], state_gdn[0], gdn_conv_w[0], gdn_a_log[0], gdn_dt_bias[0],
                                gdn_norm_w[0], db, ts)

    outs = {}
    for name, attn_o, gdn_o, p32, x, rows_per_mod, tm in (("p", attn_p, gdn_p, p32_p, xp, t, min(256, t)),
                                                          ("s", attn_s, gdn_s, p32_s, xs, db * ts, min(256, db * ts))):
        merged = _merge(attn_o, gdn_o, w_oa, w_og, p32, min(512, rows_per_mod))
        outs[name] = _outproj(merged, w_o, x, mods[name], g_norm2[0], w_router[0], b_router[0], rows_per_mod, tm)

    h2 = jnp.concatenate([outs["p"][1], outs["s"][1]], axis=0)
    topi = jnp.concatenate([outs["p"][2], outs["s"][2]], axis=0)[:, :TOP_K]
    dest, row_tok, grp_e, n_sub, live = _route(topi)
    xs = _gather_rows(h2, row_tok, live, BF16)
    yb = _moe_ffn(xs, grp_e, n_sub, w_gu, b_gu[0], w_dn, b_dn[0], tf=256)
    live_y = jnp.ones((-(-dest.size // GATHER_ROWS),), I32)
    yg = _gather_rows(yb, dest.T.reshape(-1), live_y, F32).reshape(TOP_K, h2.shape[0], d)

    n_p = bsz * t
    y_p = _final(outs["p"][0], yg, 0, outs["p"][3], mods["p"], g_final, t, min(256, t))
    y_s = _final(outs["s"][0], yg, n_p, outs["s"][3], mods["s"], g_final, db * ts, min(256, db * ts))

    kv = N_KV_HEADS * HEAD_DIM
    new_k_p = p32_p[:, C_AK:C_AK + kv].reshape(1, bsz, t, N_KV_HEADS, HEAD_DIM)
    new_v_p = p32_p[:, C_AV:C_AV + kv].reshape(1, bsz, t, N_KV_HEADS, HEAD_DIM)
    new_ki_p = p32_p[:, C_MISC:C_MISC + IDX_DIM].reshape(1, bsz, t, IDX_DIM)
    new_k_s = p32_s[:, C_AK:C_AK + kv].reshape(1, db, ts, N_KV_HEADS, HEAD_DIM)
    new_v_s = p32_s[:, C_AV:C_AV + kv].reshape(1, db, ts, N_KV_HEADS, HEAD_DIM)
    new_ki_s = p32_s[:, C_MISC:C_MISC + IDX_DIM].reshape(1, db, ts, IDX_DIM)
    conv_p = p32_p.reshape(bsz, t, PROJ_W)[:, t - (CONV_W - 1):, C_GQ:C_GQ + conv_dim][None]
    pre_s = p32_s[:, C_GQ:C_GQ + conv_dim].reshape(db, ts, conv_dim)
    conv_s = jnp.concatenate([state_conv[0], pre_s], axis=1)[:, ts:][None]
    return (y_p.reshape(bsz, t, d), y_s.reshape(db, ts, d), new_k_p, new_v_p, new_ki_p, conv_p, sfin_p[None],
            new_k_s, new_v_s, new_ki_s, conv_s, sfin_s[None])
```
